```python
import jax, jax.numpy as jnp
from jax import lax
import numpy as np

D_MODEL = 1024
BATCH = 16
SEQ = 256
DEPTH = 2
DEC_BATCH = 8
DEC_SEQ = 2048
PAST_LEN = 512

GRID_W = 64
EPS = 1e-6
N_EVEN = (DEPTH + 1) // 2
N_ODD = DEPTH // 2
HG_DK = 128
HG_DV = 128
HG_HEADS = (D_MODEL // 2) // HG_DV
HG_CHUNK = 64
HEAD_DIM = 64
N_HEADS = (D_MODEL // 2) // HEAD_DIM
KV_HEADS = 2
Q_PER_KV = N_HEADS // KV_HEADS
WINDOW = 128
ATT_BLOCK = 128
ROPE_BASE = 10000.0
NEG_INF = -1e30
EVEN_IN = 3 * HG_HEADS * HG_DK + 2 * HG_HEADS * HG_DV + N_HEADS * HEAD_DIM + 2 * KV_HEADS * HEAD_DIM
EVEN_OUT = HG_HEADS * HG_DV + N_HEADS * HEAD_DIM
D_RNN = D_MODEL
RG_BLOCKS = 4
RG_BW = D_RNN // RG_BLOCKS
CONV_W = 4
RG_C = 8.0
PEER_HEADS = 8
PEER_NK = 128
PEER_EXPERTS = PEER_NK * PEER_NK
PEER_DK = 256
PEER_TOPK = 16
PEER_BLOCK = 128

kernel_name = 'hybrid_diffusion_prefix_step'


def _rmsnorm(x, g):
    x32 = x.astype(jnp.float32)
    y = x32 * lax.rsqrt(jnp.mean(x32 * x32, axis=-1, keepdims=True) + EPS)
    return (y * g.astype(jnp.float32)).astype(x.dtype)


def _axial_rope(x):
    L = x.shape[1]
    n_rows = L // GRID_W
    rows = jnp.repeat(jnp.arange(n_rows), GRID_W)
    cols = jnp.tile(jnp.arange(GRID_W), n_rows)
    half = HEAD_DIM // 2
    nf = half // 2
    inv = 1.0 / (ROPE_BASE ** (jnp.arange(nf, dtype=jnp.float32) / nf))

    def rot(xa, pos):
        ang = pos.astype(jnp.float32)[:, None] * inv[None, :]
        cos = jnp.cos(ang)[None, :, None, :]
        sin = jnp.sin(ang)[None, :, None, :]
        x1, x2 = xa[..., :nf], xa[..., nf:]
        return jnp.concatenate([x1 * cos - x2 * sin, x2 * cos + x1 * sin], axis=-1)

    x32 = x.astype(jnp.float32)
    out = jnp.concatenate([rot(x32[..., :half], rows), rot(x32[..., half:], cols)], axis=-1)
    return out.astype(x.dtype)


def _sink_attention(q, ck, cv, sink, k=None, v=None):
    B, Lq = q.shape[0], q.shape[1]
    scale = HEAD_DIM ** -0.5
    sink_logit = jnp.broadcast_to(
        sink.astype(jnp.float32).reshape(1, KV_HEADS, Q_PER_KV, 1, 1),
        (B, KV_HEADS, Q_PER_KV, ATT_BLOCK, 1))
    windowed = k is not None
    n_ctx = ck.shape[1]
    if windowed:
        pad = ((0, 0), (ATT_BLOCK, ATT_BLOCK), (0, 0), (0, 0))
        kp = jnp.pad(k, pad)
        vp = jnp.pad(v, pad)

    def block(j):
        start = j * ATT_BLOCK
        qb = lax.dynamic_slice_in_dim(q, start, ATT_BLOCK, axis=1)
        logits = [jnp.einsum('bqkgd,bskd->bkgqs', qb, ck).astype(jnp.float32) * scale]
        if windowed:
            kb = lax.dynamic_slice_in_dim(kp, start, 3 * ATT_BLOCK, axis=1)
            vb = lax.dynamic_slice_in_dim(vp, start, 3 * ATT_BLOCK, axis=1)
            qpos = start + jnp.arange(ATT_BLOCK)
            kpos = start - ATT_BLOCK + jnp.arange(3 * ATT_BLOCK)
            ok = ((jnp.abs(qpos[:, None] - kpos[None, :]) <= WINDOW)
                  & (kpos[None, :] >= 0) & (kpos[None, :] < Lq))
            s_loc = jnp.einsum('bqkgd,bskd->bkgqs', qb, kb).astype(jnp.float32) * scale
            logits.append(jnp.where(ok, s_loc, NEG_INF))
        probs = jax.nn.softmax(jnp.concatenate(logits + [sink_logit], axis=-1), axis=-1)
        out = jnp.einsum('bkgqs,bskd->bqkgd', probs[..., :n_ctx].astype(cv.dtype), cv)
        if windowed:
            p_loc = probs[..., n_ctx:n_ctx + 3 * ATT_BLOCK].astype(vb.dtype)
            out = out + jnp.einsum('bkgqs,bskd->bqkgd', p_loc, vb)
        return out

    o = lax.map(block, jnp.arange(Lq // ATT_BLOCK))
    return jnp.moveaxis(o, 0, 1).reshape(B, Lq, N_HEADS * HEAD_DIM)


def _hgrn2_scan(q, k, v, logf, s0):
    B, L, H, DK = q.shape
    DV = v.shape[-1]
    n = L // HG_CHUNK
    mask = jnp.tril(jnp.ones((HG_CHUNK, HG_CHUNK), dtype=bool))

    def chunks(a):
        return a.reshape(B, n, HG_CHUNK, H, a.shape[-1]).swapaxes(0, 1)

    def step(S, inp):
        qc, kc, vc, gc = inp
        b = jnp.cumsum(gc, axis=1)
        bm = b[:, HG_CHUNK // 2 - 1][:, None]
        qi = qc * jnp.exp(b - bm)
        ki = kc * jnp.exp(bm - b)
        A = jnp.where(mask, jnp.einsum('bthd,bshd->bhts', qi, ki), 0.0)
        o = (jnp.einsum('bhts,bshv->bthv', A, vc)
             + jnp.einsum('bthd,bhdv->bthv', qc * jnp.exp(b), S))
        bl = b[:, -1:]
        S = (jnp.exp(bl[:, 0])[..., None] * S
             + jnp.einsum('bshd,bshv->bhdv', kc * jnp.exp(bl - b), vc))
        return S, o

    S, o = lax.scan(step, s0.astype(jnp.float32), (chunks(q), chunks(k), chunks(v), chunks(logf)))
    return o.swapaxes(0, 1).reshape(B, L, H, DV), S


def _even_mixer(h, w_in, w_out, lb_f, lb_b, gn_g, sink, s0f, s0b, ck=None, cv=None):
    B, L, _ = h.shape
    f32 = jnp.float32
    hk = HG_HEADS * HG_DK
    hv = HG_HEADS * HG_DV
    splits = np.cumsum([hk, hk, hk, hv, hv, N_HEADS * HEAD_DIM, KV_HEADS * HEAD_DIM]).tolist()
    zq, zf, zb, zi, zg, aq, ak, av = jnp.split(h @ w_in, splits, axis=-1)
    q = jax.nn.silu(zq.astype(f32)).reshape(B, L, HG_HEADS, HG_DK)
    vi = zi.astype(f32).reshape(B, L, HG_HEADS, HG_DV)

    def gates(z, lb):
        z = z.astype(f32).reshape(B, L, HG_HEADS, HG_DK)
        lb = lb.reshape(HG_HEADS, HG_DK)
        return jnp.log(lb + (1.0 - lb) * jax.nn.sigmoid(z)), (1.0 - lb) * jax.nn.sigmoid(-z)

    gf, kf = gates(zf, lb_f)
    gb, kb = gates(zb, lb_b)

    def flip(a):
        return jnp.flip(a, axis=1)

    o_f, sf = _hgrn2_scan(q, kf, vi, gf, s0f)
    o_b, sb = _hgrn2_scan(flip(q), flip(kb), flip(vi), flip(gb), s0b)
    o = o_f + flip(o_b)
    o = (o * lax.rsqrt(jnp.mean(o * o, axis=-1, keepdims=True) + EPS)).reshape(B, L, hv)
    o_hg = o * gn_g.astype(f32) * jax.nn.silu(zg.astype(f32))
    aq = aq.reshape(B, L, N_HEADS, HEAD_DIM)
    ak = ak.reshape(B, L, KV_HEADS, HEAD_DIM)
    av = av.reshape(B, L, KV_HEADS, HEAD_DIM)
    if ck is None:
        o_att = _sink_attention(aq.reshape(B, L, KV_HEADS, Q_PER_KV, HEAD_DIM), ak, av, sink)
    else:
        qr = _axial_rope(aq).reshape(B, L, KV_HEADS, Q_PER_KV, HEAD_DIM)
        o_att = _sink_attention(qr, ck, cv, sink, _axial_rope(ak), av)
    y = jnp.concatenate([o_hg.astype(h.dtype), o_att.astype(h.dtype)], axis=-1) @ w_out
    return y, ak, av, sf, sb


def _centred_dwconv(x, w, b):
    y = lax.conv_general_dilated(
        x, w[:, None, :].astype(x.dtype), window_strides=(1,),
        padding=[(CONV_W // 2, CONV_W - 1 - CONV_W // 2)],
        dimension_numbers=('NWC', 'WIO', 'NWC'), feature_group_count=x.shape[-1])
    return y + b


def _rglru(x, w_a, b_a, w_x, b_x, lam, h0):
    B, L, D = x.shape
    f32 = jnp.float32
    xb = x.reshape(B, L, RG_BLOCKS, RG_BW)
    r = jax.nn.sigmoid(jnp.einsum('blnc,ncd->blnd', xb, w_a.astype(f32)).reshape(B, L, D) + b_a.astype(f32))
    i = jax.nn.sigmoid(jnp.einsum('blnc,ncd->blnd', xb, w_x.astype(f32)).reshape(B, L, D) + b_x.astype(f32))
    log_a = -RG_C * jax.nn.softplus(-lam.astype(f32)) * r
    a = jnp.exp(log_a)
    u = jnp.sqrt(-jnp.expm1(2.0 * log_a)) * (i * x)
    u = u.at[:, 0].add(a[:, 0] * h0.astype(f32))

    def comb(lhs, rhs):
        a1, b1 = lhs
        a2, b2 = rhs
        return a1 * a2, a2 * b1 + b2

    _, hs = lax.associative_scan(comb, (a, u), axis=1)
    return hs, hs[:, -1]


def _odd_mixer(h, w_in, w_out, cw, cb, w_a, b_a, w_x, b_x, lam, h0f, h0b):
    gate, xr = jnp.split(h @ w_in, 2, axis=-1)
    xc = _centred_dwconv(xr, cw, cb).astype(jnp.float32)
    hf, sf = _rglru(xc, w_a[0], b_a[0], w_x[0], b_x[0], lam[0], h0f)
    hb, sb = _rglru(jnp.flip(xc, axis=1), w_a[1], b_a[1], w_x[1], b_x[1], lam[1], h0b)
    y = (hf + jnp.flip(hb, axis=1)) * jax.nn.gelu(gate.astype(jnp.float32))
    return y.astype(h.dtype) @ w_out, sf, sb


def _peer(h, w_q, keys, u, v):
    B, L, D = h.shape
    xt = h.reshape(B * L // PEER_BLOCK, PEER_BLOCK, D)

    def blk(xb):
        q = (xb @ w_q).reshape(PEER_BLOCK, PEER_HEADS, 2, PEER_DK // 2)
        s1 = jnp.einsum('thd,hnd->thn', q[:, :, 0], keys[0]).astype(jnp.float32)
        s2 = jnp.einsum('thd,hnd->thn', q[:, :, 1], keys[1]).astype(jnp.float32)
        v1, i1 = lax.top_k(s1, PEER_TOPK)
        v2, i2 = lax.top_k(s2, PEER_TOPK)
        cand = (v1[..., :, None] + v2[..., None, :]).reshape(PEER_BLOCK, PEER_HEADS, PEER_TOPK * PEER_TOPK)
        cidx = (i1[..., :, None] * PEER_NK + i2[..., None, :]).reshape(PEER_BLOCK, PEER_HEADS, PEER_TOPK * PEER_TOPK)
        sc, pos = lax.top_k(cand, PEER_TOPK)
        eidx = jnp.take_along_axis(cidx, pos, axis=-1)
        g = jax.nn.softmax(sc, axis=-1)
        act = jax.nn.gelu(jnp.einsum('thkd,td->thk', u[eidx], xb).astype(jnp.float32))
        return jnp.einsum('thk,thkd->td', (g * act).astype(v.dtype), v[eidx])

    return lax.map(blk, xt).reshape(B, L, D)


def _trunk(x, cond, p, ctx):
    B = x.shape[0]
    f32 = jnp.float32
    lbf_all = jnp.cumsum(jax.nn.softmax(p['hgrn_lb_fwd'].astype(f32), axis=0), axis=0)
    lbb_all = jnp.cumsum(jax.nn.softmax(p['hgrn_lb_bwd'].astype(f32), axis=0), axis=0)
    new = {'k': [], 'v': [], 'hf': [], 'hb': [], 'lf': [], 'lb': []}
    for l in range(DEPTH):
        j = l // 2
        mod = (jax.nn.silu(cond) @ p['w_mod'][l] + p['b_mod'][l])[:, None, :]
        sh1, sc1, g1, sh2, sc2, g2 = jnp.split(mod, 6, axis=-1)
        h = _rmsnorm(x, p['norm1_g'][l]) * (1 + sc1) + sh1
        if l % 2 == 0:
            if ctx is None:
                s0f = jnp.zeros((B, HG_HEADS, HG_DK, HG_DV), f32)
                s0b = jnp.zeros((B, HG_HEADS, HG_DK, HG_DV), f32)
                ck = None
                cv = None
            else:
                s0f = ctx['hf'][:, j]
                s0b = ctx['hb'][:, j]
                ck = ctx['k'][:, j]
                cv = ctx['v'][:, j]
            y, kk, vv, sf, sb = _even_mixer(
                h, p['w_in_even'][j], p['w_out_even'][j], lbf_all[j], lbb_all[j],
                p['hgrn_gnorm_g'][j], p['attn_sink'][j], s0f, s0b, ck, cv)
            if ctx is None:
                new['k'].append(kk)
                new['v'].append(vv)
                new['hf'].append(sf)
                new['hb'].append(sb)
        else:
            if ctx is None:
                h0f = jnp.zeros((B, D_RNN), f32)
                h0b = jnp.zeros((B, D_RNN), f32)
            else:
                h0f = ctx['lf'][:, j]
                h0b = ctx['lb'][:, j]
            y, sf, sb = _odd_mixer(
                h, p['w_in_odd'][j], p['w_out_odd'][j], p['conv_w'][j], p['conv_b'][j],
                p['rg_w_a'][j], p['rg_b_a'][j], p['rg_w_x'][j], p['rg_b_x'][j],
                p['rg_lambda'][j], h0f, h0b)
            if ctx is None:
                new['lf'].append(sf)
                new['lb'].append(sb)
        x = x + g1 * y
        h = _rmsnorm(x, p['norm2_g'][l]) * (1 + sc2) + sh2
        x = x + g2 * _peer(h, p['peer_w_q'][l], p['peer_keys'][l], p['peer_u'][l], p['peer_v'][l])
    return _rmsnorm(x, p['final_g']), new


def setup_inputs(seed: int = 0) -> dict:
    key = jax.random.key(seed)
    keys = jax.random.split(key, 48)
    counter = iter(range(48))
    f32 = jnp.float32

    def nk():
        return keys[next(counter)]

    def nrm(shape, scale):
        return scale * jax.random.normal(nk(), shape, f32)

    def gain(shape):
        return 1.0 + 0.05 * jax.random.normal(nk(), shape, f32)

    ua = jax.random.uniform(nk(), (N_ODD, 2, D_RNN), f32, 0.9, 0.999)
    s = ua ** (1.0 / RG_C)
    lam = jnp.log(s) - jnp.log1p(-s)
    D = D_MODEL
    return {
        'x_prompt': nrm((BATCH, SEQ, D), 1.0),
        'x_sample': nrm((DEC_BATCH, DEC_SEQ, D), 1.0),
        'cache_k': nrm((DEC_BATCH, N_EVEN, PAST_LEN, KV_HEADS, HEAD_DIM), 1.0),
        'cache_v': nrm((DEC_BATCH, N_EVEN, PAST_LEN, KV_HEADS, HEAD_DIM), 1.0),
        'state_hgrn_fwd': nrm((DEC_BATCH, N_EVEN, HG_HEADS, HG_DK, HG_DV), 0.5),
        'state_hgrn_bwd': nrm((DEC_BATCH, N_EVEN, HG_HEADS, HG_DK, HG_DV), 0.5),
        'state_lru_fwd': nrm((DEC_BATCH, N_ODD, D_RNN), 0.5),
        'state_lru_bwd': nrm((DEC_BATCH, N_ODD, D_RNN), 0.5),
        'c': nrm((DEC_BATCH, D), 1.0),
        'c_ctx': nrm((D,), 1.0),
        'norm1_g': gain((DEPTH, D)),
        'norm2_g': gain((DEPTH, D)),
        'w_mod': nrm((DEPTH, D, 6 * D), 0.5 * D ** -0.5),
        'b_mod': nrm((DEPTH, 6 * D), 0.02),
        'w_in_even': nrm((N_EVEN, D, EVEN_IN), D ** -0.5),
        'w_out_even': nrm((N_EVEN, EVEN_OUT, D), EVEN_OUT ** -0.5),
        'hgrn_lb_fwd': nrm((N_EVEN + 1, HG_HEADS * HG_DK), 0.1),
        'hgrn_lb_bwd': nrm((N_EVEN + 1, HG_HEADS * HG_DK), 0.1),
        'hgrn_gnorm_g': gain((N_EVEN, HG_HEADS * HG_DV)),
        'attn_sink': nrm((N_EVEN, N_HEADS), 0.5),
        'w_in_odd': nrm((N_ODD, D, 2 * D_RNN), D ** -0.5),
        'w_out_odd': nrm((N_ODD, D_RNN, D), D_RNN ** -0.5),
        'conv_w': nrm((N_ODD, CONV_W, D_RNN), CONV_W ** -0.5),
        'conv_b': nrm((N_ODD, D_RNN), 0.02),
        'rg_w_a': nrm((N_ODD, 2, RG_BLOCKS, RG_BW, RG_BW), RG_BW ** -0.5),
        'rg_b_a': nrm((N_ODD, 2, D_RNN), 0.02),
        'rg_w_x': nrm((N_ODD, 2, RG_BLOCKS, RG_BW, RG_BW), RG_BW ** -0.5),
        'rg_b_x': nrm((N_ODD, 2, D_RNN), 0.02),
        'rg_lambda': lam,
        'peer_w_q': nrm((DEPTH, D, PEER_HEADS * PEER_DK), D ** -0.5),
        'peer_keys': nrm((DEPTH, 2, PEER_HEADS, PEER_NK, PEER_DK // 2), (PEER_DK // 2) ** -0.5),
        'peer_u': nrm((DEPTH, PEER_EXPERTS, D), D ** -0.5),
        'peer_v': nrm((DEPTH, PEER_EXPERTS, D), 1.0),
        'final_g': gain((D,)),
    }


def reference(x_prompt, x_sample, cache_k, cache_v, state_hgrn_fwd, state_hgrn_bwd,
              state_lru_fwd, state_lru_bwd, c, c_ctx, norm1_g, norm2_g, w_mod, b_mod,
              w_in_even, w_out_even, hgrn_lb_fwd, hgrn_lb_bwd, hgrn_gnorm_g, attn_sink,
              w_in_odd, w_out_odd, conv_w, conv_b, rg_w_a, rg_b_a, rg_w_x, rg_b_x, rg_lambda,
              peer_w_q, peer_keys, peer_u, peer_v, final_g):
    params = dict(
        norm1_g=norm1_g, norm2_g=norm2_g, w_mod=w_mod, b_mod=b_mod,
        w_in_even=w_in_even, w_out_even=w_out_even, hgrn_lb_fwd=hgrn_lb_fwd,
        hgrn_lb_bwd=hgrn_lb_bwd, hgrn_gnorm_g=hgrn_gnorm_g, attn_sink=attn_sink,
        w_in_odd=w_in_odd, w_out_odd=w_out_odd, conv_w=conv_w, conv_b=conv_b,
        rg_w_a=rg_w_a, rg_b_a=rg_b_a, rg_w_x=rg_w_x, rg_b_x=rg_b_x, rg_lambda=rg_lambda,
        peer_w_q=peer_w_q, peer_keys=peer_keys, peer_u=peer_u, peer_v=peer_v, final_g=final_g)
    y_prompt, st = _trunk(x_prompt, c_ctx[None, :], params, None)
    cache = dict(k=cache_k, v=cache_v, hf=state_hgrn_fwd, hb=state_hgrn_bwd,
                 lf=state_lru_fwd, lb=state_lru_bwd)
    y_sample, _ = _trunk(x_sample, c, params, cache)
    new_k = jnp.stack(st['k'], axis=1)
    new_v = jnp.stack(st['v'], axis=1)
    new_hgrn_fwd = jnp.stack(st['hf'], axis=1)
    new_hgrn_bwd = jnp.stack(st['hb'], axis=1)
    new_lru_fwd = jnp.stack(st['lf'], axis=1)
    new_lru_bwd = jnp.stack(st['lb'], axis=1)
    return (y_prompt, y_sample, new_k, new_v, new_hgrn_fwd, new_hgrn_bwd, new_lru_fwd, new_lru_bwd)
```

```python
import functools

import numpy as np
import jax
import jax.numpy as jnp
from jax import lax
from jax.experimental import pallas as pl
from jax.experimental.pallas import tpu as pltpu

F32 = jnp.float32
BF16 = jnp.bfloat16

D_MODEL = 1024
BATCH = 16
SEQ = 256
DEPTH = 2
DEC_BATCH = 8
DEC_SEQ = 2048
PAST_LEN = 512
GRID_W = 64
EPS = 1e-6
HG_DK = 128
HG_DV = 128
HG_HEADS = 4
HG_CHUNK = 64
HG_W = HG_HEADS * HG_DK
HEAD_DIM = 64
N_HEADS = 8
KV_HEADS = 2
Q_PER_KV = 4
WINDOW = 128
ATT_BLOCK = 128
ROPE_BASE = 10000.0
ATT_W = N_HEADS * HEAD_DIM
KV_W = KV_HEADS * HEAD_DIM
D_RNN = 1024
RG_BLOCKS = 4
RG_BW = 256
RG_C = 8.0
PEER_HEADS = 8
PEER_NK = 128
PEER_DK = 256
PEER_TOPK = 16

N_PROMPT = BATCH * SEQ
N_SAMPLE = DEC_BATCH * DEC_SEQ
N_TOK = N_PROMPT + N_SAMPLE
TB = 256
N_TILES = N_TOK // TB
PROMPT_TILES = N_PROMPT // TB
TILES_PER_SAMPLE = DEC_SEQ // TB
N_COND = 1 + DEC_BATCH
COND_PAD = 16

PEER_GROUP = 1024
PEER_NGROUPS = PEER_NK * PEER_NK // PEER_GROUP
CHUNKS_PER_GROUP = PEER_GROUP // PEER_NK
G_PITCH = TB + 8

LRU_CHUNK = 256

VMEM_LIMIT = 56 * 1024 * 1024


def _cparams(*sem):
    return pltpu.CompilerParams(dimension_semantics=sem, vmem_limit_bytes=VMEM_LIMIT)


def _dot(a, b):
    return jnp.dot(a, b, preferred_element_type=F32)


def _dot_nt(a, b):
    return lax.dot_general(a, b, (((1,), (1,)), ((), ())), preferred_element_type=F32)


def _dot_tn(a, b):
    return lax.dot_general(a, b, (((0,), (0,)), ((), ())), preferred_element_type=F32)


def _sigmoid(x):
    return 1.0 / (1.0 + jnp.exp(-x))


def _silu(x):
    return x * _sigmoid(x)


def _gelu_tanh(x):
    c = np.float32(np.sqrt(2.0 / np.pi))
    return x * (0.5 * (1.0 + jnp.tanh(c * (x + 0.044715 * (x * x * x)))))


def _rms_mod(x, g, mod, sh, sc):
    y = x * lax.rsqrt(jnp.mean(x * x, axis=-1, keepdims=True) + EPS) * g
    return y * (1.0 + mod[sc:sc + 1, :]) + mod[sh:sh + 1, :]


def _mod_kernel(c_ref, w_ref, b_ref, o_ref):
    c = c_ref[...]
    s = _silu(c).astype(BF16)
    o_ref[0] = _dot(s, w_ref[0].astype(BF16)) + b_ref[0]


def _modulation(cond, w_mod, b_mod):
    nb = 6 * D_MODEL // 1024
    return pl.pallas_call(
        _mod_kernel,
        grid=(DEPTH, nb),
        in_specs=[
            pl.BlockSpec((COND_PAD, D_MODEL), lambda l, n: (0, 0)),
            pl.BlockSpec((1, D_MODEL, 1024), lambda l, n: (l, 0, n)),
            pl.BlockSpec((1, 1, 1024), lambda l, n: (l, 0, n)),
        ],
        out_specs=pl.BlockSpec((1, COND_PAD, 1024), lambda l, n: (l, 0, n)),
        out_shape=jax.ShapeDtypeStruct((DEPTH, COND_PAD, 6 * D_MODEL), F32),
        compiler_params=_cparams("parallel", "parallel"),
        name="modulation",
    )(cond, w_mod, b_mod.reshape(DEPTH, 1, 6 * D_MODEL))


def _in_proj_kernel(x_ref, mod_ref, g_ref, w_ref, *o_refs, widths):
    h = _rms_mod(x_ref[...], g_ref[...], mod_ref[0], 0, 1)
    z = _dot(h.astype(BF16), w_ref[...])
    off = 0
    for o_ref, wd in zip(o_refs, widths):
        o_ref[...] = z[:, off:off + wd]
        off += wd


def _in_proj(x, modt, g, w_bf16, widths, name):
    n_out = sum(widths)
    return pl.pallas_call(
        functools.partial(_in_proj_kernel, widths=widths),
        grid=(N_TILES,),
        in_specs=[
            pl.BlockSpec((TB, D_MODEL), lambda i: (i, 0)),
            pl.BlockSpec((1, 6, D_MODEL), lambda i: (i, 0, 0)),
            pl.BlockSpec((1, D_MODEL), lambda i: (0, 0)),
            pl.BlockSpec((D_MODEL, n_out), lambda i: (0, 0)),
        ],
        out_specs=[pl.BlockSpec((TB, wd), lambda i: (i, 0)) for wd in widths],
        out_shape=[jax.ShapeDtypeStruct((N_TOK, wd), F32) for wd in widths],
        compiler_params=_cparams("parallel"),
        name=name,
    )(x, modt, g.reshape(1, D_MODEL), w_bf16)


def _hgrn_direction(zq, zgate, zi, lb, st_ref, fwd):
    n = HG_CHUNK
    row = lax.broadcasted_iota(jnp.int32, (n, n), 0)
    col = lax.broadcasted_iota(jnp.int32, (n, n), 1)
    keep = (col <= row) if fwd else (col >= row)
    tri = jnp.where(keep, 1.0, 0.0).astype(BF16)
    q = _silu(zq)
    f = lb + (1.0 - lb) * _sigmoid(zgate)
    g = jnp.log(f)
    k = (1.0 - lb) * _sigmoid(-zgate)
    g_hi = g.astype(BF16)
    g_lo = (g - g_hi.astype(F32)).astype(BF16)
    b = _dot(tri, g_hi) + _dot(tri, g_lo)
    mid = n // 2 - 1 if fwd else n // 2
    last = n - 1 if fwd else 0
    bm = b[mid:mid + 1, :]
    bl = b[last:last + 1, :]
    qi = (q * jnp.exp(b - bm)).astype(BF16)
    ki = (k * jnp.exp(bm - b)).astype(BF16)
    qs = (q * jnp.exp(b)).astype(BF16)
    ks = (k * jnp.exp(bl - b)).astype(BF16)
    vb = zi.astype(BF16)
    decay = jnp.exp(bl)
    outs = []
    for h in range(HG_HEADS):
        sl = slice(h * HG_DK, (h + 1) * HG_DK)
        a = jnp.where(keep, _dot_nt(qi[:, sl], ki[:, sl]), 0.0)
        st = st_ref[h]
        o = _dot(a.astype(BF16), vb[:, sl]) + _dot_nt(qs[:, sl], st.astype(BF16))
        st_ref[h] = decay[:, sl] * st + _dot_tn(vb[:, sl], ks[:, sl])
        outs.append(o)
    return jnp.concatenate(outs, axis=1)


def _hgrn_kernel(tbl_ref, zqf_ref, zff_ref, zif_ref, zqb_ref, zbb_ref, zib_ref, lbf_ref, lbb_ref,
                 s0f_ref, s0b_ref, of_ref, ob_ref, sf_ref, sb_ref, stf, stb):
    s = pl.program_id(0)

    @pl.when(tbl_ref[4, s] == 1)
    def _():
        stf[...] = s0f_ref[0]
        stb[...] = s0b_ref[0]

    of_ref[...] = _hgrn_direction(zqf_ref[...], zff_ref[...], zif_ref[...], lbf_ref[...], stf, True)
    ob_ref[...] = _hgrn_direction(zqb_ref[...], zbb_ref[...], zib_ref[...], lbb_ref[...], stb, False)

    @pl.when(tbl_ref[5, s] == 1)
    def _():
        sf_ref[0] = stf[...]
        sb_ref[0] = stb[...]


def _scan_table(chunk):
    rows = []
    blk = 0
    for (nseq, length, has_init) in ((BATCH, SEQ, False), (DEC_BATCH, DEC_SEQ, True)):
        nc = length // chunk
        for b in range(nseq):
            sid = b if not has_init else BATCH + b
            for i in range(nc):
                rows.append((blk + i, blk + nc - 1 - i, (1 + b) if has_init else 0, sid,
                             int(i == 0), int(i == nc - 1)))
            blk += nc
    return np.asarray(rows, np.int32).T.copy()


def _hgrn_scan(zq, zf, zb, zi, lbf, lbb, s0f_t, s0b_t):
    tbl = _scan_table(HG_CHUNK)
    n_steps = tbl.shape[1]
    n_seq = BATCH + DEC_BATCH
    blk = lambda r: pl.BlockSpec((HG_CHUNK, HG_W), lambda s, t: (t[r, s], 0))
    vec = pl.BlockSpec((1, HG_W), lambda s, t: (0, 0))
    st_in = pl.BlockSpec((1, HG_HEADS, HG_DV, HG_DK), lambda s, t: (t[2, s], 0, 0, 0))
    st_out = pl.BlockSpec((1, HG_HEADS, HG_DV, HG_DK), lambda s, t: (t[3, s], 0, 0, 0))
    grid_spec = pltpu.PrefetchScalarGridSpec(
        num_scalar_prefetch=1,
        grid=(n_steps,),
        in_specs=[blk(0), blk(0), blk(0), blk(1), blk(1), blk(1), vec, vec, st_in, st_in],
        out_specs=[blk(0), blk(1), st_out, st_out],
        scratch_shapes=[pltpu.VMEM((HG_HEADS, HG_DV, HG_DK), F32),
                        pltpu.VMEM((HG_HEADS, HG_DV, HG_DK), F32)],
    )
    st_shape = jax.ShapeDtypeStruct((n_seq, HG_HEADS, HG_DV, HG_DK), F32)
    return pl.pallas_call(
        _hgrn_kernel,
        grid_spec=grid_spec,
        out_shape=[jax.ShapeDtypeStruct((N_TOK, HG_W), F32), jax.ShapeDtypeStruct((N_TOK, HG_W), F32),
                   st_shape, st_shape],
        compiler_params=_cparams("arbitrary"),
        name="hgrn_scan",
    )(jnp.asarray(tbl), zq, zf, zi, zq, zb, zi, lbf, lbb, s0f_t, s0b_t)


def _rope(x, cos, sgn_sin):
    w = x.shape[1]
    lane = lax.broadcasted_iota(jnp.int32, x.shape, 1)
    partner = jnp.where((lane & 31) < 16, pltpu.roll(x, w - 16, 1), pltpu.roll(x, 16, 1))
    return x * cos + partner * sgn_sin


def _dup_kv_half(x, kv):
    lane = lax.broadcasted_iota(jnp.int32, x.shape, 1)
    own = (lane < HEAD_DIM) if kv == 0 else (lane >= HEAD_DIM)
    xs = jnp.where(own, x, 0.0)
    return xs + pltpu.roll(xs, HEAD_DIM, 1)


def _attend(q, key_parts, sink_ref, o_ref):
    lane = lax.broadcasted_iota(jnp.int32, (ATT_BLOCK, 128), 1)
    lo = lane < HEAD_DIM
    for kv in range(KV_HEADS):
        qs = []
        for gq in range(Q_PER_KV):
            hd = kv * Q_PER_KV + gq
            pair = q[:, (hd // 2) * 128:(hd // 2 + 1) * 128]
            qs.append(jnp.where(lo if hd % 2 == 0 else ~lo, pair, 0.0))
        qst = jnp.concatenate(qs, axis=0).astype(BF16)
        logits = []
        vals = []
        for (k, v, mask) in key_parts:
            kd = _dup_kv_half(k, kv).astype(BF16)
            vals.append(_dup_kv_half(v, kv).astype(BF16))
            s = _dot_nt(qst, kd)
            if mask is not None:
                s = jnp.where(mask, s, -1e30)
            logits.append(s)
        m = logits[0].max(axis=-1, keepdims=True)
        for s in logits[1:]:
            m = jnp.maximum(m, s.max(axis=-1, keepdims=True))
        sink_col = jnp.concatenate(
            [jnp.full((ATT_BLOCK, 1), sink_ref[kv * Q_PER_KV + gq], F32) for gq in range(Q_PER_KV)], axis=0)
        m = jnp.maximum(m, sink_col)
        den = jnp.exp(sink_col - m)
        acc = jnp.zeros((Q_PER_KV * ATT_BLOCK, 128), F32)
        for s, v in zip(logits, vals):
            p = jnp.exp(s - m)
            den = den + p.sum(axis=-1, keepdims=True)
            acc = acc + _dot(p.astype(BF16), v)
        out = acc / den
        for pr in range(Q_PER_KV // 2):
            r0 = out[(2 * pr) * ATT_BLOCK:(2 * pr + 1) * ATT_BLOCK]
            r1 = out[(2 * pr + 1) * ATT_BLOCK:(2 * pr + 2) * ATT_BLOCK]
            c0 = (kv * Q_PER_KV // 2 + pr) * 128
            o_ref[:, c0:c0 + 128] = jnp.where(lo, r0, r1)


def _attn_prompt_kernel(sink_ref, q_ref, k_ref, v_ref, o_ref):
    q = q_ref[...] * np.float32(HEAD_DIM ** -0.5)
    _attend(q, [(k_ref[...], v_ref[...], None)], sink_ref, o_ref)


def _attn_latent_kernel(sink_ref, q_ref, kp_ref, kc_ref, kn_ref, vp_ref, vc_ref, vn_ref,
                        ck_ref, cv_ref, cos_ref, sin_ref, o_ref):
    j = pl.program_id(1)
    nq = pl.num_programs(1)
    qstart = pl.multiple_of((j + 1) * ATT_BLOCK, ATT_BLOCK)
    kstart = pl.multiple_of(j * ATT_BLOCK, ATT_BLOCK)
    cq = cos_ref[pl.ds(qstart, ATT_BLOCK), :]
    sq = sin_ref[pl.ds(qstart, ATT_BLOCK), :]
    q = _rope(q_ref[...], jnp.concatenate([cq] * 4, axis=1), jnp.concatenate([sq] * 4, axis=1))
    q = q * np.float32(HEAD_DIM ** -0.5)
    k_loc = jnp.concatenate([kp_ref[...], kc_ref[...], kn_ref[...]], axis=0)
    v_loc = jnp.concatenate([vp_ref[...], vc_ref[...], vn_ref[...]], axis=0)
    k_loc = _rope(k_loc, cos_ref[pl.ds(kstart, 3 * ATT_BLOCK), :], sin_ref[pl.ds(kstart, 3 * ATT_BLOCK), :])
    r = lax.broadcasted_iota(jnp.int32, (Q_PER_KV * ATT_BLOCK, 3 * ATT_BLOCK), 0) & (ATT_BLOCK - 1)
    c = lax.broadcasted_iota(jnp.int32, (Q_PER_KV * ATT_BLOCK, 3 * ATT_BLOCK), 1)
    rel = c - ATT_BLOCK - r
    c_lo = jnp.where(j > 0, 0, ATT_BLOCK)
    c_hi = jnp.where(j < nq - 1, 3 * ATT_BLOCK, 2 * ATT_BLOCK)
    ok = (jnp.abs(rel) <= WINDOW) & (c >= c_lo) & (c < c_hi)
    _attend(q, [(ck_ref[0], cv_ref[0], None), (k_loc, v_loc, ok)], sink_ref, o_ref)


def _rope_tables():
    pos = np.arange(-ATT_BLOCK, DEC_SEQ + ATT_BLOCK)
    nf = HEAD_DIM // 4
    inv = (1.0 / (ROPE_BASE ** (np.arange(nf, dtype=np.float32) / nf))).astype(np.float32)
    rows = (pos // GRID_W).astype(np.float32)
    cols = (pos % GRID_W).astype(np.float32)
    ar = rows[:, None] * inv[None, :]
    ac = cols[:, None] * inv[None, :]
    cos64 = np.concatenate([np.cos(ar), np.cos(ar), np.cos(ac), np.cos(ac)], axis=1)
    sin64 = np.concatenate([-np.sin(ar), np.sin(ar), -np.sin(ac), np.sin(ac)], axis=1)
    cos = np.concatenate([cos64, cos64], axis=1).astype(np.float32)
    sin = np.concatenate([sin64, sin64], axis=1).astype(np.float32)
    return jnp.asarray(cos), jnp.asarray(sin)


def _attention(aq, ak, av, ck, cv, sink):
    smem = pl.BlockSpec(memory_space=pltpu.SMEM)
    nqp = SEQ // ATT_BLOCK
    o_prompt = pl.pallas_call(
        _attn_prompt_kernel,
        grid=(BATCH, nqp),
        in_specs=[
            smem,
            pl.BlockSpec((ATT_BLOCK, ATT_W), lambda b, j: (b * nqp + j, 0)),
            pl.BlockSpec((SEQ, KV_W), lambda b, j: (b, 0)),
            pl.BlockSpec((SEQ, KV_W), lambda b, j: (b, 0)),
        ],
        out_specs=pl.BlockSpec((ATT_BLOCK, ATT_W), lambda b, j: (b * nqp + j, 0)),
        out_shape=jax.ShapeDtypeStruct((N_PROMPT, ATT_W), F32),
        compiler_params=_cparams("parallel", "parallel"),
        name="attn_prompt",
    )(sink, aq, ak, av)

    nq = DEC_SEQ // ATT_BLOCK
    base = N_PROMPT // ATT_BLOCK
    cos, sin = _rope_tables()
    kv_blk = lambda d: pl.BlockSpec(
        (ATT_BLOCK, KV_W), lambda b, j: (base + b * nq + jnp.clip(j + d, 0, nq - 1), 0))
    tab = pl.BlockSpec((DEC_SEQ + 2 * ATT_BLOCK, 128), lambda b, j: (0, 0))
    o_latent = pl.pallas_call(
        _attn_latent_kernel,
        grid=(DEC_BATCH, nq),
        in_specs=[
            smem,
            pl.BlockSpec((ATT_BLOCK, ATT_W), lambda b, j: (base + b * nq + j, 0)),
            kv_blk(-1), kv_blk(0), kv_blk(1), kv_blk(-1), kv_blk(0), kv_blk(1),
            pl.BlockSpec((1, PAST_LEN, KV_W), lambda b, j: (b, 0, 0)),
            pl.BlockSpec((1, PAST_LEN, KV_W), lambda b, j: (b, 0, 0)),
            tab, tab,
        ],
        out_specs=pl.BlockSpec((ATT_BLOCK, ATT_W), lambda b, j: (b * nq + j, 0)),
        out_shape=jax.ShapeDtypeStruct((N_SAMPLE, ATT_W), F32),
        compiler_params=_cparams("parallel", "parallel"),
        name="attn_latent",
    )(sink, aq, ak, ak, ak, av, av, av, ck, cv, cos, sin)
    return jnp.concatenate([o_prompt, o_latent], axis=0)


def _even_out_kernel(x_ref, mod_ref, of_ref, ob_ref, zg_ref, oatt_ref, gn_ref, w_ref, o_ref):
    o = of_ref[...] + ob_ref[...]
    parts = []
    for h in range(HG_HEADS):
        oh = o[:, h * HG_DV:(h + 1) * HG_DV]
        parts.append(oh * lax.rsqrt(jnp.mean(oh * oh, axis=-1, keepdims=True) + EPS))
    o_hg = jnp.concatenate(parts, axis=1) * gn_ref[...] * _silu(zg_ref[...])
    y = _dot(o_hg.astype(BF16), w_ref[0:HG_W, :]) + _dot(oatt_ref[...].astype(BF16), w_ref[HG_W:, :])
    o_ref[...] = x_ref[...] + mod_ref[0][2:3, :] * y


def _even_out(x, modt, o_f, o_b, zg, oatt, gn_g, w_bf16):
    tile = lambda w: pl.BlockSpec((TB, w), lambda i: (i, 0))
    return pl.pallas_call(
        _even_out_kernel,
        grid=(N_TILES,),
        in_specs=[
            tile(D_MODEL),
            pl.BlockSpec((1, 6, D_MODEL), lambda i: (i, 0, 0)),
            tile(HG_W), tile(HG_W), tile(HG_W), tile(ATT_W),
            pl.BlockSpec((1, HG_W), lambda i: (0, 0)),
            pl.BlockSpec((HG_W + ATT_W, D_MODEL), lambda i: (0, 0)),
        ],
        out_specs=tile(D_MODEL),
        out_shape=jax.ShapeDtypeStruct((N_TOK, D_MODEL), F32),
        compiler_params=_cparams("parallel"),
        name="even_out",
    )(x, modt, o_f, o_b, zg, oatt, gn_g.reshape(1, HG_W), w_bf16)


def _conv_kernel(xc_ref, xp_ref, xn_ref, w_ref, b_ref, o_ref):
    i = pl.program_id(0)
    in_sample = i >= PROMPT_TILES
    pos = (i - PROMPT_TILES) % TILES_PER_SAMPLE
    has_prev = in_sample & (pos != 0)
    has_next = in_sample & (pos != TILES_PER_SAMPLE - 1)
    prev = jnp.where(has_prev, xp_ref[...], 0.0)
    nxt = jnp.where(has_next, xn_ref[...], 0.0)
    ext = jnp.concatenate([prev, xc_ref[...], nxt], axis=0)
    n = TB + 16
    w = w_ref[...]
    acc = ext[8:8 + TB] * w[2:3, :] + b_ref[...]
    acc = acc + pltpu.roll(ext, 2, 0)[8:8 + TB] * w[0:1, :]
    acc = acc + pltpu.roll(ext, 1, 0)[8:8 + TB] * w[1:2, :]
    acc = acc + pltpu.roll(ext, n - 1, 0)[8:8 + TB] * w[3:4, :]
    o_ref[...] = acc


def _dwconv(xr, cw, cb):
    r8 = TB // 8
    last8 = N_TOK // 8 - 1
    return pl.pallas_call(
        _conv_kernel,
        grid=(N_TILES,),
        in_specs=[
            pl.BlockSpec((TB, D_RNN), lambda i: (i, 0)),
            pl.BlockSpec((8, D_RNN), lambda i: (jnp.maximum(i * r8 - 1, 0), 0)),
            pl.BlockSpec((8, D_RNN), lambda i: (jnp.minimum((i + 1) * r8, last8), 0)),
            pl.BlockSpec((4, D_RNN), lambda i: (0, 0)),
            pl.BlockSpec((1, D_RNN), lambda i: (0, 0)),
        ],
        out_specs=pl.BlockSpec((TB, D_RNN), lambda i: (i, 0)),
        out_shape=jax.ShapeDtypeStruct((N_TOK, D_RNN), F32),
        compiler_params=_cparams("parallel"),
        name="dwconv",
    )(xr, xr, xr, cw, cb.reshape(1, D_RNN))


def _log1p(y):
    w = 1.0 + y
    return jnp.where(w == 1.0, y, jnp.log(w) * (y / (w - 1.0)))


def _expm1(x):
    e = jnp.exp(x)
    return jnp.where(e == 1.0, x, (e - 1.0) * (x / jnp.log(e)))


def _lru_direction(xc, wa_ref, ba, wx_ref, bx, lam, h_ref, fwd):
    n = LRU_CHUNK
    xb = xc.astype(BF16)
    ra, ri = [], []
    for blk in range(RG_BLOCKS):
        sl = slice(blk * RG_BW, (blk + 1) * RG_BW)
        ra.append(_dot(xb[:, sl], wa_ref[blk]))
        ri.append(_dot(xb[:, sl], wx_ref[blk]))
    r = _sigmoid(jnp.concatenate(ra, axis=1) + ba)
    gi = _sigmoid(jnp.concatenate(ri, axis=1) + bx)
    neg = -lam
    softplus = jnp.maximum(neg, 0.0) + _log1p(jnp.exp(-jnp.abs(neg)))
    log_a = (-RG_C * softplus) * r
    a = jnp.exp(log_a)
    u = jnp.sqrt(-_expm1(2.0 * log_a)) * (gi * xc)
    row = lax.broadcasted_iota(jnp.int32, (n, D_RNN), 0)
    d = 1
    while d < n:
        if fwd:
            a_s = pltpu.roll(a, d, 0)
            u_s = pltpu.roll(u, d, 0)
            live = row >= d
        else:
            a_s = pltpu.roll(a, n - d, 0)
            u_s = pltpu.roll(u, n - d, 0)
            live = row < n - d
        u = jnp.where(live, a * u_s + u, u)
        a = jnp.where(live, a * a_s, a)
        d *= 2
    h = u + a * h_ref[...]
    h_ref[...] = h[n - 1:n, :] if fwd else h[0:1, :]
    return h


def _lru_kernel(tbl_ref, xf_ref, xb_ref, wa_ref, ba_ref, wx_ref, bx_ref, lam_ref, h0f_ref, h0b_ref,
                hf_ref, hb_ref, hf_s, hb_s):
    s = pl.program_id(0)

    @pl.when(tbl_ref[4, s] == 1)
    def _():
        hf_s[...] = h0f_ref[0]
        hb_s[...] = h0b_ref[0]

    hf_ref[...] = _lru_direction(xf_ref[...], wa_ref.at[0], ba_ref[0:1, :], wx_ref.at[0], bx_ref[0:1, :],
                                 lam_ref[0:1, :], hf_s, True)
    hb_ref[...] = _lru_direction(xb_ref[...], wa_ref.at[1], ba_ref[1:2, :], wx_ref.at[1], bx_ref[1:2, :],
                                 lam_ref[1:2, :], hb_s, False)


def _lru_scan(xc, w_a_bf16, b_a, w_x_bf16, b_x, lam, h0f_all, h0b_all):
    tbl = _scan_table(LRU_CHUNK)
    n_steps = tbl.shape[1]
    blk = lambda r: pl.BlockSpec((LRU_CHUNK, D_RNN), lambda s, t: (t[r, s], 0))
    wspec = pl.BlockSpec((2, RG_BLOCKS, RG_BW, RG_BW), lambda s, t: (0, 0, 0, 0))
    vec2 = pl.BlockSpec((2, D_RNN), lambda s, t: (0, 0))
    h0 = pl.BlockSpec((1, 1, D_RNN), lambda s, t: (t[2, s], 0, 0))
    grid_spec = pltpu.PrefetchScalarGridSpec(
        num_scalar_prefetch=1,
        grid=(n_steps,),
        in_specs=[blk(0), blk(1), wspec, vec2, wspec, vec2, vec2, h0, h0],
        out_specs=[blk(0), blk(1)],
        scratch_shapes=[pltpu.VMEM((1, D_RNN), F32), pltpu.VMEM((1, D_RNN), F32)],
    )
    return pl.pallas_call(
        _lru_kernel,
        grid_spec=grid_spec,
        out_shape=[jax.ShapeDtypeStruct((N_TOK, D_RNN), F32), jax.ShapeDtypeStruct((N_TOK, D_RNN), F32)],
        compiler_params=_cparams("arbitrary"),
        name="lru_scan",
    )(jnp.asarray(tbl), xc, xc, w_a_bf16, b_a, w_x_bf16, b_x, lam, h0f_all, h0b_all)


def _odd_out_kernel(x_ref, mod_ref, hf_ref, hb_ref, gate_ref, w_ref, o_ref):
    y = (hf_ref[...] + hb_ref[...]) * _gelu_tanh(gate_ref[...])
    o_ref[...] = x_ref[...] + mod_ref[0][2:3, :] * _dot(y.astype(BF16), w_ref[...])


def _odd_out(x, modt, hf, hb, gate, w_bf16):
    tile = pl.BlockSpec((TB, D_MODEL), lambda i: (i, 0))
    return pl.pallas_call(
        _odd_out_kernel,
        grid=(N_TILES,),
        in_specs=[tile, pl.BlockSpec((1, 6, D_MODEL), lambda i: (i, 0, 0)), tile, tile, tile,
                  pl.BlockSpec((D_RNN, D_MODEL), lambda i: (0, 0))],
        out_specs=tile,
        out_shape=jax.ShapeDtypeStruct((N_TOK, D_MODEL), F32),
        compiler_params=_cparams("parallel"),
        name="odd_out",
    )(x, modt, hf, hb, gate, w_bf16)


def _extract_top(s, n_rows, emit):
    rows = lax.broadcasted_iota(jnp.int32, (n_rows, TB), 0)

    def body(k, s):
        m = s.max(axis=0, keepdims=True)
        idx = jnp.where(s == m, rows, n_rows).min(axis=0, keepdims=True)
        emit(k, m, idx)
        return jnp.where(rows == idx, -jnp.inf, s)

    lax.fori_loop(0, PEER_TOPK, body, s)


def _router_kernel(x_ref, mod_ref, g_ref, wq_ref, keys_ref, h_ref, a_ref, b_ref, gate_ref,
                   h_s, v1_s, i1_s, v2_s, i2_s, sc_s, a_s, b_s):
    hd = pl.program_id(1)

    @pl.when(hd == 0)
    def _():
        h = _rms_mod(x_ref[...], g_ref[...], mod_ref[0], 3, 4).astype(BF16)
        h_s[...] = h
        h_ref[...] = h

    q = _dot(h_s[...], wq_ref[...])
    for half, (v_s, i_s) in enumerate(((v1_s, i1_s), (v2_s, i2_s))):
        qh = q[:, half * 128:(half + 1) * 128].astype(BF16)
        st = _dot_nt(keys_ref[half, 0], qh)

        def emit(k, m, idx, v_s=v_s, i_s=i_s):
            v_s[pl.ds(k, 1), :] = m
            i_s[pl.ds(k, 1), :] = idx

        _extract_top(st, PEER_NK, emit)

    v1 = v1_s[...]
    v2 = v2_s[...]
    cand = jnp.concatenate([v1[p:p + 1, :] + v2 for p in range(PEER_TOPK)], axis=0)
    sub = lax.broadcasted_iota(jnp.int32, (PEER_TOPK, TB), 0)

    def emit2(k, m, pos):
        p = pos >> 4
        r = pos & (PEER_TOPK - 1)
        sc_s[pl.ds(k, 1), :] = m
        a_s[pl.ds(k, 1), :] = jnp.where(sub == p, i1_s[...], 0).sum(axis=0, keepdims=True)
        b_s[pl.ds(k, 1), :] = jnp.where(sub == r, i2_s[...], 0).sum(axis=0, keepdims=True)

    _extract_top(cand, PEER_TOPK * PEER_TOPK, emit2)
    sc = sc_s[...]
    e = jnp.exp(sc - sc[0:1, :])
    gate_ref[0] = e / e.sum(axis=0, keepdims=True)
    a_ref[0] = a_s[...].astype(F32)
    b_ref[0] = b_s[...].astype(F32)


def _router(x, modt, g, wq_bf16, keys_bf16):
    kt = pl.BlockSpec((1, PEER_TOPK, TB), lambda i, h: (i, h, 0))
    kshape = jax.ShapeDtypeStruct((N_TILES, PEER_HEADS * PEER_TOPK, TB), F32)
    f32s = lambda: pltpu.VMEM((PEER_TOPK, TB), F32)
    i32s = lambda: pltpu.VMEM((PEER_TOPK, TB), jnp.int32)
    return pl.pallas_call(
        _router_kernel,
        grid=(N_TILES, PEER_HEADS),
        in_specs=[
            pl.BlockSpec((TB, D_MODEL), lambda i, h: (i, 0)),
            pl.BlockSpec((1, 6, D_MODEL), lambda i, h: (i, 0, 0)),
            pl.BlockSpec((1, D_MODEL), lambda i, h: (0, 0)),
            pl.BlockSpec((D_MODEL, PEER_DK), lambda i, h: (0, h)),
            pl.BlockSpec((2, 1, PEER_NK, PEER_DK // 2), lambda i, h: (0, h, 0, 0)),
        ],
        out_specs=[pl.BlockSpec((TB, D_MODEL), lambda i, h: (i, 0)), kt, kt, kt],
        out_shape=[jax.ShapeDtypeStruct((N_TOK, D_MODEL), BF16), kshape, kshape, kshape],
        scratch_shapes=[pltpu.VMEM((TB, D_MODEL), BF16), f32s(), i32s(), f32s(), i32s(), f32s(), i32s(), i32s()],
        compiler_params=_cparams("parallel", "arbitrary"),
        name="peer_router",
    )(x, modt, g.reshape(1, D_MODEL), wq_bf16, keys_bf16)


def _expert_kernel(h_ref, x_ref, mod_ref, at_ref, bt_ref, gt_ref, u_ref, v_ref, fg_ref, o_ref, *rest, final):
    if final:
        y_ref, g_s, acc_s, a_s, b_s, w_s = rest
    else:
        g_s, acc_s, a_s, b_s, w_s = rest
    grp = pl.program_id(1)

    @pl.when(grp == 0)
    def _():
        acc_s[...] = jnp.zeros_like(acc_s)
        a_s[...] = at_ref[0].T
        b_s[...] = bt_ref[0].T
        w_s[...] = gt_ref[0].T
        sub = lax.broadcasted_iota(jnp.int32, (PEER_NK, PEER_NK), 0).astype(F32)

        def body(t, carry):
            a_row = a_s[pl.ds(t, 1), :]
            b_row = b_s[pl.ds(t, 1), :]
            w_row = w_s[pl.ds(t, 1), :]
            m_t = jnp.where(a_row == sub, 1.0, 0.0).astype(BF16)
            r_t = jnp.where(b_row == sub, w_row, 0.0).astype(BF16)
            g_s[pl.ds(t, PEER_NK, stride=G_PITCH), :] = _dot_nt(m_t, r_t)
            return carry

        lax.fori_loop(0, TB, body, 0, unroll=4)

    hmat = _dot_nt(h_ref[...], u_ref[...])
    parts = []
    for j in range(CHUNKS_PER_GROUP):
        start = pl.multiple_of((grp * CHUNKS_PER_GROUP + j) * G_PITCH, 8)
        gates = g_s[pl.ds(start, TB), :]
        parts.append((_gelu_tanh(hmat[:, j * PEER_NK:(j + 1) * PEER_NK]) * gates).astype(BF16))
    acc_s[...] += _dot(jnp.concatenate(parts, axis=1), v_ref[...])

    @pl.when(grp == PEER_NGROUPS - 1)
    def _():
        xo = x_ref[...] + mod_ref[0][5:6, :] * acc_s[...]
        o_ref[...] = xo
        if final:
            y_ref[...] = xo * lax.rsqrt(jnp.mean(xo * xo, axis=-1, keepdims=True) + EPS) * fg_ref[...]


def _experts(h_bf16, x, modt, a_t, b_t, g_t, u_bf16, v_bf16, final_g, final):
    tile = pl.BlockSpec((TB, D_MODEL), lambda i, g: (i, 0))
    kt = pl.BlockSpec((1, PEER_HEADS * PEER_TOPK, TB), lambda i, g: (i, 0, 0))
    wblk = pl.BlockSpec((PEER_GROUP, D_MODEL), lambda i, g: (g, 0))
    xs = jax.ShapeDtypeStruct((N_TOK, D_MODEL), F32)
    sel = lambda: pltpu.VMEM((TB, PEER_HEADS * PEER_TOPK), F32)
    return pl.pallas_call(
        functools.partial(_expert_kernel, final=final),
        grid=(N_TILES, PEER_NGROUPS),
        in_specs=[tile, tile, pl.BlockSpec((1, 6, D_MODEL), lambda i, g: (i, 0, 0)), kt, kt, kt, wblk, wblk,
                  pl.BlockSpec((1, D_MODEL), lambda i, g: (0, 0))],
        out_specs=[tile, tile] if final else tile,
        out_shape=[xs, xs] if final else xs,
        scratch_shapes=[pltpu.VMEM((PEER_NK * G_PITCH, PEER_NK), F32), pltpu.VMEM((TB, D_MODEL), F32),
                        sel(), sel(), sel()],
        compiler_params=_cparams("parallel", "arbitrary"),
        name="peer_experts_final" if final else "peer_experts",
    )(h_bf16, x, modt, a_t, b_t, g_t, u_bf16, v_bf16, final_g.reshape(1, D_MODEL))


def _peer(x, modt, norm_g, w_q, keys, u, v, final_g, final):
    h, a_t, b_t, g_t = _router(x, modt, norm_g, w_q.astype(BF16), keys.astype(BF16))
    return _experts(h, x, modt, a_t, b_t, g_t, u.astype(BF16), v.astype(BF16), final_g, final)


def kernel(x_prompt, x_sample, cache_k, cache_v, state_hgrn_fwd, state_hgrn_bwd, state_lru_fwd, state_lru_bwd, c, c_ctx, norm1_g, norm2_g, w_mod, b_mod, w_in_even, w_out_even, hgrn_lb_fwd, hgrn_lb_bwd, hgrn_gnorm_g, attn_sink, w_in_odd, w_out_odd, conv_w, conv_b, rg_w_a, rg_b_a, rg_w_x, rg_b_x, rg_lambda, peer_w_q, peer_keys, peer_u, peer_v, final_g):
    x = jnp.concatenate([x_prompt.reshape(N_PROMPT, D_MODEL), x_sample.reshape(N_SAMPLE, D_MODEL)], axis=0)
    cond = jnp.concatenate([c_ctx[None, :], c, jnp.zeros((COND_PAD - N_COND, D_MODEL), F32)], axis=0)
    mod = _modulation(cond, w_mod, b_mod)
    tile_row = np.asarray([0] * PROMPT_TILES + [1 + i // TILES_PER_SAMPLE for i in range(N_TILES - PROMPT_TILES)])
    modt = [mod[l][tile_row].reshape(N_TILES, 6, D_MODEL) for l in range(DEPTH)]

    lbf = jnp.cumsum(jax.nn.softmax(hgrn_lb_fwd.astype(F32), axis=0), axis=0)[0].reshape(1, HG_W)
    lbb = jnp.cumsum(jax.nn.softmax(hgrn_lb_bwd.astype(F32), axis=0), axis=0)[0].reshape(1, HG_W)
    widths = (HG_W, HG_W, HG_W, HG_W, HG_W, ATT_W, KV_W, KV_W)
    zq, zf, zb, zi, zg, aq, ak, av = _in_proj(x, modt[0], norm1_g[0], w_in_even[0].astype(BF16), widths, "even_in")
    zero_state = jnp.zeros((1, HG_HEADS, HG_DV, HG_DK), F32)
    s0f = jnp.concatenate([zero_state, jnp.swapaxes(state_hgrn_fwd[:, 0], -1, -2)], axis=0)
    s0b = jnp.concatenate([zero_state, jnp.swapaxes(state_hgrn_bwd[:, 0], -1, -2)], axis=0)
    o_f, o_b, sf_t, sb_t = _hgrn_scan(zq, zf, zb, zi, lbf, lbb, s0f, s0b)
    oatt = _attention(aq, ak, av, cache_k[:, 0].reshape(DEC_BATCH, PAST_LEN, KV_W),
                      cache_v[:, 0].reshape(DEC_BATCH, PAST_LEN, KV_W), attn_sink[0])
    x = _even_out(x, modt[0], o_f, o_b, zg, oatt, hgrn_gnorm_g[0], w_out_even[0].astype(BF16))
    x = _peer(x, modt[0], norm2_g[0], peer_w_q[0], peer_keys[0], peer_u[0], peer_v[0], final_g, False)

    gate, xr = _in_proj(x, modt[1], norm1_g[1], w_in_odd[0].astype(BF16), (D_RNN, D_RNN), "odd_in")
    xc = _dwconv(xr, conv_w[0], conv_b[0])
    zero_h = jnp.zeros((1, 1, D_RNN), F32)
    h0f = jnp.concatenate([zero_h, state_lru_fwd[:, 0][:, None, :]], axis=0)
    h0b = jnp.concatenate([zero_h, state_lru_bwd[:, 0][:, None, :]], axis=0)
    hf, hb = _lru_scan(xc, rg_w_a[0].astype(BF16), rg_b_a[0], rg_w_x[0].astype(BF16), rg_b_x[0],
                       rg_lambda[0], h0f, h0b)
    x = _odd_out(x, modt[1], hf, hb, gate, w_out_odd[0].astype(BF16))
    _, y = _peer(x, modt[1], norm2_g[1], peer_w_q[1], peer_keys[1], peer_u[1], peer_v[1], final_g, True)

    y_prompt = y[:N_PROMPT].reshape(BATCH, SEQ, D_MODEL)
    y_sample = y[N_PROMPT:].reshape(DEC_BATCH, DEC_SEQ, D_MODEL)
    new_k = ak[:N_PROMPT].reshape(BATCH, 1, SEQ, KV_HEADS, HEAD_DIM)
    new_v = av[:N_PROMPT].reshape(BATCH, 1, SEQ, KV_HEADS, HEAD_DIM)
    new_hf = jnp.swapaxes(sf_t[:BATCH], -1, -2)[:, None]
    new_hb = jnp.swapaxes(sb_t[:BATCH], -1, -2)[:, None]
    new_lf = hf[:N_PROMPT].reshape(BATCH, SEQ, D_RNN)[:, -1][:, None, :]
    new_lb = hb[:N_PROMPT].reshape(BATCH, SEQ, D_RNN)[:, 0][:, None, :]
    return (y_prompt, y_sample, new_k, new_v, new_hf, new_hb, new_lf, new_lb)
```

```python
import functools

import numpy as np
import jax
import jax.numpy as jnp
from jax import lax
from jax.experimental import pallas as pl
from jax.experimental.pallas import tpu as pltpu

F32 = jnp.float32
BF16 = jnp.bfloat16

D_MODEL = 1024
BATCH = 16
SEQ = 256
DEPTH = 2
DEC_BATCH = 8
DEC_SEQ = 2048
PAST_LEN = 512
GRID_W = 64
EPS = 1e-6
HG_DK = 128
HG_DV = 128
HG_HEADS = 4
HG_CHUNK = 64
HG_W = HG_HEADS * HG_DK
HEAD_DIM = 64
N_HEADS = 8
KV_HEADS = 2
Q_PER_KV = 4
WINDOW = 128
ATT_BLOCK = 128
ROPE_BASE = 10000.0
ATT_W = N_HEADS * HEAD_DIM
KV_W = KV_HEADS * HEAD_DIM
D_RNN = 1024
RG_BLOCKS = 4
RG_BW = 256
RG_C = 8.0
PEER_HEADS = 8
PEER_NK = 128
PEER_DK = 256
PEER_TOPK = 16

N_PROMPT = BATCH * SEQ
N_SAMPLE = DEC_BATCH * DEC_SEQ
N_TOK = N_PROMPT + N_SAMPLE
TB = 256
N_TILES = N_TOK // TB
PROMPT_TILES = N_PROMPT // TB
TILES_PER_SAMPLE = DEC_SEQ // TB
N_COND = 1 + DEC_BATCH
COND_PAD = 16

TE = 512
PEER_GROUP = 1024
PEER_NGROUPS = PEER_NK * PEER_NK // PEER_GROUP
CHUNKS_PER_GROUP = PEER_GROUP // PEER_NK
EXPERT_SUB = 1024
G_PITCH = TE // 2 + 8
G_UNROLL = 16

LRU_CHUNK = 256

VMEM_LIMIT = 56 * 1024 * 1024


def _cparams(*sem):
    return pltpu.CompilerParams(dimension_semantics=sem, vmem_limit_bytes=VMEM_LIMIT)


def _dot(a, b):
    return jnp.dot(a, b, preferred_element_type=F32)


def _dot_nt(a, b):
    return lax.dot_general(a, b, (((1,), (1,)), ((), ())), preferred_element_type=F32)


def _dot_tn(a, b):
    return lax.dot_general(a, b, (((0,), (0,)), ((), ())), preferred_element_type=F32)


def _sigmoid(x):
    return 1.0 / (1.0 + jnp.exp(-x))


def _silu(x):
    return x * _sigmoid(x)


def _gelu_tanh(x):
    c = np.float32(np.sqrt(2.0 / np.pi))
    return x * (0.5 * (1.0 + jnp.tanh(c * (x + 0.044715 * (x * x * x)))))


def _rms_mod(x, g, mod, sh, sc):
    y = x * lax.rsqrt(jnp.mean(x * x, axis=-1, keepdims=True) + EPS) * g
    return y * (1.0 + mod[sc:sc + 1, :]) + mod[sh:sh + 1, :]


def _mod_kernel(c_ref, w_ref, b_ref, o_ref):
    c = c_ref[...]
    s = _silu(c).astype(BF16)
    o_ref[0] = _dot(s, w_ref[0].astype(BF16)) + b_ref[0]


def _modulation(cond, w_mod, b_mod):
    nb = 6 * D_MODEL // 1024
    return pl.pallas_call(
        _mod_kernel,
        grid=(DEPTH, nb),
        in_specs=[
            pl.BlockSpec((COND_PAD, D_MODEL), lambda l, n: (0, 0)),
            pl.BlockSpec((1, D_MODEL, 1024), lambda l, n: (l, 0, n)),
            pl.BlockSpec((1, 1, 1024), lambda l, n: (l, 0, n)),
        ],
        out_specs=pl.BlockSpec((1, COND_PAD, 1024), lambda l, n: (l, 0, n)),
        out_shape=jax.ShapeDtypeStruct((DEPTH, COND_PAD, 6 * D_MODEL), F32),
        compiler_params=_cparams("parallel", "parallel"),
        name="modulation",
    )(cond, w_mod, b_mod.reshape(DEPTH, 1, 6 * D_MODEL))


def _in_proj_kernel(x_ref, mod_ref, g_ref, w_ref, *o_refs, widths):
    h = _rms_mod(x_ref[...], g_ref[...], mod_ref[0], 0, 1)
    z = _dot(h.astype(BF16), w_ref[...])
    off = 0
    for o_ref, wd in zip(o_refs, widths):
        o_ref[...] = z[:, off:off + wd]
        off += wd


def _in_proj(x, modt, g, w_bf16, widths, name):
    n_out = sum(widths)
    return pl.pallas_call(
        functools.partial(_in_proj_kernel, widths=widths),
        grid=(N_TILES,),
        in_specs=[
            pl.BlockSpec((TB, D_MODEL), lambda i: (i, 0)),
            pl.BlockSpec((1, 6, D_MODEL), lambda i: (i, 0, 0)),
            pl.BlockSpec((1, D_MODEL), lambda i: (0, 0)),
            pl.BlockSpec((D_MODEL, n_out), lambda i: (0, 0)),
        ],
        out_specs=[pl.BlockSpec((TB, wd), lambda i: (i, 0)) for wd in widths],
        out_shape=[jax.ShapeDtypeStruct((N_TOK, wd), F32) for wd in widths],
        compiler_params=_cparams("parallel"),
        name=name,
    )(x, modt, g.reshape(1, D_MODEL), w_bf16)


def _hgrn_direction(zq, zgate, zi, lb, st_ref, fwd):
    n = HG_CHUNK
    row = lax.broadcasted_iota(jnp.int32, (n, n), 0)
    col = lax.broadcasted_iota(jnp.int32, (n, n), 1)
    keep = (col <= row) if fwd else (col >= row)
    tri = jnp.where(keep, 1.0, 0.0).astype(BF16)
    q = _silu(zq)
    f = lb + (1.0 - lb) * _sigmoid(zgate)
    g = jnp.log(f)
    k = (1.0 - lb) * _sigmoid(-zgate)
    g_hi = g.astype(BF16)
    g_lo = (g - g_hi.astype(F32)).astype(BF16)
    b = _dot(tri, g_hi) + _dot(tri, g_lo)
    mid = n // 2 - 1 if fwd else n // 2
    last = n - 1 if fwd else 0
    bm = b[mid:mid + 1, :]
    bl = b[last:last + 1, :]
    qi = (q * jnp.exp(b - bm)).astype(BF16)
    ki = (k * jnp.exp(bm - b)).astype(BF16)
    qs = (q * jnp.exp(b)).astype(BF16)
    ks = (k * jnp.exp(bl - b)).astype(BF16)
    vb = zi.astype(BF16)
    decay = jnp.exp(bl)
    outs = []
    for h in range(HG_HEADS):
        sl = slice(h * HG_DK, (h + 1) * HG_DK)
        a = jnp.where(keep, _dot_nt(qi[:, sl], ki[:, sl]), 0.0)
        st = st_ref[h]
        o = _dot(a.astype(BF16), vb[:, sl]) + _dot_nt(qs[:, sl], st.astype(BF16))
        st_ref[h] = decay[:, sl] * st + _dot_tn(vb[:, sl], ks[:, sl])
        outs.append(o)
    return jnp.concatenate(outs, axis=1)


def _hgrn_kernel(tbl_ref, zqf_ref, zff_ref, zif_ref, zqb_ref, zbb_ref, zib_ref, lbf_ref, lbb_ref,
                 s0f_ref, s0b_ref, of_ref, ob_ref, sf_ref, sb_ref, stf, stb):
    s = pl.program_id(0)

    @pl.when(tbl_ref[4, s] == 1)
    def _():
        stf[...] = s0f_ref[0]
        stb[...] = s0b_ref[0]

    of_ref[...] = _hgrn_direction(zqf_ref[...], zff_ref[...], zif_ref[...], lbf_ref[...], stf, True)
    ob_ref[...] = _hgrn_direction(zqb_ref[...], zbb_ref[...], zib_ref[...], lbb_ref[...], stb, False)

    @pl.when(tbl_ref[5, s] == 1)
    def _():
        sf_ref[0] = stf[...]
        sb_ref[0] = stb[...]


def _scan_table(chunk):
    rows = []
    blk = 0
    for (nseq, length, has_init) in ((BATCH, SEQ, False), (DEC_BATCH, DEC_SEQ, True)):
        nc = length // chunk
        for b in range(nseq):
            sid = b if not has_init else BATCH + b
            for i in range(nc):
                rows.append((blk + i, blk + nc - 1 - i, (1 + b) if has_init else 0, sid,
                             int(i == 0), int(i == nc - 1)))
            blk += nc
    return np.asarray(rows, np.int32).T.copy()


def _hgrn_scan(zq, zf, zb, zi, lbf, lbb, s0f_t, s0b_t):
    tbl = _scan_table(HG_CHUNK)
    n_steps = tbl.shape[1]
    n_seq = BATCH + DEC_BATCH
    blk = lambda r: pl.BlockSpec((HG_CHUNK, HG_W), lambda s, t: (t[r, s], 0))
    vec = pl.BlockSpec((1, HG_W), lambda s, t: (0, 0))
    st_in = pl.BlockSpec((1, HG_HEADS, HG_DV, HG_DK), lambda s, t: (t[2, s], 0, 0, 0))
    st_out = pl.BlockSpec((1, HG_HEADS, HG_DV, HG_DK), lambda s, t: (t[3, s], 0, 0, 0))
    grid_spec = pltpu.PrefetchScalarGridSpec(
        num_scalar_prefetch=1,
        grid=(n_steps,),
        in_specs=[blk(0), blk(0), blk(0), blk(1), blk(1), blk(1), vec, vec, st_in, st_in],
        out_specs=[blk(0), blk(1), st_out, st_out],
        scratch_shapes=[pltpu.VMEM((HG_HEADS, HG_DV, HG_DK), F32),
                        pltpu.VMEM((HG_HEADS, HG_DV, HG_DK), F32)],
    )
    st_shape = jax.ShapeDtypeStruct((n_seq, HG_HEADS, HG_DV, HG_DK), F32)
    return pl.pallas_call(
        _hgrn_kernel,
        grid_spec=grid_spec,
        out_shape=[jax.ShapeDtypeStruct((N_TOK, HG_W), F32), jax.ShapeDtypeStruct((N_TOK, HG_W), F32),
                   st_shape, st_shape],
        compiler_params=_cparams("arbitrary"),
        name="hgrn_scan",
    )(jnp.asarray(tbl), zq, zf, zi, zq, zb, zi, lbf, lbb, s0f_t, s0b_t)


def _rope(x, cos, sgn_sin):
    w = x.shape[1]
    lane = lax.broadcasted_iota(jnp.int32, x.shape, 1)
    partner = jnp.where((lane & 31) < 16, pltpu.roll(x, w - 16, 1), pltpu.roll(x, 16, 1))
    return x * cos + partner * sgn_sin


def _dup_kv_half(x, kv):
    lane = lax.broadcasted_iota(jnp.int32, x.shape, 1)
    own = (lane < HEAD_DIM) if kv == 0 else (lane >= HEAD_DIM)
    xs = jnp.where(own, x, 0.0)
    return xs + pltpu.roll(xs, HEAD_DIM, 1)


def _attend(q, key_parts, sink_ref, o_ref):
    lane = lax.broadcasted_iota(jnp.int32, (ATT_BLOCK, 128), 1)
    lo = lane < HEAD_DIM
    for kv in range(KV_HEADS):
        qs = []
        for gq in range(Q_PER_KV):
            hd = kv * Q_PER_KV + gq
            pair = q[:, (hd // 2) * 128:(hd // 2 + 1) * 128]
            qs.append(jnp.where(lo if hd % 2 == 0 else ~lo, pair, 0.0))
        qst = jnp.concatenate(qs, axis=0).astype(BF16)
        logits = []
        vals = []
        for (k, v, mask) in key_parts:
            kd = _dup_kv_half(k, kv).astype(BF16)
            vals.append(_dup_kv_half(v, kv).astype(BF16))
            s = _dot_nt(qst, kd)
            if mask is not None:
                s = jnp.where(mask, s, -1e30)
            logits.append(s)
        m = logits[0].max(axis=-1, keepdims=True)
        for s in logits[1:]:
            m = jnp.maximum(m, s.max(axis=-1, keepdims=True))
        sink_col = jnp.concatenate(
            [jnp.full((ATT_BLOCK, 1), sink_ref[kv * Q_PER_KV + gq], F32) for gq in range(Q_PER_KV)], axis=0)
        m = jnp.maximum(m, sink_col)
        den = jnp.exp(sink_col - m)
        acc = jnp.zeros((Q_PER_KV * ATT_BLOCK, 128), F32)
        for s, v in zip(logits, vals):
            p = jnp.exp(s - m)
            den = den + p.sum(axis=-1, keepdims=True)
            acc = acc + _dot(p.astype(BF16), v)
        out = acc / den
        for pr in range(Q_PER_KV // 2):
            r0 = out[(2 * pr) * ATT_BLOCK:(2 * pr + 1) * ATT_BLOCK]
            r1 = out[(2 * pr + 1) * ATT_BLOCK:(2 * pr + 2) * ATT_BLOCK]
            c0 = (kv * Q_PER_KV // 2 + pr) * 128
            o_ref[:, c0:c0 + 128] = jnp.where(lo, r0, r1)


def _attn_prompt_kernel(sink_ref, q_ref, k_ref, v_ref, o_ref):
    q = q_ref[...] * np.float32(HEAD_DIM ** -0.5)
    _attend(q, [(k_ref[...], v_ref[...], None)], sink_ref, o_ref)


def _attn_latent_kernel(sink_ref, q_ref, kp_ref, kc_ref, kn_ref, vp_ref, vc_ref, vn_ref,
                        ck_ref, cv_ref, cos_ref, sin_ref, o_ref):
    j = pl.program_id(1)
    nq = pl.num_programs(1)
    qstart = pl.multiple_of((j + 1) * ATT_BLOCK, ATT_BLOCK)
    kstart = pl.multiple_of(j * ATT_BLOCK, ATT_BLOCK)
    cq = cos_ref[pl.ds(qstart, ATT_BLOCK), :]
    sq = sin_ref[pl.ds(qstart, ATT_BLOCK), :]
    q = _rope(q_ref[...], jnp.concatenate([cq] * 4, axis=1), jnp.concatenate([sq] * 4, axis=1))
    q = q * np.float32(HEAD_DIM ** -0.5)
    k_loc = jnp.concatenate([kp_ref[...], kc_ref[...], kn_ref[...]], axis=0)
    v_loc = jnp.concatenate([vp_ref[...], vc_ref[...], vn_ref[...]], axis=0)
    k_loc = _rope(k_loc, cos_ref[pl.ds(kstart, 3 * ATT_BLOCK), :], sin_ref[pl.ds(kstart, 3 * ATT_BLOCK), :])
    r = lax.broadcasted_iota(jnp.int32, (Q_PER_KV * ATT_BLOCK, 3 * ATT_BLOCK), 0) & (ATT_BLOCK - 1)
    c = lax.broadcasted_iota(jnp.int32, (Q_PER_KV * ATT_BLOCK, 3 * ATT_BLOCK), 1)
    rel = c - ATT_BLOCK - r
    c_lo = jnp.where(j > 0, 0, ATT_BLOCK)
    c_hi = jnp.where(j < nq - 1, 3 * ATT_BLOCK, 2 * ATT_BLOCK)
    ok = (jnp.abs(rel) <= WINDOW) & (c >= c_lo) & (c < c_hi)
    _attend(q, [(ck_ref[0], cv_ref[0], None), (k_loc, v_loc, ok)], sink_ref, o_ref)


def _rope_tables():
    pos = np.arange(-ATT_BLOCK, DEC_SEQ + ATT_BLOCK)
    nf = HEAD_DIM // 4
    inv = (1.0 / (ROPE_BASE ** (np.arange(nf, dtype=np.float32) / nf))).astype(np.float32)
    rows = (pos // GRID_W).astype(np.float32)
    cols = (pos % GRID_W).astype(np.float32)
    ar = rows[:, None] * inv[None, :]
    ac = cols[:, None] * inv[None, :]
    cos64 = np.concatenate([np.cos(ar), np.cos(ar), np.cos(ac), np.cos(ac)], axis=1)
    sin64 = np.concatenate([-np.sin(ar), np.sin(ar), -np.sin(ac), np.sin(ac)], axis=1)
    cos = np.concatenate([cos64, cos64], axis=1).astype(np.float32)
    sin = np.concatenate([sin64, sin64], axis=1).astype(np.float32)
    return jnp.asarray(cos), jnp.asarray(sin)


def _attention(aq, ak, av, ck, cv, sink):
    smem = pl.BlockSpec(memory_space=pltpu.SMEM)
    nqp = SEQ // ATT_BLOCK
    o_prompt = pl.pallas_call(
        _attn_prompt_kernel,
        grid=(BATCH, nqp),
        in_specs=[
            smem,
            pl.BlockSpec((ATT_BLOCK, ATT_W), lambda b, j: (b * nqp + j, 0)),
            pl.BlockSpec((SEQ, KV_W), lambda b, j: (b, 0)),
            pl.BlockSpec((SEQ, KV_W), lambda b, j: (b, 0)),
        ],
        out_specs=pl.BlockSpec((ATT_BLOCK, ATT_W), lambda b, j: (b * nqp + j, 0)),
        out_shape=jax.ShapeDtypeStruct((N_PROMPT, ATT_W), F32),
        compiler_params=_cparams("parallel", "parallel"),
        name="attn_prompt",
    )(sink, aq, ak, av)

    nq = DEC_SEQ // ATT_BLOCK
    base = N_PROMPT // ATT_BLOCK
    cos, sin = _rope_tables()
    kv_blk = lambda d: pl.BlockSpec(
        (ATT_BLOCK, KV_W), lambda b, j: (base + b * nq + jnp.clip(j + d, 0, nq - 1), 0))
    tab = pl.BlockSpec((DEC_SEQ + 2 * ATT_BLOCK, 128), lambda b, j: (0, 0))
    o_latent = pl.pallas_call(
        _attn_latent_kernel,
        grid=(DEC_BATCH, nq),
        in_specs=[
            smem,
            pl.BlockSpec((ATT_BLOCK, ATT_W), lambda b, j: (base + b * nq + j, 0)),
            kv_blk(-1), kv_blk(0), kv_blk(1), kv_blk(-1), kv_blk(0), kv_blk(1),
            pl.BlockSpec((1, PAST_LEN, KV_W), lambda b, j: (b, 0, 0)),
            pl.BlockSpec((1, PAST_LEN, KV_W), lambda b, j: (b, 0, 0)),
            tab, tab,
        ],
        out_specs=pl.BlockSpec((ATT_BLOCK, ATT_W), lambda b, j: (b * nq + j, 0)),
        out_shape=jax.ShapeDtypeStruct((N_SAMPLE, ATT_W), F32),
        compiler_params=_cparams("parallel", "parallel"),
        name="attn_latent",
    )(sink, aq, ak, ak, ak, av, av, av, ck, cv, cos, sin)
    return jnp.concatenate([o_prompt, o_latent], axis=0)


def _even_out_kernel(x_ref, mod_ref, of_ref, ob_ref, zg_ref, oatt_ref, gn_ref, w_ref, o_ref):
    o = of_ref[...] + ob_ref[...]
    parts = []
    for h in range(HG_HEADS):
        oh = o[:, h * HG_DV:(h + 1) * HG_DV]
        parts.append(oh * lax.rsqrt(jnp.mean(oh * oh, axis=-1, keepdims=True) + EPS))
    o_hg = jnp.concatenate(parts, axis=1) * gn_ref[...] * _silu(zg_ref[...])
    y = _dot(o_hg.astype(BF16), w_ref[0:HG_W, :]) + _dot(oatt_ref[...].astype(BF16), w_ref[HG_W:, :])
    o_ref[...] = x_ref[...] + mod_ref[0][2:3, :] * y


def _even_out(x, modt, o_f, o_b, zg, oatt, gn_g, w_bf16):
    tile = lambda w: pl.BlockSpec((TB, w), lambda i: (i, 0))
    return pl.pallas_call(
        _even_out_kernel,
        grid=(N_TILES,),
        in_specs=[
            tile(D_MODEL),
            pl.BlockSpec((1, 6, D_MODEL), lambda i: (i, 0, 0)),
            tile(HG_W), tile(HG_W), tile(HG_W), tile(ATT_W),
            pl.BlockSpec((1, HG_W), lambda i: (0, 0)),
            pl.BlockSpec((HG_W + ATT_W, D_MODEL), lambda i: (0, 0)),
        ],
        out_specs=tile(D_MODEL),
        out_shape=jax.ShapeDtypeStruct((N_TOK, D_MODEL), F32),
        compiler_params=_cparams("parallel"),
        name="even_out",
    )(x, modt, o_f, o_b, zg, oatt, gn_g.reshape(1, HG_W), w_bf16)


def _conv_kernel(xc_ref, xp_ref, xn_ref, w_ref, b_ref, o_ref):
    i = pl.program_id(0)
    in_sample = i >= PROMPT_TILES
    pos = (i - PROMPT_TILES) % TILES_PER_SAMPLE
    has_prev = in_sample & (pos != 0)
    has_next = in_sample & (pos != TILES_PER_SAMPLE - 1)
    prev = jnp.where(has_prev, xp_ref[...], 0.0)
    nxt = jnp.where(has_next, xn_ref[...], 0.0)
    ext = jnp.concatenate([prev, xc_ref[...], nxt], axis=0)
    n = TB + 16
    w = w_ref[...]
    acc = ext[8:8 + TB] * w[2:3, :] + b_ref[...]
    acc = acc + pltpu.roll(ext, 2, 0)[8:8 + TB] * w[0:1, :]
    acc = acc + pltpu.roll(ext, 1, 0)[8:8 + TB] * w[1:2, :]
    acc = acc + pltpu.roll(ext, n - 1, 0)[8:8 + TB] * w[3:4, :]
    o_ref[...] = acc


def _dwconv(xr, cw, cb):
    r8 = TB // 8
    last8 = N_TOK // 8 - 1
    return pl.pallas_call(
        _conv_kernel,
        grid=(N_TILES,),
        in_specs=[
            pl.BlockSpec((TB, D_RNN), lambda i: (i, 0)),
            pl.BlockSpec((8, D_RNN), lambda i: (jnp.maximum(i * r8 - 1, 0), 0)),
            pl.BlockSpec((8, D_RNN), lambda i: (jnp.minimum((i + 1) * r8, last8), 0)),
            pl.BlockSpec((4, D_RNN), lambda i: (0, 0)),
            pl.BlockSpec((1, D_RNN), lambda i: (0, 0)),
        ],
        out_specs=pl.BlockSpec((TB, D_RNN), lambda i: (i, 0)),
        out_shape=jax.ShapeDtypeStruct((N_TOK, D_RNN), F32),
        compiler_params=_cparams("parallel"),
        name="dwconv",
    )(xr, xr, xr, cw, cb.reshape(1, D_RNN))


def _log1p(y):
    w = 1.0 + y
    return jnp.where(w == 1.0, y, jnp.log(w) * (y / (w - 1.0)))


def _expm1(x):
    e = jnp.exp(x)
    return jnp.where(e == 1.0, x, (e - 1.0) * (x / jnp.log(e)))


def _lru_direction(xc, wa_ref, ba, wx_ref, bx, lam, h_ref, fwd):
    n = LRU_CHUNK
    xb = xc.astype(BF16)
    ra, ri = [], []
    for blk in range(RG_BLOCKS):
        sl = slice(blk * RG_BW, (blk + 1) * RG_BW)
        ra.append(_dot(xb[:, sl], wa_ref[blk]))
        ri.append(_dot(xb[:, sl], wx_ref[blk]))
    r = _sigmoid(jnp.concatenate(ra, axis=1) + ba)
    gi = _sigmoid(jnp.concatenate(ri, axis=1) + bx)
    neg = -lam
    softplus = jnp.maximum(neg, 0.0) + _log1p(jnp.exp(-jnp.abs(neg)))
    log_a = (-RG_C * softplus) * r
    a = jnp.exp(log_a)
    u = jnp.sqrt(-_expm1(2.0 * log_a)) * (gi * xc)
    row = lax.broadcasted_iota(jnp.int32, (n, D_RNN), 0)
    d = 1
    while d < n:
        if fwd:
            a_s = pltpu.roll(a, d, 0)
            u_s = pltpu.roll(u, d, 0)
            live = row >= d
        else:
            a_s = pltpu.roll(a, n - d, 0)
            u_s = pltpu.roll(u, n - d, 0)
            live = row < n - d
        u = jnp.where(live, a * u_s + u, u)
        a = jnp.where(live, a * a_s, a)
        d *= 2
    h = u + a * h_ref[...]
    h_ref[...] = h[n - 1:n, :] if fwd else h[0:1, :]
    return h


def _lru_kernel(tbl_ref, xf_ref, xb_ref, wa_ref, ba_ref, wx_ref, bx_ref, lam_ref, h0f_ref, h0b_ref,
                hf_ref, hb_ref, hf_s, hb_s):
    s = pl.program_id(0)

    @pl.when(tbl_ref[4, s] == 1)
    def _():
        hf_s[...] = h0f_ref[0]
        hb_s[...] = h0b_ref[0]

    hf_ref[...] = _lru_direction(xf_ref[...], wa_ref.at[0], ba_ref[0:1, :], wx_ref.at[0], bx_ref[0:1, :],
                                 lam_ref[0:1, :], hf_s, True)
    hb_ref[...] = _lru_direction(xb_ref[...], wa_ref.at[1], ba_ref[1:2, :], wx_ref.at[1], bx_ref[1:2, :],
                                 lam_ref[1:2, :], hb_s, False)


def _lru_scan(xc, w_a_bf16, b_a, w_x_bf16, b_x, lam, h0f_all, h0b_all):
    tbl = _scan_table(LRU_CHUNK)
    n_steps = tbl.shape[1]
    blk = lambda r: pl.BlockSpec((LRU_CHUNK, D_RNN), lambda s, t: (t[r, s], 0))
    wspec = pl.BlockSpec((2, RG_BLOCKS, RG_BW, RG_BW), lambda s, t: (0, 0, 0, 0))
    vec2 = pl.BlockSpec((2, D_RNN), lambda s, t: (0, 0))
    h0 = pl.BlockSpec((1, 1, D_RNN), lambda s, t: (t[2, s], 0, 0))
    grid_spec = pltpu.PrefetchScalarGridSpec(
        num_scalar_prefetch=1,
        grid=(n_steps,),
        in_specs=[blk(0), blk(1), wspec, vec2, wspec, vec2, vec2, h0, h0],
        out_specs=[blk(0), blk(1)],
        scratch_shapes=[pltpu.VMEM((1, D_RNN), F32), pltpu.VMEM((1, D_RNN), F32)],
    )
    return pl.pallas_call(
        _lru_kernel,
        grid_spec=grid_spec,
        out_shape=[jax.ShapeDtypeStruct((N_TOK, D_RNN), F32), jax.ShapeDtypeStruct((N_TOK, D_RNN), F32)],
        compiler_params=_cparams("arbitrary"),
        name="lru_scan",
    )(jnp.asarray(tbl), xc, xc, w_a_bf16, b_a, w_x_bf16, b_x, lam, h0f_all, h0b_all)


def _odd_out_kernel(x_ref, mod_ref, hf_ref, hb_ref, gate_ref, w_ref, o_ref):
    y = (hf_ref[...] + hb_ref[...]) * _gelu_tanh(gate_ref[...])
    o_ref[...] = x_ref[...] + mod_ref[0][2:3, :] * _dot(y.astype(BF16), w_ref[...])


def _odd_out(x, modt, hf, hb, gate, w_bf16):
    tile = pl.BlockSpec((TB, D_MODEL), lambda i: (i, 0))
    return pl.pallas_call(
        _odd_out_kernel,
        grid=(N_TILES,),
        in_specs=[tile, pl.BlockSpec((1, 6, D_MODEL), lambda i: (i, 0, 0)), tile, tile, tile,
                  pl.BlockSpec((D_RNN, D_MODEL), lambda i: (0, 0))],
        out_specs=tile,
        out_shape=jax.ShapeDtypeStruct((N_TOK, D_MODEL), F32),
        compiler_params=_cparams("parallel"),
        name="odd_out",
    )(x, modt, hf, hb, gate, w_bf16)


def _extract_top(s, n_rows, emit):
    rows = lax.broadcasted_iota(jnp.int32, (n_rows, TB), 0)

    def body(k, s):
        m = s.max(axis=0, keepdims=True)
        idx = jnp.where(s == m, rows, n_rows).min(axis=0, keepdims=True)
        emit(k, m, idx)
        return jnp.where(rows == idx, -jnp.inf, s)

    lax.fori_loop(0, PEER_TOPK, body, s)


def _router_kernel(x_ref, mod_ref, g_ref, wq_ref, keys_ref, h_ref, a_ref, b_ref, gate_ref,
                   h_s, v1_s, i1_s, v2_s, i2_s, sc_s, a_s, b_s):
    hd = pl.program_id(1)

    @pl.when(hd == 0)
    def _():
        h = _rms_mod(x_ref[...], g_ref[...], mod_ref[0], 3, 4).astype(BF16)
        h_s[...] = h
        h_ref[...] = h

    q = _dot(h_s[...], wq_ref[...])
    for half, (v_s, i_s) in enumerate(((v1_s, i1_s), (v2_s, i2_s))):
        qh = q[:, half * 128:(half + 1) * 128].astype(BF16)
        st = _dot_nt(keys_ref[half, 0], qh)

        def emit(k, m, idx, v_s=v_s, i_s=i_s):
            v_s[pl.ds(k, 1), :] = m
            i_s[pl.ds(k, 1), :] = idx

        _extract_top(st, PEER_NK, emit)

    v1 = v1_s[...]
    v2 = v2_s[...]
    sub8 = lax.broadcasted_iota(jnp.int32, (8, TB), 0)
    pieces = [v1[0:1, :] + v2]
    for p in range(1, 8):
        pieces.append(jnp.where(sub8 < PEER_TOPK // (p + 1), v1[p:p + 1, :] + v2[0:8, :], -jnp.inf))
    pieces.append(v1[8:16, :] + v2[0:1, :])
    cand = jnp.concatenate(pieces, axis=0)
    sub = lax.broadcasted_iota(jnp.int32, (PEER_TOPK, TB), 0)

    def emit2(k, m, pos):
        p = jnp.where(pos < 72, jnp.maximum((pos - 8) >> 3, 0), pos - 64)
        r = jnp.where(pos < 16, pos, jnp.where(pos < 72, pos & 7, 0))
        sc_s[pl.ds(k, 1), :] = m
        a_s[pl.ds(k, 1), :] = jnp.where(sub == p, i1_s[...], 0).sum(axis=0, keepdims=True)
        b_s[pl.ds(k, 1), :] = jnp.where(sub == r, i2_s[...], 0).sum(axis=0, keepdims=True)

    _extract_top(cand, 80, emit2)
    sc = sc_s[...]
    e = jnp.exp(sc - sc[0:1, :])
    gate_ref[0] = e / e.sum(axis=0, keepdims=True)
    a_ref[0] = a_s[...].astype(F32)
    b_ref[0] = b_s[...].astype(F32)


def _router(x, modt, g, wq_bf16, keys_bf16):
    kt = pl.BlockSpec((1, PEER_TOPK, TB), lambda i, h: (i, h, 0))
    kshape = jax.ShapeDtypeStruct((N_TILES, PEER_HEADS * PEER_TOPK, TB), F32)
    f32s = lambda: pltpu.VMEM((PEER_TOPK, TB), F32)
    i32s = lambda: pltpu.VMEM((PEER_TOPK, TB), jnp.int32)
    return pl.pallas_call(
        _router_kernel,
        grid=(N_TILES, PEER_HEADS),
        in_specs=[
            pl.BlockSpec((TB, D_MODEL), lambda i, h: (i, 0)),
            pl.BlockSpec((1, 6, D_MODEL), lambda i, h: (i, 0, 0)),
            pl.BlockSpec((1, D_MODEL), lambda i, h: (0, 0)),
            pl.BlockSpec((D_MODEL, PEER_DK), lambda i, h: (0, h)),
            pl.BlockSpec((2, 1, PEER_NK, PEER_DK // 2), lambda i, h: (0, h, 0, 0)),
        ],
        out_specs=[pl.BlockSpec((TB, D_MODEL), lambda i, h: (i, 0)), kt, kt, kt],
        out_shape=[jax.ShapeDtypeStruct((N_TOK, D_MODEL), BF16), kshape, kshape, kshape],
        scratch_shapes=[pltpu.VMEM((TB, D_MODEL), BF16), f32s(), i32s(), f32s(), i32s(), f32s(), i32s(), i32s()],
        compiler_params=_cparams("parallel", "arbitrary"),
        name="peer_router",
    )(x, modt, g.reshape(1, D_MODEL), wq_bf16, keys_bf16)


def _expert_kernel(h_ref, x_ref, mod_ref, at_ref, bt_ref, gt_ref, u_ref, v_ref, fg_ref, o_ref, *rest, final):
    if final:
        y_ref, g_s, acc_s, a_s, b_s, w_s = rest
    else:
        g_s, acc_s, a_s, b_s, w_s = rest
    grp = pl.program_id(1)

    @pl.when(grp == 0)
    def _():
        acc_s[...] = jnp.zeros_like(acc_s)
        for part in range(TE // TB):
            a_s[part * TB:(part + 1) * TB, :] = at_ref[part].T
            b_s[part * TB:(part + 1) * TB, :] = bt_ref[part].T
            w_s[part * TB:(part + 1) * TB, :] = gt_ref[part].T
        sub = lax.broadcasted_iota(jnp.int32, (PEER_NK, PEER_NK), 0).astype(F32)

        def gate_grid(t):
            a_row = a_s[pl.ds(t, 1), :]
            b_row = b_s[pl.ds(t, 1), :]
            w_row = w_s[pl.ds(t, 1), :]
            m_t = jnp.where(a_row == sub, 1.0, 0.0).astype(BF16)
            r_t = jnp.where(b_row == sub, w_row, 0.0).astype(BF16)
            g = _dot_nt(m_t, r_t).astype(BF16).astype(F32)
            return lax.bitcast_convert_type(g, jnp.uint32)

        def body(t, carry):
            word = (gate_grid(t) >> 16) | (gate_grid(t + TE // 2) & jnp.uint32(0xFFFF0000))
            g_s[pl.ds(t, PEER_NK, stride=G_PITCH), :] = word
            return carry

        lax.fori_loop(0, TE // 2, body, 0, unroll=G_UNROLL)

    h = h_ref[...]
    acc = acc_s[...]
    for sb in range(PEER_GROUP // EXPERT_SUB):
        e0 = sb * EXPERT_SUB
        hmat = _dot_nt(h, u_ref[e0:e0 + EXPERT_SUB, :])
        parts = []
        for j in range(EXPERT_SUB // PEER_NK):
            chunk = grp * CHUNKS_PER_GROUP + sb * (EXPERT_SUB // PEER_NK) + j
            word = g_s[pl.ds(pl.multiple_of(chunk * G_PITCH, 8), TE // 2), :]
            gates = jnp.concatenate(
                [lax.bitcast_convert_type(word << 16, F32),
                 lax.bitcast_convert_type(word & jnp.uint32(0xFFFF0000), F32)], axis=0)
            parts.append((_gelu_tanh(hmat[:, j * PEER_NK:(j + 1) * PEER_NK]) * gates).astype(BF16))
        acc = acc + _dot(jnp.concatenate(parts, axis=1), v_ref[e0:e0 + EXPERT_SUB, :])
    acc_s[...] = acc

    @pl.when(grp == PEER_NGROUPS - 1)
    def _():
        xo = x_ref[...] + mod_ref[0][5:6, :] * acc_s[...]
        o_ref[...] = xo
        if final:
            y_ref[...] = xo * lax.rsqrt(jnp.mean(xo * xo, axis=-1, keepdims=True) + EPS) * fg_ref[...]


def _experts(h_bf16, x, modt_e, a_t, b_t, g_t, u_bf16, v_bf16, final_g, final):
    tile = pl.BlockSpec((TE, D_MODEL), lambda i, g: (i, 0))
    kt = pl.BlockSpec((TE // TB, PEER_HEADS * PEER_TOPK, TB), lambda i, g: (i, 0, 0))
    wblk = pl.BlockSpec((PEER_GROUP, D_MODEL), lambda i, g: (g, 0))
    xs = jax.ShapeDtypeStruct((N_TOK, D_MODEL), F32)
    sel = lambda: pltpu.VMEM((TE, PEER_HEADS * PEER_TOPK), F32)
    return pl.pallas_call(
        functools.partial(_expert_kernel, final=final),
        grid=(N_TOK // TE, PEER_NGROUPS),
        in_specs=[tile, tile, pl.BlockSpec((1, 6, D_MODEL), lambda i, g: (i, 0, 0)), kt, kt, kt, wblk, wblk,
                  pl.BlockSpec((1, D_MODEL), lambda i, g: (0, 0))],
        out_specs=[tile, tile] if final else tile,
        out_shape=[xs, xs] if final else xs,
        scratch_shapes=[pltpu.VMEM((PEER_NK * G_PITCH, PEER_NK), jnp.uint32), pltpu.VMEM((TE, D_MODEL), F32),
                        sel(), sel(), sel()],
        compiler_params=_cparams("parallel", "arbitrary"),
        name="peer_experts_final" if final else "peer_experts",
    )(h_bf16, x, modt_e, a_t, b_t, g_t, u_bf16, v_bf16, final_g.reshape(1, D_MODEL))


def _peer(x, modt, modt_e, norm_g, w_q, keys, u, v, final_g, final):
    h, a_t, b_t, g_t = _router(x, modt, norm_g, w_q.astype(BF16), keys.astype(BF16))
    return _experts(h, x, modt_e, a_t, b_t, g_t, u.astype(BF16), v.astype(BF16), final_g, final)


def kernel(x_prompt, x_sample, cache_k, cache_v, state_hgrn_fwd, state_hgrn_bwd, state_lru_fwd, state_lru_bwd, c, c_ctx, norm1_g, norm2_g, w_mod, b_mod, w_in_even, w_out_even, hgrn_lb_fwd, hgrn_lb_bwd, hgrn_gnorm_g, attn_sink, w_in_odd, w_out_odd, conv_w, conv_b, rg_w_a, rg_b_a, rg_w_x, rg_b_x, rg_lambda, peer_w_q, peer_keys, peer_u, peer_v, final_g):
    x = jnp.concatenate([x_prompt.reshape(N_PROMPT, D_MODEL), x_sample.reshape(N_SAMPLE, D_MODEL)], axis=0)
    cond = jnp.concatenate([c_ctx[None, :], c, jnp.zeros((COND_PAD - N_COND, D_MODEL), F32)], axis=0)
    mod = _modulation(cond, w_mod, b_mod)
    tile_row = np.asarray([0] * PROMPT_TILES + [1 + i // TILES_PER_SAMPLE for i in range(N_TILES - PROMPT_TILES)])
    modt = [mod[l][tile_row].reshape(N_TILES, 6, D_MODEL) for l in range(DEPTH)]
    modt_e = [m[::TE // TB] for m in modt]

    lbf = jnp.cumsum(jax.nn.softmax(hgrn_lb_fwd.astype(F32), axis=0), axis=0)[0].reshape(1, HG_W)
    lbb = jnp.cumsum(jax.nn.softmax(hgrn_lb_bwd.astype(F32), axis=0), axis=0)[0].reshape(1, HG_W)
    widths = (HG_W, HG_W, HG_W, HG_W, HG_W, ATT_W, KV_W, KV_W)
    zq, zf, zb, zi, zg, aq, ak, av = _in_proj(x, modt[0], norm1_g[0], w_in_even[0].astype(BF16), widths, "even_in")
    zero_state = jnp.zeros((1, HG_HEADS, HG_DV, HG_DK), F32)
    s0f = jnp.concatenate([zero_state, jnp.swapaxes(state_hgrn_fwd[:, 0], -1, -2)], axis=0)
    s0b = jnp.concatenate([zero_state, jnp.swapaxes(state_hgrn_bwd[:, 0], -1, -2)], axis=0)
    o_f, o_b, sf_t, sb_t = _hgrn_scan(zq, zf, zb, zi, lbf, lbb, s0f, s0b)
    oatt = _attention(aq, ak, av, cache_k[:, 0].reshape(DEC_BATCH, PAST_LEN, KV_W),
                      cache_v[:, 0].reshape(DEC_BATCH, PAST_LEN, KV_W), attn_sink[0])
    x = _even_out(x, modt[0], o_f, o_b, zg, oatt, hgrn_gnorm_g[0], w_out_even[0].astype(BF16))
    x = _peer(x, modt[0], modt_e[0], norm2_g[0], peer_w_q[0], peer_keys[0], peer_u[0], peer_v[0], final_g, False)

    gate, xr = _in_proj(x, modt[1], norm1_g[1], w_in_odd[0].astype(BF16), (D_RNN, D_RNN), "odd_in")
    xc = _dwconv(xr, conv_w[0], conv_b[0])
    zero_h = jnp.zeros((1, 1, D_RNN), F32)
    h0f = jnp.concatenate([zero_h, state_lru_fwd[:, 0][:, None, :]], axis=0)
    h0b = jnp.concatenate([zero_h, state_lru_bwd[:, 0][:, None, :]], axis=0)
    hf, hb = _lru_scan(xc, rg_w_a[0].astype(BF16), rg_b_a[0], rg_w_x[0].astype(BF16), rg_b_x[0],
                       rg_lambda[0], h0f, h0b)
    x = _odd_out(x, modt[1], hf, hb, gate, w_out_odd[0].astype(BF16))
    _, y = _peer(x, modt[1], modt_e[1], norm2_g[1], peer_w_q[1], peer_keys[1], peer_u[1], peer_v[1], final_g, True)

    y_prompt = y[:N_PROMPT].reshape(BATCH, SEQ, D_MODEL)
    y_sample = y[N_PROMPT:].reshape(DEC_BATCH, DEC_SEQ, D_MODEL)
    new_k = ak[:N_PROMPT].reshape(BATCH, 1, SEQ, KV_HEADS, HEAD_DIM)
    new_v = av[:N_PROMPT].reshape(BATCH, 1, SEQ, KV_HEADS, HEAD_DIM)
    new_hf = jnp.swapaxes(sf_t[:BATCH], -1, -2)[:, None]
    new_hb = jnp.swapaxes(sb_t[:BATCH], -1, -2)[:, None]
    new_lf = hf[:N_PROMPT].reshape(BATCH, SEQ, D_RNN)[:, -1][:, None, :]
    new_lb = hb[:N_PROMPT].reshape(BATCH, SEQ, D_RNN)[:, 0][:, None, :]
    return (y_prompt, y_sample, new_k, new_v, new_hf, new_hb, new_lf, new_lb)
```

```python
import functools

import numpy as np
import jax
import jax.numpy as jnp
from jax import lax
from jax.experimental import pallas as pl
from jax.experimental.pallas import tpu as pltpu

F32 = jnp.float32
BF16 = jnp.bfloat16

D_MODEL = 1024
BATCH = 16
SEQ = 256
DEPTH = 2
DEC_BATCH = 8
DEC_SEQ = 2048
PAST_LEN = 512
GRID_W = 64
EPS = 1e-6
HG_DK = 128
HG_DV = 128
HG_HEADS = 4
HG_CHUNK = 64
HG_W = HG_HEADS * HG_DK
HEAD_DIM = 64
N_HEADS = 8
KV_HEADS = 2
Q_PER_KV = 4
WINDOW = 128
ATT_BLOCK = 128
ROPE_BASE = 10000.0
ATT_W = N_HEADS * HEAD_DIM
KV_W = KV_HEADS * HEAD_DIM
D_RNN = 1024
RG_BLOCKS = 4
RG_BW = 256
RG_C = 8.0
PEER_HEADS = 8
PEER_NK = 128
PEER_DK = 256
PEER_TOPK = 16

N_PROMPT = BATCH * SEQ
N_SAMPLE = DEC_BATCH * DEC_SEQ
N_TOK = N_PROMPT + N_SAMPLE
TB = 256
N_TILES = N_TOK // TB
PROMPT_TILES = N_PROMPT // TB
TILES_PER_SAMPLE = DEC_SEQ // TB
N_COND = 1 + DEC_BATCH
COND_PAD = 16

TE = 512
PEER_GROUP = 1024
PEER_NGROUPS = PEER_NK * PEER_NK // PEER_GROUP
CHUNKS_PER_GROUP = PEER_GROUP // PEER_NK
EXPERT_SUB = 1024
G_PITCH = TE // 2 + 8
G_UNROLL = 16

LRU_CHUNK = 256

VMEM_LIMIT = 56 * 1024 * 1024


def _cparams(*sem):
    return pltpu.CompilerParams(dimension_semantics=sem, vmem_limit_bytes=VMEM_LIMIT)


def _dot(a, b):
    return jnp.dot(a, b, preferred_element_type=F32)


def _dot_nt(a, b):
    return lax.dot_general(a, b, (((1,), (1,)), ((), ())), preferred_element_type=F32)


def _dot_tn(a, b):
    return lax.dot_general(a, b, (((0,), (0,)), ((), ())), preferred_element_type=F32)


def _sigmoid(x):
    return 1.0 / (1.0 + jnp.exp(-x))


def _silu(x):
    return x * _sigmoid(x)


def _gelu_tanh(x):
    c = np.float32(np.sqrt(2.0 / np.pi))
    return x * (0.5 * (1.0 + jnp.tanh(c * (x + 0.044715 * (x * x * x)))))


def _rms_mod(x, g, mod, sh, sc):
    y = x * lax.rsqrt(jnp.mean(x * x, axis=-1, keepdims=True) + EPS) * g
    return y * (1.0 + mod[sc:sc + 1, :]) + mod[sh:sh + 1, :]


def _mod_kernel(c_ref, w_ref, b_ref, o_ref):
    c = c_ref[...]
    s = _silu(c).astype(BF16)
    o_ref[0] = _dot(s, w_ref[0].astype(BF16)) + b_ref[0]


def _modulation(cond, w_mod, b_mod):
    nb = 6 * D_MODEL // 1024
    return pl.pallas_call(
        _mod_kernel,
        grid=(DEPTH, nb),
        in_specs=[
            pl.BlockSpec((COND_PAD, D_MODEL), lambda l, n: (0, 0)),
            pl.BlockSpec((1, D_MODEL, 1024), lambda l, n: (l, 0, n)),
            pl.BlockSpec((1, 1, 1024), lambda l, n: (l, 0, n)),
        ],
        out_specs=pl.BlockSpec((1, COND_PAD, 1024), lambda l, n: (l, 0, n)),
        out_shape=jax.ShapeDtypeStruct((DEPTH, COND_PAD, 6 * D_MODEL), F32),
        compiler_params=_cparams("parallel", "parallel"),
        name="modulation",
    )(cond, w_mod, b_mod.reshape(DEPTH, 1, 6 * D_MODEL))


def _in_proj_kernel(x_ref, mod_ref, g_ref, w_ref, *o_refs, widths):
    h = _rms_mod(x_ref[...], g_ref[...], mod_ref[0], 0, 1)
    z = _dot(h.astype(BF16), w_ref[...])
    off = 0
    for o_ref, wd in zip(o_refs, widths):
        o_ref[...] = z[:, off:off + wd]
        off += wd


def _in_proj(x, modt, g, w_bf16, widths, name):
    n_out = sum(widths)
    return pl.pallas_call(
        functools.partial(_in_proj_kernel, widths=widths),
        grid=(N_TILES,),
        in_specs=[
            pl.BlockSpec((TB, D_MODEL), lambda i: (i, 0)),
            pl.BlockSpec((1, 6, D_MODEL), lambda i: (i, 0, 0)),
            pl.BlockSpec((1, D_MODEL), lambda i: (0, 0)),
            pl.BlockSpec((D_MODEL, n_out), lambda i: (0, 0)),
        ],
        out_specs=[pl.BlockSpec((TB, wd), lambda i: (i, 0)) for wd in widths],
        out_shape=[jax.ShapeDtypeStruct((N_TOK, wd), F32) for wd in widths],
        compiler_params=_cparams("parallel"),
        name=name,
    )(x, modt, g.reshape(1, D_MODEL), w_bf16)


def _hgrn_direction(zq, zgate, zi, lb, st_ref, fwd):
    n = HG_CHUNK
    row = lax.broadcasted_iota(jnp.int32, (n, n), 0)
    col = lax.broadcasted_iota(jnp.int32, (n, n), 1)
    keep = (col <= row) if fwd else (col >= row)
    tri = jnp.where(keep, 1.0, 0.0).astype(BF16)
    q = _silu(zq)
    f = lb + (1.0 - lb) * _sigmoid(zgate)
    g = jnp.log(f)
    k = (1.0 - lb) * _sigmoid(-zgate)
    g_hi = g.astype(BF16)
    g_lo = (g - g_hi.astype(F32)).astype(BF16)
    b = _dot(tri, g_hi) + _dot(tri, g_lo)
    mid = n // 2 - 1 if fwd else n // 2
    last = n - 1 if fwd else 0
    bm = b[mid:mid + 1, :]
    bl = b[last:last + 1, :]
    qi = (q * jnp.exp(b - bm)).astype(BF16)
    ki = (k * jnp.exp(bm - b)).astype(BF16)
    qs = (q * jnp.exp(b)).astype(BF16)
    ks = (k * jnp.exp(bl - b)).astype(BF16)
    vb = zi.astype(BF16)
    decay = jnp.exp(bl)
    outs = []
    for h in range(HG_HEADS):
        sl = slice(h * HG_DK, (h + 1) * HG_DK)
        a = jnp.where(keep, _dot_nt(qi[:, sl], ki[:, sl]), 0.0)
        st = st_ref[h]
        o = _dot(a.astype(BF16), vb[:, sl]) + _dot_nt(qs[:, sl], st.astype(BF16))
        st_ref[h] = decay[:, sl] * st + _dot_tn(vb[:, sl], ks[:, sl])
        outs.append(o)
    return jnp.concatenate(outs, axis=1)


def _hgrn_kernel(tbl_ref, zqf_ref, zff_ref, zif_ref, zqb_ref, zbb_ref, zib_ref, lbf_ref, lbb_ref,
                 s0f_ref, s0b_ref, of_ref, ob_ref, sf_ref, sb_ref, stf, stb):
    s = pl.program_id(0)

    @pl.when(tbl_ref[4, s] == 1)
    def _():
        stf[...] = s0f_ref[0]
        stb[...] = s0b_ref[0]

    of_ref[...] = _hgrn_direction(zqf_ref[...], zff_ref[...], zif_ref[...], lbf_ref[...], stf, True)
    ob_ref[...] = _hgrn_direction(zqb_ref[...], zbb_ref[...], zib_ref[...], lbb_ref[...], stb, False)

    @pl.when(tbl_ref[5, s] == 1)
    def _():
        sf_ref[0] = stf[...]
        sb_ref[0] = stb[...]


def _scan_table(chunk):
    rows = []
    blk = 0
    for (nseq, length, has_init) in ((BATCH, SEQ, False), (DEC_BATCH, DEC_SEQ, True)):
        nc = length // chunk
        for b in range(nseq):
            sid = b if not has_init else BATCH + b
            for i in range(nc):
                rows.append((blk + i, blk + nc - 1 - i, (1 + b) if has_init else 0, sid,
                             int(i == 0), int(i == nc - 1)))
            blk += nc
    return np.asarray(rows, np.int32).T.copy()


def _hgrn_scan(zq, zf, zb, zi, lbf, lbb, s0f_t, s0b_t):
    tbl = _scan_table(HG_CHUNK)
    n_steps = tbl.shape[1]
    n_seq = BATCH + DEC_BATCH
    blk = lambda r: pl.BlockSpec((HG_CHUNK, HG_W), lambda s, t: (t[r, s], 0))
    vec = pl.BlockSpec((1, HG_W), lambda s, t: (0, 0))
    st_in = pl.BlockSpec((1, HG_HEADS, HG_DV, HG_DK), lambda s, t: (t[2, s], 0, 0, 0))
    st_out = pl.BlockSpec((1, HG_HEADS, HG_DV, HG_DK), lambda s, t: (t[3, s], 0, 0, 0))
    grid_spec = pltpu.PrefetchScalarGridSpec(
        num_scalar_prefetch=1,
        grid=(n_steps,),
        in_specs=[blk(0), blk(0), blk(0), blk(1), blk(1), blk(1), vec, vec, st_in, st_in],
        out_specs=[blk(0), blk(1), st_out, st_out],
        scratch_shapes=[pltpu.VMEM((HG_HEADS, HG_DV, HG_DK), F32),
                        pltpu.VMEM((HG_HEADS, HG_DV, HG_DK), F32)],
    )
    st_shape = jax.ShapeDtypeStruct((n_seq, HG_HEADS, HG_DV, HG_DK), F32)
    return pl.pallas_call(
        _hgrn_kernel,
        grid_spec=grid_spec,
        out_shape=[jax.ShapeDtypeStruct((N_TOK, HG_W), F32), jax.ShapeDtypeStruct((N_TOK, HG_W), F32),
                   st_shape, st_shape],
        compiler_params=_cparams("arbitrary"),
        name="hgrn_scan",
    )(jnp.asarray(tbl), zq, zf, zi, zq, zb, zi, lbf, lbb, s0f_t, s0b_t)


def _rope(x, cos, sgn_sin):
    w = x.shape[1]
    lane = lax.broadcasted_iota(jnp.int32, x.shape, 1)
    partner = jnp.where((lane & 31) < 16, pltpu.roll(x, w - 16, 1), pltpu.roll(x, 16, 1))
    return x * cos + partner * sgn_sin


def _dup_kv_half(x, kv):
    lane = lax.broadcasted_iota(jnp.int32, x.shape, 1)
    own = (lane < HEAD_DIM) if kv == 0 else (lane >= HEAD_DIM)
    xs = jnp.where(own, x, 0.0)
    return xs + pltpu.roll(xs, HEAD_DIM, 1)


def _attend(q, key_parts, sink_ref, o_ref):
    lane = lax.broadcasted_iota(jnp.int32, (ATT_BLOCK, 128), 1)
    lo = lane < HEAD_DIM
    for kv in range(KV_HEADS):
        qs = []
        for gq in range(Q_PER_KV):
            hd = kv * Q_PER_KV + gq
            pair = q[:, (hd // 2) * 128:(hd // 2 + 1) * 128]
            qs.append(jnp.where(lo if hd % 2 == 0 else ~lo, pair, 0.0))
        qst = jnp.concatenate(qs, axis=0).astype(BF16)
        logits = []
        vals = []
        for (k, v, mask) in key_parts:
            kd = _dup_kv_half(k, kv).astype(BF16)
            vals.append(_dup_kv_half(v, kv).astype(BF16))
            s = _dot_nt(qst, kd)
            if mask is not None:
                s = jnp.where(mask, s, -1e30)
            logits.append(s)
        m = logits[0].max(axis=-1, keepdims=True)
        for s in logits[1:]:
            m = jnp.maximum(m, s.max(axis=-1, keepdims=True))
        sink_col = jnp.concatenate(
            [jnp.full((ATT_BLOCK, 1), sink_ref[kv * Q_PER_KV + gq], F32) for gq in range(Q_PER_KV)], axis=0)
        m = jnp.maximum(m, sink_col)
        den = jnp.exp(sink_col - m)
        acc = jnp.zeros((Q_PER_KV * ATT_BLOCK, 128), F32)
        for s, v in zip(logits, vals):
            p = jnp.exp(s - m)
            den = den + p.sum(axis=-1, keepdims=True)
            acc = acc + _dot(p.astype(BF16), v)
        out = acc / den
        for pr in range(Q_PER_KV // 2):
            r0 = out[(2 * pr) * ATT_BLOCK:(2 * pr + 1) * ATT_BLOCK]
            r1 = out[(2 * pr + 1) * ATT_BLOCK:(2 * pr + 2) * ATT_BLOCK]
            c0 = (kv * Q_PER_KV // 2 + pr) * 128
            o_ref[:, c0:c0 + 128] = jnp.where(lo, r0, r1)


def _attn_prompt_kernel(sink_ref, q_ref, k_ref, v_ref, o_ref):
    q = q_ref[...] * np.float32(HEAD_DIM ** -0.5)
    _attend(q, [(k_ref[...], v_ref[...], None)], sink_ref, o_ref)


def _attn_latent_kernel(sink_ref, q_ref, kp_ref, kc_ref, kn_ref, vp_ref, vc_ref, vn_ref,
                        ck_ref, cv_ref, cos_ref, sin_ref, o_ref):
    j = pl.program_id(1)
    nq = pl.num_programs(1)
    qstart = pl.multiple_of((j + 1) * ATT_BLOCK, ATT_BLOCK)
    kstart = pl.multiple_of(j * ATT_BLOCK, ATT_BLOCK)
    cq = cos_ref[pl.ds(qstart, ATT_BLOCK), :]
    sq = sin_ref[pl.ds(qstart, ATT_BLOCK), :]
    q = _rope(q_ref[...], jnp.concatenate([cq] * 4, axis=1), jnp.concatenate([sq] * 4, axis=1))
    q = q * np.float32(HEAD_DIM ** -0.5)
    k_loc = jnp.concatenate([kp_ref[...], kc_ref[...], kn_ref[...]], axis=0)
    v_loc = jnp.concatenate([vp_ref[...], vc_ref[...], vn_ref[...]], axis=0)
    k_loc = _rope(k_loc, cos_ref[pl.ds(kstart, 3 * ATT_BLOCK), :], sin_ref[pl.ds(kstart, 3 * ATT_BLOCK), :])
    r = lax.broadcasted_iota(jnp.int32, (Q_PER_KV * ATT_BLOCK, 3 * ATT_BLOCK), 0) & (ATT_BLOCK - 1)
    c = lax.broadcasted_iota(jnp.int32, (Q_PER_KV * ATT_BLOCK, 3 * ATT_BLOCK), 1)
    rel = c - ATT_BLOCK - r
    c_lo = jnp.where(j > 0, 0, ATT_BLOCK)
    c_hi = jnp.where(j < nq - 1, 3 * ATT_BLOCK, 2 * ATT_BLOCK)
    ok = (jnp.abs(rel) <= WINDOW) & (c >= c_lo) & (c < c_hi)
    _attend(q, [(ck_ref[0], cv_ref[0], None), (k_loc, v_loc, ok)], sink_ref, o_ref)


def _rope_tables():
    pos = np.arange(-ATT_BLOCK, DEC_SEQ + ATT_BLOCK)
    nf = HEAD_DIM // 4
    inv = (1.0 / (ROPE_BASE ** (np.arange(nf, dtype=np.float32) / nf))).astype(np.float32)
    rows = (pos // GRID_W).astype(np.float32)
    cols = (pos % GRID_W).astype(np.float32)
    ar = rows[:, None] * inv[None, :]
    ac = cols[:, None] * inv[None, :]
    cos64 = np.concatenate([np.cos(ar), np.cos(ar), np.cos(ac), np.cos(ac)], axis=1)
    sin64 = np.concatenate([-np.sin(ar), np.sin(ar), -np.sin(ac), np.sin(ac)], axis=1)
    cos = np.concatenate([cos64, cos64], axis=1).astype(np.float32)
    sin = np.concatenate([sin64, sin64], axis=1).astype(np.float32)
    return jnp.asarray(cos), jnp.asarray(sin)


def _attention(aq, ak, av, ck, cv, sink):
    smem = pl.BlockSpec(memory_space=pltpu.SMEM)
    nqp = SEQ // ATT_BLOCK
    o_prompt = pl.pallas_call(
        _attn_prompt_kernel,
        grid=(BATCH, nqp),
        in_specs=[
            smem,
            pl.BlockSpec((ATT_BLOCK, ATT_W), lambda b, j: (b * nqp + j, 0)),
            pl.BlockSpec((SEQ, KV_W), lambda b, j: (b, 0)),
            pl.BlockSpec((SEQ, KV_W), lambda b, j: (b, 0)),
        ],
        out_specs=pl.BlockSpec((ATT_BLOCK, ATT_W), lambda b, j: (b * nqp + j, 0)),
        out_shape=jax.ShapeDtypeStruct((N_PROMPT, ATT_W), F32),
        compiler_params=_cparams("parallel", "parallel"),
        name="attn_prompt",
    )(sink, aq, ak, av)

    nq = DEC_SEQ // ATT_BLOCK
    base = N_PROMPT // ATT_BLOCK
    cos, sin = _rope_tables()
    kv_blk = lambda d: pl.BlockSpec(
        (ATT_BLOCK, KV_W), lambda b, j: (base + b * nq + jnp.clip(j + d, 0, nq - 1), 0))
    tab = pl.BlockSpec((DEC_SEQ + 2 * ATT_BLOCK, 128), lambda b, j: (0, 0))
    o_latent = pl.pallas_call(
        _attn_latent_kernel,
        grid=(DEC_BATCH, nq),
        in_specs=[
            smem,
            pl.BlockSpec((ATT_BLOCK, ATT_W), lambda b, j: (base + b * nq + j, 0)),
            kv_blk(-1), kv_blk(0), kv_blk(1), kv_blk(-1), kv_blk(0), kv_blk(1),
            pl.BlockSpec((1, PAST_LEN, KV_W), lambda b, j: (b, 0, 0)),
            pl.BlockSpec((1, PAST_LEN, KV_W), lambda b, j: (b, 0, 0)),
            tab, tab,
        ],
        out_specs=pl.BlockSpec((ATT_BLOCK, ATT_W), lambda b, j: (b * nq + j, 0)),
        out_shape=jax.ShapeDtypeStruct((N_SAMPLE, ATT_W), F32),
        compiler_params=_cparams("parallel", "parallel"),
        name="attn_latent",
    )(sink, aq, ak, ak, ak, av, av, av, ck, cv, cos, sin)
    return jnp.concatenate([o_prompt, o_latent], axis=0)


def _even_out_kernel(x_ref, mod_ref, of_ref, ob_ref, zg_ref, oatt_ref, gn_ref, w_ref, o_ref):
    o = of_ref[...] + ob_ref[...]
    parts = []
    for h in range(HG_HEADS):
        oh = o[:, h * HG_DV:(h + 1) * HG_DV]
        parts.append(oh * lax.rsqrt(jnp.mean(oh * oh, axis=-1, keepdims=True) + EPS))
    o_hg = jnp.concatenate(parts, axis=1) * gn_ref[...] * _silu(zg_ref[...])
    y = _dot(o_hg.astype(BF16), w_ref[0:HG_W, :]) + _dot(oatt_ref[...].astype(BF16), w_ref[HG_W:, :])
    o_ref[...] = x_ref[...] + mod_ref[0][2:3, :] * y


def _even_out(x, modt, o_f, o_b, zg, oatt, gn_g, w_bf16):
    tile = lambda w: pl.BlockSpec((TB, w), lambda i: (i, 0))
    return pl.pallas_call(
        _even_out_kernel,
        grid=(N_TILES,),
        in_specs=[
            tile(D_MODEL),
            pl.BlockSpec((1, 6, D_MODEL), lambda i: (i, 0, 0)),
            tile(HG_W), tile(HG_W), tile(HG_W), tile(ATT_W),
            pl.BlockSpec((1, HG_W), lambda i: (0, 0)),
            pl.BlockSpec((HG_W + ATT_W, D_MODEL), lambda i: (0, 0)),
        ],
        out_specs=tile(D_MODEL),
        out_shape=jax.ShapeDtypeStruct((N_TOK, D_MODEL), F32),
        compiler_params=_cparams("parallel"),
        name="even_out",
    )(x, modt, o_f, o_b, zg, oatt, gn_g.reshape(1, HG_W), w_bf16)


def _conv_kernel(xc_ref, xp_ref, xn_ref, w_ref, b_ref, o_ref):
    i = pl.program_id(0)
    in_sample = i >= PROMPT_TILES
    pos = (i - PROMPT_TILES) % TILES_PER_SAMPLE
    has_prev = in_sample & (pos != 0)
    has_next = in_sample & (pos != TILES_PER_SAMPLE - 1)
    prev = jnp.where(has_prev, xp_ref[...], 0.0)
    nxt = jnp.where(has_next, xn_ref[...], 0.0)
    ext = jnp.concatenate([prev, xc_ref[...], nxt], axis=0)
    n = TB + 16
    w = w_ref[...]
    acc = ext[8:8 + TB] * w[2:3, :] + b_ref[...]
    acc = acc + pltpu.roll(ext, 2, 0)[8:8 + TB] * w[0:1, :]
    acc = acc + pltpu.roll(ext, 1, 0)[8:8 + TB] * w[1:2, :]
    acc = acc + pltpu.roll(ext, n - 1, 0)[8:8 + TB] * w[3:4, :]
    o_ref[...] = acc


def _dwconv(xr, cw, cb):
    r8 = TB // 8
    last8 = N_TOK // 8 - 1
    return pl.pallas_call(
        _conv_kernel,
        grid=(N_TILES,),
        in_specs=[
            pl.BlockSpec((TB, D_RNN), lambda i: (i, 0)),
            pl.BlockSpec((8, D_RNN), lambda i: (jnp.maximum(i * r8 - 1, 0), 0)),
            pl.BlockSpec((8, D_RNN), lambda i: (jnp.minimum((i + 1) * r8, last8), 0)),
            pl.BlockSpec((4, D_RNN), lambda i: (0, 0)),
            pl.BlockSpec((1, D_RNN), lambda i: (0, 0)),
        ],
        out_specs=pl.BlockSpec((TB, D_RNN), lambda i: (i, 0)),
        out_shape=jax.ShapeDtypeStruct((N_TOK, D_RNN), F32),
        compiler_params=_cparams("parallel"),
        name="dwconv",
    )(xr, xr, xr, cw, cb.reshape(1, D_RNN))


def _log1p(y):
    w = 1.0 + y
    return jnp.where(w == 1.0, y, jnp.log(w) * (y / (w - 1.0)))


def _expm1(x):
    e = jnp.exp(x)
    return jnp.where(e == 1.0, x, (e - 1.0) * (x / jnp.log(e)))


def _lru_direction(xc, wa_ref, ba, wx_ref, bx, lam, h_ref, fwd):
    n = LRU_CHUNK
    xb = xc.astype(BF16)
    ra, ri = [], []
    for blk in range(RG_BLOCKS):
        sl = slice(blk * RG_BW, (blk + 1) * RG_BW)
        ra.append(_dot(xb[:, sl], wa_ref[blk]))
        ri.append(_dot(xb[:, sl], wx_ref[blk]))
    r = _sigmoid(jnp.concatenate(ra, axis=1) + ba)
    gi = _sigmoid(jnp.concatenate(ri, axis=1) + bx)
    neg = -lam
    softplus = jnp.maximum(neg, 0.0) + _log1p(jnp.exp(-jnp.abs(neg)))
    log_a = (-RG_C * softplus) * r
    a = jnp.exp(log_a)
    u = jnp.sqrt(-_expm1(2.0 * log_a)) * (gi * xc)
    row = lax.broadcasted_iota(jnp.int32, (n, D_RNN), 0)
    d = 1
    while d < n:
        if fwd:
            a_s = pltpu.roll(a, d, 0)
            u_s = pltpu.roll(u, d, 0)
            live = row >= d
        else:
            a_s = pltpu.roll(a, n - d, 0)
            u_s = pltpu.roll(u, n - d, 0)
            live = row < n - d
        u = jnp.where(live, a * u_s + u, u)
        a = jnp.where(live, a * a_s, a)
        d *= 2
    h = u + a * h_ref[...]
    h_ref[...] = h[n - 1:n, :] if fwd else h[0:1, :]
    return h


def _lru_kernel(tbl_ref, xf_ref, xb_ref, wa_ref, ba_ref, wx_ref, bx_ref, lam_ref, h0f_ref, h0b_ref,
                hf_ref, hb_ref, hf_s, hb_s):
    s = pl.program_id(0)

    @pl.when(tbl_ref[4, s] == 1)
    def _():
        hf_s[...] = h0f_ref[0]
        hb_s[...] = h0b_ref[0]

    hf_ref[...] = _lru_direction(xf_ref[...], wa_ref.at[0], ba_ref[0:1, :], wx_ref.at[0], bx_ref[0:1, :],
                                 lam_ref[0:1, :], hf_s, True)
    hb_ref[...] = _lru_direction(xb_ref[...], wa_ref.at[1], ba_ref[1:2, :], wx_ref.at[1], bx_ref[1:2, :],
                                 lam_ref[1:2, :], hb_s, False)


def _lru_scan(xc, w_a_bf16, b_a, w_x_bf16, b_x, lam, h0f_all, h0b_all):
    tbl = _scan_table(LRU_CHUNK)
    n_steps = tbl.shape[1]
    blk = lambda r: pl.BlockSpec((LRU_CHUNK, D_RNN), lambda s, t: (t[r, s], 0))
    wspec = pl.BlockSpec((2, RG_BLOCKS, RG_BW, RG_BW), lambda s, t: (0, 0, 0, 0))
    vec2 = pl.BlockSpec((2, D_RNN), lambda s, t: (0, 0))
    h0 = pl.BlockSpec((1, 1, D_RNN), lambda s, t: (t[2, s], 0, 0))
    grid_spec = pltpu.PrefetchScalarGridSpec(
        num_scalar_prefetch=1,
        grid=(n_steps,),
        in_specs=[blk(0), blk(1), wspec, vec2, wspec, vec2, vec2, h0, h0],
        out_specs=[blk(0), blk(1)],
        scratch_shapes=[pltpu.VMEM((1, D_RNN), F32), pltpu.VMEM((1, D_RNN), F32)],
    )
    return pl.pallas_call(
        _lru_kernel,
        grid_spec=grid_spec,
        out_shape=[jax.ShapeDtypeStruct((N_TOK, D_RNN), F32), jax.ShapeDtypeStruct((N_TOK, D_RNN), F32)],
        compiler_params=_cparams("arbitrary"),
        name="lru_scan",
    )(jnp.asarray(tbl), xc, xc, w_a_bf16, b_a, w_x_bf16, b_x, lam, h0f_all, h0b_all)


def _odd_out_kernel(x_ref, mod_ref, hf_ref, hb_ref, gate_ref, w_ref, o_ref):
    y = (hf_ref[...] + hb_ref[...]) * _gelu_tanh(gate_ref[...])
    o_ref[...] = x_ref[...] + mod_ref[0][2:3, :] * _dot(y.astype(BF16), w_ref[...])


def _odd_out(x, modt, hf, hb, gate, w_bf16):
    tile = pl.BlockSpec((TB, D_MODEL), lambda i: (i, 0))
    return pl.pallas_call(
        _odd_out_kernel,
        grid=(N_TILES,),
        in_specs=[tile, pl.BlockSpec((1, 6, D_MODEL), lambda i: (i, 0, 0)), tile, tile, tile,
                  pl.BlockSpec((D_RNN, D_MODEL), lambda i: (0, 0))],
        out_specs=tile,
        out_shape=jax.ShapeDtypeStruct((N_TOK, D_MODEL), F32),
        compiler_params=_cparams("parallel"),
        name="odd_out",
    )(x, modt, hf, hb, gate, w_bf16)


def _extract_top(scores, n_rows, emit):
    rows = lax.broadcasted_iota(jnp.int32, (n_rows, TB), 0)

    def body(k, ss):
        out = []
        for which, s in enumerate(ss):
            m = s.max(axis=0, keepdims=True)
            idx = jnp.where(s == m, rows, n_rows).min(axis=0, keepdims=True)
            emit(which, k, m, idx)
            out.append(jnp.where(rows == idx, -jnp.inf, s))
        return tuple(out)

    lax.fori_loop(0, PEER_TOPK, body, tuple(scores))


def _router_kernel(x_ref, mod_ref, g_ref, wq_ref, keys_ref, h_ref, a_ref, b_ref, gate_ref,
                   q_s, v1_s, i1_s, v2_s, i2_s, sc_s, a_s, b_s):
    h = _rms_mod(x_ref[...], g_ref[...], mod_ref[0], 3, 4).astype(BF16)
    h_ref[...] = h
    q_s[...] = _dot(h, wq_ref[...])
    lax.fori_loop(0, PEER_HEADS, functools.partial(
        _route_head, keys_ref, a_ref, b_ref, gate_ref, q_s, v1_s, i1_s, v2_s, i2_s, sc_s, a_s, b_s), 0)


def _route_head(keys_ref, a_ref, b_ref, gate_ref, q_s, v1_s, i1_s, v2_s, i2_s, sc_s, a_s, b_s, hd, carry):
    scores = []
    for half in range(2):
        col = pl.multiple_of(hd * PEER_DK + half * (PEER_DK // 2), PEER_DK // 2)
        qh = q_s[:, pl.ds(col, PEER_DK // 2)].astype(BF16)
        scores.append(_dot_nt(keys_ref[half, hd], qh))

    def emit(which, k, m, idx):
        v_s, i_s = ((v1_s, i1_s), (v2_s, i2_s))[which]
        v_s[pl.ds(k, 1), :] = m
        i_s[pl.ds(k, 1), :] = idx

    _extract_top(scores, PEER_NK, emit)

    v1 = v1_s[...]
    v2 = v2_s[...]
    sub8 = lax.broadcasted_iota(jnp.int32, (8, TB), 0)
    pieces = [v1[0:1, :] + v2]
    for p in range(1, 8):
        pieces.append(jnp.where(sub8 < PEER_TOPK // (p + 1), v1[p:p + 1, :] + v2[0:8, :], -jnp.inf))
    pieces.append(v1[8:16, :] + v2[0:1, :])
    cand = jnp.concatenate(pieces, axis=0)
    sub = lax.broadcasted_iota(jnp.int32, (PEER_TOPK, TB), 0)

    def emit2(which, k, m, pos):
        p = jnp.where(pos < 72, jnp.maximum((pos - 8) >> 3, 0), pos - 64)
        r = jnp.where(pos < 16, pos, jnp.where(pos < 72, pos & 7, 0))
        sc_s[pl.ds(k, 1), :] = m
        a_s[pl.ds(k, 1), :] = jnp.where(sub == p, i1_s[...], 0).sum(axis=0, keepdims=True)
        b_s[pl.ds(k, 1), :] = jnp.where(sub == r, i2_s[...], 0).sum(axis=0, keepdims=True)

    _extract_top([cand], 80, emit2)
    sc = sc_s[...]
    e = jnp.exp(sc - sc[0:1, :])
    out_rows = pl.ds(pl.multiple_of(hd * PEER_TOPK, PEER_TOPK), PEER_TOPK)
    gate_ref[0, out_rows, :] = e / e.sum(axis=0, keepdims=True)
    a_ref[0, out_rows, :] = a_s[...].astype(F32)
    b_ref[0, out_rows, :] = b_s[...].astype(F32)
    return carry


def _router(x, modt, g, wq_bf16, keys_bf16):
    kt = pl.BlockSpec((1, PEER_HEADS * PEER_TOPK, TB), lambda i: (i, 0, 0))
    kshape = jax.ShapeDtypeStruct((N_TILES, PEER_HEADS * PEER_TOPK, TB), F32)
    f32s = lambda: pltpu.VMEM((PEER_TOPK, TB), F32)
    i32s = lambda: pltpu.VMEM((PEER_TOPK, TB), jnp.int32)
    return pl.pallas_call(
        _router_kernel,
        grid=(N_TILES,),
        in_specs=[
            pl.BlockSpec((TB, D_MODEL), lambda i: (i, 0)),
            pl.BlockSpec((1, 6, D_MODEL), lambda i: (i, 0, 0)),
            pl.BlockSpec((1, D_MODEL), lambda i: (0, 0)),
            pl.BlockSpec((D_MODEL, PEER_HEADS * PEER_DK), lambda i: (0, 0)),
            pl.BlockSpec((2, PEER_HEADS, PEER_NK, PEER_DK // 2), lambda i: (0, 0, 0, 0)),
        ],
        out_specs=[pl.BlockSpec((TB, D_MODEL), lambda i: (i, 0)), kt, kt, kt],
        out_shape=[jax.ShapeDtypeStruct((N_TOK, D_MODEL), BF16), kshape, kshape, kshape],
        scratch_shapes=[pltpu.VMEM((TB, PEER_HEADS * PEER_DK), F32),
                        f32s(), i32s(), f32s(), i32s(), f32s(), i32s(), i32s()],
        compiler_params=_cparams("parallel"),
        name="peer_router",
    )(x, modt, g.reshape(1, D_MODEL), wq_bf16, keys_bf16)


def _expert_kernel(h_ref, x_ref, mod_ref, at_ref, bt_ref, gt_ref, u_ref, v_ref, fg_ref, o_ref, *rest, final):
    if final:
        y_ref, g_s, acc_s, a_s, b_s, w_s = rest
    else:
        g_s, acc_s, a_s, b_s, w_s = rest
    grp = pl.program_id(1)

    @pl.when(grp == 0)
    def _():
        acc_s[...] = jnp.zeros_like(acc_s)
        for part in range(TE // TB):
            a_s[part * TB:(part + 1) * TB, :] = at_ref[part].T
            b_s[part * TB:(part + 1) * TB, :] = bt_ref[part].T
            w_s[part * TB:(part + 1) * TB, :] = gt_ref[part].T
        sub = lax.broadcasted_iota(jnp.int32, (PEER_NK, PEER_NK), 0).astype(F32)

        def gate_grid(t):
            a_row = a_s[pl.ds(t, 1), :]
            b_row = b_s[pl.ds(t, 1), :]
            w_row = w_s[pl.ds(t, 1), :]
            m_t = jnp.where(a_row == sub, 1.0, 0.0).astype(BF16)
            r_t = jnp.where(b_row == sub, w_row, 0.0).astype(BF16)
            g = _dot_nt(m_t, r_t).astype(BF16).astype(F32)
            return lax.bitcast_convert_type(g, jnp.uint32)

        def body(t, carry):
            word = (gate_grid(t) >> 16) | (gate_grid(t + TE // 2) & jnp.uint32(0xFFFF0000))
            g_s[pl.ds(t, PEER_NK, stride=G_PITCH), :] = word
            return carry

        lax.fori_loop(0, TE // 2, body, 0, unroll=G_UNROLL)

    h = h_ref[...]
    acc = acc_s[...]
    for sb in range(PEER_GROUP // EXPERT_SUB):
        e0 = sb * EXPERT_SUB
        hmat = _dot_nt(h, u_ref[e0:e0 + EXPERT_SUB, :])
        parts = []
        for j in range(EXPERT_SUB // PEER_NK):
            chunk = grp * CHUNKS_PER_GROUP + sb * (EXPERT_SUB // PEER_NK) + j
            word = g_s[pl.ds(pl.multiple_of(chunk * G_PITCH, 8), TE // 2), :]
            gates = jnp.concatenate(
                [lax.bitcast_convert_type(word << 16, F32),
                 lax.bitcast_convert_type(word & jnp.uint32(0xFFFF0000), F32)], axis=0)
            parts.append((_gelu_tanh(hmat[:, j * PEER_NK:(j + 1) * PEER_NK]) * gates).astype(BF16))
        acc = acc + _dot(jnp.concatenate(parts, axis=1), v_ref[e0:e0 + EXPERT_SUB, :])
    acc_s[...] = acc

    @pl.when(grp == PEER_NGROUPS - 1)
    def _():
        xo = x_ref[...] + mod_ref[0][5:6, :] * acc_s[...]
        o_ref[...] = xo
        if final:
            y_ref[...] = xo * lax.rsqrt(jnp.mean(xo * xo, axis=-1, keepdims=True) + EPS) * fg_ref[...]


def _experts(h_bf16, x, modt_e, a_t, b_t, g_t, u_bf16, v_bf16, final_g, final):
    tile = pl.BlockSpec((TE, D_MODEL), lambda i, g: (i, 0))
    kt = pl.BlockSpec((TE // TB, PEER_HEADS * PEER_TOPK, TB), lambda i, g: (i, 0, 0))
    wblk = pl.BlockSpec((PEER_GROUP, D_MODEL), lambda i, g: (g, 0))
    xs = jax.ShapeDtypeStruct((N_TOK, D_MODEL), F32)
    sel = lambda: pltpu.VMEM((TE, PEER_HEADS * PEER_TOPK), F32)
    return pl.pallas_call(
        functools.partial(_expert_kernel, final=final),
        grid=(N_TOK // TE, PEER_NGROUPS),
        in_specs=[tile, tile, pl.BlockSpec((1, 6, D_MODEL), lambda i, g: (i, 0, 0)), kt, kt, kt, wblk, wblk,
                  pl.BlockSpec((1, D_MODEL), lambda i, g: (0, 0))],
        out_specs=[tile, tile] if final else tile,
        out_shape=[xs, xs] if final else xs,
        scratch_shapes=[pltpu.VMEM((PEER_NK * G_PITCH, PEER_NK), jnp.uint32), pltpu.VMEM((TE, D_MODEL), F32),
                        sel(), sel(), sel()],
        compiler_params=_cparams("parallel", "arbitrary"),
        name="peer_experts_final" if final else "peer_experts",
    )(h_bf16, x, modt_e, a_t, b_t, g_t, u_bf16, v_bf16, final_g.reshape(1, D_MODEL))


def _peer(x, modt, modt_e, norm_g, w_q, keys, u, v, final_g, final):
    h, a_t, b_t, g_t = _router(x, modt, norm_g, w_q.astype(BF16), keys.astype(BF16))
    return _experts(h, x, modt_e, a_t, b_t, g_t, u.astype(BF16), v.astype(BF16), final_g, final)


def kernel(x_prompt, x_sample, cache_k, cache_v, state_hgrn_fwd, state_hgrn_bwd, state_lru_fwd, state_lru_bwd, c, c_ctx, norm1_g, norm2_g, w_mod, b_mod, w_in_even, w_out_even, hgrn_lb_fwd, hgrn_lb_bwd, hgrn_gnorm_g, attn_sink, w_in_odd, w_out_odd, conv_w, conv_b, rg_w_a, rg_b_a, rg_w_x, rg_b_x, rg_lambda, peer_w_q, peer_keys, peer_u, peer_v, final_g):
    x = jnp.concatenate([x_prompt.reshape(N_PROMPT, D_MODEL), x_sample.reshape(N_SAMPLE, D_MODEL)], axis=0)
    cond = jnp.concatenate([c_ctx[None, :], c, jnp.zeros((COND_PAD - N_COND, D_MODEL), F32)], axis=0)
    mod = _modulation(cond, w_mod, b_mod)
    tile_row = np.asarray([0] * PROMPT_TILES + [1 + i // TILES_PER_SAMPLE for i in range(N_TILES - PROMPT_TILES)])
    modt = [mod[l][tile_row].reshape(N_TILES, 6, D_MODEL) for l in range(DEPTH)]
    modt_e = [m[::TE // TB] for m in modt]

    lbf = jnp.cumsum(jax.nn.softmax(hgrn_lb_fwd.astype(F32), axis=0), axis=0)[0].reshape(1, HG_W)
    lbb = jnp.cumsum(jax.nn.softmax(hgrn_lb_bwd.astype(F32), axis=0), axis=0)[0].reshape(1, HG_W)
    widths = (HG_W, HG_W, HG_W, HG_W, HG_W, ATT_W, KV_W, KV_W)
    zq, zf, zb, zi, zg, aq, ak, av = _in_proj(x, modt[0], norm1_g[0], w_in_even[0].astype(BF16), widths, "even_in")
    zero_state = jnp.zeros((1, HG_HEADS, HG_DV, HG_DK), F32)
    s0f = jnp.concatenate([zero_state, jnp.swapaxes(state_hgrn_fwd[:, 0], -1, -2)], axis=0)
    s0b = jnp.concatenate([zero_state, jnp.swapaxes(state_hgrn_bwd[:, 0], -1, -2)], axis=0)
    o_f, o_b, sf_t, sb_t = _hgrn_scan(zq, zf, zb, zi, lbf, lbb, s0f, s0b)
    oatt = _attention(aq, ak, av, cache_k[:, 0].reshape(DEC_BATCH, PAST_LEN, KV_W),
                      cache_v[:, 0].reshape(DEC_BATCH, PAST_LEN, KV_W), attn_sink[0])
    x = _even_out(x, modt[0], o_f, o_b, zg, oatt, hgrn_gnorm_g[0], w_out_even[0].astype(BF16))
    x = _peer(x, modt[0], modt_e[0], norm2_g[0], peer_w_q[0], peer_keys[0], peer_u[0], peer_v[0], final_g, False)

    gate, xr = _in_proj(x, modt[1], norm1_g[1], w_in_odd[0].astype(BF16), (D_RNN, D_RNN), "odd_in")
    xc = _dwconv(xr, conv_w[0], conv_b[0])
    zero_h = jnp.zeros((1, 1, D_RNN), F32)
    h0f = jnp.concatenate([zero_h, state_lru_fwd[:, 0][:, None, :]], axis=0)
    h0b = jnp.concatenate([zero_h, state_lru_bwd[:, 0][:, None, :]], axis=0)
    hf, hb = _lru_scan(xc, rg_w_a[0].astype(BF16), rg_b_a[0], rg_w_x[0].astype(BF16), rg_b_x[0],
                       rg_lambda[0], h0f, h0b)
    x = _odd_out(x, modt[1], hf, hb, gate, w_out_odd[0].astype(BF16))
    _, y = _peer(x, modt[1], modt_e[1], norm2_g[1], peer_w_q[1], peer_keys[1], peer_u[1], peer_v[1], final_g, True)

    y_prompt = y[:N_PROMPT].reshape(BATCH, SEQ, D_MODEL)
    y_sample = y[N_PROMPT:].reshape(DEC_BATCH, DEC_SEQ, D_MODEL)
    new_k = ak[:N_PROMPT].reshape(BATCH, 1, SEQ, KV_HEADS, HEAD_DIM)
    new_v = av[:N_PROMPT].reshape(BATCH, 1, SEQ, KV_HEADS, HEAD_DIM)
    new_hf = jnp.swapaxes(sf_t[:BATCH], -1, -2)[:, None]
    new_hb = jnp.swapaxes(sb_t[:BATCH], -1, -2)[:, None]
    new_lf = hf[:N_PROMPT].reshape(BATCH, SEQ, D_RNN)[:, -1][:, None, :]
    new_lb = hb[:N_PROMPT].reshape(BATCH, SEQ, D_RNN)[:, 0][:, None, :]
    return (y_prompt, y_sample, new_k, new_v, new_hf, new_hb, new_lf, new_lb)
```

```python
import functools

import numpy as np
import jax
import jax.numpy as jnp
from jax import lax
from jax.experimental import pallas as pl
from jax.experimental.pallas import tpu as pltpu

F32 = jnp.float32
BF16 = jnp.bfloat16

D_MODEL = 1024
BATCH = 16
SEQ = 256
DEPTH = 2
DEC_BATCH = 8
DEC_SEQ = 2048
PAST_LEN = 512
GRID_W = 64
EPS = 1e-6
HG_DK = 128
HG_DV = 128
HG_HEADS = 4
HG_CHUNK = 64
HG_W = HG_HEADS * HG_DK
HEAD_DIM = 64
N_HEADS = 8
KV_HEADS = 2
Q_PER_KV = 4
WINDOW = 128
ATT_BLOCK = 128
ROPE_BASE = 10000.0
ATT_W = N_HEADS * HEAD_DIM
KV_W = KV_HEADS * HEAD_DIM
D_RNN = 1024
RG_BLOCKS = 4
RG_BW = 256
RG_C = 8.0
PEER_HEADS = 8
PEER_NK = 128
PEER_DK = 256
PEER_TOPK = 16

N_PROMPT = BATCH * SEQ
N_SAMPLE = DEC_BATCH * DEC_SEQ
N_TOK = N_PROMPT + N_SAMPLE
TB = 256
N_TILES = N_TOK // TB
PROMPT_TILES = N_PROMPT // TB
TILES_PER_SAMPLE = DEC_SEQ // TB
N_COND = 1 + DEC_BATCH
COND_PAD = 16

TE = 512
PEER_GROUP = 1024
PEER_NGROUPS = PEER_NK * PEER_NK // PEER_GROUP
CHUNKS_PER_GROUP = PEER_GROUP // PEER_NK
EXPERT_SUB = 1024
G_PITCH = TE // 2 + 8
G_UNROLL = 16

LRU_CHUNK = 256

VMEM_LIMIT = 56 * 1024 * 1024


def _cparams(*sem):
    return pltpu.CompilerParams(dimension_semantics=sem, vmem_limit_bytes=VMEM_LIMIT)


def _dot(a, b):
    return jnp.dot(a, b, preferred_element_type=F32)


def _dot_nt(a, b):
    return lax.dot_general(a, b, (((1,), (1,)), ((), ())), preferred_element_type=F32)


def _dot_tn(a, b):
    return lax.dot_general(a, b, (((0,), (0,)), ((), ())), preferred_element_type=F32)


def _sigmoid(x):
    return 1.0 / (1.0 + jnp.exp(-x))


def _silu(x):
    return x * _sigmoid(x)


def _gelu_tanh(x):
    c = np.float32(np.sqrt(2.0 / np.pi))
    return x * (0.5 * (1.0 + jnp.tanh(c * (x + 0.044715 * (x * x * x)))))


def _rms_mod(x, g, mod, sh, sc):
    y = x * lax.rsqrt(jnp.mean(x * x, axis=-1, keepdims=True) + EPS) * g
    return y * (1.0 + mod[sc:sc + 1, :]) + mod[sh:sh + 1, :]


def _mod_kernel(c_ref, w_ref, b_ref, o_ref):
    c = c_ref[...]
    s = _silu(c).astype(BF16)
    o_ref[0] = _dot(s, w_ref[0].astype(BF16)) + b_ref[0]


def _modulation(cond, w_mod, b_mod):
    nb = 6 * D_MODEL // 1024
    return pl.pallas_call(
        _mod_kernel,
        grid=(DEPTH, nb),
        in_specs=[
            pl.BlockSpec((COND_PAD, D_MODEL), lambda l, n: (0, 0)),
            pl.BlockSpec((1, D_MODEL, 1024), lambda l, n: (l, 0, n)),
            pl.BlockSpec((1, 1, 1024), lambda l, n: (l, 0, n)),
        ],
        out_specs=pl.BlockSpec((1, COND_PAD, 1024), lambda l, n: (l, 0, n)),
        out_shape=jax.ShapeDtypeStruct((DEPTH, COND_PAD, 6 * D_MODEL), F32),
        compiler_params=_cparams("parallel", "parallel"),
        name="modulation",
    )(cond, w_mod, b_mod.reshape(DEPTH, 1, 6 * D_MODEL))


def _in_proj_kernel(x_ref, mod_ref, g_ref, w_ref, *o_refs, widths):
    h = _rms_mod(x_ref[...], g_ref[...], mod_ref[0], 0, 1)
    z = _dot(h.astype(BF16), w_ref[...])
    off = 0
    for o_ref, wd in zip(o_refs, widths):
        o_ref[...] = z[:, off:off + wd]
        off += wd


def _in_proj(x, modt, g, w_bf16, widths, name):
    n_out = sum(widths)
    return pl.pallas_call(
        functools.partial(_in_proj_kernel, widths=widths),
        grid=(N_TILES,),
        in_specs=[
            pl.BlockSpec((TB, D_MODEL), lambda i: (i, 0)),
            pl.BlockSpec((1, 6, D_MODEL), lambda i: (i, 0, 0)),
            pl.BlockSpec((1, D_MODEL), lambda i: (0, 0)),
            pl.BlockSpec((D_MODEL, n_out), lambda i: (0, 0)),
        ],
        out_specs=[pl.BlockSpec((TB, wd), lambda i: (i, 0)) for wd in widths],
        out_shape=[jax.ShapeDtypeStruct((N_TOK, wd), F32) for wd in widths],
        compiler_params=_cparams("parallel"),
        name=name,
    )(x, modt, g.reshape(1, D_MODEL), w_bf16)


def _hgrn_direction(zq, zgate, zi, lb, st_ref, fwd):
    n = HG_CHUNK
    row = lax.broadcasted_iota(jnp.int32, (n, n), 0)
    col = lax.broadcasted_iota(jnp.int32, (n, n), 1)
    keep = (col <= row) if fwd else (col >= row)
    tri = jnp.where(keep, 1.0, 0.0).astype(BF16)
    q = _silu(zq)
    f = lb + (1.0 - lb) * _sigmoid(zgate)
    g = jnp.log(f)
    k = (1.0 - lb) * _sigmoid(-zgate)
    g_hi = g.astype(BF16)
    g_lo = (g - g_hi.astype(F32)).astype(BF16)
    b = _dot(tri, g_hi) + _dot(tri, g_lo)
    mid = n // 2 - 1 if fwd else n // 2
    last = n - 1 if fwd else 0
    bm = b[mid:mid + 1, :]
    bl = b[last:last + 1, :]
    qi = (q * jnp.exp(b - bm)).astype(BF16)
    ki = (k * jnp.exp(bm - b)).astype(BF16)
    qs = (q * jnp.exp(b)).astype(BF16)
    ks = (k * jnp.exp(bl - b)).astype(BF16)
    vb = zi.astype(BF16)
    decay = jnp.exp(bl)
    outs = []
    for h in range(HG_HEADS):
        sl = slice(h * HG_DK, (h + 1) * HG_DK)
        a = jnp.where(keep, _dot_nt(qi[:, sl], ki[:, sl]), 0.0)
        st = st_ref[h]
        o = _dot(a.astype(BF16), vb[:, sl]) + _dot_nt(qs[:, sl], st.astype(BF16))
        st_ref[h] = decay[:, sl] * st + _dot_tn(vb[:, sl], ks[:, sl])
        outs.append(o)
    return jnp.concatenate(outs, axis=1)


def _hgrn_kernel(tbl_ref, zqf_ref, zff_ref, zif_ref, zqb_ref, zbb_ref, zib_ref, lbf_ref, lbb_ref,
                 s0f_ref, s0b_ref, of_ref, ob_ref, sf_ref, sb_ref, stf, stb):
    s = pl.program_id(0)

    @pl.when(tbl_ref[4, s] == 1)
    def _():
        stf[...] = s0f_ref[0]
        stb[...] = s0b_ref[0]

    of_ref[...] = _hgrn_direction(zqf_ref[...], zff_ref[...], zif_ref[...], lbf_ref[...], stf, True)
    ob_ref[...] = _hgrn_direction(zqb_ref[...], zbb_ref[...], zib_ref[...], lbb_ref[...], stb, False)

    @pl.when(tbl_ref[5, s] == 1)
    def _():
        sf_ref[0] = stf[...]
        sb_ref[0] = stb[...]


def _scan_table(chunk):
    rows = []
    blk = 0
    for (nseq, length, has_init) in ((BATCH, SEQ, False), (DEC_BATCH, DEC_SEQ, True)):
        nc = length // chunk
        for b in range(nseq):
            sid = b if not has_init else BATCH + b
            for i in range(nc):
                rows.append((blk + i, blk + nc - 1 - i, (1 + b) if has_init else 0, sid,
                             int(i == 0), int(i == nc - 1)))
            blk += nc
    return np.asarray(rows, np.int32).T.copy()


def _hgrn_scan(zq, zf, zb, zi, lbf, lbb, s0f_t, s0b_t):
    tbl = _scan_table(HG_CHUNK)
    n_steps = tbl.shape[1]
    n_seq = BATCH + DEC_BATCH
    blk = lambda r: pl.BlockSpec((HG_CHUNK, HG_W), lambda s, t: (t[r, s], 0))
    vec = pl.BlockSpec((1, HG_W), lambda s, t: (0, 0))
    st_in = pl.BlockSpec((1, HG_HEADS, HG_DV, HG_DK), lambda s, t: (t[2, s], 0, 0, 0))
    st_out = pl.BlockSpec((1, HG_HEADS, HG_DV, HG_DK), lambda s, t: (t[3, s], 0, 0, 0))
    grid_spec = pltpu.PrefetchScalarGridSpec(
        num_scalar_prefetch=1,
        grid=(n_steps,),
        in_specs=[blk(0), blk(0), blk(0), blk(1), blk(1), blk(1), vec, vec, st_in, st_in],
        out_specs=[blk(0), blk(1), st_out, st_out],
        scratch_shapes=[pltpu.VMEM((HG_HEADS, HG_DV, HG_DK), F32),
                        pltpu.VMEM((HG_HEADS, HG_DV, HG_DK), F32)],
    )
    st_shape = jax.ShapeDtypeStruct((n_seq, HG_HEADS, HG_DV, HG_DK), F32)
    return pl.pallas_call(
        _hgrn_kernel,
        grid_spec=grid_spec,
        out_shape=[jax.ShapeDtypeStruct((N_TOK, HG_W), F32), jax.ShapeDtypeStruct((N_TOK, HG_W), F32),
                   st_shape, st_shape],
        compiler_params=_cparams("arbitrary"),
        name="hgrn_scan",
    )(jnp.asarray(tbl), zq, zf, zi, zq, zb, zi, lbf, lbb, s0f_t, s0b_t)


def _rope(x, cos, sgn_sin):
    w = x.shape[1]
    lane = lax.broadcasted_iota(jnp.int32, x.shape, 1)
    partner = jnp.where((lane & 31) < 16, pltpu.roll(x, w - 16, 1), pltpu.roll(x, 16, 1))
    return x * cos + partner * sgn_sin


def _dup_kv_half(x, kv):
    lane = lax.broadcasted_iota(jnp.int32, x.shape, 1)
    own = (lane < HEAD_DIM) if kv == 0 else (lane >= HEAD_DIM)
    xs = jnp.where(own, x, 0.0)
    return xs + pltpu.roll(xs, HEAD_DIM, 1)


def _attend(q, key_parts, sink_ref, o_ref):
    lane = lax.broadcasted_iota(jnp.int32, (ATT_BLOCK, 128), 1)
    lo = lane < HEAD_DIM
    for kv in range(KV_HEADS):
        qs = []
        for gq in range(Q_PER_KV):
            hd = kv * Q_PER_KV + gq
            pair = q[:, (hd // 2) * 128:(hd // 2 + 1) * 128]
            qs.append(jnp.where(lo if hd % 2 == 0 else ~lo, pair, 0.0))
        qst = jnp.concatenate(qs, axis=0).astype(BF16)
        logits = []
        vals = []
        for (k, v, mask) in key_parts:
            kd = _dup_kv_half(k, kv).astype(BF16)
            vals.append(_dup_kv_half(v, kv).astype(BF16))
            s = _dot_nt(qst, kd)
            if mask is not None:
                s = jnp.where(mask, s, -1e30)
            logits.append(s)
        m = logits[0].max(axis=-1, keepdims=True)
        for s in logits[1:]:
            m = jnp.maximum(m, s.max(axis=-1, keepdims=True))
        sink_col = jnp.concatenate(
            [jnp.full((ATT_BLOCK, 1), sink_ref[kv * Q_PER_KV + gq], F32) for gq in range(Q_PER_KV)], axis=0)
        m = jnp.maximum(m, sink_col)
        den = jnp.exp(sink_col - m)
        acc = jnp.zeros((Q_PER_KV * ATT_BLOCK, 128), F32)
        for s, v in zip(logits, vals):
            p = jnp.exp(s - m)
            den = den + p.sum(axis=-1, keepdims=True)
            acc = acc + _dot(p.astype(BF16), v)
        out = acc / den
        for pr in range(Q_PER_KV // 2):
            r0 = out[(2 * pr) * ATT_BLOCK:(2 * pr + 1) * ATT_BLOCK]
            r1 = out[(2 * pr + 1) * ATT_BLOCK:(2 * pr + 2) * ATT_BLOCK]
            c0 = (kv * Q_PER_KV // 2 + pr) * 128
            o_ref[:, c0:c0 + 128] = jnp.where(lo, r0, r1)


def _attn_prompt_kernel(sink_ref, q_ref, k_ref, v_ref, o_ref):
    q = q_ref[...] * np.float32(HEAD_DIM ** -0.5)
    _attend(q, [(k_ref[...], v_ref[...], None)], sink_ref, o_ref)


def _attn_latent_kernel(sink_ref, q_ref, kp_ref, kc_ref, kn_ref, vp_ref, vc_ref, vn_ref,
                        ck_ref, cv_ref, cos_ref, sin_ref, o_ref):
    j = pl.program_id(1)
    nq = pl.num_programs(1)
    qstart = pl.multiple_of((j + 1) * ATT_BLOCK, ATT_BLOCK)
    kstart = pl.multiple_of(j * ATT_BLOCK, ATT_BLOCK)
    cq = cos_ref[pl.ds(qstart, ATT_BLOCK), :]
    sq = sin_ref[pl.ds(qstart, ATT_BLOCK), :]
    q = _rope(q_ref[...], jnp.concatenate([cq] * 4, axis=1), jnp.concatenate([sq] * 4, axis=1))
    q = q * np.float32(HEAD_DIM ** -0.5)
    k_loc = jnp.concatenate([kp_ref[...], kc_ref[...], kn_ref[...]], axis=0)
    v_loc = jnp.concatenate([vp_ref[...], vc_ref[...], vn_ref[...]], axis=0)
    k_loc = _rope(k_loc, cos_ref[pl.ds(kstart, 3 * ATT_BLOCK), :], sin_ref[pl.ds(kstart, 3 * ATT_BLOCK), :])
    r = lax.broadcasted_iota(jnp.int32, (Q_PER_KV * ATT_BLOCK, 3 * ATT_BLOCK), 0) & (ATT_BLOCK - 1)
    c = lax.broadcasted_iota(jnp.int32, (Q_PER_KV * ATT_BLOCK, 3 * ATT_BLOCK), 1)
    rel = c - ATT_BLOCK - r
    c_lo = jnp.where(j > 0, 0, ATT_BLOCK)
    c_hi = jnp.where(j < nq - 1, 3 * ATT_BLOCK, 2 * ATT_BLOCK)
    ok = (jnp.abs(rel) <= WINDOW) & (c >= c_lo) & (c < c_hi)
    _attend(q, [(ck_ref[0], cv_ref[0], None), (k_loc, v_loc, ok)], sink_ref, o_ref)


def _rope_tables():
    pos = np.arange(-ATT_BLOCK, DEC_SEQ + ATT_BLOCK)
    nf = HEAD_DIM // 4
    inv = (1.0 / (ROPE_BASE ** (np.arange(nf, dtype=np.float32) / nf))).astype(np.float32)
    rows = (pos // GRID_W).astype(np.float32)
    cols = (pos % GRID_W).astype(np.float32)
    ar = rows[:, None] * inv[None, :]
    ac = cols[:, None] * inv[None, :]
    cos64 = np.concatenate([np.cos(ar), np.cos(ar), np.cos(ac), np.cos(ac)], axis=1)
    sin64 = np.concatenate([-np.sin(ar), np.sin(ar), -np.sin(ac), np.sin(ac)], axis=1)
    cos = np.concatenate([cos64, cos64], axis=1).astype(np.float32)
    sin = np.concatenate([sin64, sin64], axis=1).astype(np.float32)
    return jnp.asarray(cos), jnp.asarray(sin)


def _attention(aq, ak, av, ck, cv, sink):
    smem = pl.BlockSpec(memory_space=pltpu.SMEM)
    nqp = SEQ // ATT_BLOCK
    o_prompt = pl.pallas_call(
        _attn_prompt_kernel,
        grid=(BATCH, nqp),
        in_specs=[
            smem,
            pl.BlockSpec((ATT_BLOCK, ATT_W), lambda b, j: (b * nqp + j, 0)),
            pl.BlockSpec((SEQ, KV_W), lambda b, j: (b, 0)),
            pl.BlockSpec((SEQ, KV_W), lambda b, j: (b, 0)),
        ],
        out_specs=pl.BlockSpec((ATT_BLOCK, ATT_W), lambda b, j: (b * nqp + j, 0)),
        out_shape=jax.ShapeDtypeStruct((N_PROMPT, ATT_W), F32),
        compiler_params=_cparams("parallel", "parallel"),
        name="attn_prompt",
    )(sink, aq, ak, av)

    nq = DEC_SEQ // ATT_BLOCK
    base = N_PROMPT // ATT_BLOCK
    cos, sin = _rope_tables()
    kv_blk = lambda d: pl.BlockSpec(
        (ATT_BLOCK, KV_W), lambda b, j: (base + b * nq + jnp.clip(j + d, 0, nq - 1), 0))
    tab = pl.BlockSpec((DEC_SEQ + 2 * ATT_BLOCK, 128), lambda b, j: (0, 0))
    o_latent = pl.pallas_call(
        _attn_latent_kernel,
        grid=(DEC_BATCH, nq),
        in_specs=[
            smem,
            pl.BlockSpec((ATT_BLOCK, ATT_W), lambda b, j: (base + b * nq + j, 0)),
            kv_blk(-1), kv_blk(0), kv_blk(1), kv_blk(-1), kv_blk(0), kv_blk(1),
            pl.BlockSpec((1, PAST_LEN, KV_W), lambda b, j: (b, 0, 0)),
            pl.BlockSpec((1, PAST_LEN, KV_W), lambda b, j: (b, 0, 0)),
            tab, tab,
        ],
        out_specs=pl.BlockSpec((ATT_BLOCK, ATT_W), lambda b, j: (b * nq + j, 0)),
        out_shape=jax.ShapeDtypeStruct((N_SAMPLE, ATT_W), F32),
        compiler_params=_cparams("parallel", "parallel"),
        name="attn_latent",
    )(sink, aq, ak, ak, ak, av, av, av, ck, cv, cos, sin)
    return jnp.concatenate([o_prompt, o_latent], axis=0)


def _even_out_kernel(x_ref, mod_ref, of_ref, ob_ref, zg_ref, oatt_ref, gn_ref, w_ref, o_ref):
    o = of_ref[...] + ob_ref[...]
    parts = []
    for h in range(HG_HEADS):
        oh = o[:, h * HG_DV:(h + 1) * HG_DV]
        parts.append(oh * lax.rsqrt(jnp.mean(oh * oh, axis=-1, keepdims=True) + EPS))
    o_hg = jnp.concatenate(parts, axis=1) * gn_ref[...] * _silu(zg_ref[...])
    y = _dot(o_hg.astype(BF16), w_ref[0:HG_W, :]) + _dot(oatt_ref[...].astype(BF16), w_ref[HG_W:, :])
    o_ref[...] = x_ref[...] + mod_ref[0][2:3, :] * y


def _even_out(x, modt, o_f, o_b, zg, oatt, gn_g, w_bf16):
    tile = lambda w: pl.BlockSpec((TB, w), lambda i: (i, 0))
    return pl.pallas_call(
        _even_out_kernel,
        grid=(N_TILES,),
        in_specs=[
            tile(D_MODEL),
            pl.BlockSpec((1, 6, D_MODEL), lambda i: (i, 0, 0)),
            tile(HG_W), tile(HG_W), tile(HG_W), tile(ATT_W),
            pl.BlockSpec((1, HG_W), lambda i: (0, 0)),
            pl.BlockSpec((HG_W + ATT_W, D_MODEL), lambda i: (0, 0)),
        ],
        out_specs=tile(D_MODEL),
        out_shape=jax.ShapeDtypeStruct((N_TOK, D_MODEL), F32),
        compiler_params=_cparams("parallel"),
        name="even_out",
    )(x, modt, o_f, o_b, zg, oatt, gn_g.reshape(1, HG_W), w_bf16)


def _conv_kernel(xc_ref, xp_ref, xn_ref, w_ref, b_ref, o_ref):
    i = pl.program_id(0)
    in_sample = i >= PROMPT_TILES
    pos = (i - PROMPT_TILES) % TILES_PER_SAMPLE
    has_prev = in_sample & (pos != 0)
    has_next = in_sample & (pos != TILES_PER_SAMPLE - 1)
    prev = jnp.where(has_prev, xp_ref[...], 0.0)
    nxt = jnp.where(has_next, xn_ref[...], 0.0)
    ext = jnp.concatenate([prev, xc_ref[...], nxt], axis=0)
    n = TB + 16
    w = w_ref[...]
    acc = ext[8:8 + TB] * w[2:3, :] + b_ref[...]
    acc = acc + pltpu.roll(ext, 2, 0)[8:8 + TB] * w[0:1, :]
    acc = acc + pltpu.roll(ext, 1, 0)[8:8 + TB] * w[1:2, :]
    acc = acc + pltpu.roll(ext, n - 1, 0)[8:8 + TB] * w[3:4, :]
    o_ref[...] = acc


def _dwconv(xr, cw, cb):
    r8 = TB // 8
    last8 = N_TOK // 8 - 1
    return pl.pallas_call(
        _conv_kernel,
        grid=(N_TILES,),
        in_specs=[
            pl.BlockSpec((TB, D_RNN), lambda i: (i, 0)),
            pl.BlockSpec((8, D_RNN), lambda i: (jnp.maximum(i * r8 - 1, 0), 0)),
            pl.BlockSpec((8, D_RNN), lambda i: (jnp.minimum((i + 1) * r8, last8), 0)),
            pl.BlockSpec((4, D_RNN), lambda i: (0, 0)),
            pl.BlockSpec((1, D_RNN), lambda i: (0, 0)),
        ],
        out_specs=pl.BlockSpec((TB, D_RNN), lambda i: (i, 0)),
        out_shape=jax.ShapeDtypeStruct((N_TOK, D_RNN), F32),
        compiler_params=_cparams("parallel"),
        name="dwconv",
    )(xr, xr, xr, cw, cb.reshape(1, D_RNN))


def _log1p(y):
    w = 1.0 + y
    return jnp.where(w == 1.0, y, jnp.log(w) * (y / (w - 1.0)))


def _expm1(x):
    e = jnp.exp(x)
    return jnp.where(e == 1.0, x, (e - 1.0) * (x / jnp.log(e)))


def _lru_direction(xc, wa_ref, ba, wx_ref, bx, lam, h_ref, fwd):
    n = LRU_CHUNK
    xb = xc.astype(BF16)
    ra, ri = [], []
    for blk in range(RG_BLOCKS):
        sl = slice(blk * RG_BW, (blk + 1) * RG_BW)
        ra.append(_dot(xb[:, sl], wa_ref[blk]))
        ri.append(_dot(xb[:, sl], wx_ref[blk]))
    r = _sigmoid(jnp.concatenate(ra, axis=1) + ba)
    gi = _sigmoid(jnp.concatenate(ri, axis=1) + bx)
    neg = -lam
    softplus = jnp.maximum(neg, 0.0) + _log1p(jnp.exp(-jnp.abs(neg)))
    log_a = (-RG_C * softplus) * r
    a = jnp.exp(log_a)
    u = jnp.sqrt(-_expm1(2.0 * log_a)) * (gi * xc)
    row = lax.broadcasted_iota(jnp.int32, (n, D_RNN), 0)
    d = 1
    while d < n:
        if fwd:
            a_s = pltpu.roll(a, d, 0)
            u_s = pltpu.roll(u, d, 0)
            live = row >= d
        else:
            a_s = pltpu.roll(a, n - d, 0)
            u_s = pltpu.roll(u, n - d, 0)
            live = row < n - d
        u = jnp.where(live, a * u_s + u, u)
        a = jnp.where(live, a * a_s, a)
        d *= 2
    h = u + a * h_ref[...]
    h_ref[...] = h[n - 1:n, :] if fwd else h[0:1, :]
    return h


def _lru_kernel(tbl_ref, xf_ref, xb_ref, wa_ref, ba_ref, wx_ref, bx_ref, lam_ref, h0f_ref, h0b_ref,
                hf_ref, hb_ref, hf_s, hb_s):
    s = pl.program_id(0)

    @pl.when(tbl_ref[4, s] == 1)
    def _():
        hf_s[...] = h0f_ref[0]
        hb_s[...] = h0b_ref[0]

    hf_ref[...] = _lru_direction(xf_ref[...], wa_ref.at[0], ba_ref[0:1, :], wx_ref.at[0], bx_ref[0:1, :],
                                 lam_ref[0:1, :], hf_s, True)
    hb_ref[...] = _lru_direction(xb_ref[...], wa_ref.at[1], ba_ref[1:2, :], wx_ref.at[1], bx_ref[1:2, :],
                                 lam_ref[1:2, :], hb_s, False)


def _lru_scan(xc, w_a_bf16, b_a, w_x_bf16, b_x, lam, h0f_all, h0b_all):
    tbl = _scan_table(LRU_CHUNK)
    n_steps = tbl.shape[1]
    blk = lambda r: pl.BlockSpec((LRU_CHUNK, D_RNN), lambda s, t: (t[r, s], 0))
    wspec = pl.BlockSpec((2, RG_BLOCKS, RG_BW, RG_BW), lambda s, t: (0, 0, 0, 0))
    vec2 = pl.BlockSpec((2, D_RNN), lambda s, t: (0, 0))
    h0 = pl.BlockSpec((1, 1, D_RNN), lambda s, t: (t[2, s], 0, 0))
    grid_spec = pltpu.PrefetchScalarGridSpec(
        num_scalar_prefetch=1,
        grid=(n_steps,),
        in_specs=[blk(0), blk(1), wspec, vec2, wspec, vec2, vec2, h0, h0],
        out_specs=[blk(0), blk(1)],
        scratch_shapes=[pltpu.VMEM((1, D_RNN), F32), pltpu.VMEM((1, D_RNN), F32)],
    )
    return pl.pallas_call(
        _lru_kernel,
        grid_spec=grid_spec,
        out_shape=[jax.ShapeDtypeStruct((N_TOK, D_RNN), F32), jax.ShapeDtypeStruct((N_TOK, D_RNN), F32)],
        compiler_params=_cparams("arbitrary"),
        name="lru_scan",
    )(jnp.asarray(tbl), xc, xc, w_a_bf16, b_a, w_x_bf16, b_x, lam, h0f_all, h0b_all)


def _odd_out_kernel(x_ref, mod_ref, hf_ref, hb_ref, gate_ref, w_ref, o_ref):
    y = (hf_ref[...] + hb_ref[...]) * _gelu_tanh(gate_ref[...])
    o_ref[...] = x_ref[...] + mod_ref[0][2:3, :] * _dot(y.astype(BF16), w_ref[...])


def _odd_out(x, modt, hf, hb, gate, w_bf16):
    tile = pl.BlockSpec((TB, D_MODEL), lambda i: (i, 0))
    return pl.pallas_call(
        _odd_out_kernel,
        grid=(N_TILES,),
        in_specs=[tile, pl.BlockSpec((1, 6, D_MODEL), lambda i: (i, 0, 0)), tile, tile, tile,
                  pl.BlockSpec((D_RNN, D_MODEL), lambda i: (0, 0))],
        out_specs=tile,
        out_shape=jax.ShapeDtypeStruct((N_TOK, D_MODEL), F32),
        compiler_params=_cparams("parallel"),
        name="odd_out",
    )(x, modt, hf, hb, gate, w_bf16)


SUBLANES = 8


def _order_key(x):
    b = lax.bitcast_convert_type(x, jnp.int32)
    return b ^ ((b >> 31) & 0x7FFFFFFF)


def _key_value(k):
    return lax.bitcast_convert_type(k ^ ((k >> 31) & 0x7FFFFFFF), F32)


def _precedes(ky, ry, kx, rx):
    return (ky + jnp.where(ry < rx, 1, 0)) > kx


def _merge_exchange(n):
    pairs = []
    t = max(1, (n - 1).bit_length())
    p = 1 << (t - 1)
    while p > 0:
        q, r, d = 1 << (t - 1), 0, p
        while d > 0:
            pairs.extend((i, i + d) for i in range(n - d) if (i & p) == r)
            d, q, r = q - p, q >> 1, p
        p >>= 1
    return pairs


def _top_rows(values, count, emit):
    n = len(values)
    sub = lax.broadcasted_iota(jnp.int32, (SUBLANES, TB), 0)
    keys = [_order_key(v) for v in values]
    rows = [sub + SUBLANES * i for i in range(n)]
    for i, j in _merge_exchange(n):
        swap = _precedes(keys[j], rows[j], keys[i], rows[i])
        keys[i], keys[j] = jnp.where(swap, keys[j], keys[i]), jnp.where(swap, keys[i], keys[j])
        rows[i], rows[j] = jnp.where(swap, rows[j], rows[i]), jnp.where(swap, rows[i], rows[j])
    lowest = jnp.full((SUBLANES, TB), jnp.iinfo(jnp.int32).min + 1, jnp.int32)
    for step in range(count):
        k, r = keys[0], rows[0]
        for shift in (4, 2, 1):
            k2, r2 = pltpu.roll(k, shift, 0), pltpu.roll(r, shift, 0)
            take = _precedes(k2, r2, k, r)
            k, r = jnp.where(take, k2, k), jnp.where(take, r2, r)
        emit(step, _key_value(k[0:1, :]), r[0:1, :])
        depth = min(n - 1, count - 1 - step)
        column = rows[0] == r
        for d in range(depth):
            keys[d] = jnp.where(column, keys[d + 1], keys[d])
            rows[d] = jnp.where(column, rows[d + 1], rows[d])
        if depth == n - 1:
            keys[n - 1] = jnp.where(column, lowest, keys[n - 1])


def _router_kernel(x_ref, mod_ref, g_ref, wq_ref, keys_ref, h_ref, a_ref, b_ref, gate_ref,
                   q_s, v1_s, i1_s, v2_s, i2_s, sc_s, a_s, b_s):
    h = _rms_mod(x_ref[...], g_ref[...], mod_ref[0], 3, 4).astype(BF16)
    h_ref[...] = h
    q_s[...] = _dot(h, wq_ref[...])
    lax.fori_loop(0, PEER_HEADS, functools.partial(
        _route_head, keys_ref, a_ref, b_ref, gate_ref, q_s, v1_s, i1_s, v2_s, i2_s, sc_s, a_s, b_s), 0)


def _route_head(keys_ref, a_ref, b_ref, gate_ref, q_s, v1_s, i1_s, v2_s, i2_s, sc_s, a_s, b_s, hd, carry):
    for half, (v_s, i_s) in enumerate(((v1_s, i1_s), (v2_s, i2_s))):
        col = pl.multiple_of(hd * PEER_DK + half * (PEER_DK // 2), PEER_DK // 2)
        qh = q_s[:, pl.ds(col, PEER_DK // 2)].astype(BF16)
        st = _dot_nt(keys_ref[half, hd], qh)

        def emit(step, value, row, v_s=v_s, i_s=i_s):
            v_s[step:step + 1, :] = value
            i_s[step:step + 1, :] = row

        _top_rows([st[SUBLANES * i:SUBLANES * (i + 1), :] for i in range(PEER_NK // SUBLANES)], PEER_TOPK, emit)

    v1 = v1_s[...]
    v2 = v2_s[...]
    sub8 = lax.broadcasted_iota(jnp.int32, (8, TB), 0)
    pieces = [v1[0:1, :] + v2[0:8, :], v1[0:1, :] + v2[8:16, :]]
    for p in range(1, 8):
        pieces.append(jnp.where(sub8 < PEER_TOPK // (p + 1), v1[p:p + 1, :] + v2[0:8, :], -jnp.inf))
    pieces.append(v1[8:16, :] + v2[0:1, :])
    sub = lax.broadcasted_iota(jnp.int32, (PEER_TOPK, TB), 0)
    i1 = i1_s[...]
    i2 = i2_s[...]

    def emit2(step, value, pos):
        p = jnp.where(pos < 72, jnp.maximum((pos - 8) >> 3, 0), pos - 64)
        r = jnp.where(pos < 16, pos, jnp.where(pos < 72, pos & 7, 0))
        sc_s[step:step + 1, :] = value
        a_s[step:step + 1, :] = jnp.where(sub == p, i1, 0).sum(axis=0, keepdims=True)
        b_s[step:step + 1, :] = jnp.where(sub == r, i2, 0).sum(axis=0, keepdims=True)

    _top_rows(pieces, PEER_TOPK, emit2)
    sc = sc_s[...]
    e = jnp.exp(sc - sc[0:1, :])
    out_rows = pl.ds(pl.multiple_of(hd * PEER_TOPK, PEER_TOPK), PEER_TOPK)
    gate_ref[0, out_rows, :] = e / e.sum(axis=0, keepdims=True)
    a_ref[0, out_rows, :] = a_s[...].astype(F32)
    b_ref[0, out_rows, :] = b_s[...].astype(F32)
    return carry


def _router(x, modt, g, wq_bf16, keys_bf16):
    kt = pl.BlockSpec((1, PEER_HEADS * PEER_TOPK, TB), lambda i: (i, 0, 0))
    kshape = jax.ShapeDtypeStruct((N_TILES, PEER_HEADS * PEER_TOPK, TB), F32)
    f32s = lambda: pltpu.VMEM((PEER_TOPK, TB), F32)
    i32s = lambda: pltpu.VMEM((PEER_TOPK, TB), jnp.int32)
    return pl.pallas_call(
        _router_kernel,
        grid=(N_TILES,),
        in_specs=[
            pl.BlockSpec((TB, D_MODEL), lambda i: (i, 0)),
            pl.BlockSpec((1, 6, D_MODEL), lambda i: (i, 0, 0)),
            pl.BlockSpec((1, D_MODEL), lambda i: (0, 0)),
            pl.BlockSpec((D_MODEL, PEER_HEADS * PEER_DK), lambda i: (0, 0)),
            pl.BlockSpec((2, PEER_HEADS, PEER_NK, PEER_DK // 2), lambda i: (0, 0, 0, 0)),
        ],
        out_specs=[pl.BlockSpec((TB, D_MODEL), lambda i: (i, 0)), kt, kt, kt],
        out_shape=[jax.ShapeDtypeStruct((N_TOK, D_MODEL), BF16), kshape, kshape, kshape],
        scratch_shapes=[pltpu.VMEM((TB, PEER_HEADS * PEER_DK), F32),
                        f32s(), i32s(), f32s(), i32s(), f32s(), i32s(), i32s()],
        compiler_params=_cparams("parallel"),
        name="peer_router",
    )(x, modt, g.reshape(1, D_MODEL), wq_bf16, keys_bf16)


def _expert_kernel(h_ref, x_ref, mod_ref, at_ref, bt_ref, gt_ref, u_ref, v_ref, fg_ref, o_ref, *rest, final):
    if final:
        y_ref, g_s, acc_s, a_s, b_s, w_s = rest
    else:
        g_s, acc_s, a_s, b_s, w_s = rest
    grp = pl.program_id(1)

    @pl.when(grp == 0)
    def _():
        acc_s[...] = jnp.zeros_like(acc_s)
        for part in range(TE // TB):
            a_s[part * TB:(part + 1) * TB, :] = at_ref[part].T
            b_s[part * TB:(part + 1) * TB, :] = bt_ref[part].T
            w_s[part * TB:(part + 1) * TB, :] = gt_ref[part].T
        sub = lax.broadcasted_iota(jnp.int32, (PEER_NK, PEER_NK), 0).astype(F32)

        def gate_grid(t):
            a_row = a_s[pl.ds(t, 1), :]
            b_row = b_s[pl.ds(t, 1), :]
            w_row = w_s[pl.ds(t, 1), :]
            m_t = jnp.where(a_row == sub, 1.0, 0.0).astype(BF16)
            r_t = jnp.where(b_row == sub, w_row, 0.0).astype(BF16)
            g = _dot_nt(m_t, r_t).astype(BF16).astype(F32)
            return lax.bitcast_convert_type(g, jnp.uint32)

        def body(t, carry):
            word = (gate_grid(t) >> 16) | (gate_grid(t + TE // 2) & jnp.uint32(0xFFFF0000))
            g_s[pl.ds(t, PEER_NK, stride=G_PITCH), :] = word
            return carry

        lax.fori_loop(0, TE // 2, body, 0, unroll=G_UNROLL)

    h = h_ref[...]
    acc = acc_s[...]
    for sb in range(PEER_GROUP // EXPERT_SUB):
        e0 = sb * EXPERT_SUB
        hmat = _dot_nt(h, u_ref[e0:e0 + EXPERT_SUB, :])
        parts = []
        for j in range(EXPERT_SUB // PEER_NK):
            chunk = grp * CHUNKS_PER_GROUP + sb * (EXPERT_SUB // PEER_NK) + j
            word = g_s[pl.ds(pl.multiple_of(chunk * G_PITCH, 8), TE // 2), :]
            gates = jnp.concatenate(
                [lax.bitcast_convert_type(word << 16, F32),
                 lax.bitcast_convert_type(word & jnp.uint32(0xFFFF0000), F32)], axis=0)
            parts.append((_gelu_tanh(hmat[:, j * PEER_NK:(j + 1) * PEER_NK]) * gates).astype(BF16))
        acc = acc + _dot(jnp.concatenate(parts, axis=1), v_ref[e0:e0 + EXPERT_SUB, :])
    acc_s[...] = acc

    @pl.when(grp == PEER_NGROUPS - 1)
    def _():
        xo = x_ref[...] + mod_ref[0][5:6, :] * acc_s[...]
        o_ref[...] = xo
        if final:
            y_ref[...] = xo * lax.rsqrt(jnp.mean(xo * xo, axis=-1, keepdims=True) + EPS) * fg_ref[...]


def _experts(h_bf16, x, modt_e, a_t, b_t, g_t, u_bf16, v_bf16, final_g, final):
    tile = pl.BlockSpec((TE, D_MODEL), lambda i, g: (i, 0))
    kt = pl.BlockSpec((TE // TB, PEER_HEADS * PEER_TOPK, TB), lambda i, g: (i, 0, 0))
    wblk = pl.BlockSpec((PEER_GROUP, D_MODEL), lambda i, g: (g, 0))
    xs = jax.ShapeDtypeStruct((N_TOK, D_MODEL), F32)
    sel = lambda: pltpu.VMEM((TE, PEER_HEADS * PEER_TOPK), F32)
    return pl.pallas_call(
        functools.partial(_expert_kernel, final=final),
        grid=(N_TOK // TE, PEER_NGROUPS),
        in_specs=[tile, tile, pl.BlockSpec((1, 6, D_MODEL), lambda i, g: (i, 0, 0)), kt, kt, kt, wblk, wblk,
                  pl.BlockSpec((1, D_MODEL), lambda i, g: (0, 0))],
        out_specs=[tile, tile] if final else tile,
        out_shape=[xs, xs] if final else xs,
        scratch_shapes=[pltpu.VMEM((PEER_NK * G_PITCH, PEER_NK), jnp.uint32), pltpu.VMEM((TE, D_MODEL), F32),
                        sel(), sel(), sel()],
        compiler_params=_cparams("parallel", "arbitrary"),
        name="peer_experts_final" if final else "peer_experts",
    )(h_bf16, x, modt_e, a_t, b_t, g_t, u_bf16, v_bf16, final_g.reshape(1, D_MODEL))


def _peer(x, modt, modt_e, norm_g, w_q, keys, u, v, final_g, final):
    h, a_t, b_t, g_t = _router(x, modt, norm_g, w_q.astype(BF16), keys.astype(BF16))
    return _experts(h, x, modt_e, a_t, b_t, g_t, u.astype(BF16), v.astype(BF16), final_g, final)


def kernel(x_prompt, x_sample, cache_k, cache_v, state_hgrn_fwd, state_hgrn_bwd, state_lru_fwd, state_lru_bwd, c, c_ctx, norm1_g, norm2_g, w_mod, b_mod, w_in_even, w_out_even, hgrn_lb_fwd, hgrn_lb_bwd, hgrn_gnorm_g, attn_sink, w_in_odd, w_out_odd, conv_w, conv_b, rg_w_a, rg_b_a, rg_w_x, rg_b_x, rg_lambda, peer_w_q, peer_keys, peer_u, peer_v, final_g):
    x = jnp.concatenate([x_prompt.reshape(N_PROMPT, D_MODEL), x_sample.reshape(N_SAMPLE, D_MODEL)], axis=0)
    cond = jnp.concatenate([c_ctx[None, :], c, jnp.zeros((COND_PAD - N_COND, D_MODEL), F32)], axis=0)
    mod = _modulation(cond, w_mod, b_mod)
    tile_row = np.asarray([0] * PROMPT_TILES + [1 + i // TILES_PER_SAMPLE for i in range(N_TILES - PROMPT_TILES)])
    modt = [mod[l][tile_row].reshape(N_TILES, 6, D_MODEL) for l in range(DEPTH)]
    modt_e = [m[::TE // TB] for m in modt]

    lbf = jnp.cumsum(jax.nn.softmax(hgrn_lb_fwd.astype(F32), axis=0), axis=0)[0].reshape(1, HG_W)
    lbb = jnp.cumsum(jax.nn.softmax(hgrn_lb_bwd.astype(F32), axis=0), axis=0)[0].reshape(1, HG_W)
    widths = (HG_W, HG_W, HG_W, HG_W, HG_W, ATT_W, KV_W, KV_W)
    zq, zf, zb, zi, zg, aq, ak, av = _in_proj(x, modt[0], norm1_g[0], w_in_even[0].astype(BF16), widths, "even_in")
    zero_state = jnp.zeros((1, HG_HEADS, HG_DV, HG_DK), F32)
    s0f = jnp.concatenate([zero_state, jnp.swapaxes(state_hgrn_fwd[:, 0], -1, -2)], axis=0)
    s0b = jnp.concatenate([zero_state, jnp.swapaxes(state_hgrn_bwd[:, 0], -1, -2)], axis=0)
    o_f, o_b, sf_t, sb_t = _hgrn_scan(zq, zf, zb, zi, lbf, lbb, s0f, s0b)
    oatt = _attention(aq, ak, av, cache_k[:, 0].reshape(DEC_BATCH, PAST_LEN, KV_W),
                      cache_v[:, 0].reshape(DEC_BATCH, PAST_LEN, KV_W), attn_sink[0])
    x = _even_out(x, modt[0], o_f, o_b, zg, oatt, hgrn_gnorm_g[0], w_out_even[0].astype(BF16))
    x = _peer(x, modt[0], modt_e[0], norm2_g[0], peer_w_q[0], peer_keys[0], peer_u[0], peer_v[0], final_g, False)

    gate, xr = _in_proj(x, modt[1], norm1_g[1], w_in_odd[0].astype(BF16), (D_RNN, D_RNN), "odd_in")
    xc = _dwconv(xr, conv_w[0], conv_b[0])
    zero_h = jnp.zeros((1, 1, D_RNN), F32)
    h0f = jnp.concatenate([zero_h, state_lru_fwd[:, 0][:, None, :]], axis=0)
    h0b = jnp.concatenate([zero_h, state_lru_bwd[:, 0][:, None, :]], axis=0)
    hf, hb = _lru_scan(xc, rg_w_a[0].astype(BF16), rg_b_a[0], rg_w_x[0].astype(BF16), rg_b_x[0],
                       rg_lambda[0], h0f, h0b)
    x = _odd_out(x, modt[1], hf, hb, gate, w_out_odd[0].astype(BF16))
    _, y = _peer(x, modt[1], modt_e[1], norm2_g[1], peer_w_q[1], peer_keys[1], peer_u[1], peer_v[1], final_g, True)

    y_prompt = y[:N_PROMPT].reshape(BATCH, SEQ, D_MODEL)
    y_sample = y[N_PROMPT:].reshape(DEC_BATCH, DEC_SEQ, D_MODEL)
    new_k = ak[:N_PROMPT].reshape(BATCH, 1, SEQ, KV_HEADS, HEAD_DIM)
    new_v = av[:N_PROMPT].reshape(BATCH, 1, SEQ, KV_HEADS, HEAD_DIM)
    new_hf = jnp.swapaxes(sf_t[:BATCH], -1, -2)[:, None]
    new_hb = jnp.swapaxes(sb_t[:BATCH], -1, -2)[:, None]
    new_lf = hf[:N_PROMPT].reshape(BATCH, SEQ, D_RNN)[:, -1][:, None, :]
    new_lb = hb[:N_PROMPT].reshape(BATCH, SEQ, D_RNN)[:, 0][:, None, :]
    return (y_prompt, y_sample, new_k, new_v, new_hf, new_hb, new_lf, new_lb)
```

```python
import functools

import numpy as np
import jax
import jax.numpy as jnp
from jax import lax
from jax.experimental import pallas as pl
from jax.experimental.pallas import tpu as pltpu

F32 = jnp.float32
BF16 = jnp.bfloat16

D_MODEL = 1024
BATCH = 16
SEQ = 256
DEPTH = 2
DEC_BATCH = 8
DEC_SEQ = 2048
PAST_LEN = 512
GRID_W = 64
EPS = 1e-6
HG_DK = 128
HG_DV = 128
HG_HEADS = 4
HG_CHUNK = 64
HG_BLOCK = 256
HG_W = HG_HEADS * HG_DK
HEAD_DIM = 64
N_HEADS = 8
KV_HEADS = 2
Q_PER_KV = 4
WINDOW = 128
ATT_BLOCK = 128
ROPE_BASE = 10000.0
ATT_W = N_HEADS * HEAD_DIM
KV_W = KV_HEADS * HEAD_DIM
D_RNN = 1024
RG_BLOCKS = 4
RG_BW = 256
RG_C = 8.0
PEER_HEADS = 8
PEER_NK = 128
PEER_DK = 256
PEER_TOPK = 16

N_PROMPT = BATCH * SEQ
N_SAMPLE = DEC_BATCH * DEC_SEQ
N_TOK = N_PROMPT + N_SAMPLE
TB = 256
N_TILES = N_TOK // TB
PROMPT_TILES = N_PROMPT // TB
TILES_PER_SAMPLE = DEC_SEQ // TB
N_COND = 1 + DEC_BATCH
COND_PAD = 16

TE = 512
PEER_GROUP = 2048
PEER_NGROUPS = PEER_NK * PEER_NK // PEER_GROUP
CHUNKS_PER_GROUP = PEER_GROUP // PEER_NK
EXPERT_SUB = 1024
G_PITCH = TE // 2 + 8
G_UNROLL = 16

LRU_CHUNK = 256
SUBLANES = 8

VMEM_LIMIT = 56 * 1024 * 1024


def _cparams(*sem):
    return pltpu.CompilerParams(dimension_semantics=sem, vmem_limit_bytes=VMEM_LIMIT)


def _dot(a, b):
    return jnp.dot(a, b, preferred_element_type=F32)


def _dot_nt(a, b):
    return lax.dot_general(a, b, (((1,), (1,)), ((), ())), preferred_element_type=F32)


def _dot_tn(a, b):
    return lax.dot_general(a, b, (((0,), (0,)), ((), ())), preferred_element_type=F32)


def _sigmoid(x):
    return 1.0 / (1.0 + jnp.exp(-x))


def _silu(x):
    return x * _sigmoid(x)


def _gelu_tanh(x):
    c = np.float32(np.sqrt(2.0 / np.pi))
    return x * (0.5 * (1.0 + jnp.tanh(c * (x + 0.044715 * (x * x * x)))))


def _rms_mod(x, g, mod, sh, sc):
    y = x * lax.rsqrt(jnp.mean(x * x, axis=-1, keepdims=True) + EPS) * g
    return y * (1.0 + mod[sc:sc + 1, :]) + mod[sh:sh + 1, :]


def _mod_kernel(c_ref, w_ref, b_ref, o_ref):
    c = c_ref[...]
    s = _silu(c).astype(BF16)
    o_ref[0] = _dot(s, w_ref[0].astype(BF16)) + b_ref[0]


def _modulation(cond, w_mod, b_mod):
    nb = 6 * D_MODEL // 1024
    return pl.pallas_call(
        _mod_kernel,
        grid=(DEPTH, nb),
        in_specs=[
            pl.BlockSpec((COND_PAD, D_MODEL), lambda l, n: (0, 0)),
            pl.BlockSpec((1, D_MODEL, 1024), lambda l, n: (l, 0, n)),
            pl.BlockSpec((1, 1, 1024), lambda l, n: (l, 0, n)),
        ],
        out_specs=pl.BlockSpec((1, COND_PAD, 1024), lambda l, n: (l, 0, n)),
        out_shape=jax.ShapeDtypeStruct((DEPTH, COND_PAD, 6 * D_MODEL), F32),
        compiler_params=_cparams("parallel", "parallel"),
        name="modulation",
    )(cond, w_mod, b_mod.reshape(DEPTH, 1, 6 * D_MODEL))


def _in_proj_kernel(x_ref, mod_ref, g_ref, w_ref, *o_refs, widths):
    h = _rms_mod(x_ref[...], g_ref[...], mod_ref[0], 0, 1)
    z = _dot(h.astype(BF16), w_ref[...])
    off = 0
    for o_ref, wd in zip(o_refs, widths):
        o_ref[...] = z[:, off:off + wd]
        off += wd


def _in_proj(x, modt, g, w_bf16, widths, name):
    n_out = sum(widths)
    return pl.pallas_call(
        functools.partial(_in_proj_kernel, widths=widths),
        grid=(N_TILES,),
        in_specs=[
            pl.BlockSpec((TB, D_MODEL), lambda i: (i, 0)),
            pl.BlockSpec((1, 6, D_MODEL), lambda i: (i, 0, 0)),
            pl.BlockSpec((1, D_MODEL), lambda i: (0, 0)),
            pl.BlockSpec((D_MODEL, n_out), lambda i: (0, 0)),
        ],
        out_specs=[pl.BlockSpec((TB, wd), lambda i: (i, 0)) for wd in widths],
        out_shape=[jax.ShapeDtypeStruct((N_TOK, wd), F32) for wd in widths],
        compiler_params=_cparams("parallel"),
        name=name,
    )(x, modt, g.reshape(1, D_MODEL), w_bf16)


def _hgrn_direction(zq, zgate, zi, lb, states, fwd):
    n = HG_CHUNK
    row = lax.broadcasted_iota(jnp.int32, (n, n), 0)
    col = lax.broadcasted_iota(jnp.int32, (n, n), 1)
    keep = (col <= row) if fwd else (col >= row)
    tri = jnp.where(keep, 1.0, 0.0).astype(BF16)
    q = _silu(zq)
    f = lb + (1.0 - lb) * _sigmoid(zgate)
    g = jnp.log(f)
    k = (1.0 - lb) * _sigmoid(-zgate)
    g_hi = g.astype(BF16)
    g_lo = (g - g_hi.astype(F32)).astype(BF16)
    b = _dot(tri, g_hi) + _dot(tri, g_lo)
    mid = n // 2 - 1 if fwd else n // 2
    last = n - 1 if fwd else 0
    bm = b[mid:mid + 1, :]
    bl = b[last:last + 1, :]
    qi = (q * jnp.exp(b - bm)).astype(BF16)
    ki = (k * jnp.exp(bm - b)).astype(BF16)
    qs = (q * jnp.exp(b)).astype(BF16)
    ks = (k * jnp.exp(bl - b)).astype(BF16)
    vb = zi.astype(BF16)
    decay = jnp.exp(bl)
    outs = []
    new_states = []
    for h in range(HG_HEADS):
        sl = slice(h * HG_DK, (h + 1) * HG_DK)
        a = jnp.where(keep, _dot_nt(qi[:, sl], ki[:, sl]), 0.0)
        st = states[h]
        outs.append(_dot(a.astype(BF16), vb[:, sl]) + _dot_nt(qs[:, sl], st.astype(BF16)))
        new_states.append(decay[:, sl] * st + _dot_tn(vb[:, sl], ks[:, sl]))
    return jnp.concatenate(outs, axis=1), new_states


def _lower_bound(p):
    e = jnp.exp(p - p.max(axis=0, keepdims=True))
    return e[0:1, :] / e.sum(axis=0, keepdims=True)


def _hgrn_block(zq_ref, zgate_ref, zi_ref, lb, st_ref, o_ref, fwd):
    states = [st_ref[h] for h in range(HG_HEADS)]
    n_sub = HG_BLOCK // HG_CHUNK
    for c in (range(n_sub) if fwd else reversed(range(n_sub))):
        rows = slice(c * HG_CHUNK, (c + 1) * HG_CHUNK)
        o_ref[rows, :], states = _hgrn_direction(zq_ref[rows, :], zgate_ref[rows, :], zi_ref[rows, :], lb, states, fwd)
    for h in range(HG_HEADS):
        st_ref[h] = states[h]


def _hgrn_kernel(tbl_ref, zqf_ref, zff_ref, zif_ref, zqb_ref, zbb_ref, zib_ref, lbf_ref, lbb_ref,
                 s0f_ref, s0b_ref, of_ref, ob_ref, sf_ref, sb_ref, stf, stb):
    s = pl.program_id(0)

    @pl.when(tbl_ref[4, s] == 1)
    def _():
        stf[...] = s0f_ref[0]
        stb[...] = s0b_ref[0]

    _hgrn_block(zqf_ref, zff_ref, zif_ref, _lower_bound(lbf_ref[...]), stf, of_ref, True)
    _hgrn_block(zqb_ref, zbb_ref, zib_ref, _lower_bound(lbb_ref[...]), stb, ob_ref, False)

    @pl.when(tbl_ref[5, s] == 1)
    def _():
        sf_ref[0] = stf[...]
        sb_ref[0] = stb[...]


def _scan_table(chunk):
    rows = []
    blk = 0
    for (nseq, length, has_init) in ((BATCH, SEQ, False), (DEC_BATCH, DEC_SEQ, True)):
        nc = length // chunk
        for b in range(nseq):
            sid = b if not has_init else BATCH + b
            for i in range(nc):
                rows.append((blk + i, blk + nc - 1 - i, (1 + b) if has_init else 0, sid,
                             int(i == 0), int(i == nc - 1)))
            blk += nc
    return np.asarray(rows, np.int32).T.copy()


def _hgrn_scan(zq, zf, zb, zi, lbf, lbb, s0f_t, s0b_t):
    tbl = _scan_table(HG_BLOCK)
    n_steps = tbl.shape[1]
    n_seq = BATCH + DEC_BATCH
    blk = lambda r: pl.BlockSpec((HG_BLOCK, HG_W), lambda s, t: (t[r, s], 0))
    vec = pl.BlockSpec(lbf.shape, lambda s, t: (0, 0))
    st_in = pl.BlockSpec((1, HG_HEADS, HG_DV, HG_DK), lambda s, t: (t[2, s], 0, 0, 0))
    st_out = pl.BlockSpec((1, HG_HEADS, HG_DV, HG_DK), lambda s, t: (t[3, s], 0, 0, 0))
    grid_spec = pltpu.PrefetchScalarGridSpec(
        num_scalar_prefetch=1,
        grid=(n_steps,),
        in_specs=[blk(0), blk(0), blk(0), blk(1), blk(1), blk(1), vec, vec, st_in, st_in],
        out_specs=[blk(0), blk(1), st_out, st_out],
        scratch_shapes=[pltpu.VMEM((HG_HEADS, HG_DV, HG_DK), F32),
                        pltpu.VMEM((HG_HEADS, HG_DV, HG_DK), F32)],
    )
    st_shape = jax.ShapeDtypeStruct((n_seq, HG_HEADS, HG_DV, HG_DK), F32)
    return pl.pallas_call(
        _hgrn_kernel,
        grid_spec=grid_spec,
        out_shape=[jax.ShapeDtypeStruct((N_TOK, HG_W), F32), jax.ShapeDtypeStruct((N_TOK, HG_W), F32),
                   st_shape, st_shape],
        compiler_params=_cparams("arbitrary"),
        name="hgrn_scan",
    )(jnp.asarray(tbl), zq, zf, zi, zq, zb, zi, lbf, lbb, s0f_t, s0b_t)


def _rope(x, cos, sgn_sin):
    w = x.shape[1]
    lane = lax.broadcasted_iota(jnp.int32, x.shape, 1)
    partner = jnp.where((lane & 31) < 16, pltpu.roll(x, w - 16, 1), pltpu.roll(x, 16, 1))
    return x * cos + partner * sgn_sin


def _dup_kv_half(x, kv):
    lane = lax.broadcasted_iota(jnp.int32, x.shape, 1)
    own = (lane < HEAD_DIM) if kv == 0 else (lane >= HEAD_DIM)
    xs = jnp.where(own, x, 0.0)
    return xs + pltpu.roll(xs, HEAD_DIM, 1)


def _attend(q, key_parts, sink_ref, o_ref):
    lane = lax.broadcasted_iota(jnp.int32, (ATT_BLOCK, 128), 1)
    lo = lane < HEAD_DIM
    for kv in range(KV_HEADS):
        qs = []
        for gq in range(Q_PER_KV):
            hd = kv * Q_PER_KV + gq
            pair = q[:, (hd // 2) * 128:(hd // 2 + 1) * 128]
            qs.append(jnp.where(lo if hd % 2 == 0 else ~lo, pair, 0.0))
        qst = jnp.concatenate(qs, axis=0).astype(BF16)
        logits = []
        vals = []
        for (k, v, mask) in key_parts:
            kd = _dup_kv_half(k, kv).astype(BF16)
            vals.append(_dup_kv_half(v, kv).astype(BF16))
            s = _dot_nt(qst, kd)
            if mask is not None:
                s = jnp.where(mask, s, -1e30)
            logits.append(s)
        m = logits[0].max(axis=-1, keepdims=True)
        for s in logits[1:]:
            m = jnp.maximum(m, s.max(axis=-1, keepdims=True))
        sink_col = jnp.concatenate(
            [jnp.full((ATT_BLOCK, 1), sink_ref[kv * Q_PER_KV + gq], F32) for gq in range(Q_PER_KV)], axis=0)
        m = jnp.maximum(m, sink_col)
        den = jnp.exp(sink_col - m)
        acc = jnp.zeros((Q_PER_KV * ATT_BLOCK, 128), F32)
        for s, v in zip(logits, vals):
            p = jnp.exp(s - m)
            den = den + p.sum(axis=-1, keepdims=True)
            acc = acc + _dot(p.astype(BF16), v)
        out = acc / den
        for pr in range(Q_PER_KV // 2):
            r0 = out[(2 * pr) * ATT_BLOCK:(2 * pr + 1) * ATT_BLOCK]
            r1 = out[(2 * pr + 1) * ATT_BLOCK:(2 * pr + 2) * ATT_BLOCK]
            c0 = (kv * Q_PER_KV // 2 + pr) * 128
            o_ref[:, c0:c0 + 128] = jnp.where(lo, r0, r1)


def _attn_prompt_kernel(sink_ref, q_ref, k_ref, v_ref, o_ref):
    q = q_ref[...] * np.float32(HEAD_DIM ** -0.5)
    _attend(q, [(k_ref[...], v_ref[...], None)], sink_ref, o_ref)


def _attn_latent_kernel(sink_ref, q_ref, kp_ref, kc_ref, kn_ref, vp_ref, vc_ref, vn_ref,
                        ck_ref, cv_ref, cos_ref, sin_ref, o_ref):
    j = pl.program_id(1)
    nq = pl.num_programs(1)
    qstart = pl.multiple_of((j + 1) * ATT_BLOCK, ATT_BLOCK)
    kstart = pl.multiple_of(j * ATT_BLOCK, ATT_BLOCK)
    cq = cos_ref[pl.ds(qstart, ATT_BLOCK), :]
    sq = sin_ref[pl.ds(qstart, ATT_BLOCK), :]
    q = _rope(q_ref[...], jnp.concatenate([cq] * 4, axis=1), jnp.concatenate([sq] * 4, axis=1))
    q = q * np.float32(HEAD_DIM ** -0.5)
    k_loc = jnp.concatenate([kp_ref[...], kc_ref[...], kn_ref[...]], axis=0)
    v_loc = jnp.concatenate([vp_ref[...], vc_ref[...], vn_ref[...]], axis=0)
    k_loc = _rope(k_loc, cos_ref[pl.ds(kstart, 3 * ATT_BLOCK), :], sin_ref[pl.ds(kstart, 3 * ATT_BLOCK), :])
    r = lax.broadcasted_iota(jnp.int32, (Q_PER_KV * ATT_BLOCK, 3 * ATT_BLOCK), 0) & (ATT_BLOCK - 1)
    c = lax.broadcasted_iota(jnp.int32, (Q_PER_KV * ATT_BLOCK, 3 * ATT_BLOCK), 1)
    rel = c - ATT_BLOCK - r
    c_lo = jnp.where(j > 0, 0, ATT_BLOCK)
    c_hi = jnp.where(j < nq - 1, 3 * ATT_BLOCK, 2 * ATT_BLOCK)
    ok = (jnp.abs(rel) <= WINDOW) & (c >= c_lo) & (c < c_hi)
    _attend(q, [(ck_ref[0], cv_ref[0], None), (k_loc, v_loc, ok)], sink_ref, o_ref)


def _rope_tables():
    pos = np.arange(-ATT_BLOCK, DEC_SEQ + ATT_BLOCK)
    nf = HEAD_DIM // 4
    inv = (1.0 / (ROPE_BASE ** (np.arange(nf, dtype=np.float32) / nf))).astype(np.float32)
    rows = (pos // GRID_W).astype(np.float32)
    cols = (pos % GRID_W).astype(np.float32)
    ar = rows[:, None] * inv[None, :]
    ac = cols[:, None] * inv[None, :]
    cos64 = np.concatenate([np.cos(ar), np.cos(ar), np.cos(ac), np.cos(ac)], axis=1)
    sin64 = np.concatenate([-np.sin(ar), np.sin(ar), -np.sin(ac), np.sin(ac)], axis=1)
    cos = np.concatenate([cos64, cos64], axis=1).astype(np.float32)
    sin = np.concatenate([sin64, sin64], axis=1).astype(np.float32)
    return jnp.asarray(cos), jnp.asarray(sin)


def _attention(aq, ak, av, ck, cv, sink):
    smem = pl.BlockSpec(memory_space=pltpu.SMEM)
    nqp = SEQ // ATT_BLOCK
    o_prompt = pl.pallas_call(
        _attn_prompt_kernel,
        grid=(BATCH, nqp),
        in_specs=[
            smem,
            pl.BlockSpec((ATT_BLOCK, ATT_W), lambda b, j: (b * nqp + j, 0)),
            pl.BlockSpec((SEQ, KV_W), lambda b, j: (b, 0)),
            pl.BlockSpec((SEQ, KV_W), lambda b, j: (b, 0)),
        ],
        out_specs=pl.BlockSpec((ATT_BLOCK, ATT_W), lambda b, j: (b * nqp + j, 0)),
        out_shape=jax.ShapeDtypeStruct((N_PROMPT, ATT_W), F32),
        compiler_params=_cparams("parallel", "parallel"),
        name="attn_prompt",
    )(sink, aq, ak, av)

    nq = DEC_SEQ // ATT_BLOCK
    base = N_PROMPT // ATT_BLOCK
    cos, sin = _rope_tables()
    kv_blk = lambda d: pl.BlockSpec(
        (ATT_BLOCK, KV_W), lambda b, j: (base + b * nq + jnp.clip(j + d, 0, nq - 1), 0))
    tab = pl.BlockSpec((DEC_SEQ + 2 * ATT_BLOCK, 128), lambda b, j: (0, 0))
    o_latent = pl.pallas_call(
        _attn_latent_kernel,
        grid=(DEC_BATCH, nq),
        in_specs=[
            smem,
            pl.BlockSpec((ATT_BLOCK, ATT_W), lambda b, j: (base + b * nq + j, 0)),
            kv_blk(-1), kv_blk(0), kv_blk(1), kv_blk(-1), kv_blk(0), kv_blk(1),
            pl.BlockSpec((1, PAST_LEN, KV_W), lambda b, j: (b, 0, 0)),
            pl.BlockSpec((1, PAST_LEN, KV_W), lambda b, j: (b, 0, 0)),
            tab, tab,
        ],
        out_specs=pl.BlockSpec((ATT_BLOCK, ATT_W), lambda b, j: (b * nq + j, 0)),
        out_shape=jax.ShapeDtypeStruct((N_SAMPLE, ATT_W), F32),
        compiler_params=_cparams("parallel", "parallel"),
        name="attn_latent",
    )(sink, aq, ak, ak, ak, av, av, av, ck, cv, cos, sin)
    return jnp.concatenate([o_prompt, o_latent], axis=0)


def _even_out_kernel(x_ref, mod_ref, of_ref, ob_ref, zg_ref, oatt_ref, gn_ref, w_ref, o_ref):
    o = of_ref[...] + ob_ref[...]
    parts = []
    for h in range(HG_HEADS):
        oh = o[:, h * HG_DV:(h + 1) * HG_DV]
        parts.append(oh * lax.rsqrt(jnp.mean(oh * oh, axis=-1, keepdims=True) + EPS))
    o_hg = jnp.concatenate(parts, axis=1) * gn_ref[...] * _silu(zg_ref[...])
    y = _dot(o_hg.astype(BF16), w_ref[0:HG_W, :]) + _dot(oatt_ref[...].astype(BF16), w_ref[HG_W:, :])
    o_ref[...] = x_ref[...] + mod_ref[0][2:3, :] * y


def _even_out(x, modt, o_f, o_b, zg, oatt, gn_g, w_bf16):
    tile = lambda w: pl.BlockSpec((TB, w), lambda i: (i, 0))
    return pl.pallas_call(
        _even_out_kernel,
        grid=(N_TILES,),
        in_specs=[
            tile(D_MODEL),
            pl.BlockSpec((1, 6, D_MODEL), lambda i: (i, 0, 0)),
            tile(HG_W), tile(HG_W), tile(HG_W), tile(ATT_W),
            pl.BlockSpec((1, HG_W), lambda i: (0, 0)),
            pl.BlockSpec((HG_W + ATT_W, D_MODEL), lambda i: (0, 0)),
        ],
        out_specs=tile(D_MODEL),
        out_shape=jax.ShapeDtypeStruct((N_TOK, D_MODEL), F32),
        compiler_params=_cparams("parallel"),
        name="even_out",
    )(x, modt, o_f, o_b, zg, oatt, gn_g.reshape(1, HG_W), w_bf16)


def _conv_kernel(xc_ref, xp_ref, xn_ref, w_ref, b_ref, o_ref):
    i = pl.program_id(0)
    in_sample = i >= PROMPT_TILES
    pos = (i - PROMPT_TILES) % TILES_PER_SAMPLE
    has_prev = in_sample & (pos != 0)
    has_next = in_sample & (pos != TILES_PER_SAMPLE - 1)
    prev = jnp.where(has_prev, xp_ref[...], 0.0)
    nxt = jnp.where(has_next, xn_ref[...], 0.0)
    ext = jnp.concatenate([prev, xc_ref[...], nxt], axis=0)
    n = TB + 16
    w = w_ref[...]
    acc = ext[8:8 + TB] * w[2:3, :] + b_ref[...]
    acc = acc + pltpu.roll(ext, 2, 0)[8:8 + TB] * w[0:1, :]
    acc = acc + pltpu.roll(ext, 1, 0)[8:8 + TB] * w[1:2, :]
    acc = acc + pltpu.roll(ext, n - 1, 0)[8:8 + TB] * w[3:4, :]
    o_ref[...] = acc


def _dwconv(xr, cw, cb):
    r8 = TB // 8
    last8 = N_TOK // 8 - 1
    return pl.pallas_call(
        _conv_kernel,
        grid=(N_TILES,),
        in_specs=[
            pl.BlockSpec((TB, D_RNN), lambda i: (i, 0)),
            pl.BlockSpec((8, D_RNN), lambda i: (jnp.maximum(i * r8 - 1, 0), 0)),
            pl.BlockSpec((8, D_RNN), lambda i: (jnp.minimum((i + 1) * r8, last8), 0)),
            pl.BlockSpec((4, D_RNN), lambda i: (0, 0)),
            pl.BlockSpec((1, D_RNN), lambda i: (0, 0)),
        ],
        out_specs=pl.BlockSpec((TB, D_RNN), lambda i: (i, 0)),
        out_shape=jax.ShapeDtypeStruct((N_TOK, D_RNN), F32),
        compiler_params=_cparams("parallel"),
        name="dwconv",
    )(xr, xr, xr, cw, cb.reshape(1, D_RNN))


def _log1p(y):
    w = 1.0 + y
    return jnp.where(w == 1.0, y, jnp.log(w) * (y / (w - 1.0)))


def _expm1_given(x, e):
    return jnp.where(e == 1.0, x, (e - 1.0) * (x / jnp.log(e)))


def _sigmoid_tanh(x):
    return 0.5 + 0.5 * jnp.tanh(0.5 * x)


def _lru_direction(xc, wa_ref, ba, wx_ref, bx, lam, h_ref, o_ref, fwd):
    n = LRU_CHUNK
    xb = xc.astype(BF16)
    ra, ri = [], []
    for blk in range(RG_BLOCKS):
        sl = slice(blk * RG_BW, (blk + 1) * RG_BW)
        ra.append(_dot(xb[:, sl], wa_ref[blk]))
        ri.append(_dot(xb[:, sl], wx_ref[blk]))
    r = _sigmoid_tanh(jnp.concatenate(ra, axis=1) + ba)
    gi = _sigmoid_tanh(jnp.concatenate(ri, axis=1) + bx)
    neg = -lam
    softplus = jnp.maximum(neg, 0.0) + _log1p(jnp.exp(-jnp.abs(neg)))
    log_a = (-RG_C * softplus) * r
    a = jnp.exp(log_a)
    u = jnp.sqrt(-_expm1_given(2.0 * log_a, a * a)) * (gi * xc)
    sub = lax.broadcasted_iota(jnp.int32, (SUBLANES, D_RNN), 0)
    carry = h_ref[...]
    groups = range(n // SUBLANES)
    for g in (groups if fwd else reversed(groups)):
        rows = slice(g * SUBLANES, (g + 1) * SUBLANES)
        ag, ug = a[rows, :], u[rows, :]
        for d in (1, 2, 4):
            live = (sub >= d) if fwd else (sub < SUBLANES - d)
            shift = d if fwd else SUBLANES - d
            a_s = jnp.where(live, pltpu.roll(ag, shift, 0), 1.0)
            u_s = jnp.where(live, pltpu.roll(ug, shift, 0), 0.0)
            ug = ag * u_s + ug
            ag = ag * a_s
        h = ug + ag * carry
        o_ref[rows, :] = h
        carry = h[SUBLANES - 1:SUBLANES, :] if fwd else h[0:1, :]
    h_ref[...] = carry


def _lru_kernel(tbl_ref, xf_ref, xb_ref, wa_ref, ba_ref, wx_ref, bx_ref, lam_ref, h0f_ref, h0b_ref,
                hf_ref, hb_ref, hf_s, hb_s):
    s = pl.program_id(0)

    @pl.when(tbl_ref[4, s] == 1)
    def _():
        hf_s[...] = h0f_ref[0]
        hb_s[...] = h0b_ref[0]

    _lru_direction(xf_ref[...], wa_ref.at[0], ba_ref[0:1, :], wx_ref.at[0], bx_ref[0:1, :],
                   lam_ref[0:1, :], hf_s, hf_ref, True)
    _lru_direction(xb_ref[...], wa_ref.at[1], ba_ref[1:2, :], wx_ref.at[1], bx_ref[1:2, :],
                   lam_ref[1:2, :], hb_s, hb_ref, False)


def _lru_scan(xc, w_a_bf16, b_a, w_x_bf16, b_x, lam, h0f_all, h0b_all):
    tbl = _scan_table(LRU_CHUNK)
    n_steps = tbl.shape[1]
    blk = lambda r: pl.BlockSpec((LRU_CHUNK, D_RNN), lambda s, t: (t[r, s], 0))
    wspec = pl.BlockSpec((2, RG_BLOCKS, RG_BW, RG_BW), lambda s, t: (0, 0, 0, 0))
    vec2 = pl.BlockSpec((2, D_RNN), lambda s, t: (0, 0))
    h0 = pl.BlockSpec((1, 1, D_RNN), lambda s, t: (t[2, s], 0, 0))
    grid_spec = pltpu.PrefetchScalarGridSpec(
        num_scalar_prefetch=1,
        grid=(n_steps,),
        in_specs=[blk(0), blk(1), wspec, vec2, wspec, vec2, vec2, h0, h0],
        out_specs=[blk(0), blk(1)],
        scratch_shapes=[pltpu.VMEM((1, D_RNN), F32), pltpu.VMEM((1, D_RNN), F32)],
    )
    return pl.pallas_call(
        _lru_kernel,
        grid_spec=grid_spec,
        out_shape=[jax.ShapeDtypeStruct((N_TOK, D_RNN), F32), jax.ShapeDtypeStruct((N_TOK, D_RNN), F32)],
        compiler_params=_cparams("arbitrary"),
        name="lru_scan",
    )(jnp.asarray(tbl), xc, xc, w_a_bf16, b_a, w_x_bf16, b_x, lam, h0f_all, h0b_all)


def _odd_out_kernel(x_ref, mod_ref, hf_ref, hb_ref, gate_ref, w_ref, o_ref):
    y = (hf_ref[...] + hb_ref[...]) * _gelu_tanh(gate_ref[...])
    o_ref[...] = x_ref[...] + mod_ref[0][2:3, :] * _dot(y.astype(BF16), w_ref[...])


def _odd_out(x, modt, hf, hb, gate, w_bf16):
    tile = pl.BlockSpec((TB, D_MODEL), lambda i: (i, 0))
    return pl.pallas_call(
        _odd_out_kernel,
        grid=(N_TILES,),
        in_specs=[tile, pl.BlockSpec((1, 6, D_MODEL), lambda i: (i, 0, 0)), tile, tile, tile,
                  pl.BlockSpec((D_RNN, D_MODEL), lambda i: (0, 0))],
        out_specs=tile,
        out_shape=jax.ShapeDtypeStruct((N_TOK, D_MODEL), F32),
        compiler_params=_cparams("parallel"),
        name="odd_out",
    )(x, modt, hf, hb, gate, w_bf16)


def _order_key(x):
    b = lax.bitcast_convert_type(x, jnp.int32)
    return b ^ ((b >> 31) & 0x7FFFFFFF)


def _key_value(k):
    return lax.bitcast_convert_type(k ^ ((k >> 31) & 0x7FFFFFFF), F32)


def _precedes(ky, ry, kx, rx):
    return (ky + jnp.where(ry < rx, 1, 0)) > kx


def _merge_exchange(n):
    pairs = []
    t = max(1, (n - 1).bit_length())
    p = 1 << (t - 1)
    while p > 0:
        q, r, d = 1 << (t - 1), 0, p
        while d > 0:
            pairs.extend((i, i + d) for i in range(n - d) if (i & p) == r)
            d, q, r = q - p, q >> 1, p
        p >>= 1
    return pairs


def _top_rows(values, count, emit):
    n = len(values)
    sub = lax.broadcasted_iota(jnp.int32, (SUBLANES, TB), 0)
    keys = [_order_key(v) for v in values]
    rows = [sub + SUBLANES * i for i in range(n)]
    for i, j in _merge_exchange(n):
        swap = _precedes(keys[j], rows[j], keys[i], rows[i])
        keys[i], keys[j] = jnp.where(swap, keys[j], keys[i]), jnp.where(swap, keys[i], keys[j])
        rows[i], rows[j] = jnp.where(swap, rows[j], rows[i]), jnp.where(swap, rows[i], rows[j])
    lowest = jnp.full((SUBLANES, TB), jnp.iinfo(jnp.int32).min + 1, jnp.int32)
    for step in range(count):
        k = keys[0]
        for shift in (4, 2, 1):
            k = jnp.maximum(k, pltpu.roll(k, shift, 0))
        r = jnp.where(keys[0] == k, rows[0], jnp.iinfo(jnp.int32).max)
        for shift in (4, 2, 1):
            r = jnp.minimum(r, pltpu.roll(r, shift, 0))
        emit(step, _key_value(k[0:1, :]), r[0:1, :])
        depth = min(n - 1, count - 1 - step)
        column = rows[0] == r
        for d in range(depth):
            keys[d] = jnp.where(column, keys[d + 1], keys[d])
            rows[d] = jnp.where(column, rows[d + 1], rows[d])
        if depth == n - 1:
            keys[n - 1] = jnp.where(column, lowest, keys[n - 1])


def _router_kernel(x_ref, mod_ref, g_ref, wq_ref, keys_ref, h_ref, a_ref, b_ref, gate_ref,
                   q_s, v1_s, i1_s, v2_s, i2_s, sc_s, a_s, b_s):
    h = _rms_mod(x_ref[...], g_ref[...], mod_ref[0], 3, 4).astype(BF16)
    h_ref[...] = h
    q_s[...] = _dot(h, wq_ref[...])
    lax.fori_loop(0, PEER_HEADS, functools.partial(
        _route_head, keys_ref, a_ref, b_ref, gate_ref, q_s, v1_s, i1_s, v2_s, i2_s, sc_s, a_s, b_s), 0)


def _route_head(keys_ref, a_ref, b_ref, gate_ref, q_s, v1_s, i1_s, v2_s, i2_s, sc_s, a_s, b_s, hd, carry):
    for half, (v_s, i_s) in enumerate(((v1_s, i1_s), (v2_s, i2_s))):
        col = pl.multiple_of(hd * PEER_DK + half * (PEER_DK // 2), PEER_DK // 2)
        qh = q_s[:, pl.ds(col, PEER_DK // 2)].astype(BF16)
        st = _dot_nt(keys_ref[half, hd], qh)

        def emit(step, value, row, v_s=v_s, i_s=i_s):
            v_s[step:step + 1, :] = value
            i_s[step:step + 1, :] = row

        _top_rows([st[SUBLANES * i:SUBLANES * (i + 1), :] for i in range(PEER_NK // SUBLANES)], PEER_TOPK, emit)

    v1 = v1_s[...]
    v2 = v2_s[...]
    sub8 = lax.broadcasted_iota(jnp.int32, (8, TB), 0)
    pieces = [v1[0:1, :] + v2[0:8, :], v1[0:1, :] + v2[8:16, :]]
    for p in range(1, 8):
        pieces.append(jnp.where(sub8 < PEER_TOPK // (p + 1), v1[p:p + 1, :] + v2[0:8, :], -jnp.inf))
    pieces.append(v1[8:16, :] + v2[0:1, :])
    sub = lax.broadcasted_iota(jnp.int32, (PEER_TOPK, TB), 0)
    i1 = i1_s[...]
    i2 = i2_s[...]

    def emit2(step, value, pos):
        p = jnp.where(pos < 72, jnp.maximum((pos - 8) >> 3, 0), pos - 64)
        r = jnp.where(pos < 16, pos, jnp.where(pos < 72, pos & 7, 0))
        sc_s[step:step + 1, :] = value
        a_s[step:step + 1, :] = jnp.where(sub == p, i1, 0).sum(axis=0, keepdims=True)
        b_s[step:step + 1, :] = jnp.where(sub == r, i2, 0).sum(axis=0, keepdims=True)

    _top_rows(pieces, PEER_TOPK, emit2)
    sc = sc_s[...]
    e = jnp.exp(sc - sc[0:1, :])
    out_rows = pl.ds(pl.multiple_of(hd * PEER_TOPK, PEER_TOPK), PEER_TOPK)
    gate_ref[0, out_rows, :] = e / e.sum(axis=0, keepdims=True)
    a_ref[0, out_rows, :] = a_s[...].astype(F32)
    b_ref[0, out_rows, :] = b_s[...].astype(F32)
    return carry


def _router(x, modt, g, wq_bf16, keys_bf16):
    kt = pl.BlockSpec((1, PEER_HEADS * PEER_TOPK, TB), lambda i: (i, 0, 0))
    kshape = jax.ShapeDtypeStruct((N_TILES, PEER_HEADS * PEER_TOPK, TB), F32)
    f32s = lambda: pltpu.VMEM((PEER_TOPK, TB), F32)
    i32s = lambda: pltpu.VMEM((PEER_TOPK, TB), jnp.int32)
    return pl.pallas_call(
        _router_kernel,
        grid=(N_TILES,),
        in_specs=[
            pl.BlockSpec((TB, D_MODEL), lambda i: (i, 0)),
            pl.BlockSpec((1, 6, D_MODEL), lambda i: (i, 0, 0)),
            pl.BlockSpec((1, D_MODEL), lambda i: (0, 0)),
            pl.BlockSpec((D_MODEL, PEER_HEADS * PEER_DK), lambda i: (0, 0)),
            pl.BlockSpec((2, PEER_HEADS, PEER_NK, PEER_DK // 2), lambda i: (0, 0, 0, 0)),
        ],
        out_specs=[pl.BlockSpec((TB, D_MODEL), lambda i: (i, 0)), kt, kt, kt],
        out_shape=[jax.ShapeDtypeStruct((N_TOK, D_MODEL), BF16), kshape, kshape, kshape],
        scratch_shapes=[pltpu.VMEM((TB, PEER_HEADS * PEER_DK), F32),
                        f32s(), i32s(), f32s(), i32s(), f32s(), i32s(), i32s()],
        compiler_params=_cparams("parallel"),
        name="peer_router",
    )(x, modt, g.reshape(1, D_MODEL), wq_bf16, keys_bf16)


def _expert_kernel(h_ref, x_ref, mod_ref, at_ref, bt_ref, gt_ref, u_ref, v_ref, fg_ref, o_ref, *rest, final):
    if final:
        y_ref, g_s, acc_s, a_s, b_s, w_s = rest
    else:
        g_s, acc_s, a_s, b_s, w_s = rest
    grp = pl.program_id(1)

    @pl.when(grp == 0)
    def _():
        acc_s[...] = jnp.zeros_like(acc_s)
        for part in range(TE // TB):
            a_s[part * TB:(part + 1) * TB, :] = at_ref[part].T
            b_s[part * TB:(part + 1) * TB, :] = bt_ref[part].T
            w_s[part * TB:(part + 1) * TB, :] = gt_ref[part].T
        sub = lax.broadcasted_iota(jnp.int32, (PEER_NK, PEER_NK), 0).astype(F32)

        def gate_grid(t):
            a_row = a_s[pl.ds(t, 1), :]
            b_row = b_s[pl.ds(t, 1), :]
            w_row = w_s[pl.ds(t, 1), :]
            m_t = jnp.where(a_row == sub, 1.0, 0.0).astype(BF16)
            r_t = jnp.where(b_row == sub, w_row, 0.0).astype(BF16)
            g = _dot_nt(m_t, r_t).astype(BF16).astype(F32)
            return lax.bitcast_convert_type(g, jnp.uint32)

        def body(t, carry):
            word = (gate_grid(t) >> 16) | (gate_grid(t + TE // 2) & jnp.uint32(0xFFFF0000))
            g_s[pl.ds(t, PEER_NK, stride=G_PITCH), :] = word
            return carry

        lax.fori_loop(0, TE // 2, body, 0, unroll=G_UNROLL)

    h = h_ref[...]
    acc = acc_s[...]
    for sb in range(PEER_GROUP // EXPERT_SUB):
        e0 = sb * EXPERT_SUB
        hmat = _dot_nt(h, u_ref[e0:e0 + EXPERT_SUB, :])
        parts = []
        for j in range(EXPERT_SUB // PEER_NK):
            chunk = grp * CHUNKS_PER_GROUP + sb * (EXPERT_SUB // PEER_NK) + j
            word = g_s[pl.ds(pl.multiple_of(chunk * G_PITCH, 8), TE // 2), :]
            gates = jnp.concatenate(
                [lax.bitcast_convert_type(word << 16, F32),
                 lax.bitcast_convert_type(word & jnp.uint32(0xFFFF0000), F32)], axis=0)
            parts.append((_gelu_tanh(hmat[:, j * PEER_NK:(j + 1) * PEER_NK]) * gates).astype(BF16))
        acc = acc + _dot(jnp.concatenate(parts, axis=1), v_ref[e0:e0 + EXPERT_SUB, :])
    acc_s[...] = acc

    @pl.when(grp == PEER_NGROUPS - 1)
    def _():
        xo = x_ref[...] + mod_ref[0][5:6, :] * acc_s[...]
        if final:
            y = xo * lax.rsqrt(jnp.mean(xo * xo, axis=-1, keepdims=True) + EPS) * fg_ref[...]
            tile = pl.program_id(0)

            @pl.when(tile < N_PROMPT // TE)
            def _():
                o_ref[...] = y

            @pl.when(tile >= N_PROMPT // TE)
            def _():
                y_ref[...] = y
        else:
            o_ref[...] = xo


def _experts(h_bf16, x, modt_e, a_t, b_t, g_t, u_bf16, v_bf16, final_g, final):
    tile = pl.BlockSpec((TE, D_MODEL), lambda i, g: (i, 0))
    kt = pl.BlockSpec((TE // TB, PEER_HEADS * PEER_TOPK, TB), lambda i, g: (i, 0, 0))
    wblk = pl.BlockSpec((PEER_GROUP, D_MODEL), lambda i, g: (g, 0))
    xs = jax.ShapeDtypeStruct((N_TOK, D_MODEL), F32)
    n_prompt = N_PROMPT // TE
    sel = lambda: pltpu.VMEM((TE, PEER_HEADS * PEER_TOPK), F32)
    return pl.pallas_call(
        functools.partial(_expert_kernel, final=final),
        grid=(N_TOK // TE, PEER_NGROUPS),
        in_specs=[tile, tile, pl.BlockSpec((1, 6, D_MODEL), lambda i, g: (i, 0, 0)), kt, kt, kt, wblk, wblk,
                  pl.BlockSpec((1, D_MODEL), lambda i, g: (0, 0))],
        out_specs=[pl.BlockSpec((TE, D_MODEL), lambda i, g: (jnp.minimum(i, n_prompt - 1), 0)),
                   pl.BlockSpec((TE, D_MODEL), lambda i, g: (jnp.maximum(i - n_prompt, 0), 0))] if final else tile,
        out_shape=[jax.ShapeDtypeStruct((N_PROMPT, D_MODEL), F32),
                   jax.ShapeDtypeStruct((N_SAMPLE, D_MODEL), F32)] if final else xs,
        scratch_shapes=[pltpu.VMEM((PEER_NK * G_PITCH, PEER_NK), jnp.uint32), pltpu.VMEM((TE, D_MODEL), F32),
                        sel(), sel(), sel()],
        compiler_params=_cparams("arbitrary" if final else "parallel", "arbitrary"),
        name="peer_experts_final" if final else "peer_experts",
    )(h_bf16, x, modt_e, a_t, b_t, g_t, u_bf16, v_bf16, final_g.reshape(1, D_MODEL))


def _peer(x, modt, modt_e, norm_g, w_q, keys, u, v, final_g, final):
    h, a_t, b_t, g_t = _router(x, modt, norm_g, w_q.astype(BF16), keys.astype(BF16))
    return _experts(h, x, modt_e, a_t, b_t, g_t, u.astype(BF16), v.astype(BF16), final_g, final)


def kernel(x_prompt, x_sample, cache_k, cache_v, state_hgrn_fwd, state_hgrn_bwd, state_lru_fwd, state_lru_bwd, c, c_ctx, norm1_g, norm2_g, w_mod, b_mod, w_in_even, w_out_even, hgrn_lb_fwd, hgrn_lb_bwd, hgrn_gnorm_g, attn_sink, w_in_odd, w_out_odd, conv_w, conv_b, rg_w_a, rg_b_a, rg_w_x, rg_b_x, rg_lambda, peer_w_q, peer_keys, peer_u, peer_v, final_g):
    x = jnp.concatenate([x_prompt.reshape(N_PROMPT, D_MODEL), x_sample.reshape(N_SAMPLE, D_MODEL)], axis=0)
    cond = jnp.concatenate([c_ctx[None, :], c, jnp.zeros((COND_PAD - N_COND, D_MODEL), F32)], axis=0)
    mod = _modulation(cond, w_mod, b_mod)
    tile_row = np.asarray([0] * PROMPT_TILES + [1 + i // TILES_PER_SAMPLE for i in range(N_TILES - PROMPT_TILES)])
    modt = [mod[l][tile_row].reshape(N_TILES, 6, D_MODEL) for l in range(DEPTH)]
    modt_e = [m[::TE // TB] for m in modt]

    widths = (HG_W, HG_W, HG_W, HG_W, HG_W, ATT_W, KV_W, KV_W)
    zq, zf, zb, zi, zg, aq, ak, av = _in_proj(x, modt[0], norm1_g[0], w_in_even[0].astype(BF16), widths, "even_in")
    zero_state = jnp.zeros((1, HG_HEADS, HG_DV, HG_DK), F32)
    s0f = jnp.concatenate([zero_state, jnp.swapaxes(state_hgrn_fwd[:, 0], -1, -2)], axis=0)
    s0b = jnp.concatenate([zero_state, jnp.swapaxes(state_hgrn_bwd[:, 0], -1, -2)], axis=0)
    o_f, o_b, sf_t, sb_t = _hgrn_scan(zq, zf, zb, zi, hgrn_lb_fwd, hgrn_lb_bwd, s0f, s0b)
    oatt = _attention(aq, ak, av, cache_k[:, 0].reshape(DEC_BATCH, PAST_LEN, KV_W),
                      cache_v[:, 0].reshape(DEC_BATCH, PAST_LEN, KV_W), attn_sink[0])
    x = _even_out(x, modt[0], o_f, o_b, zg, oatt, hgrn_gnorm_g[0], w_out_even[0].astype(BF16))
    x = _peer(x, modt[0], modt_e[0], norm2_g[0], peer_w_q[0], peer_keys[0], peer_u[0], peer_v[0], final_g, False)

    gate, xr = _in_proj(x, modt[1], norm1_g[1], w_in_odd[0].astype(BF16), (D_RNN, D_RNN), "odd_in")
    xc = _dwconv(xr, conv_w[0], conv_b[0])
    zero_h = jnp.zeros((1, 1, D_RNN), F32)
    h0f = jnp.concatenate([zero_h, state_lru_fwd[:, 0][:, None, :]], axis=0)
    h0b = jnp.concatenate([zero_h, state_lru_bwd[:, 0][:, None, :]], axis=0)
    hf, hb = _lru_scan(xc, rg_w_a[0].astype(BF16), rg_b_a[0], rg_w_x[0].astype(BF16), rg_b_x[0],
                       rg_lambda[0], h0f, h0b)
    x = _odd_out(x, modt[1], hf, hb, gate, w_out_odd[0].astype(BF16))
    y_p, y_s = _peer(x, modt[1], modt_e[1], norm2_g[1], peer_w_q[1], peer_keys[1], peer_u[1], peer_v[1], final_g, True)

    y_prompt = y_p.reshape(BATCH, SEQ, D_MODEL)
    y_sample = y_s.reshape(DEC_BATCH, DEC_SEQ, D_MODEL)
    new_k = ak[:N_PROMPT].reshape(BATCH, 1, SEQ, KV_HEADS, HEAD_DIM)
    new_v = av[:N_PROMPT].reshape(BATCH, 1, SEQ, KV_HEADS, HEAD_DIM)
    new_hf = jnp.swapaxes(sf_t[:BATCH], -1, -2)[:, None]
    new_hb = jnp.swapaxes(sb_t[:BATCH], -1, -2)[:, None]
    new_lf = hf[:N_PROMPT].reshape(BATCH, SEQ, D_RNN)[:, -1][:, None, :]
    new_lb = hb[:N_PROMPT].reshape(BATCH, SEQ, D_RNN)[:, 0][:, None, :]
    return (y_prompt, y_sample, new_k, new_v, new_hf, new_hb, new_lf, new_lb)
```

```python
import functools

import numpy as np
import jax
import jax.numpy as jnp
from jax import lax
from jax.experimental import pallas as pl
from jax.experimental.pallas import tpu as pltpu

F32 = jnp.float32
BF16 = jnp.bfloat16

D_MODEL = 1024
BATCH = 16
SEQ = 256
DEPTH = 2
DEC_BATCH = 8
DEC_SEQ = 2048
PAST_LEN = 512
GRID_W = 64
EPS = 1e-6
HG_DK = 128
HG_DV = 128
HG_HEADS = 4
HG_CHUNK = 64
HG_BLOCK = 256
HG_W = HG_HEADS * HG_DK
HEAD_DIM = 64
N_HEADS = 8
KV_HEADS = 2
Q_PER_KV = 4
WINDOW = 128
ATT_BLOCK = 128
ROPE_BASE = 10000.0
ATT_W = N_HEADS * HEAD_DIM
KV_W = KV_HEADS * HEAD_DIM
D_RNN = 1024
RG_BLOCKS = 4
RG_BW = 256
RG_C = 8.0
PEER_HEADS = 8
PEER_NK = 128
PEER_DK = 256
PEER_TOPK = 16

N_PROMPT = BATCH * SEQ
N_SAMPLE = DEC_BATCH * DEC_SEQ
N_TOK = N_PROMPT + N_SAMPLE
TB = 256
N_TILES = N_TOK // TB
PROMPT_TILES = N_PROMPT // TB
TILES_PER_SAMPLE = DEC_SEQ // TB
N_COND = 1 + DEC_BATCH
COND_PAD = 16

TE = 512
PEER_GROUP = 2048
PEER_NGROUPS = PEER_NK * PEER_NK // PEER_GROUP
CHUNKS_PER_GROUP = PEER_GROUP // PEER_NK
EXPERT_SUB = 1024
G_PITCH = TE // 2 + 8
G_UNROLL = 16

LRU_CHUNK = 256
SUBLANES = 8

VMEM_LIMIT = 58 * 1024 * 1024


def _cparams(*sem):
    return pltpu.CompilerParams(dimension_semantics=sem, vmem_limit_bytes=VMEM_LIMIT)


def _dot(a, b):
    return jnp.dot(a, b, preferred_element_type=F32)


def _dot_nt(a, b):
    return lax.dot_general(a, b, (((1,), (1,)), ((), ())), preferred_element_type=F32)


def _dot_tn(a, b):
    return lax.dot_general(a, b, (((0,), (0,)), ((), ())), preferred_element_type=F32)


def _sigmoid(x):
    return 1.0 / (1.0 + jnp.exp(-x))


def _silu(x):
    return x * _sigmoid(x)


def _gelu_tanh(x):
    c = np.float32(np.sqrt(2.0 / np.pi))
    return x * (0.5 * (1.0 + jnp.tanh(c * (x + 0.044715 * (x * x * x)))))


def _rms_mod(x, g, mod, sh, sc):
    y = x * lax.rsqrt(jnp.mean(x * x, axis=-1, keepdims=True) + EPS) * g
    return y * (1.0 + mod[sc:sc + 1, :]) + mod[sh:sh + 1, :]


def _mod_kernel(c_ref, w_ref, b_ref, o_ref):
    c = c_ref[...]
    s = _silu(c).astype(BF16)
    o_ref[0] = _dot(s, w_ref[0].astype(BF16)) + b_ref[0]


def _modulation(cond, w_mod, b_mod):
    nb = 6 * D_MODEL // 1024
    return pl.pallas_call(
        _mod_kernel,
        grid=(DEPTH, nb),
        in_specs=[
            pl.BlockSpec((COND_PAD, D_MODEL), lambda l, n: (0, 0)),
            pl.BlockSpec((1, D_MODEL, 1024), lambda l, n: (l, 0, n)),
            pl.BlockSpec((1, 1, 1024), lambda l, n: (l, 0, n)),
        ],
        out_specs=pl.BlockSpec((1, COND_PAD, 1024), lambda l, n: (l, 0, n)),
        out_shape=jax.ShapeDtypeStruct((DEPTH, COND_PAD, 6 * D_MODEL), F32),
        compiler_params=_cparams("parallel", "parallel"),
        name="modulation",
    )(cond, w_mod, b_mod.reshape(DEPTH, 1, 6 * D_MODEL))


def _in_proj_kernel(x_ref, mod_ref, g_ref, w_ref, *o_refs, widths):
    h = _rms_mod(x_ref[...], g_ref[...], mod_ref[0], 0, 1)
    z = _dot(h.astype(BF16), w_ref[...])
    off = 0
    for o_ref, wd in zip(o_refs, widths):
        o_ref[...] = z[:, off:off + wd]
        off += wd


def _in_proj(x, modt, g, w_bf16, widths, name):
    n_out = sum(widths)
    return pl.pallas_call(
        functools.partial(_in_proj_kernel, widths=widths),
        grid=(N_TILES,),
        in_specs=[
            pl.BlockSpec((TB, D_MODEL), lambda i: (i, 0)),
            pl.BlockSpec((1, 6, D_MODEL), lambda i: (i, 0, 0)),
            pl.BlockSpec((1, D_MODEL), lambda i: (0, 0)),
            pl.BlockSpec((D_MODEL, n_out), lambda i: (0, 0)),
        ],
        out_specs=[pl.BlockSpec((TB, wd), lambda i: (i, 0)) for wd in widths],
        out_shape=[jax.ShapeDtypeStruct((N_TOK, wd), F32) for wd in widths],
        compiler_params=_cparams("parallel"),
        name=name,
    )(x, modt, g.reshape(1, D_MODEL), w_bf16)


def _hgrn_direction(zq, zgate, zi, lb, states, fwd):
    n = HG_CHUNK
    row = lax.broadcasted_iota(jnp.int32, (n, n), 0)
    col = lax.broadcasted_iota(jnp.int32, (n, n), 1)
    keep = (col <= row) if fwd else (col >= row)
    tri = jnp.where(keep, 1.0, 0.0).astype(BF16)
    q = _silu(zq)
    f = lb + (1.0 - lb) * _sigmoid(zgate)
    g = jnp.log(f)
    k = (1.0 - lb) * _sigmoid(-zgate)
    g_hi = g.astype(BF16)
    g_lo = (g - g_hi.astype(F32)).astype(BF16)
    b = _dot(tri, g_hi) + _dot(tri, g_lo)
    mid = n // 2 - 1 if fwd else n // 2
    last = n - 1 if fwd else 0
    bm = b[mid:mid + 1, :]
    bl = b[last:last + 1, :]
    qi = (q * jnp.exp(b - bm)).astype(BF16)
    ki = (k * jnp.exp(bm - b)).astype(BF16)
    qs = (q * jnp.exp(b)).astype(BF16)
    ks = (k * jnp.exp(bl - b)).astype(BF16)
    vb = zi.astype(BF16)
    decay = jnp.exp(bl)
    outs = []
    new_states = []
    for h in range(HG_HEADS):
        sl = slice(h * HG_DK, (h + 1) * HG_DK)
        a = jnp.where(keep, _dot_nt(qi[:, sl], ki[:, sl]), 0.0)
        st = states[h]
        outs.append(_dot(a.astype(BF16), vb[:, sl]) + _dot_nt(qs[:, sl], st.astype(BF16)))
        new_states.append(decay[:, sl] * st + _dot_tn(vb[:, sl], ks[:, sl]))
    return jnp.concatenate(outs, axis=1), new_states


def _lower_bound(p):
    e = jnp.exp(p - p.max(axis=0, keepdims=True))
    return e[0:1, :] / e.sum(axis=0, keepdims=True)


def _hgrn_block(zq_ref, zgate_ref, zi_ref, lb, st_ref, o_ref, fwd):
    states = [st_ref[h] for h in range(HG_HEADS)]
    n_sub = HG_BLOCK // HG_CHUNK
    for c in (range(n_sub) if fwd else reversed(range(n_sub))):
        rows = slice(c * HG_CHUNK, (c + 1) * HG_CHUNK)
        o_ref[rows, :], states = _hgrn_direction(zq_ref[rows, :], zgate_ref[rows, :], zi_ref[rows, :], lb, states, fwd)
    for h in range(HG_HEADS):
        st_ref[h] = states[h]


def _hgrn_kernel(tbl_ref, zqf_ref, zff_ref, zif_ref, zqb_ref, zbb_ref, zib_ref, lbf_ref, lbb_ref,
                 s0f_ref, s0b_ref, of_ref, ob_ref, sf_ref, sb_ref, stf, stb):
    s = pl.program_id(0)

    @pl.when(tbl_ref[4, s] == 1)
    def _():
        stf[...] = s0f_ref[0]
        stb[...] = s0b_ref[0]

    _hgrn_block(zqf_ref, zff_ref, zif_ref, _lower_bound(lbf_ref[...]), stf, of_ref, True)
    _hgrn_block(zqb_ref, zbb_ref, zib_ref, _lower_bound(lbb_ref[...]), stb, ob_ref, False)

    @pl.when(tbl_ref[5, s] == 1)
    def _():
        sf_ref[0] = stf[...]
        sb_ref[0] = stb[...]


def _scan_table(chunk):
    rows = []
    blk = 0
    for (nseq, length, has_init) in ((BATCH, SEQ, False), (DEC_BATCH, DEC_SEQ, True)):
        nc = length // chunk
        for b in range(nseq):
            sid = b if not has_init else BATCH + b
            for i in range(nc):
                rows.append((blk + i, blk + nc - 1 - i, (1 + b) if has_init else 0, sid,
                             int(i == 0), int(i == nc - 1)))
            blk += nc
    return np.asarray(rows, np.int32).T.copy()


def _hgrn_scan(zq, zf, zb, zi, lbf, lbb, s0f_t, s0b_t):
    tbl = _scan_table(HG_BLOCK)
    n_steps = tbl.shape[1]
    n_seq = BATCH + DEC_BATCH
    blk = lambda r: pl.BlockSpec((HG_BLOCK, HG_W), lambda s, t: (t[r, s], 0))
    vec = pl.BlockSpec(lbf.shape, lambda s, t: (0, 0))
    st_in = pl.BlockSpec((1, HG_HEADS, HG_DV, HG_DK), lambda s, t: (t[2, s], 0, 0, 0))
    st_out = pl.BlockSpec((1, HG_HEADS, HG_DV, HG_DK), lambda s, t: (t[3, s], 0, 0, 0))
    grid_spec = pltpu.PrefetchScalarGridSpec(
        num_scalar_prefetch=1,
        grid=(n_steps,),
        in_specs=[blk(0), blk(0), blk(0), blk(1), blk(1), blk(1), vec, vec, st_in, st_in],
        out_specs=[blk(0), blk(1), st_out, st_out],
        scratch_shapes=[pltpu.VMEM((HG_HEADS, HG_DV, HG_DK), F32),
                        pltpu.VMEM((HG_HEADS, HG_DV, HG_DK), F32)],
    )
    st_shape = jax.ShapeDtypeStruct((n_seq, HG_HEADS, HG_DV, HG_DK), F32)
    return pl.pallas_call(
        _hgrn_kernel,
        grid_spec=grid_spec,
        out_shape=[jax.ShapeDtypeStruct((N_TOK, HG_W), F32), jax.ShapeDtypeStruct((N_TOK, HG_W), F32),
                   st_shape, st_shape],
        compiler_params=_cparams("arbitrary"),
        name="hgrn_scan",
    )(jnp.asarray(tbl), zq, zf, zi, zq, zb, zi, lbf, lbb, s0f_t, s0b_t)


def _rope(x, cos, sgn_sin):
    w = x.shape[1]
    lane = lax.broadcasted_iota(jnp.int32, x.shape, 1)
    partner = jnp.where((lane & 31) < 16, pltpu.roll(x, w - 16, 1), pltpu.roll(x, 16, 1))
    return x * cos + partner * sgn_sin


def _dup_kv_half(x, kv):
    lane = lax.broadcasted_iota(jnp.int32, x.shape, 1)
    own = (lane < HEAD_DIM) if kv == 0 else (lane >= HEAD_DIM)
    xs = jnp.where(own, x, 0.0)
    return xs + pltpu.roll(xs, HEAD_DIM, 1)


def _attend(q, key_parts, sink_ref, o_ref):
    lane = lax.broadcasted_iota(jnp.int32, (ATT_BLOCK, 128), 1)
    lo = lane < HEAD_DIM
    for kv in range(KV_HEADS):
        qs = []
        for gq in range(Q_PER_KV):
            hd = kv * Q_PER_KV + gq
            pair = q[:, (hd // 2) * 128:(hd // 2 + 1) * 128]
            qs.append(jnp.where(lo if hd % 2 == 0 else ~lo, pair, 0.0))
        qst = jnp.concatenate(qs, axis=0).astype(BF16)
        logits = []
        vals = []
        for (k, v, mask) in key_parts:
            kd = _dup_kv_half(k, kv).astype(BF16)
            vals.append(_dup_kv_half(v, kv).astype(BF16))
            s = _dot_nt(qst, kd)
            if mask is not None:
                s = jnp.where(mask, s, -1e30)
            logits.append(s)
        m = logits[0].max(axis=-1, keepdims=True)
        for s in logits[1:]:
            m = jnp.maximum(m, s.max(axis=-1, keepdims=True))
        sink_col = jnp.concatenate(
            [jnp.full((ATT_BLOCK, 1), sink_ref[kv * Q_PER_KV + gq], F32) for gq in range(Q_PER_KV)], axis=0)
        m = jnp.maximum(m, sink_col)
        den = jnp.exp(sink_col - m)
        acc = jnp.zeros((Q_PER_KV * ATT_BLOCK, 128), F32)
        for s, v in zip(logits, vals):
            p = jnp.exp(s - m)
            den = den + p.sum(axis=-1, keepdims=True)
            acc = acc + _dot(p.astype(BF16), v)
        out = acc / den
        for pr in range(Q_PER_KV // 2):
            r0 = out[(2 * pr) * ATT_BLOCK:(2 * pr + 1) * ATT_BLOCK]
            r1 = out[(2 * pr + 1) * ATT_BLOCK:(2 * pr + 2) * ATT_BLOCK]
            c0 = (kv * Q_PER_KV // 2 + pr) * 128
            o_ref[:, c0:c0 + 128] = jnp.where(lo, r0, r1)


def _attn_prompt_kernel(sink_ref, q_ref, k_ref, v_ref, o_ref):
    q = q_ref[...] * np.float32(HEAD_DIM ** -0.5)
    _attend(q, [(k_ref[...], v_ref[...], None)], sink_ref, o_ref)


def _attn_latent_kernel(sink_ref, q_ref, kp_ref, kc_ref, kn_ref, vp_ref, vc_ref, vn_ref,
                        ck_ref, cv_ref, cos_ref, sin_ref, prompt_rows_ref, o_ref):
    j = pl.program_id(1)
    nq = pl.num_programs(1)
    qstart = pl.multiple_of((j + 1) * ATT_BLOCK, ATT_BLOCK)
    kstart = pl.multiple_of(j * ATT_BLOCK, ATT_BLOCK)
    cq = cos_ref[pl.ds(qstart, ATT_BLOCK), :]
    sq = sin_ref[pl.ds(qstart, ATT_BLOCK), :]
    q = _rope(q_ref[...], jnp.concatenate([cq] * 4, axis=1), jnp.concatenate([sq] * 4, axis=1))
    q = q * np.float32(HEAD_DIM ** -0.5)
    k_loc = jnp.concatenate([kp_ref[...], kc_ref[...], kn_ref[...]], axis=0)
    v_loc = jnp.concatenate([vp_ref[...], vc_ref[...], vn_ref[...]], axis=0)
    k_loc = _rope(k_loc, cos_ref[pl.ds(kstart, 3 * ATT_BLOCK), :], sin_ref[pl.ds(kstart, 3 * ATT_BLOCK), :])
    r = lax.broadcasted_iota(jnp.int32, (Q_PER_KV * ATT_BLOCK, 3 * ATT_BLOCK), 0) & (ATT_BLOCK - 1)
    c = lax.broadcasted_iota(jnp.int32, (Q_PER_KV * ATT_BLOCK, 3 * ATT_BLOCK), 1)
    rel = c - ATT_BLOCK - r
    c_lo = jnp.where(j > 0, 0, ATT_BLOCK)
    c_hi = jnp.where(j < nq - 1, 3 * ATT_BLOCK, 2 * ATT_BLOCK)
    ok = (jnp.abs(rel) <= WINDOW) & (c >= c_lo) & (c < c_hi)
    _attend(q, [(ck_ref[0], cv_ref[0], None), (k_loc, v_loc, ok)], sink_ref, o_ref)


def _rope_tables():
    pos = np.arange(-ATT_BLOCK, DEC_SEQ + ATT_BLOCK)
    nf = HEAD_DIM // 4
    inv = (1.0 / (ROPE_BASE ** (np.arange(nf, dtype=np.float32) / nf))).astype(np.float32)
    rows = (pos // GRID_W).astype(np.float32)
    cols = (pos % GRID_W).astype(np.float32)
    ar = rows[:, None] * inv[None, :]
    ac = cols[:, None] * inv[None, :]
    cos64 = np.concatenate([np.cos(ar), np.cos(ar), np.cos(ac), np.cos(ac)], axis=1)
    sin64 = np.concatenate([-np.sin(ar), np.sin(ar), -np.sin(ac), np.sin(ac)], axis=1)
    cos = np.concatenate([cos64, cos64], axis=1).astype(np.float32)
    sin = np.concatenate([sin64, sin64], axis=1).astype(np.float32)
    return jnp.asarray(cos), jnp.asarray(sin)


def _attention(aq, ak, av, ck, cv, sink):
    smem = pl.BlockSpec(memory_space=pltpu.SMEM)
    nqp = SEQ // ATT_BLOCK
    o_prompt = pl.pallas_call(
        _attn_prompt_kernel,
        grid=(BATCH, nqp),
        in_specs=[
            smem,
            pl.BlockSpec((ATT_BLOCK, ATT_W), lambda b, j: (b * nqp + j, 0)),
            pl.BlockSpec((SEQ, KV_W), lambda b, j: (b, 0)),
            pl.BlockSpec((SEQ, KV_W), lambda b, j: (b, 0)),
        ],
        out_specs=pl.BlockSpec((ATT_BLOCK, ATT_W), lambda b, j: (b * nqp + j, 0)),
        out_shape=jax.ShapeDtypeStruct((N_TOK, ATT_W), F32),
        compiler_params=_cparams("parallel", "parallel"),
        name="attn_prompt",
    )(sink, aq, ak, av)

    nq = DEC_SEQ // ATT_BLOCK
    base = N_PROMPT // ATT_BLOCK
    cos, sin = _rope_tables()
    kv_blk = lambda d: pl.BlockSpec(
        (ATT_BLOCK, KV_W), lambda b, j: (base + b * nq + jnp.clip(j + d, 0, nq - 1), 0))
    tab = pl.BlockSpec((DEC_SEQ + 2 * ATT_BLOCK, 128), lambda b, j: (0, 0))
    o_latent = pl.pallas_call(
        _attn_latent_kernel,
        grid=(DEC_BATCH, nq),
        in_specs=[
            smem,
            pl.BlockSpec((ATT_BLOCK, ATT_W), lambda b, j: (base + b * nq + j, 0)),
            kv_blk(-1), kv_blk(0), kv_blk(1), kv_blk(-1), kv_blk(0), kv_blk(1),
            pl.BlockSpec((1, PAST_LEN, KV_W), lambda b, j: (b, 0, 0)),
            pl.BlockSpec((1, PAST_LEN, KV_W), lambda b, j: (b, 0, 0)),
            tab, tab,
            pl.BlockSpec(memory_space=pl.ANY),
        ],
        out_specs=pl.BlockSpec((ATT_BLOCK, ATT_W), lambda b, j: (base + b * nq + j, 0)),
        out_shape=jax.ShapeDtypeStruct((N_TOK, ATT_W), F32),
        input_output_aliases={12: 0},
        compiler_params=_cparams("parallel", "parallel"),
        name="attn_latent",
    )(sink, aq, ak, ak, ak, av, av, av, ck, cv, cos, sin, o_prompt)
    return o_latent


def _even_out_kernel(x_ref, mod_ref, of_ref, ob_ref, zg_ref, oatt_ref, gn_ref, w_ref, o_ref):
    o = of_ref[...] + ob_ref[...]
    parts = []
    for h in range(HG_HEADS):
        oh = o[:, h * HG_DV:(h + 1) * HG_DV]
        parts.append(oh * lax.rsqrt(jnp.mean(oh * oh, axis=-1, keepdims=True) + EPS))
    o_hg = jnp.concatenate(parts, axis=1) * gn_ref[...] * _silu(zg_ref[...])
    y = _dot(o_hg.astype(BF16), w_ref[0:HG_W, :]) + _dot(oatt_ref[...].astype(BF16), w_ref[HG_W:, :])
    o_ref[...] = x_ref[...] + mod_ref[0][2:3, :] * y


def _even_out(x, modt, o_f, o_b, zg, oatt, gn_g, w_bf16):
    tile = lambda w: pl.BlockSpec((TB, w), lambda i: (i, 0))
    return pl.pallas_call(
        _even_out_kernel,
        grid=(N_TILES,),
        in_specs=[
            tile(D_MODEL),
            pl.BlockSpec((1, 6, D_MODEL), lambda i: (i, 0, 0)),
            tile(HG_W), tile(HG_W), tile(HG_W), tile(ATT_W),
            pl.BlockSpec((1, HG_W), lambda i: (0, 0)),
            pl.BlockSpec((HG_W + ATT_W, D_MODEL), lambda i: (0, 0)),
        ],
        out_specs=tile(D_MODEL),
        out_shape=jax.ShapeDtypeStruct((N_TOK, D_MODEL), F32),
        compiler_params=_cparams("parallel"),
        name="even_out",
    )(x, modt, o_f, o_b, zg, oatt, gn_g.reshape(1, HG_W), w_bf16)


def _conv_kernel(xc_ref, xp_ref, xn_ref, w_ref, b_ref, o_ref):
    i = pl.program_id(0)
    in_sample = i >= PROMPT_TILES
    pos = (i - PROMPT_TILES) % TILES_PER_SAMPLE
    has_prev = in_sample & (pos != 0)
    has_next = in_sample & (pos != TILES_PER_SAMPLE - 1)
    prev = jnp.where(has_prev, xp_ref[...], 0.0)
    nxt = jnp.where(has_next, xn_ref[...], 0.0)
    ext = jnp.concatenate([prev, xc_ref[...], nxt], axis=0)
    n = TB + 16
    w = w_ref[...]
    acc = ext[8:8 + TB] * w[2:3, :] + b_ref[...]
    acc = acc + pltpu.roll(ext, 2, 0)[8:8 + TB] * w[0:1, :]
    acc = acc + pltpu.roll(ext, 1, 0)[8:8 + TB] * w[1:2, :]
    acc = acc + pltpu.roll(ext, n - 1, 0)[8:8 + TB] * w[3:4, :]
    o_ref[...] = acc


def _dwconv(xr, cw, cb):
    r8 = TB // 8
    last8 = N_TOK // 8 - 1
    return pl.pallas_call(
        _conv_kernel,
        grid=(N_TILES,),
        in_specs=[
            pl.BlockSpec((TB, D_RNN), lambda i: (i, 0)),
            pl.BlockSpec((8, D_RNN), lambda i: (jnp.maximum(i * r8 - 1, 0), 0)),
            pl.BlockSpec((8, D_RNN), lambda i: (jnp.minimum((i + 1) * r8, last8), 0)),
            pl.BlockSpec((4, D_RNN), lambda i: (0, 0)),
            pl.BlockSpec((1, D_RNN), lambda i: (0, 0)),
        ],
        out_specs=pl.BlockSpec((TB, D_RNN), lambda i: (i, 0)),
        out_shape=jax.ShapeDtypeStruct((N_TOK, D_RNN), F32),
        compiler_params=_cparams("parallel"),
        name="dwconv",
    )(xr, xr, xr, cw, cb.reshape(1, D_RNN))


def _log1p(y):
    w = 1.0 + y
    return jnp.where(w == 1.0, y, jnp.log(w) * (y / (w - 1.0)))


def _expm1_given(x, e):
    return jnp.where(e == 1.0, x, (e - 1.0) * (x / jnp.log(e)))


def _sigmoid_tanh(x):
    return 0.5 + 0.5 * jnp.tanh(0.5 * x)


def _lru_direction(xc, wa_ref, ba, wx_ref, bx, lam, h_ref, o_ref, fwd):
    n = LRU_CHUNK
    xb = xc.astype(BF16)
    ra, ri = [], []
    for blk in range(RG_BLOCKS):
        sl = slice(blk * RG_BW, (blk + 1) * RG_BW)
        ra.append(_dot(xb[:, sl], wa_ref[blk]))
        ri.append(_dot(xb[:, sl], wx_ref[blk]))
    r = _sigmoid_tanh(jnp.concatenate(ra, axis=1) + ba)
    gi = _sigmoid_tanh(jnp.concatenate(ri, axis=1) + bx)
    neg = -lam
    softplus = jnp.maximum(neg, 0.0) + _log1p(jnp.exp(-jnp.abs(neg)))
    log_a = (-RG_C * softplus) * r
    a = jnp.exp(log_a)
    u = jnp.sqrt(-_expm1_given(2.0 * log_a, a * a)) * (gi * xc)
    sub = lax.broadcasted_iota(jnp.int32, (SUBLANES, D_RNN), 0)
    carry = h_ref[...]
    groups = range(n // SUBLANES)
    for g in (groups if fwd else reversed(groups)):
        rows = slice(g * SUBLANES, (g + 1) * SUBLANES)
        ag, ug = a[rows, :], u[rows, :]
        for d in (1, 2, 4):
            live = (sub >= d) if fwd else (sub < SUBLANES - d)
            shift = d if fwd else SUBLANES - d
            a_s = jnp.where(live, pltpu.roll(ag, shift, 0), 1.0)
            u_s = jnp.where(live, pltpu.roll(ug, shift, 0), 0.0)
            ug = ag * u_s + ug
            ag = ag * a_s
        h = ug + ag * carry
        o_ref[rows, :] = h
        carry = h[SUBLANES - 1:SUBLANES, :] if fwd else h[0:1, :]
    h_ref[...] = carry


def _lru_kernel(tbl_ref, xf_ref, xb_ref, wa_ref, ba_ref, wx_ref, bx_ref, lam_ref, h0f_ref, h0b_ref,
                hf_ref, hb_ref, hf_s, hb_s):
    s = pl.program_id(0)

    @pl.when(tbl_ref[4, s] == 1)
    def _():
        hf_s[...] = h0f_ref[0]
        hb_s[...] = h0b_ref[0]

    _lru_direction(xf_ref[...], wa_ref.at[0], ba_ref[0:1, :], wx_ref.at[0], bx_ref[0:1, :],
                   lam_ref[0:1, :], hf_s, hf_ref, True)
    _lru_direction(xb_ref[...], wa_ref.at[1], ba_ref[1:2, :], wx_ref.at[1], bx_ref[1:2, :],
                   lam_ref[1:2, :], hb_s, hb_ref, False)


def _lru_scan(xc, w_a_bf16, b_a, w_x_bf16, b_x, lam, h0f_all, h0b_all):
    tbl = _scan_table(LRU_CHUNK)
    n_steps = tbl.shape[1]
    blk = lambda r: pl.BlockSpec((LRU_CHUNK, D_RNN), lambda s, t: (t[r, s], 0))
    wspec = pl.BlockSpec((2, RG_BLOCKS, RG_BW, RG_BW), lambda s, t: (0, 0, 0, 0))
    vec2 = pl.BlockSpec((2, D_RNN), lambda s, t: (0, 0))
    h0 = pl.BlockSpec((1, 1, D_RNN), lambda s, t: (t[2, s], 0, 0))
    grid_spec = pltpu.PrefetchScalarGridSpec(
        num_scalar_prefetch=1,
        grid=(n_steps,),
        in_specs=[blk(0), blk(1), wspec, vec2, wspec, vec2, vec2, h0, h0],
        out_specs=[blk(0), blk(1)],
        scratch_shapes=[pltpu.VMEM((1, D_RNN), F32), pltpu.VMEM((1, D_RNN), F32)],
    )
    return pl.pallas_call(
        _lru_kernel,
        grid_spec=grid_spec,
        out_shape=[jax.ShapeDtypeStruct((N_TOK, D_RNN), F32), jax.ShapeDtypeStruct((N_TOK, D_RNN), F32)],
        compiler_params=_cparams("arbitrary"),
        name="lru_scan",
    )(jnp.asarray(tbl), xc, xc, w_a_bf16, b_a, w_x_bf16, b_x, lam, h0f_all, h0b_all)


def _odd_out_kernel(x_ref, mod_ref, hf_ref, hb_ref, gate_ref, w_ref, o_ref):
    y = (hf_ref[...] + hb_ref[...]) * _gelu_tanh(gate_ref[...])
    o_ref[...] = x_ref[...] + mod_ref[0][2:3, :] * _dot(y.astype(BF16), w_ref[...])


def _odd_out(x, modt, hf, hb, gate, w_bf16):
    tile = pl.BlockSpec((TB, D_MODEL), lambda i: (i, 0))
    return pl.pallas_call(
        _odd_out_kernel,
        grid=(N_TILES,),
        in_specs=[tile, pl.BlockSpec((1, 6, D_MODEL), lambda i: (i, 0, 0)), tile, tile, tile,
                  pl.BlockSpec((D_RNN, D_MODEL), lambda i: (0, 0))],
        out_specs=tile,
        out_shape=jax.ShapeDtypeStruct((N_TOK, D_MODEL), F32),
        compiler_params=_cparams("parallel"),
        name="odd_out",
    )(x, modt, hf, hb, gate, w_bf16)


def _order_key(x):
    b = lax.bitcast_convert_type(x, jnp.int32)
    return b ^ ((b >> 31) & 0x7FFFFFFF)


def _key_value(k):
    return lax.bitcast_convert_type(k ^ ((k >> 31) & 0x7FFFFFFF), F32)


def _precedes(ky, ry, kx, rx):
    return (ky + jnp.where(ry < rx, 1, 0)) > kx


def _merge_exchange(n):
    pairs = []
    t = max(1, (n - 1).bit_length())
    p = 1 << (t - 1)
    while p > 0:
        q, r, d = 1 << (t - 1), 0, p
        while d > 0:
            pairs.extend((i, i + d) for i in range(n - d) if (i & p) == r)
            d, q, r = q - p, q >> 1, p
        p >>= 1
    return pairs


def _top_rows(values, count, emit):
    n = len(values)
    sub = lax.broadcasted_iota(jnp.int32, (SUBLANES, TB), 0)
    keys = [_order_key(v) for v in values]
    rows = [sub + SUBLANES * i for i in range(n)]
    for i, j in _merge_exchange(n):
        swap = _precedes(keys[j], rows[j], keys[i], rows[i])
        keys[i], keys[j] = jnp.where(swap, keys[j], keys[i]), jnp.where(swap, keys[i], keys[j])
        rows[i], rows[j] = jnp.where(swap, rows[j], rows[i]), jnp.where(swap, rows[i], rows[j])
    lowest = jnp.full((SUBLANES, TB), jnp.iinfo(jnp.int32).min + 1, jnp.int32)
    for step in range(count):
        k = keys[0]
        for shift in (4, 2, 1):
            k = jnp.maximum(k, pltpu.roll(k, shift, 0))
        r = jnp.where(keys[0] == k, rows[0], jnp.iinfo(jnp.int32).max)
        for shift in (4, 2, 1):
            r = jnp.minimum(r, pltpu.roll(r, shift, 0))
        emit(step, _key_value(k[0:1, :]), r[0:1, :])
        depth = min(n - 1, count - 1 - step)
        column = rows[0] == r
        for d in range(depth):
            keys[d] = jnp.where(column, keys[d + 1], keys[d])
            rows[d] = jnp.where(column, rows[d + 1], rows[d])
        if depth == n - 1:
            keys[n - 1] = jnp.where(column, lowest, keys[n - 1])


def _router_kernel(x_ref, mod_ref, g_ref, wq_ref, keys_ref, h_ref, a_ref, b_ref, gate_ref,
                   q_s, v1_s, i1_s, v2_s, i2_s, sc_s, a_s, b_s):
    h = _rms_mod(x_ref[...], g_ref[...], mod_ref[0], 3, 4).astype(BF16)
    h_ref[...] = h
    q_s[...] = _dot(h, wq_ref[...])
    lax.fori_loop(0, PEER_HEADS, functools.partial(
        _route_head, keys_ref, a_ref, b_ref, gate_ref, q_s, v1_s, i1_s, v2_s, i2_s, sc_s, a_s, b_s), 0)


def _route_head(keys_ref, a_ref, b_ref, gate_ref, q_s, v1_s, i1_s, v2_s, i2_s, sc_s, a_s, b_s, hd, carry):
    for half, (v_s, i_s) in enumerate(((v1_s, i1_s), (v2_s, i2_s))):
        col = pl.multiple_of(hd * PEER_DK + half * (PEER_DK // 2), PEER_DK // 2)
        qh = q_s[:, pl.ds(col, PEER_DK // 2)].astype(BF16)
        st = _dot_nt(keys_ref[half, hd], qh)

        def emit(step, value, row, v_s=v_s, i_s=i_s):
            v_s[step:step + 1, :] = value
            i_s[step:step + 1, :] = row

        _top_rows([st[SUBLANES * i:SUBLANES * (i + 1), :] for i in range(PEER_NK // SUBLANES)], PEER_TOPK, emit)

    v1 = v1_s[...]
    v2 = v2_s[...]
    sub8 = lax.broadcasted_iota(jnp.int32, (8, TB), 0)
    pieces = [v1[0:1, :] + v2[0:8, :], v1[0:1, :] + v2[8:16, :]]
    for p in range(1, 8):
        pieces.append(jnp.where(sub8 < PEER_TOPK // (p + 1), v1[p:p + 1, :] + v2[0:8, :], -jnp.inf))
    pieces.append(v1[8:16, :] + v2[0:1, :])
    sub = lax.broadcasted_iota(jnp.int32, (PEER_TOPK, TB), 0)
    i1 = i1_s[...]
    i2 = i2_s[...]

    def emit2(step, value, pos):
        p = jnp.where(pos < 72, jnp.maximum((pos - 8) >> 3, 0), pos - 64)
        r = jnp.where(pos < 16, pos, jnp.where(pos < 72, pos & 7, 0))
        sc_s[step:step + 1, :] = value
        a_s[step:step + 1, :] = jnp.where(sub == p, i1, 0).sum(axis=0, keepdims=True)
        b_s[step:step + 1, :] = jnp.where(sub == r, i2, 0).sum(axis=0, keepdims=True)

    _top_rows(pieces, PEER_TOPK, emit2)
    sc = sc_s[...]
    e = jnp.exp(sc - sc[0:1, :])
    out_rows = pl.ds(pl.multiple_of(hd * PEER_TOPK, PEER_TOPK), PEER_TOPK)
    gate_ref[0, out_rows, :] = e / e.sum(axis=0, keepdims=True)
    a_ref[0, out_rows, :] = a_s[...].astype(F32)
    b_ref[0, out_rows, :] = b_s[...].astype(F32)
    return carry


def _router(x, modt, g, wq_bf16, keys_bf16, layer):
    kt = pl.BlockSpec((1, PEER_HEADS * PEER_TOPK, TB), lambda i: (i, 0, 0))
    kshape = jax.ShapeDtypeStruct((N_TILES, PEER_HEADS * PEER_TOPK, TB), F32)
    f32s = lambda: pltpu.VMEM((PEER_TOPK, TB), F32)
    i32s = lambda: pltpu.VMEM((PEER_TOPK, TB), jnp.int32)
    return pl.pallas_call(
        _router_kernel,
        grid=(N_TILES,),
        in_specs=[
            pl.BlockSpec((TB, D_MODEL), lambda i: (i, 0)),
            pl.BlockSpec((1, 6, D_MODEL), lambda i: (i, 0, 0)),
            pl.BlockSpec((1, D_MODEL), lambda i: (0, 0)),
            pl.BlockSpec((None, D_MODEL, PEER_HEADS * PEER_DK), lambda i: (layer, 0, 0)),
            pl.BlockSpec((None, 2, PEER_HEADS, PEER_NK, PEER_DK // 2), lambda i: (layer, 0, 0, 0, 0)),
        ],
        out_specs=[pl.BlockSpec((TB, D_MODEL), lambda i: (i, 0)), kt, kt, kt],
        out_shape=[jax.ShapeDtypeStruct((N_TOK, D_MODEL), BF16), kshape, kshape, kshape],
        scratch_shapes=[pltpu.VMEM((TB, PEER_HEADS * PEER_DK), F32),
                        f32s(), i32s(), f32s(), i32s(), f32s(), i32s(), i32s()],
        compiler_params=_cparams("parallel"),
        name="peer_router",
    )(x, modt, g.reshape(1, D_MODEL), wq_bf16, keys_bf16)


def _expert_kernel(h_ref, x_ref, mod_ref, at_ref, bt_ref, gt_ref, u_ref, v_ref, fg_ref, o_ref, *rest, final):
    if final:
        y_ref, g_s, acc_s, a_s, b_s, w_s, rt_s = rest
    else:
        g_s, acc_s, a_s, b_s, w_s, rt_s = rest
    grp = pl.program_id(1)

    @pl.when(grp == 0)
    def _():
        acc_s[...] = jnp.zeros_like(acc_s)
        for part in range(TE // TB):
            a_s[part * TB:(part + 1) * TB, :] = at_ref[part].T
            b_s[part * TB:(part + 1) * TB, :] = bt_ref[part].T
            w_s[part * TB:(part + 1) * TB, :] = gt_ref[part].T
        sub = lax.broadcasted_iota(jnp.int32, (PEER_NK, PEER_NK), 0).astype(F32).astype(BF16)
        one = jnp.ones((PEER_NK, PEER_NK), BF16)
        zero = jnp.zeros((PEER_NK, PEER_NK), BF16)

        def gate_grid(t, slot):
            a_row = a_s[pl.ds(t, 1), :].astype(BF16)
            b_row = b_s[pl.ds(t, 1), :].astype(BF16)
            w_row = jnp.broadcast_to(w_s[pl.ds(t, 1), :].astype(BF16), (PEER_NK, PEER_NK))
            m_t = jnp.where(a_row == sub, one, zero)
            r_t = jnp.where(b_row == sub, w_row, zero)
            rt_s[slot] = r_t.T
            g = _dot(m_t, rt_s[slot]).astype(BF16).astype(F32)
            return lax.bitcast_convert_type(g, jnp.uint32)

        def body(t, carry):
            slot = t & (G_UNROLL - 1)
            word = (gate_grid(t, slot) >> 16) | (gate_grid(t + TE // 2, slot + G_UNROLL) & jnp.uint32(0xFFFF0000))
            g_s[pl.ds(t, PEER_NK, stride=G_PITCH), :] = word
            return carry

        lax.fori_loop(0, TE // 2, body, 0, unroll=G_UNROLL)

    h = h_ref[...]
    acc = acc_s[...]
    for sb in range(PEER_GROUP // EXPERT_SUB):
        e0 = sb * EXPERT_SUB
        hmat = _dot_nt(h, u_ref[e0:e0 + EXPERT_SUB, :])
        parts = []
        for j in range(EXPERT_SUB // PEER_NK):
            chunk = grp * CHUNKS_PER_GROUP + sb * (EXPERT_SUB // PEER_NK) + j
            word = g_s[pl.ds(pl.multiple_of(chunk * G_PITCH, 8), TE // 2), :]
            gates = jnp.concatenate(
                [lax.bitcast_convert_type(word << 16, F32),
                 lax.bitcast_convert_type(word & jnp.uint32(0xFFFF0000), F32)], axis=0)
            parts.append((_gelu_tanh(hmat[:, j * PEER_NK:(j + 1) * PEER_NK]) * gates).astype(BF16))
        acc = acc + _dot(jnp.concatenate(parts, axis=1), v_ref[e0:e0 + EXPERT_SUB, :])
    acc_s[...] = acc

    @pl.when(grp == PEER_NGROUPS - 1)
    def _():
        xo = x_ref[...] + mod_ref[0][5:6, :] * acc_s[...]
        if final:
            y = xo * lax.rsqrt(jnp.mean(xo * xo, axis=-1, keepdims=True) + EPS) * fg_ref[...]
            tile = pl.program_id(0)

            @pl.when(tile < N_PROMPT // TE)
            def _():
                o_ref[...] = y

            @pl.when(tile >= N_PROMPT // TE)
            def _():
                y_ref[...] = y
        else:
            o_ref[...] = xo


def _experts(h_bf16, x, modt_e, a_t, b_t, g_t, u_bf16, v_bf16, layer, final_g, final):
    tile = pl.BlockSpec((TE, D_MODEL), lambda i, g: (i, 0))
    kt = pl.BlockSpec((TE // TB, PEER_HEADS * PEER_TOPK, TB), lambda i, g: (i, 0, 0))
    wblk = pl.BlockSpec((None, PEER_GROUP, D_MODEL), lambda i, g: (layer, g, 0))
    xs = jax.ShapeDtypeStruct((N_TOK, D_MODEL), F32)
    n_prompt = N_PROMPT // TE
    sel = lambda: pltpu.VMEM((TE, PEER_HEADS * PEER_TOPK), F32)
    return pl.pallas_call(
        functools.partial(_expert_kernel, final=final),
        grid=(N_TOK // TE, PEER_NGROUPS),
        in_specs=[tile, tile, pl.BlockSpec((1, 6, D_MODEL), lambda i, g: (i, 0, 0)), kt, kt, kt, wblk, wblk,
                  pl.BlockSpec((1, D_MODEL), lambda i, g: (0, 0))],
        out_specs=[pl.BlockSpec((TE, D_MODEL), lambda i, g: (jnp.minimum(i, n_prompt - 1), 0)),
                   pl.BlockSpec((TE, D_MODEL), lambda i, g: (jnp.maximum(i - n_prompt, 0), 0))] if final else tile,
        out_shape=[jax.ShapeDtypeStruct((N_PROMPT, D_MODEL), F32),
                   jax.ShapeDtypeStruct((N_SAMPLE, D_MODEL), F32)] if final else xs,
        scratch_shapes=[pltpu.VMEM((PEER_NK * G_PITCH, PEER_NK), jnp.uint32), pltpu.VMEM((TE, D_MODEL), F32),
                        sel(), sel(), sel(), pltpu.VMEM((2 * G_UNROLL, PEER_NK, PEER_NK), BF16)],
        compiler_params=_cparams("arbitrary" if final else "parallel", "arbitrary"),
        name="peer_experts_final" if final else "peer_experts",
    )(h_bf16, x, modt_e, a_t, b_t, g_t, u_bf16, v_bf16, final_g.reshape(1, D_MODEL))


def _peer(x, modt, modt_e, norm_g, peer_bf16, layer, final_g, final):
    w_q, keys, u, v = peer_bf16
    h, a_t, b_t, g_t = _router(x, modt, norm_g, w_q, keys, layer)
    return _experts(h, x, modt_e, a_t, b_t, g_t, u, v, layer, final_g, final)


def kernel(x_prompt, x_sample, cache_k, cache_v, state_hgrn_fwd, state_hgrn_bwd, state_lru_fwd, state_lru_bwd, c, c_ctx, norm1_g, norm2_g, w_mod, b_mod, w_in_even, w_out_even, hgrn_lb_fwd, hgrn_lb_bwd, hgrn_gnorm_g, attn_sink, w_in_odd, w_out_odd, conv_w, conv_b, rg_w_a, rg_b_a, rg_w_x, rg_b_x, rg_lambda, peer_w_q, peer_keys, peer_u, peer_v, final_g):
    x = jnp.concatenate([x_prompt.reshape(N_PROMPT, D_MODEL), x_sample.reshape(N_SAMPLE, D_MODEL)], axis=0)
    cond = jnp.concatenate([c_ctx[None, :], c, jnp.zeros((COND_PAD - N_COND, D_MODEL), F32)], axis=0)
    mod = _modulation(cond, w_mod, b_mod)
    tile_row = np.asarray([0] * PROMPT_TILES + [1 + i // TILES_PER_SAMPLE for i in range(N_TILES - PROMPT_TILES)])
    modt = [mod[l][tile_row].reshape(N_TILES, 6, D_MODEL) for l in range(DEPTH)]
    modt_e = [m[::TE // TB] for m in modt]

    widths = (HG_W, HG_W, HG_W, HG_W, HG_W, ATT_W, KV_W, KV_W)
    zq, zf, zb, zi, zg, aq, ak, av = _in_proj(x, modt[0], norm1_g[0], w_in_even[0].astype(BF16), widths, "even_in")
    zero_state = jnp.zeros((1, HG_HEADS, HG_DV, HG_DK), F32)
    s0f = jnp.concatenate([zero_state, jnp.swapaxes(state_hgrn_fwd[:, 0], -1, -2)], axis=0)
    s0b = jnp.concatenate([zero_state, jnp.swapaxes(state_hgrn_bwd[:, 0], -1, -2)], axis=0)
    o_f, o_b, sf_t, sb_t = _hgrn_scan(zq, zf, zb, zi, hgrn_lb_fwd, hgrn_lb_bwd, s0f, s0b)
    oatt = _attention(aq, ak, av, cache_k[:, 0].reshape(DEC_BATCH, PAST_LEN, KV_W),
                      cache_v[:, 0].reshape(DEC_BATCH, PAST_LEN, KV_W), attn_sink[0])
    x = _even_out(x, modt[0], o_f, o_b, zg, oatt, hgrn_gnorm_g[0], w_out_even[0].astype(BF16))
    peer_bf16 = (peer_w_q.astype(BF16), peer_keys.astype(BF16), peer_u.astype(BF16), peer_v.astype(BF16))
    x = _peer(x, modt[0], modt_e[0], norm2_g[0], peer_bf16, 0, final_g, False)

    gate, xr = _in_proj(x, modt[1], norm1_g[1], w_in_odd[0].astype(BF16), (D_RNN, D_RNN), "odd_in")
    xc = _dwconv(xr, conv_w[0], conv_b[0])
    zero_h = jnp.zeros((1, 1, D_RNN), F32)
    h0f = jnp.concatenate([zero_h, state_lru_fwd[:, 0][:, None, :]], axis=0)
    h0b = jnp.concatenate([zero_h, state_lru_bwd[:, 0][:, None, :]], axis=0)
    hf, hb = _lru_scan(xc, rg_w_a[0].astype(BF16), rg_b_a[0], rg_w_x[0].astype(BF16), rg_b_x[0],
                       rg_lambda[0], h0f, h0b)
    x = _odd_out(x, modt[1], hf, hb, gate, w_out_odd[0].astype(BF16))
    y_p, y_s = _peer(x, modt[1], modt_e[1], norm2_g[1], peer_bf16, 1, final_g, True)

    y_prompt = y_p.reshape(BATCH, SEQ, D_MODEL)
    y_sample = y_s.reshape(DEC_BATCH, DEC_SEQ, D_MODEL)
    new_k = ak[:N_PROMPT].reshape(BATCH, 1, SEQ, KV_HEADS, HEAD_DIM)
    new_v = av[:N_PROMPT].reshape(BATCH, 1, SEQ, KV_HEADS, HEAD_DIM)
    new_hf = jnp.swapaxes(sf_t[:BATCH], -1, -2)[:, None]
    new_hb = jnp.swapaxes(sb_t[:BATCH], -1, -2)[:, None]
    new_lf = hf[:N_PROMPT].reshape(BATCH, SEQ, D_RNN)[:, -1][:, None, :]
    new_lb = hb[:N_PROMPT].reshape(BATCH, SEQ, D_RNN)[:, 0][:, None, :]
    return (y_prompt, y_sample, new_k, new_v, new_hf, new_hb, new_lf, new_lb)
```

```python
import functools

import numpy as np
import jax
import jax.numpy as jnp
from jax import lax
from jax.experimental import pallas as pl
from jax.experimental.pallas import tpu as pltpu

F32 = jnp.float32
BF16 = jnp.bfloat16

D_MODEL = 1024
BATCH = 16
SEQ = 256
DEPTH = 2
DEC_BATCH = 8
DEC_SEQ = 2048
PAST_LEN = 512
GRID_W = 64
EPS = 1e-6
HG_DK = 128
HG_DV = 128
HG_HEADS = 4
HG_CHUNK = 64
HG_BLOCK = 256
HG_W = HG_HEADS * HG_DK
HEAD_DIM = 64
N_HEADS = 8
KV_HEADS = 2
Q_PER_KV = 4
WINDOW = 128
ATT_BLOCK = 128
ROPE_BASE = 10000.0
ATT_W = N_HEADS * HEAD_DIM
KV_W = KV_HEADS * HEAD_DIM
D_RNN = 1024
RG_BLOCKS = 4
RG_BW = 256
RG_C = 8.0
PEER_HEADS = 8
PEER_NK = 128
PEER_DK = 256
PEER_TOPK = 16

N_PROMPT = BATCH * SEQ
N_SAMPLE = DEC_BATCH * DEC_SEQ
N_TOK = N_PROMPT + N_SAMPLE
TB = 256
N_TILES = N_TOK // TB
PROMPT_TILES = N_PROMPT // TB
TILES_PER_SAMPLE = DEC_SEQ // TB
N_COND = 1 + DEC_BATCH
COND_PAD = 16

TE = 512
PEER_GROUP = 2048
PEER_NGROUPS = PEER_NK * PEER_NK // PEER_GROUP
CHUNKS_PER_GROUP = PEER_GROUP // PEER_NK
EXPERT_SUB = 1024
G_PITCH = TE // 2 + 8
G_UNROLL = 16

LRU_CHUNK = 256
SUBLANES = 8

VMEM_LIMIT = 58 * 1024 * 1024


def _cparams(*sem):
    return pltpu.CompilerParams(dimension_semantics=sem, vmem_limit_bytes=VMEM_LIMIT)


def _dot(a, b):
    return jnp.dot(a, b, preferred_element_type=F32)


def _dot_nt(a, b):
    return lax.dot_general(a, b, (((1,), (1,)), ((), ())), preferred_element_type=F32)


def _dot_tn(a, b):
    return lax.dot_general(a, b, (((0,), (0,)), ((), ())), preferred_element_type=F32)


def _sigmoid(x):
    return 1.0 / (1.0 + jnp.exp(-x))


def _silu(x):
    return x * _sigmoid(x)


def _gelu_tanh(x):
    c = np.float32(np.sqrt(2.0 / np.pi))
    return x * (0.5 * (1.0 + jnp.tanh(c * (x + 0.044715 * (x * x * x)))))


def _rms_mod(x, g, mod, sh, sc):
    y = x * lax.rsqrt(jnp.mean(x * x, axis=-1, keepdims=True) + EPS) * g
    return y * (1.0 + mod[sc:sc + 1, :]) + mod[sh:sh + 1, :]


def _mod_kernel(c_ref, w_ref, b_ref, o_ref):
    c = c_ref[...]
    s = _silu(c).astype(BF16)
    o_ref[0] = _dot(s, w_ref[0].astype(BF16)) + b_ref[0]


def _modulation(cond, w_mod, b_mod):
    nb = 6 * D_MODEL // 1024
    return pl.pallas_call(
        _mod_kernel,
        grid=(DEPTH, nb),
        in_specs=[
            pl.BlockSpec((COND_PAD, D_MODEL), lambda l, n: (0, 0)),
            pl.BlockSpec((1, D_MODEL, 1024), lambda l, n: (l, 0, n)),
            pl.BlockSpec((1, 1, 1024), lambda l, n: (l, 0, n)),
        ],
        out_specs=pl.BlockSpec((1, COND_PAD, 1024), lambda l, n: (l, 0, n)),
        out_shape=jax.ShapeDtypeStruct((DEPTH, COND_PAD, 6 * D_MODEL), F32),
        compiler_params=_cparams("parallel", "parallel"),
        name="modulation",
    )(cond, w_mod, b_mod.reshape(DEPTH, 1, 6 * D_MODEL))


def _token_specs(arrays, width):
    if not isinstance(arrays, tuple):
        return [pl.BlockSpec((TB, width), lambda i: (i, 0))]
    return [pl.BlockSpec((TB, width), lambda i: (jnp.minimum(i, PROMPT_TILES - 1), 0)),
            pl.BlockSpec((TB, width), lambda i: (jnp.maximum(i - PROMPT_TILES, 0), 0))]


def _token_tile(refs):
    if len(refs) == 1:
        return refs[0][...]
    return jnp.where(pl.program_id(0) < PROMPT_TILES, refs[0][...], refs[1][...])


def _in_proj_kernel(*refs, widths, n_x):
    x_refs, (mod_ref, g_ref, w_ref), o_refs = refs[:n_x], refs[n_x:n_x + 3], refs[n_x + 3:]
    h = _rms_mod(_token_tile(x_refs), g_ref[...], mod_ref[0], 0, 1)
    z = _dot(h.astype(BF16), w_ref[...])
    off = 0
    for o_ref, wd in zip(o_refs, widths):
        o_ref[...] = z[:, off:off + wd]
        off += wd


def _in_proj(x, modt, g, w_bf16, widths, name):
    n_out = sum(widths)
    xs = x if isinstance(x, tuple) else (x,)
    return pl.pallas_call(
        functools.partial(_in_proj_kernel, widths=widths, n_x=len(xs)),
        grid=(N_TILES,),
        in_specs=_token_specs(x, D_MODEL) + [
            pl.BlockSpec((1, 6, D_MODEL), lambda i: (i, 0, 0)),
            pl.BlockSpec((1, D_MODEL), lambda i: (0, 0)),
            pl.BlockSpec((D_MODEL, n_out), lambda i: (0, 0)),
        ],
        out_specs=[pl.BlockSpec((TB, wd), lambda i: (i, 0)) for wd in widths],
        out_shape=[jax.ShapeDtypeStruct((N_TOK, wd), F32) for wd in widths],
        compiler_params=_cparams("parallel"),
        name=name,
    )(*xs, modt, g.reshape(1, D_MODEL), w_bf16)


def _hgrn_direction(zq, zgate, zi, lb, states, fwd):
    n = HG_CHUNK
    row = lax.broadcasted_iota(jnp.int32, (n, n), 0)
    col = lax.broadcasted_iota(jnp.int32, (n, n), 1)
    keep = (col <= row) if fwd else (col >= row)
    tri = jnp.where(keep, 1.0, 0.0).astype(BF16)
    q = _silu(zq)
    f = lb + (1.0 - lb) * _sigmoid(zgate)
    g = jnp.log(f)
    k = (1.0 - lb) * _sigmoid(-zgate)
    g_hi = g.astype(BF16)
    g_lo = (g - g_hi.astype(F32)).astype(BF16)
    b = _dot(tri, g_hi) + _dot(tri, g_lo)
    mid = n // 2 - 1 if fwd else n // 2
    last = n - 1 if fwd else 0
    bm = b[mid:mid + 1, :]
    bl = b[last:last + 1, :]
    qi = (q * jnp.exp(b - bm)).astype(BF16)
    ki = (k * jnp.exp(bm - b)).astype(BF16)
    qs = (q * jnp.exp(b)).astype(BF16)
    ks = (k * jnp.exp(bl - b)).astype(BF16)
    vb = zi.astype(BF16)
    decay = jnp.exp(bl)
    outs = []
    new_states = []
    for h in range(HG_HEADS):
        sl = slice(h * HG_DK, (h + 1) * HG_DK)
        a = jnp.where(keep, _dot_nt(qi[:, sl], ki[:, sl]), 0.0)
        st = states[h]
        outs.append(_dot(a.astype(BF16), vb[:, sl]) + _dot_nt(qs[:, sl], st.astype(BF16)))
        new_states.append(decay[:, sl] * st + _dot_tn(vb[:, sl], ks[:, sl]))
    return jnp.concatenate(outs, axis=1), new_states


def _lower_bound(p):
    e = jnp.exp(p - p.max(axis=0, keepdims=True))
    return e[0:1, :] / e.sum(axis=0, keepdims=True)


def _hgrn_block(zq_ref, zgate_ref, zi_ref, lb, st_ref, o_ref, fwd):
    states = [st_ref[h] for h in range(HG_HEADS)]
    n_sub = HG_BLOCK // HG_CHUNK
    for c in (range(n_sub) if fwd else reversed(range(n_sub))):
        rows = slice(c * HG_CHUNK, (c + 1) * HG_CHUNK)
        o_ref[rows, :], states = _hgrn_direction(zq_ref[rows, :], zgate_ref[rows, :], zi_ref[rows, :], lb, states, fwd)
    for h in range(HG_HEADS):
        st_ref[h] = states[h]


def _hgrn_kernel(tbl_ref, zqf_ref, zff_ref, zif_ref, zqb_ref, zbb_ref, zib_ref, lbf_ref, lbb_ref,
                 s0f_ref, s0b_ref, of_ref, ob_ref, sf_ref, sb_ref, stf, stb):
    s = pl.program_id(0)

    @pl.when(tbl_ref[4, s] == 1)
    def _():
        stf[...] = s0f_ref[0]
        stb[...] = s0b_ref[0]

    _hgrn_block(zqf_ref, zff_ref, zif_ref, _lower_bound(lbf_ref[...]), stf, of_ref, True)
    _hgrn_block(zqb_ref, zbb_ref, zib_ref, _lower_bound(lbb_ref[...]), stb, ob_ref, False)

    @pl.when(tbl_ref[5, s] == 1)
    def _():
        sf_ref[0] = stf[...]
        sb_ref[0] = stb[...]


def _scan_table(chunk):
    rows = []
    blk = 0
    for (nseq, length, has_init) in ((BATCH, SEQ, False), (DEC_BATCH, DEC_SEQ, True)):
        nc = length // chunk
        for b in range(nseq):
            sid = b if not has_init else BATCH + b
            for i in range(nc):
                rows.append((blk + i, blk + nc - 1 - i, (1 + b) if has_init else 0, sid,
                             int(i == 0), int(i == nc - 1)))
            blk += nc
    return np.asarray(rows, np.int32).T.copy()


def _hgrn_scan(zq, zf, zb, zi, lbf, lbb, s0f_t, s0b_t):
    tbl = _scan_table(HG_BLOCK)
    n_steps = tbl.shape[1]
    n_seq = BATCH + DEC_BATCH
    blk = lambda r: pl.BlockSpec((HG_BLOCK, HG_W), lambda s, t: (t[r, s], 0))
    vec = pl.BlockSpec(lbf.shape, lambda s, t: (0, 0))
    st_in = pl.BlockSpec((1, HG_HEADS, HG_DV, HG_DK), lambda s, t: (t[2, s], 0, 0, 0))
    st_out = pl.BlockSpec((1, HG_HEADS, HG_DV, HG_DK), lambda s, t: (t[3, s], 0, 0, 0))
    grid_spec = pltpu.PrefetchScalarGridSpec(
        num_scalar_prefetch=1,
        grid=(n_steps,),
        in_specs=[blk(0), blk(0), blk(0), blk(1), blk(1), blk(1), vec, vec, st_in, st_in],
        out_specs=[blk(0), blk(1), st_out, st_out],
        scratch_shapes=[pltpu.VMEM((HG_HEADS, HG_DV, HG_DK), F32),
                        pltpu.VMEM((HG_HEADS, HG_DV, HG_DK), F32)],
    )
    st_shape = jax.ShapeDtypeStruct((n_seq, HG_HEADS, HG_DV, HG_DK), F32)
    return pl.pallas_call(
        _hgrn_kernel,
        grid_spec=grid_spec,
        out_shape=[jax.ShapeDtypeStruct((N_TOK, HG_W), F32), jax.ShapeDtypeStruct((N_TOK, HG_W), F32),
                   st_shape, st_shape],
        compiler_params=_cparams("arbitrary"),
        name="hgrn_scan",
    )(jnp.asarray(tbl), zq, zf, zi, zq, zb, zi, lbf, lbb, s0f_t, s0b_t)


def _rope(x, cos, sgn_sin):
    w = x.shape[1]
    lane = lax.broadcasted_iota(jnp.int32, x.shape, 1)
    partner = jnp.where((lane & 31) < 16, pltpu.roll(x, w - 16, 1), pltpu.roll(x, 16, 1))
    return x * cos + partner * sgn_sin


def _dup_kv_half(x, kv):
    lane = lax.broadcasted_iota(jnp.int32, x.shape, 1)
    own = (lane < HEAD_DIM) if kv == 0 else (lane >= HEAD_DIM)
    xs = jnp.where(own, x, 0.0)
    return xs + pltpu.roll(xs, HEAD_DIM, 1)


def _attend(q, key_parts, sink_ref, o_ref):
    lane = lax.broadcasted_iota(jnp.int32, (ATT_BLOCK, 128), 1)
    lo = lane < HEAD_DIM
    for kv in range(KV_HEADS):
        qs = []
        for gq in range(Q_PER_KV):
            hd = kv * Q_PER_KV + gq
            pair = q[:, (hd // 2) * 128:(hd // 2 + 1) * 128]
            qs.append(jnp.where(lo if hd % 2 == 0 else ~lo, pair, 0.0))
        qst = jnp.concatenate(qs, axis=0).astype(BF16)
        logits = []
        vals = []
        for (k, v, mask) in key_parts:
            kd = _dup_kv_half(k, kv).astype(BF16)
            vals.append(_dup_kv_half(v, kv).astype(BF16))
            s = _dot_nt(qst, kd)
            if mask is not None:
                s = jnp.where(mask, s, -1e30)
            logits.append(s)
        m = logits[0].max(axis=-1, keepdims=True)
        for s in logits[1:]:
            m = jnp.maximum(m, s.max(axis=-1, keepdims=True))
        sink_col = jnp.concatenate(
            [jnp.full((ATT_BLOCK, 1), sink_ref[kv * Q_PER_KV + gq], F32) for gq in range(Q_PER_KV)], axis=0)
        m = jnp.maximum(m, sink_col)
        den = jnp.exp(sink_col - m)
        acc = jnp.zeros((Q_PER_KV * ATT_BLOCK, 128), F32)
        for s, v in zip(logits, vals):
            p = jnp.exp(s - m)
            den = den + p.sum(axis=-1, keepdims=True)
            acc = acc + _dot(p.astype(BF16), v)
        out = acc / den
        for pr in range(Q_PER_KV // 2):
            r0 = out[(2 * pr) * ATT_BLOCK:(2 * pr + 1) * ATT_BLOCK]
            r1 = out[(2 * pr + 1) * ATT_BLOCK:(2 * pr + 2) * ATT_BLOCK]
            c0 = (kv * Q_PER_KV // 2 + pr) * 128
            o_ref[:, c0:c0 + 128] = jnp.where(lo, r0, r1)


def _attn_prompt_kernel(sink_ref, q_ref, k_ref, v_ref, o_ref):
    q = q_ref[...] * np.float32(HEAD_DIM ** -0.5)
    _attend(q, [(k_ref[...], v_ref[...], None)], sink_ref, o_ref)


def _attn_latent_kernel(sink_ref, q_ref, kp_ref, kc_ref, kn_ref, vp_ref, vc_ref, vn_ref,
                        ck_ref, cv_ref, cos_ref, sin_ref, o_ref):
    j = pl.program_id(1)
    nq = pl.num_programs(1)
    qstart = pl.multiple_of((j + 1) * ATT_BLOCK, ATT_BLOCK)
    kstart = pl.multiple_of(j * ATT_BLOCK, ATT_BLOCK)
    cq = cos_ref[pl.ds(qstart, ATT_BLOCK), :]
    sq = sin_ref[pl.ds(qstart, ATT_BLOCK), :]
    q = _rope(q_ref[...], jnp.concatenate([cq] * 4, axis=1), jnp.concatenate([sq] * 4, axis=1))
    q = q * np.float32(HEAD_DIM ** -0.5)
    k_loc = jnp.concatenate([kp_ref[...], kc_ref[...], kn_ref[...]], axis=0)
    v_loc = jnp.concatenate([vp_ref[...], vc_ref[...], vn_ref[...]], axis=0)
    k_loc = _rope(k_loc, cos_ref[pl.ds(kstart, 3 * ATT_BLOCK), :], sin_ref[pl.ds(kstart, 3 * ATT_BLOCK), :])
    r = lax.broadcasted_iota(jnp.int32, (Q_PER_KV * ATT_BLOCK, 3 * ATT_BLOCK), 0) & (ATT_BLOCK - 1)
    c = lax.broadcasted_iota(jnp.int32, (Q_PER_KV * ATT_BLOCK, 3 * ATT_BLOCK), 1)
    rel = c - ATT_BLOCK - r
    c_lo = jnp.where(j > 0, 0, ATT_BLOCK)
    c_hi = jnp.where(j < nq - 1, 3 * ATT_BLOCK, 2 * ATT_BLOCK)
    ok = (jnp.abs(rel) <= WINDOW) & (c >= c_lo) & (c < c_hi)
    _attend(q, [(ck_ref[0], cv_ref[0], None), (k_loc, v_loc, ok)], sink_ref, o_ref)


def _rope_tables():
    pos = np.arange(-ATT_BLOCK, DEC_SEQ + ATT_BLOCK)
    nf = HEAD_DIM // 4
    inv = (1.0 / (ROPE_BASE ** (np.arange(nf, dtype=np.float32) / nf))).astype(np.float32)
    rows = (pos // GRID_W).astype(np.float32)
    cols = (pos % GRID_W).astype(np.float32)
    ar = rows[:, None] * inv[None, :]
    ac = cols[:, None] * inv[None, :]
    cos64 = np.concatenate([np.cos(ar), np.cos(ar), np.cos(ac), np.cos(ac)], axis=1)
    sin64 = np.concatenate([-np.sin(ar), np.sin(ar), -np.sin(ac), np.sin(ac)], axis=1)
    cos = np.concatenate([cos64, cos64], axis=1).astype(np.float32)
    sin = np.concatenate([sin64, sin64], axis=1).astype(np.float32)
    return jnp.asarray(cos), jnp.asarray(sin)


def _attention(aq, ak, av, ck, cv, sink):
    smem = pl.BlockSpec(memory_space=pltpu.SMEM)
    nqp = SEQ // ATT_BLOCK
    o_prompt = pl.pallas_call(
        _attn_prompt_kernel,
        grid=(BATCH, nqp),
        in_specs=[
            smem,
            pl.BlockSpec((ATT_BLOCK, ATT_W), lambda b, j: (b * nqp + j, 0)),
            pl.BlockSpec((SEQ, KV_W), lambda b, j: (b, 0)),
            pl.BlockSpec((SEQ, KV_W), lambda b, j: (b, 0)),
        ],
        out_specs=pl.BlockSpec((ATT_BLOCK, ATT_W), lambda b, j: (b * nqp + j, 0)),
        out_shape=jax.ShapeDtypeStruct((N_PROMPT, ATT_W), F32),
        compiler_params=_cparams("parallel", "parallel"),
        name="attn_prompt",
    )(sink, aq, ak, av)

    nq = DEC_SEQ // ATT_BLOCK
    base = N_PROMPT // ATT_BLOCK
    cos, sin = _rope_tables()
    kv_blk = lambda d: pl.BlockSpec(
        (ATT_BLOCK, KV_W), lambda b, j: (base + b * nq + jnp.clip(j + d, 0, nq - 1), 0))
    tab = pl.BlockSpec((DEC_SEQ + 2 * ATT_BLOCK, 128), lambda b, j: (0, 0))
    o_latent = pl.pallas_call(
        _attn_latent_kernel,
        grid=(DEC_BATCH, nq),
        in_specs=[
            smem,
            pl.BlockSpec((ATT_BLOCK, ATT_W), lambda b, j: (base + b * nq + j, 0)),
            kv_blk(-1), kv_blk(0), kv_blk(1), kv_blk(-1), kv_blk(0), kv_blk(1),
            pl.BlockSpec((1, PAST_LEN, KV_W), lambda b, j: (b, 0, 0)),
            pl.BlockSpec((1, PAST_LEN, KV_W), lambda b, j: (b, 0, 0)),
            tab, tab,
        ],
        out_specs=pl.BlockSpec((ATT_BLOCK, ATT_W), lambda b, j: (b * nq + j, 0)),
        out_shape=jax.ShapeDtypeStruct((N_SAMPLE, ATT_W), F32),
        compiler_params=_cparams("parallel", "parallel"),
        name="attn_latent",
    )(sink, aq, ak, ak, ak, av, av, av, ck, cv, cos, sin)
    return o_prompt, o_latent


def _even_out_kernel(xp_ref, xs_ref, mod_ref, of_ref, ob_ref, zg_ref, oattp_ref, oatts_ref, gn_ref, w_ref, o_ref):
    x = _token_tile((xp_ref, xs_ref))
    oatt = _token_tile((oattp_ref, oatts_ref))
    o = of_ref[...] + ob_ref[...]
    parts = []
    for h in range(HG_HEADS):
        oh = o[:, h * HG_DV:(h + 1) * HG_DV]
        parts.append(oh * lax.rsqrt(jnp.mean(oh * oh, axis=-1, keepdims=True) + EPS))
    o_hg = jnp.concatenate(parts, axis=1) * gn_ref[...] * _silu(zg_ref[...])
    y = _dot(o_hg.astype(BF16), w_ref[0:HG_W, :]) + _dot(oatt.astype(BF16), w_ref[HG_W:, :])
    o_ref[...] = x + mod_ref[0][2:3, :] * y


def _even_out(x_pair, modt, o_f, o_b, zg, oatt_pair, gn_g, w_bf16):
    tile = lambda w: pl.BlockSpec((TB, w), lambda i: (i, 0))
    return pl.pallas_call(
        _even_out_kernel,
        grid=(N_TILES,),
        in_specs=_token_specs(x_pair, D_MODEL) + [
            pl.BlockSpec((1, 6, D_MODEL), lambda i: (i, 0, 0)),
            tile(HG_W), tile(HG_W), tile(HG_W)] + _token_specs(oatt_pair, ATT_W) + [
            pl.BlockSpec((1, HG_W), lambda i: (0, 0)),
            pl.BlockSpec((HG_W + ATT_W, D_MODEL), lambda i: (0, 0)),
        ],
        out_specs=tile(D_MODEL),
        out_shape=jax.ShapeDtypeStruct((N_TOK, D_MODEL), F32),
        compiler_params=_cparams("parallel"),
        name="even_out",
    )(*x_pair, modt, o_f, o_b, zg, *oatt_pair, gn_g.reshape(1, HG_W), w_bf16)


def _conv_kernel(xc_ref, xp_ref, xn_ref, w_ref, b_ref, o_ref):
    i = pl.program_id(0)
    in_sample = i >= PROMPT_TILES
    pos = (i - PROMPT_TILES) % TILES_PER_SAMPLE
    has_prev = in_sample & (pos != 0)
    has_next = in_sample & (pos != TILES_PER_SAMPLE - 1)
    prev = jnp.where(has_prev, xp_ref[...], 0.0)
    nxt = jnp.where(has_next, xn_ref[...], 0.0)
    ext = jnp.concatenate([prev, xc_ref[...], nxt], axis=0)
    n = TB + 16
    w = w_ref[...]
    acc = ext[8:8 + TB] * w[2:3, :] + b_ref[...]
    acc = acc + pltpu.roll(ext, 2, 0)[8:8 + TB] * w[0:1, :]
    acc = acc + pltpu.roll(ext, 1, 0)[8:8 + TB] * w[1:2, :]
    acc = acc + pltpu.roll(ext, n - 1, 0)[8:8 + TB] * w[3:4, :]
    o_ref[...] = acc


def _dwconv(xr, cw, cb):
    r8 = TB // 8
    last8 = N_TOK // 8 - 1
    return pl.pallas_call(
        _conv_kernel,
        grid=(N_TILES,),
        in_specs=[
            pl.BlockSpec((TB, D_RNN), lambda i: (i, 0)),
            pl.BlockSpec((8, D_RNN), lambda i: (jnp.maximum(i * r8 - 1, 0), 0)),
            pl.BlockSpec((8, D_RNN), lambda i: (jnp.minimum((i + 1) * r8, last8), 0)),
            pl.BlockSpec((4, D_RNN), lambda i: (0, 0)),
            pl.BlockSpec((1, D_RNN), lambda i: (0, 0)),
        ],
        out_specs=pl.BlockSpec((TB, D_RNN), lambda i: (i, 0)),
        out_shape=jax.ShapeDtypeStruct((N_TOK, D_RNN), F32),
        compiler_params=_cparams("parallel"),
        name="dwconv",
    )(xr, xr, xr, cw, cb.reshape(1, D_RNN))


def _log1p(y):
    w = 1.0 + y
    return jnp.where(w == 1.0, y, jnp.log(w) * (y / (w - 1.0)))


def _expm1_given(x, e):
    return jnp.where(e == 1.0, x, (e - 1.0) * (x / jnp.log(e)))


def _sigmoid_tanh(x):
    return 0.5 + 0.5 * jnp.tanh(0.5 * x)


def _lru_direction(xc, wa_ref, ba, wx_ref, bx, lam, h_ref, o_ref, fwd):
    n = LRU_CHUNK
    xb = xc.astype(BF16)
    ra, ri = [], []
    for blk in range(RG_BLOCKS):
        sl = slice(blk * RG_BW, (blk + 1) * RG_BW)
        ra.append(_dot(xb[:, sl], wa_ref[blk]))
        ri.append(_dot(xb[:, sl], wx_ref[blk]))
    r = _sigmoid_tanh(jnp.concatenate(ra, axis=1) + ba)
    gi = _sigmoid_tanh(jnp.concatenate(ri, axis=1) + bx)
    neg = -lam
    softplus = jnp.maximum(neg, 0.0) + _log1p(jnp.exp(-jnp.abs(neg)))
    log_a = (-RG_C * softplus) * r
    a = jnp.exp(log_a)
    u = jnp.sqrt(-_expm1_given(2.0 * log_a, a * a)) * (gi * xc)
    sub = lax.broadcasted_iota(jnp.int32, (SUBLANES, D_RNN), 0)
    carry = h_ref[...]
    groups = range(n // SUBLANES)
    for g in (groups if fwd else reversed(groups)):
        rows = slice(g * SUBLANES, (g + 1) * SUBLANES)
        ag, ug = a[rows, :], u[rows, :]
        for d in (1, 2, 4):
            live = (sub >= d) if fwd else (sub < SUBLANES - d)
            shift = d if fwd else SUBLANES - d
            a_s = jnp.where(live, pltpu.roll(ag, shift, 0), 1.0)
            u_s = jnp.where(live, pltpu.roll(ug, shift, 0), 0.0)
            ug = ag * u_s + ug
            ag = ag * a_s
        h = ug + ag * carry
        o_ref[rows, :] = h
        carry = h[SUBLANES - 1:SUBLANES, :] if fwd else h[0:1, :]
    h_ref[...] = carry


def _lru_kernel(tbl_ref, xf_ref, xb_ref, wa_ref, ba_ref, wx_ref, bx_ref, lam_ref, h0f_ref, h0b_ref,
                hf_ref, hb_ref, hf_s, hb_s):
    s = pl.program_id(0)

    @pl.when(tbl_ref[4, s] == 1)
    def _():
        hf_s[...] = h0f_ref[0]
        hb_s[...] = h0b_ref[0]

    _lru_direction(xf_ref[...], wa_ref.at[0], ba_ref[0:1, :], wx_ref.at[0], bx_ref[0:1, :],
                   lam_ref[0:1, :], hf_s, hf_ref, True)
    _lru_direction(xb_ref[...], wa_ref.at[1], ba_ref[1:2, :], wx_ref.at[1], bx_ref[1:2, :],
                   lam_ref[1:2, :], hb_s, hb_ref, False)


def _lru_scan(xc, w_a_bf16, b_a, w_x_bf16, b_x, lam, h0f_all, h0b_all):
    tbl = _scan_table(LRU_CHUNK)
    n_steps = tbl.shape[1]
    blk = lambda r: pl.BlockSpec((LRU_CHUNK, D_RNN), lambda s, t: (t[r, s], 0))
    wspec = pl.BlockSpec((2, RG_BLOCKS, RG_BW, RG_BW), lambda s, t: (0, 0, 0, 0))
    vec2 = pl.BlockSpec((2, D_RNN), lambda s, t: (0, 0))
    h0 = pl.BlockSpec((1, 1, D_RNN), lambda s, t: (t[2, s], 0, 0))
    grid_spec = pltpu.PrefetchScalarGridSpec(
        num_scalar_prefetch=1,
        grid=(n_steps,),
        in_specs=[blk(0), blk(1), wspec, vec2, wspec, vec2, vec2, h0, h0],
        out_specs=[blk(0), blk(1)],
        scratch_shapes=[pltpu.VMEM((1, D_RNN), F32), pltpu.VMEM((1, D_RNN), F32)],
    )
    return pl.pallas_call(
        _lru_kernel,
        grid_spec=grid_spec,
        out_shape=[jax.ShapeDtypeStruct((N_TOK, D_RNN), F32), jax.ShapeDtypeStruct((N_TOK, D_RNN), F32)],
        compiler_params=_cparams("arbitrary"),
        name="lru_scan",
    )(jnp.asarray(tbl), xc, xc, w_a_bf16, b_a, w_x_bf16, b_x, lam, h0f_all, h0b_all)


def _odd_out_kernel(x_ref, mod_ref, hf_ref, hb_ref, gate_ref, w_ref, o_ref):
    y = (hf_ref[...] + hb_ref[...]) * _gelu_tanh(gate_ref[...])
    o_ref[...] = x_ref[...] + mod_ref[0][2:3, :] * _dot(y.astype(BF16), w_ref[...])


def _odd_out(x, modt, hf, hb, gate, w_bf16):
    tile = pl.BlockSpec((TB, D_MODEL), lambda i: (i, 0))
    return pl.pallas_call(
        _odd_out_kernel,
        grid=(N_TILES,),
        in_specs=[tile, pl.BlockSpec((1, 6, D_MODEL), lambda i: (i, 0, 0)), tile, tile, tile,
                  pl.BlockSpec((D_RNN, D_MODEL), lambda i: (0, 0))],
        out_specs=tile,
        out_shape=jax.ShapeDtypeStruct((N_TOK, D_MODEL), F32),
        compiler_params=_cparams("parallel"),
        name="odd_out",
    )(x, modt, hf, hb, gate, w_bf16)


def _order_key(x):
    b = lax.bitcast_convert_type(x, jnp.int32)
    return b ^ ((b >> 31) & 0x7FFFFFFF)


def _key_value(k):
    return lax.bitcast_convert_type(k ^ ((k >> 31) & 0x7FFFFFFF), F32)


def _precedes(ky, ry, kx, rx):
    return (ky + jnp.where(ry < rx, 1, 0)) > kx


def _merge_exchange(n):
    pairs = []
    t = max(1, (n - 1).bit_length())
    p = 1 << (t - 1)
    while p > 0:
        q, r, d = 1 << (t - 1), 0, p
        while d > 0:
            pairs.extend((i, i + d) for i in range(n - d) if (i & p) == r)
            d, q, r = q - p, q >> 1, p
        p >>= 1
    return pairs


def _top_rows(values, count, emit):
    n = len(values)
    sub = lax.broadcasted_iota(jnp.int32, (SUBLANES, TB), 0)
    keys = [_order_key(v) for v in values]
    rows = [sub + SUBLANES * i for i in range(n)]
    for i, j in _merge_exchange(n):
        swap = _precedes(keys[j], rows[j], keys[i], rows[i])
        keys[i], keys[j] = jnp.where(swap, keys[j], keys[i]), jnp.where(swap, keys[i], keys[j])
        rows[i], rows[j] = jnp.where(swap, rows[j], rows[i]), jnp.where(swap, rows[i], rows[j])
    lowest = jnp.full((SUBLANES, TB), jnp.iinfo(jnp.int32).min + 1, jnp.int32)
    for step in range(count):
        k = keys[0]
        for shift in (4, 2, 1):
            k = jnp.maximum(k, pltpu.roll(k, shift, 0))
        r = jnp.where(keys[0] == k, rows[0], jnp.iinfo(jnp.int32).max)
        for shift in (4, 2, 1):
            r = jnp.minimum(r, pltpu.roll(r, shift, 0))
        emit(step, _key_value(k[0:1, :]), r[0:1, :])
        depth = min(n - 1, count - 1 - step)
        column = rows[0] == r
        for d in range(depth):
            keys[d] = jnp.where(column, keys[d + 1], keys[d])
            rows[d] = jnp.where(column, rows[d + 1], rows[d])
        if depth == n - 1:
            keys[n - 1] = jnp.where(column, lowest, keys[n - 1])


def _router_kernel(x_ref, mod_ref, g_ref, wq_ref, keys_ref, h_ref, a_ref, b_ref, gate_ref,
                   q_s, v1_s, i1_s, v2_s, i2_s, sc_s, a_s, b_s):
    h = _rms_mod(x_ref[...], g_ref[...], mod_ref[0], 3, 4).astype(BF16)
    h_ref[...] = h
    q_s[...] = _dot(h, wq_ref[...])
    lax.fori_loop(0, PEER_HEADS, functools.partial(
        _route_head, keys_ref, a_ref, b_ref, gate_ref, q_s, v1_s, i1_s, v2_s, i2_s, sc_s, a_s, b_s), 0)


def _route_head(keys_ref, a_ref, b_ref, gate_ref, q_s, v1_s, i1_s, v2_s, i2_s, sc_s, a_s, b_s, hd, carry):
    for half, (v_s, i_s) in enumerate(((v1_s, i1_s), (v2_s, i2_s))):
        col = pl.multiple_of(hd * PEER_DK + half * (PEER_DK // 2), PEER_DK // 2)
        qh = q_s[:, pl.ds(col, PEER_DK // 2)].astype(BF16)
        st = _dot_nt(keys_ref[half, hd], qh)

        def emit(step, value, row, v_s=v_s, i_s=i_s):
            v_s[step:step + 1, :] = value
            i_s[step:step + 1, :] = row

        _top_rows([st[SUBLANES * i:SUBLANES * (i + 1), :] for i in range(PEER_NK // SUBLANES)], PEER_TOPK, emit)

    v1 = v1_s[...]
    v2 = v2_s[...]
    sub8 = lax.broadcasted_iota(jnp.int32, (8, TB), 0)
    pieces = [v1[0:1, :] + v2[0:8, :], v1[0:1, :] + v2[8:16, :]]
    for p in range(1, 8):
        pieces.append(jnp.where(sub8 < PEER_TOPK // (p + 1), v1[p:p + 1, :] + v2[0:8, :], -jnp.inf))
    pieces.append(v1[8:16, :] + v2[0:1, :])
    sub = lax.broadcasted_iota(jnp.int32, (PEER_TOPK, TB), 0)
    i1 = i1_s[...]
    i2 = i2_s[...]

    def emit2(step, value, pos):
        p = jnp.where(pos < 72, jnp.maximum((pos - 8) >> 3, 0), pos - 64)
        r = jnp.where(pos < 16, pos, jnp.where(pos < 72, pos & 7, 0))
        sc_s[step:step + 1, :] = value
        a_s[step:step + 1, :] = jnp.where(sub == p, i1, 0).sum(axis=0, keepdims=True)
        b_s[step:step + 1, :] = jnp.where(sub == r, i2, 0).sum(axis=0, keepdims=True)

    _top_rows(pieces, PEER_TOPK, emit2)
    sc = sc_s[...]
    e = jnp.exp(sc - sc[0:1, :])
    out_rows = pl.ds(pl.multiple_of(hd * PEER_TOPK, PEER_TOPK), PEER_TOPK)
    gate_ref[0, out_rows, :] = e / e.sum(axis=0, keepdims=True)
    a_ref[0, out_rows, :] = a_s[...].astype(F32)
    b_ref[0, out_rows, :] = b_s[...].astype(F32)
    return carry


def _router(x, modt, g, wq_bf16, keys_bf16, layer):
    kt = pl.BlockSpec((1, PEER_HEADS * PEER_TOPK, TB), lambda i: (i, 0, 0))
    kshape = jax.ShapeDtypeStruct((N_TILES, PEER_HEADS * PEER_TOPK, TB), F32)
    f32s = lambda: pltpu.VMEM((PEER_TOPK, TB), F32)
    i32s = lambda: pltpu.VMEM((PEER_TOPK, TB), jnp.int32)
    return pl.pallas_call(
        _router_kernel,
        grid=(N_TILES,),
        in_specs=[
            pl.BlockSpec((TB, D_MODEL), lambda i: (i, 0)),
            pl.BlockSpec((1, 6, D_MODEL), lambda i: (i, 0, 0)),
            pl.BlockSpec((1, D_MODEL), lambda i: (0, 0)),
            pl.BlockSpec((None, D_MODEL, PEER_HEADS * PEER_DK), lambda i: (layer, 0, 0)),
            pl.BlockSpec((None, 2, PEER_HEADS, PEER_NK, PEER_DK // 2), lambda i: (layer, 0, 0, 0, 0)),
        ],
        out_specs=[pl.BlockSpec((TB, D_MODEL), lambda i: (i, 0)), kt, kt, kt],
        out_shape=[jax.ShapeDtypeStruct((N_TOK, D_MODEL), BF16), kshape, kshape, kshape],
        scratch_shapes=[pltpu.VMEM((TB, PEER_HEADS * PEER_DK), F32),
                        f32s(), i32s(), f32s(), i32s(), f32s(), i32s(), i32s()],
        compiler_params=_cparams("parallel"),
        name="peer_router",
    )(x, modt, g.reshape(1, D_MODEL), wq_bf16, keys_bf16)


def _expert_kernel(h_ref, x_ref, mod_ref, at_ref, bt_ref, gt_ref, u_ref, v_ref, fg_ref, o_ref, *rest, final):
    if final:
        y_ref, g_s, acc_s, a_s, b_s, w_s, rt_s = rest
    else:
        g_s, acc_s, a_s, b_s, w_s, rt_s = rest
    grp = pl.program_id(1)

    @pl.when(grp == 0)
    def _():
        acc_s[...] = jnp.zeros_like(acc_s)
        for part in range(TE // TB):
            a_s[part * TB:(part + 1) * TB, :] = at_ref[part].T
            b_s[part * TB:(part + 1) * TB, :] = bt_ref[part].T
            w_s[part * TB:(part + 1) * TB, :] = gt_ref[part].T
        sub = lax.broadcasted_iota(jnp.int32, (PEER_NK, PEER_NK), 0).astype(F32).astype(BF16)
        one = jnp.ones((PEER_NK, PEER_NK), BF16)
        zero = jnp.zeros((PEER_NK, PEER_NK), BF16)

        def gate_grid(t, slot):
            a_row = a_s[pl.ds(t, 1), :].astype(BF16)
            b_row = b_s[pl.ds(t, 1), :].astype(BF16)
            w_row = jnp.broadcast_to(w_s[pl.ds(t, 1), :].astype(BF16), (PEER_NK, PEER_NK))
            m_t = jnp.where(a_row == sub, one, zero)
            r_t = jnp.where(b_row == sub, w_row, zero)
            rt_s[slot] = r_t.T
            g = _dot(m_t, rt_s[slot]).astype(BF16).astype(F32)
            return lax.bitcast_convert_type(g, jnp.uint32)

        def body(t, carry):
            slot = t & (G_UNROLL - 1)
            word = (gate_grid(t, slot) >> 16) | (gate_grid(t + TE // 2, slot + G_UNROLL) & jnp.uint32(0xFFFF0000))
            g_s[pl.ds(t, PEER_NK, stride=G_PITCH), :] = word
            return carry

        lax.fori_loop(0, TE // 2, body, 0, unroll=G_UNROLL)

    h = h_ref[...]
    acc = acc_s[...]
    for sb in range(PEER_GROUP // EXPERT_SUB):
        e0 = sb * EXPERT_SUB
        hmat = _dot_nt(h, u_ref[e0:e0 + EXPERT_SUB, :])
        parts = []
        for j in range(EXPERT_SUB // PEER_NK):
            chunk = grp * CHUNKS_PER_GROUP + sb * (EXPERT_SUB // PEER_NK) + j
            word = g_s[pl.ds(pl.multiple_of(chunk * G_PITCH, 8), TE // 2), :]
            gates = jnp.concatenate(
                [lax.bitcast_convert_type(word << 16, F32),
                 lax.bitcast_convert_type(word & jnp.uint32(0xFFFF0000), F32)], axis=0)
            parts.append((_gelu_tanh(hmat[:, j * PEER_NK:(j + 1) * PEER_NK]) * gates).astype(BF16))
        acc = acc + _dot(jnp.concatenate(parts, axis=1), v_ref[e0:e0 + EXPERT_SUB, :])
    acc_s[...] = acc

    @pl.when(grp == PEER_NGROUPS - 1)
    def _():
        xo = x_ref[...] + mod_ref[0][5:6, :] * acc_s[...]
        if final:
            y = xo * lax.rsqrt(jnp.mean(xo * xo, axis=-1, keepdims=True) + EPS) * fg_ref[...]
            tile = pl.program_id(0)

            @pl.when(tile < N_PROMPT // TE)
            def _():
                o_ref[...] = y

            @pl.when(tile >= N_PROMPT // TE)
            def _():
                y_ref[...] = y
        else:
            o_ref[...] = xo


def _experts(h_bf16, x, modt_e, a_t, b_t, g_t, u_bf16, v_bf16, layer, final_g, final):
    tile = pl.BlockSpec((TE, D_MODEL), lambda i, g: (i, 0))
    kt = pl.BlockSpec((TE // TB, PEER_HEADS * PEER_TOPK, TB), lambda i, g: (i, 0, 0))
    wblk = pl.BlockSpec((None, PEER_GROUP, D_MODEL), lambda i, g: (layer, g, 0))
    xs = jax.ShapeDtypeStruct((N_TOK, D_MODEL), F32)
    n_prompt = N_PROMPT // TE
    sel = lambda: pltpu.VMEM((TE, PEER_HEADS * PEER_TOPK), F32)
    return pl.pallas_call(
        functools.partial(_expert_kernel, final=final),
        grid=(N_TOK // TE, PEER_NGROUPS),
        in_specs=[tile, tile, pl.BlockSpec((1, 6, D_MODEL), lambda i, g: (i, 0, 0)), kt, kt, kt, wblk, wblk,
                  pl.BlockSpec((1, D_MODEL), lambda i, g: (0, 0))],
        out_specs=[pl.BlockSpec((TE, D_MODEL), lambda i, g: (jnp.minimum(i, n_prompt - 1), 0)),
                   pl.BlockSpec((TE, D_MODEL), lambda i, g: (jnp.maximum(i - n_prompt, 0), 0))] if final else tile,
        out_shape=[jax.ShapeDtypeStruct((N_PROMPT, D_MODEL), F32),
                   jax.ShapeDtypeStruct((N_SAMPLE, D_MODEL), F32)] if final else xs,
        scratch_shapes=[pltpu.VMEM((PEER_NK * G_PITCH, PEER_NK), jnp.uint32), pltpu.VMEM((TE, D_MODEL), F32),
                        sel(), sel(), sel(), pltpu.VMEM((2 * G_UNROLL, PEER_NK, PEER_NK), BF16)],
        compiler_params=_cparams("arbitrary" if final else "parallel", "arbitrary"),
        name="peer_experts_final" if final else "peer_experts",
    )(h_bf16, x, modt_e, a_t, b_t, g_t, u_bf16, v_bf16, final_g.reshape(1, D_MODEL))


def _peer(x, modt, modt_e, norm_g, peer_bf16, layer, final_g, final):
    w_q, keys, u, v = peer_bf16
    h, a_t, b_t, g_t = _router(x, modt, norm_g, w_q, keys, layer)
    return _experts(h, x, modt_e, a_t, b_t, g_t, u, v, layer, final_g, final)


def kernel(x_prompt, x_sample, cache_k, cache_v, state_hgrn_fwd, state_hgrn_bwd, state_lru_fwd, state_lru_bwd, c, c_ctx, norm1_g, norm2_g, w_mod, b_mod, w_in_even, w_out_even, hgrn_lb_fwd, hgrn_lb_bwd, hgrn_gnorm_g, attn_sink, w_in_odd, w_out_odd, conv_w, conv_b, rg_w_a, rg_b_a, rg_w_x, rg_b_x, rg_lambda, peer_w_q, peer_keys, peer_u, peer_v, final_g):
    x = (x_prompt.reshape(N_PROMPT, D_MODEL), x_sample.reshape(N_SAMPLE, D_MODEL))
    cond = jnp.concatenate([c_ctx[None, :], c, jnp.zeros((COND_PAD - N_COND, D_MODEL), F32)], axis=0)
    mod = _modulation(cond, w_mod, b_mod)
    tile_row = np.asarray([0] * PROMPT_TILES + [1 + i // TILES_PER_SAMPLE for i in range(N_TILES - PROMPT_TILES)])
    modt = [mod[l][tile_row].reshape(N_TILES, 6, D_MODEL) for l in range(DEPTH)]
    modt_e = [m[::TE // TB] for m in modt]

    widths = (HG_W, HG_W, HG_W, HG_W, HG_W, ATT_W, KV_W, KV_W)
    zq, zf, zb, zi, zg, aq, ak, av = _in_proj(x, modt[0], norm1_g[0], w_in_even[0].astype(BF16), widths, "even_in")
    zero_state = jnp.zeros((1, HG_HEADS, HG_DV, HG_DK), F32)
    s0f = jnp.concatenate([zero_state, jnp.swapaxes(state_hgrn_fwd[:, 0], -1, -2)], axis=0)
    s0b = jnp.concatenate([zero_state, jnp.swapaxes(state_hgrn_bwd[:, 0], -1, -2)], axis=0)
    o_f, o_b, sf_t, sb_t = _hgrn_scan(zq, zf, zb, zi, hgrn_lb_fwd, hgrn_lb_bwd, s0f, s0b)
    oatt = _attention(aq, ak, av, cache_k[:, 0].reshape(DEC_BATCH, PAST_LEN, KV_W),
                      cache_v[:, 0].reshape(DEC_BATCH, PAST_LEN, KV_W), attn_sink[0])
    x = _even_out(x, modt[0], o_f, o_b, zg, oatt, hgrn_gnorm_g[0], w_out_even[0].astype(BF16))
    peer_bf16 = (peer_w_q.astype(BF16), peer_keys.astype(BF16), peer_u.astype(BF16), peer_v.astype(BF16))
    x = _peer(x, modt[0], modt_e[0], norm2_g[0], peer_bf16, 0, final_g, False)

    gate, xr = _in_proj(x, modt[1], norm1_g[1], w_in_odd[0].astype(BF16), (D_RNN, D_RNN), "odd_in")
    xc = _dwconv(xr, conv_w[0], conv_b[0])
    zero_h = jnp.zeros((1, 1, D_RNN), F32)
    h0f = jnp.concatenate([zero_h, state_lru_fwd[:, 0][:, None, :]], axis=0)
    h0b = jnp.concatenate([zero_h, state_lru_bwd[:, 0][:, None, :]], axis=0)
    hf, hb = _lru_scan(xc, rg_w_a[0].astype(BF16), rg_b_a[0], rg_w_x[0].astype(BF16), rg_b_x[0],
                       rg_lambda[0], h0f, h0b)
    x = _odd_out(x, modt[1], hf, hb, gate, w_out_odd[0].astype(BF16))
    y_p, y_s = _peer(x, modt[1], modt_e[1], norm2_g[1], peer_bf16, 1, final_g, True)

    y_prompt = y_p.reshape(BATCH, SEQ, D_MODEL)
    y_sample = y_s.reshape(DEC_BATCH, DEC_SEQ, D_MODEL)
    new_k = ak[:N_PROMPT].reshape(BATCH, 1, SEQ, KV_HEADS, HEAD_DIM)
    new_v = av[:N_PROMPT].reshape(BATCH, 1, SEQ, KV_HEADS, HEAD_DIM)
    new_hf = jnp.swapaxes(sf_t[:BATCH], -1, -2)[:, None]
    new_hb = jnp.swapaxes(sb_t[:BATCH], -1, -2)[:, None]
    new_lf = hf[:N_PROMPT].reshape(BATCH, SEQ, D_RNN)[:, -1][:, None, :]
    new_lb = hb[:N_PROMPT].reshape(BATCH, SEQ, D_RNN)[:, 0][:, None, :]
    return (y_prompt, y_sample, new_k, new_v, new_hf, new_hb, new_lf, new_lb)
```

```python
import functools

import numpy as np
import jax
import jax.numpy as jnp
from jax import lax
from jax.experimental import pallas as pl
from jax.experimental.pallas import tpu as pltpu

F32 = jnp.float32
BF16 = jnp.bfloat16

D_MODEL = 1024
BATCH = 16
SEQ = 256
DEPTH = 2
DEC_BATCH = 8
DEC_SEQ = 2048
PAST_LEN = 512
GRID_W = 64
EPS = 1e-6
HG_DK = 128
HG_DV = 128
HG_HEADS = 4
HG_CHUNK = 64
HG_BLOCK = 256
HG_W = HG_HEADS * HG_DK
HEAD_DIM = 64
N_HEADS = 8
KV_HEADS = 2
Q_PER_KV = 4
WINDOW = 128
ATT_BLOCK = 128
ROPE_BASE = 10000.0
ATT_W = N_HEADS * HEAD_DIM
KV_W = KV_HEADS * HEAD_DIM
D_RNN = 1024
RG_BLOCKS = 4
RG_BW = 256
RG_C = 8.0
PEER_HEADS = 8
PEER_NK = 128
PEER_DK = 256
PEER_TOPK = 16

N_PROMPT = BATCH * SEQ
N_SAMPLE = DEC_BATCH * DEC_SEQ
N_TOK = N_PROMPT + N_SAMPLE
TB = 256
N_TILES = N_TOK // TB
PROMPT_TILES = N_PROMPT // TB
TILES_PER_SAMPLE = DEC_SEQ // TB
N_COND = 1 + DEC_BATCH
COND_PAD = 16

TE = 512
PEER_GROUP = 2048
PEER_NGROUPS = PEER_NK * PEER_NK // PEER_GROUP
CHUNKS_PER_GROUP = PEER_GROUP // PEER_NK
EXPERT_SUB = 1024
G_PITCH = TE // 2 + 8
G_UNROLL = 16

LRU_CHUNK = 256
SUBLANES = 8

VMEM_LIMIT = 58 * 1024 * 1024


def _cparams(*sem):
    return pltpu.CompilerParams(dimension_semantics=sem, vmem_limit_bytes=VMEM_LIMIT)


def _dot(a, b):
    return jnp.dot(a, b, preferred_element_type=F32)


def _dot_nt(a, b):
    return lax.dot_general(a, b, (((1,), (1,)), ((), ())), preferred_element_type=F32)


def _dot_tn(a, b):
    return lax.dot_general(a, b, (((0,), (0,)), ((), ())), preferred_element_type=F32)


def _sigmoid(x):
    return 1.0 / (1.0 + jnp.exp(-x))


def _sigmoid_tanh(x):
    return 0.5 + 0.5 * jnp.tanh(0.5 * x)


def _silu(x):
    return x * _sigmoid(x)


def _gelu_tanh(x):
    c = np.float32(np.sqrt(2.0 / np.pi))
    return x * (0.5 * (1.0 + jnp.tanh(c * (x + 0.044715 * (x * x * x)))))


def _rms_mod(x, g, mod, sh, sc):
    y = x * lax.rsqrt(jnp.mean(x * x, axis=-1, keepdims=True) + EPS) * g
    return y * (1.0 + mod[sc:sc + 1, :]) + mod[sh:sh + 1, :]


def _mod_kernel(c_ref, w_ref, b_ref, o_ref):
    c = c_ref[...]
    s = _silu(c).astype(BF16)
    o_ref[0] = _dot(s, w_ref[0].astype(BF16)) + b_ref[0]


def _modulation(cond, w_mod, b_mod):
    nb = 6 * D_MODEL // 1024
    return pl.pallas_call(
        _mod_kernel,
        grid=(DEPTH, nb),
        in_specs=[
            pl.BlockSpec((COND_PAD, D_MODEL), lambda l, n: (0, 0)),
            pl.BlockSpec((1, D_MODEL, 1024), lambda l, n: (l, 0, n)),
            pl.BlockSpec((1, 1, 1024), lambda l, n: (l, 0, n)),
        ],
        out_specs=pl.BlockSpec((1, COND_PAD, 1024), lambda l, n: (l, 0, n)),
        out_shape=jax.ShapeDtypeStruct((DEPTH, COND_PAD, 6 * D_MODEL), F32),
        compiler_params=_cparams("parallel", "parallel"),
        name="modulation",
    )(cond, w_mod, b_mod.reshape(DEPTH, 1, 6 * D_MODEL))


def _token_specs(arrays, width):
    if not isinstance(arrays, tuple):
        return [pl.BlockSpec((TB, width), lambda i: (i, 0))]
    return [pl.BlockSpec((TB, width), lambda i: (jnp.minimum(i, PROMPT_TILES - 1), 0)),
            pl.BlockSpec((TB, width), lambda i: (jnp.maximum(i - PROMPT_TILES, 0), 0))]


def _token_tile(refs):
    if len(refs) == 1:
        return refs[0][...]
    return jnp.where(pl.program_id(0) < PROMPT_TILES, refs[0][...], refs[1][...])


def _in_proj_kernel(*refs, widths, n_x):
    x_refs, (mod_ref, g_ref, w_ref), o_refs = refs[:n_x], refs[n_x:n_x + 3], refs[n_x + 3:]
    h = _rms_mod(_token_tile(x_refs), g_ref[...], mod_ref[0], 0, 1)
    z = _dot(h.astype(BF16), w_ref[...])
    off = 0
    for o_ref, wd in zip(o_refs, widths):
        o_ref[...] = z[:, off:off + wd]
        off += wd


def _in_proj(x, modt, g, w_bf16, widths, name):
    n_out = sum(widths)
    xs = x if isinstance(x, tuple) else (x,)
    return pl.pallas_call(
        functools.partial(_in_proj_kernel, widths=widths, n_x=len(xs)),
        grid=(N_TILES,),
        in_specs=_token_specs(x, D_MODEL) + [
            pl.BlockSpec((1, 6, D_MODEL), lambda i: (i, 0, 0)),
            pl.BlockSpec((1, D_MODEL), lambda i: (0, 0)),
            pl.BlockSpec((D_MODEL, n_out), lambda i: (0, 0)),
        ],
        out_specs=[pl.BlockSpec((TB, wd), lambda i: (i, 0)) for wd in widths],
        out_shape=[jax.ShapeDtypeStruct((N_TOK, wd), F32) for wd in widths],
        compiler_params=_cparams("parallel"),
        name=name,
    )(*xs, modt, g.reshape(1, D_MODEL), w_bf16)


def _hgrn_direction(zq, zgate, zi, lb, states, fwd):
    n = HG_CHUNK
    row = lax.broadcasted_iota(jnp.int32, (n, n), 0)
    col = lax.broadcasted_iota(jnp.int32, (n, n), 1)
    keep = (col <= row) if fwd else (col >= row)
    tri = jnp.where(keep, 1.0, 0.0).astype(BF16)
    q = zq * _sigmoid_tanh(zq)
    sg = _sigmoid_tanh(zgate)
    f = lb + (1.0 - lb) * sg
    g = jnp.log(f)
    k = (1.0 - lb) * (1.0 - sg)
    g_hi = g.astype(BF16)
    g_lo = (g - g_hi.astype(F32)).astype(BF16)
    b = _dot(tri, g_hi) + _dot(tri, g_lo)
    mid = n // 2 - 1 if fwd else n // 2
    last = n - 1 if fwd else 0
    bm = b[mid:mid + 1, :]
    bl = b[last:last + 1, :]
    qi32 = q * jnp.exp(b - bm)
    ki32 = k * jnp.exp(bm - b)
    qi = qi32.astype(BF16)
    ki = ki32.astype(BF16)
    qs = (qi32 * jnp.exp(bm)).astype(BF16)
    ks = (ki32 * jnp.exp(bl - bm)).astype(BF16)
    vb = zi.astype(BF16)
    decay = jnp.exp(bl)
    outs = []
    new_states = []
    for h in range(HG_HEADS):
        sl = slice(h * HG_DK, (h + 1) * HG_DK)
        a = jnp.where(keep, _dot_nt(qi[:, sl], ki[:, sl]), 0.0)
        st = states[h]
        outs.append(_dot(a.astype(BF16), vb[:, sl]) + _dot_nt(qs[:, sl], st.astype(BF16)))
        new_states.append(decay[:, sl] * st + _dot_tn(vb[:, sl], ks[:, sl]))
    return jnp.concatenate(outs, axis=1), new_states


def _lower_bound(p):
    e = jnp.exp(p - p.max(axis=0, keepdims=True))
    return e[0:1, :] / e.sum(axis=0, keepdims=True)


def _hgrn_block(zq_ref, zgate_ref, zi_ref, lb, st_ref, o_ref, fwd):
    states = [st_ref[h] for h in range(HG_HEADS)]
    n_sub = HG_BLOCK // HG_CHUNK
    for c in (range(n_sub) if fwd else reversed(range(n_sub))):
        rows = slice(c * HG_CHUNK, (c + 1) * HG_CHUNK)
        o_ref[rows, :], states = _hgrn_direction(zq_ref[rows, :], zgate_ref[rows, :], zi_ref[rows, :], lb, states, fwd)
    for h in range(HG_HEADS):
        st_ref[h] = states[h]


def _hgrn_kernel(tbl_ref, zqf_ref, zff_ref, zif_ref, zqb_ref, zbb_ref, zib_ref, lbf_ref, lbb_ref,
                 s0f_ref, s0b_ref, of_ref, ob_ref, sf_ref, sb_ref, stf, stb):
    s = pl.program_id(0)

    @pl.when(tbl_ref[4, s] == 1)
    def _():
        stf[...] = s0f_ref[0]
        stb[...] = s0b_ref[0]

    _hgrn_block(zqf_ref, zff_ref, zif_ref, _lower_bound(lbf_ref[...]), stf, of_ref, True)
    _hgrn_block(zqb_ref, zbb_ref, zib_ref, _lower_bound(lbb_ref[...]), stb, ob_ref, False)

    @pl.when(tbl_ref[5, s] == 1)
    def _():
        sf_ref[0] = stf[...]
        sb_ref[0] = stb[...]


def _scan_table(chunk):
    rows = []
    blk = 0
    for (nseq, length, has_init) in ((BATCH, SEQ, False), (DEC_BATCH, DEC_SEQ, True)):
        nc = length // chunk
        for b in range(nseq):
            sid = b if not has_init else BATCH + b
            for i in range(nc):
                rows.append((blk + i, blk + nc - 1 - i, (1 + b) if has_init else 0, sid,
                             int(i == 0), int(i == nc - 1)))
            blk += nc
    return np.asarray(rows, np.int32).T.copy()


def _hgrn_scan(zq, zf, zb, zi, lbf, lbb, s0f_t, s0b_t):
    tbl = _scan_table(HG_BLOCK)
    n_steps = tbl.shape[1]
    n_seq = BATCH + DEC_BATCH
    blk = lambda r: pl.BlockSpec((HG_BLOCK, HG_W), lambda s, t: (t[r, s], 0))
    vec = pl.BlockSpec(lbf.shape, lambda s, t: (0, 0))
    st_in = pl.BlockSpec((1, HG_HEADS, HG_DV, HG_DK), lambda s, t: (t[2, s], 0, 0, 0))
    st_out = pl.BlockSpec((1, HG_HEADS, HG_DV, HG_DK), lambda s, t: (t[3, s], 0, 0, 0))
    grid_spec = pltpu.PrefetchScalarGridSpec(
        num_scalar_prefetch=1,
        grid=(n_steps,),
        in_specs=[blk(0), blk(0), blk(0), blk(1), blk(1), blk(1), vec, vec, st_in, st_in],
        out_specs=[blk(0), blk(1), st_out, st_out],
        scratch_shapes=[pltpu.VMEM((HG_HEADS, HG_DV, HG_DK), F32),
                        pltpu.VMEM((HG_HEADS, HG_DV, HG_DK), F32)],
    )
    st_shape = jax.ShapeDtypeStruct((n_seq, HG_HEADS, HG_DV, HG_DK), F32)
    return pl.pallas_call(
        _hgrn_kernel,
        grid_spec=grid_spec,
        out_shape=[jax.ShapeDtypeStruct((N_TOK, HG_W), F32), jax.ShapeDtypeStruct((N_TOK, HG_W), F32),
                   st_shape, st_shape],
        compiler_params=_cparams("arbitrary"),
        name="hgrn_scan",
    )(jnp.asarray(tbl), zq, zf, zi, zq, zb, zi, lbf, lbb, s0f_t, s0b_t)


def _rope(x, cos, sgn_sin):
    w = x.shape[1]
    lane = lax.broadcasted_iota(jnp.int32, x.shape, 1)
    partner = jnp.where((lane & 31) < 16, pltpu.roll(x, w - 16, 1), pltpu.roll(x, 16, 1))
    return x * cos + partner * sgn_sin


def _dup_kv_half(x, kv):
    lane = lax.broadcasted_iota(jnp.int32, x.shape, 1)
    own = (lane < HEAD_DIM) if kv == 0 else (lane >= HEAD_DIM)
    xs = jnp.where(own, x, 0.0)
    return xs + pltpu.roll(xs, HEAD_DIM, 1)


def _attend(q, key_parts, sink_ref, o_ref):
    lane = lax.broadcasted_iota(jnp.int32, (ATT_BLOCK, 128), 1)
    lo = lane < HEAD_DIM
    for kv in range(KV_HEADS):
        qs = []
        for gq in range(Q_PER_KV):
            hd = kv * Q_PER_KV + gq
            pair = q[:, (hd // 2) * 128:(hd // 2 + 1) * 128]
            qs.append(jnp.where(lo if hd % 2 == 0 else ~lo, pair, 0.0))
        qst = jnp.concatenate(qs, axis=0).astype(BF16)
        logits = []
        vals = []
        for (k, v, mask) in key_parts:
            kd = _dup_kv_half(k, kv).astype(BF16)
            vals.append(_dup_kv_half(v, kv).astype(BF16))
            s = _dot_nt(qst, kd)
            if mask is not None:
                s = jnp.where(mask, s, -1e30)
            logits.append(s)
        m = logits[0].max(axis=-1, keepdims=True)
        for s in logits[1:]:
            m = jnp.maximum(m, s.max(axis=-1, keepdims=True))
        sink_col = jnp.concatenate(
            [jnp.full((ATT_BLOCK, 1), sink_ref[kv * Q_PER_KV + gq], F32) for gq in range(Q_PER_KV)], axis=0)
        m = jnp.maximum(m, sink_col)
        den = jnp.exp(sink_col - m)
        acc = jnp.zeros((Q_PER_KV * ATT_BLOCK, 128), F32)
        for s, v in zip(logits, vals):
            p = jnp.exp(s - m)
            den = den + p.sum(axis=-1, keepdims=True)
            acc = acc + _dot(p.astype(BF16), v)
        out = acc / den
        for pr in range(Q_PER_KV // 2):
            r0 = out[(2 * pr) * ATT_BLOCK:(2 * pr + 1) * ATT_BLOCK]
            r1 = out[(2 * pr + 1) * ATT_BLOCK:(2 * pr + 2) * ATT_BLOCK]
            c0 = (kv * Q_PER_KV // 2 + pr) * 128
            o_ref[:, c0:c0 + 128] = jnp.where(lo, r0, r1)


def _attn_prompt_kernel(sink_ref, q_ref, k_ref, v_ref, o_ref):
    q = q_ref[...] * np.float32(HEAD_DIM ** -0.5)
    _attend(q, [(k_ref[...], v_ref[...], None)], sink_ref, o_ref)


def _attn_latent_kernel(sink_ref, q_ref, kp_ref, kc_ref, kn_ref, vp_ref, vc_ref, vn_ref,
                        ck_ref, cv_ref, cos_ref, sin_ref, o_ref):
    j = pl.program_id(1)
    nq = pl.num_programs(1)
    qstart = pl.multiple_of((j + 1) * ATT_BLOCK, ATT_BLOCK)
    kstart = pl.multiple_of(j * ATT_BLOCK, ATT_BLOCK)
    cq = cos_ref[pl.ds(qstart, ATT_BLOCK), :]
    sq = sin_ref[pl.ds(qstart, ATT_BLOCK), :]
    q = _rope(q_ref[...], jnp.concatenate([cq] * 4, axis=1), jnp.concatenate([sq] * 4, axis=1))
    q = q * np.float32(HEAD_DIM ** -0.5)
    k_loc = jnp.concatenate([kp_ref[...], kc_ref[...], kn_ref[...]], axis=0)
    v_loc = jnp.concatenate([vp_ref[...], vc_ref[...], vn_ref[...]], axis=0)
    k_loc = _rope(k_loc, cos_ref[pl.ds(kstart, 3 * ATT_BLOCK), :], sin_ref[pl.ds(kstart, 3 * ATT_BLOCK), :])
    r = lax.broadcasted_iota(jnp.int32, (Q_PER_KV * ATT_BLOCK, 3 * ATT_BLOCK), 0) & (ATT_BLOCK - 1)
    c = lax.broadcasted_iota(jnp.int32, (Q_PER_KV * ATT_BLOCK, 3 * ATT_BLOCK), 1)
    rel = c - ATT_BLOCK - r
    c_lo = jnp.where(j > 0, 0, ATT_BLOCK)
    c_hi = jnp.where(j < nq - 1, 3 * ATT_BLOCK, 2 * ATT_BLOCK)
    ok = (jnp.abs(rel) <= WINDOW) & (c >= c_lo) & (c < c_hi)
    _attend(q, [(ck_ref[0], cv_ref[0], None), (k_loc, v_loc, ok)], sink_ref, o_ref)


def _rope_tables():
    pos = np.arange(-ATT_BLOCK, DEC_SEQ + ATT_BLOCK)
    nf = HEAD_DIM // 4
    inv = (1.0 / (ROPE_BASE ** (np.arange(nf, dtype=np.float32) / nf))).astype(np.float32)
    rows = (pos // GRID_W).astype(np.float32)
    cols = (pos % GRID_W).astype(np.float32)
    ar = rows[:, None] * inv[None, :]
    ac = cols[:, None] * inv[None, :]
    cos64 = np.concatenate([np.cos(ar), np.cos(ar), np.cos(ac), np.cos(ac)], axis=1)
    sin64 = np.concatenate([-np.sin(ar), np.sin(ar), -np.sin(ac), np.sin(ac)], axis=1)
    cos = np.concatenate([cos64, cos64], axis=1).astype(np.float32)
    sin = np.concatenate([sin64, sin64], axis=1).astype(np.float32)
    return jnp.asarray(cos), jnp.asarray(sin)


def _attention(aq, ak, av, ck, cv, sink):
    smem = pl.BlockSpec(memory_space=pltpu.SMEM)
    nqp = SEQ // ATT_BLOCK
    o_prompt = pl.pallas_call(
        _attn_prompt_kernel,
        grid=(BATCH, nqp),
        in_specs=[
            smem,
            pl.BlockSpec((ATT_BLOCK, ATT_W), lambda b, j: (b * nqp + j, 0)),
            pl.BlockSpec((SEQ, KV_W), lambda b, j: (b, 0)),
            pl.BlockSpec((SEQ, KV_W), lambda b, j: (b, 0)),
        ],
        out_specs=pl.BlockSpec((ATT_BLOCK, ATT_W), lambda b, j: (b * nqp + j, 0)),
        out_shape=jax.ShapeDtypeStruct((N_PROMPT, ATT_W), F32),
        compiler_params=_cparams("parallel", "parallel"),
        name="attn_prompt",
    )(sink, aq, ak, av)

    nq = DEC_SEQ // ATT_BLOCK
    base = N_PROMPT // ATT_BLOCK
    cos, sin = _rope_tables()
    kv_blk = lambda d: pl.BlockSpec(
        (ATT_BLOCK, KV_W), lambda b, j: (base + b * nq + jnp.clip(j + d, 0, nq - 1), 0))
    tab = pl.BlockSpec((DEC_SEQ + 2 * ATT_BLOCK, 128), lambda b, j: (0, 0))
    o_latent = pl.pallas_call(
        _attn_latent_kernel,
        grid=(DEC_BATCH, nq),
        in_specs=[
            smem,
            pl.BlockSpec((ATT_BLOCK, ATT_W), lambda b, j: (base + b * nq + j, 0)),
            kv_blk(-1), kv_blk(0), kv_blk(1), kv_blk(-1), kv_blk(0), kv_blk(1),
            pl.BlockSpec((1, PAST_LEN, KV_W), lambda b, j: (b, 0, 0)),
            pl.BlockSpec((1, PAST_LEN, KV_W), lambda b, j: (b, 0, 0)),
            tab, tab,
        ],
        out_specs=pl.BlockSpec((ATT_BLOCK, ATT_W), lambda b, j: (b * nq + j, 0)),
        out_shape=jax.ShapeDtypeStruct((N_SAMPLE, ATT_W), F32),
        compiler_params=_cparams("parallel", "parallel"),
        name="attn_latent",
    )(sink, aq, ak, ak, ak, av, av, av, ck, cv, cos, sin)
    return o_prompt, o_latent


def _even_mix(refs, mod):
    xp_ref, xs_ref, of_ref, ob_ref, zg_ref, oattp_ref, oatts_ref, gn_ref, w_ref = refs
    x = _token_tile((xp_ref, xs_ref))
    oatt = _token_tile((oattp_ref, oatts_ref))
    o = of_ref[...] + ob_ref[...]
    parts = []
    for h in range(HG_HEADS):
        oh = o[:, h * HG_DV:(h + 1) * HG_DV]
        parts.append(oh * lax.rsqrt(jnp.mean(oh * oh, axis=-1, keepdims=True) + EPS))
    o_hg = jnp.concatenate(parts, axis=1) * gn_ref[...] * _silu(zg_ref[...])
    y = _dot(o_hg.astype(BF16), w_ref[0:HG_W, :]) + _dot(oatt.astype(BF16), w_ref[HG_W:, :])
    return x + mod[2:3, :] * y


def _even_mix_operands(x_pair, o_f, o_b, zg, oatt_pair, gn_g, w_bf16):
    tile = lambda w: pl.BlockSpec((TB, w), lambda i: (i, 0))
    specs = (_token_specs(x_pair, D_MODEL) + [tile(HG_W), tile(HG_W), tile(HG_W)] + _token_specs(oatt_pair, ATT_W)
             + [pl.BlockSpec((1, HG_W), lambda i: (0, 0)), pl.BlockSpec((HG_W + ATT_W, D_MODEL), lambda i: (0, 0))])
    return specs, (*x_pair, o_f, o_b, zg, *oatt_pair, gn_g.reshape(1, HG_W), w_bf16)


def _odd_in_kernel(x_ref, xp_ref, xn_ref, mod_ref, g_ref, w_ref, cw_ref, cb_ref, gate_ref, xc_ref):
    i = pl.program_id(0)
    in_sample = i >= PROMPT_TILES
    pos = (i - PROMPT_TILES) % TILES_PER_SAMPLE
    has_prev = in_sample & (pos != 0)
    has_next = in_sample & (pos != TILES_PER_SAMPLE - 1)
    mod = mod_ref[0]
    g = g_ref[...]
    z = _dot(_rms_mod(x_ref[...], g, mod, 0, 1).astype(BF16), w_ref[...])
    gate_ref[...] = z[:, :D_RNN]
    w_rec = w_ref[:, D_RNN:]
    prev = _dot(_rms_mod(xp_ref[...], g, mod, 0, 1).astype(BF16), w_rec)
    nxt = _dot(_rms_mod(xn_ref[...], g, mod, 0, 1).astype(BF16), w_rec)
    prev = jnp.where(has_prev, prev, 0.0)
    nxt = jnp.where(has_next, nxt, 0.0)
    ext = jnp.concatenate([prev, z[:, D_RNN:], nxt], axis=0)
    n = TB + 16
    w = cw_ref[...]
    acc = ext[8:8 + TB] * w[2:3, :] + cb_ref[...]
    acc = acc + pltpu.roll(ext, 2, 0)[8:8 + TB] * w[0:1, :]
    acc = acc + pltpu.roll(ext, 1, 0)[8:8 + TB] * w[1:2, :]
    acc = acc + pltpu.roll(ext, n - 1, 0)[8:8 + TB] * w[3:4, :]
    xc_ref[...] = acc


def _odd_in(x, modt, g, w_bf16, cw, cb):
    r8 = TB // 8
    last8 = N_TOK // 8 - 1
    tile = pl.BlockSpec((TB, D_MODEL), lambda i: (i, 0))
    return pl.pallas_call(
        _odd_in_kernel,
        grid=(N_TILES,),
        in_specs=[
            tile,
            pl.BlockSpec((8, D_MODEL), lambda i: (jnp.maximum(i * r8 - 1, 0), 0)),
            pl.BlockSpec((8, D_MODEL), lambda i: (jnp.minimum((i + 1) * r8, last8), 0)),
            pl.BlockSpec((1, 6, D_MODEL), lambda i: (i, 0, 0)),
            pl.BlockSpec((1, D_MODEL), lambda i: (0, 0)),
            pl.BlockSpec((D_MODEL, 2 * D_RNN), lambda i: (0, 0)),
            pl.BlockSpec((4, D_RNN), lambda i: (0, 0)),
            pl.BlockSpec((1, D_RNN), lambda i: (0, 0)),
        ],
        out_specs=[tile, tile],
        out_shape=[jax.ShapeDtypeStruct((N_TOK, D_RNN), F32), jax.ShapeDtypeStruct((N_TOK, D_RNN), F32)],
        compiler_params=_cparams("parallel"),
        name="odd_in",
    )(x, x, x, modt, g.reshape(1, D_MODEL), w_bf16, cw, cb.reshape(1, D_RNN))


def _log1p(y):
    w = 1.0 + y
    return jnp.where(w == 1.0, y, jnp.log(w) * (y / (w - 1.0)))


def _expm1_given(x, e):
    return jnp.where(e == 1.0, x, (e - 1.0) * (x / jnp.log(e)))


def _lru_direction(xc, wa_ref, ba, wx_ref, bx, lam, h_ref, o_ref, fwd):
    n = LRU_CHUNK
    xb = xc.astype(BF16)
    ra, ri = [], []
    for blk in range(RG_BLOCKS):
        sl = slice(blk * RG_BW, (blk + 1) * RG_BW)
        ra.append(_dot(xb[:, sl], wa_ref[blk]))
        ri.append(_dot(xb[:, sl], wx_ref[blk]))
    r = _sigmoid_tanh(jnp.concatenate(ra, axis=1) + ba)
    gi = _sigmoid_tanh(jnp.concatenate(ri, axis=1) + bx)
    neg = -lam
    softplus = jnp.maximum(neg, 0.0) + _log1p(jnp.exp(-jnp.abs(neg)))
    log_a = (-RG_C * softplus) * r
    a = jnp.exp(log_a)
    u = jnp.sqrt(-_expm1_given(2.0 * log_a, a * a)) * (gi * xc)
    sub = lax.broadcasted_iota(jnp.int32, (SUBLANES, D_RNN), 0)
    carry = h_ref[...]
    groups = range(n // SUBLANES)
    for g in (groups if fwd else reversed(groups)):
        rows = slice(g * SUBLANES, (g + 1) * SUBLANES)
        ag, ug = a[rows, :], u[rows, :]
        for d in (1, 2, 4):
            live = (sub >= d) if fwd else (sub < SUBLANES - d)
            shift = d if fwd else SUBLANES - d
            a_s = jnp.where(live, pltpu.roll(ag, shift, 0), 1.0)
            u_s = jnp.where(live, pltpu.roll(ug, shift, 0), 0.0)
            ug = ag * u_s + ug
            ag = ag * a_s
        h = ug + ag * carry
        o_ref[rows, :] = h
        carry = h[SUBLANES - 1:SUBLANES, :] if fwd else h[0:1, :]
    h_ref[...] = carry


def _lru_kernel(tbl_ref, xf_ref, xb_ref, wa_ref, ba_ref, wx_ref, bx_ref, lam_ref, h0f_ref, h0b_ref,
                hf_ref, hb_ref, hf_s, hb_s):
    s = pl.program_id(0)

    @pl.when(tbl_ref[4, s] == 1)
    def _():
        hf_s[...] = h0f_ref[0]
        hb_s[...] = h0b_ref[0]

    _lru_direction(xf_ref[...], wa_ref.at[0], ba_ref[0:1, :], wx_ref.at[0], bx_ref[0:1, :],
                   lam_ref[0:1, :], hf_s, hf_ref, True)
    _lru_direction(xb_ref[...], wa_ref.at[1], ba_ref[1:2, :], wx_ref.at[1], bx_ref[1:2, :],
                   lam_ref[1:2, :], hb_s, hb_ref, False)


def _lru_scan(xc, w_a_bf16, b_a, w_x_bf16, b_x, lam, h0f_all, h0b_all):
    tbl = _scan_table(LRU_CHUNK)
    n_steps = tbl.shape[1]
    blk = lambda r: pl.BlockSpec((LRU_CHUNK, D_RNN), lambda s, t: (t[r, s], 0))
    wspec = pl.BlockSpec((2, RG_BLOCKS, RG_BW, RG_BW), lambda s, t: (0, 0, 0, 0))
    vec2 = pl.BlockSpec((2, D_RNN), lambda s, t: (0, 0))
    h0 = pl.BlockSpec((1, 1, D_RNN), lambda s, t: (t[2, s], 0, 0))
    grid_spec = pltpu.PrefetchScalarGridSpec(
        num_scalar_prefetch=1,
        grid=(n_steps,),
        in_specs=[blk(0), blk(1), wspec, vec2, wspec, vec2, vec2, h0, h0],
        out_specs=[blk(0), blk(1)],
        scratch_shapes=[pltpu.VMEM((1, D_RNN), F32), pltpu.VMEM((1, D_RNN), F32)],
    )
    return pl.pallas_call(
        _lru_kernel,
        grid_spec=grid_spec,
        out_shape=[jax.ShapeDtypeStruct((N_TOK, D_RNN), F32), jax.ShapeDtypeStruct((N_TOK, D_RNN), F32)],
        compiler_params=_cparams("arbitrary"),
        name="lru_scan",
    )(jnp.asarray(tbl), xc, xc, w_a_bf16, b_a, w_x_bf16, b_x, lam, h0f_all, h0b_all)


def _odd_mix(refs, mod):
    x_ref, hf_ref, hb_ref, gate_ref, w_ref = refs
    y = (hf_ref[...] + hb_ref[...]) * _gelu_tanh(gate_ref[...])
    return x_ref[...] + mod[2:3, :] * _dot(y.astype(BF16), w_ref[...])


def _odd_mix_operands(x, hf, hb, gate, w_bf16):
    tile = pl.BlockSpec((TB, D_MODEL), lambda i: (i, 0))
    return [tile, tile, tile, tile, pl.BlockSpec((D_RNN, D_MODEL), lambda i: (0, 0))], (x, hf, hb, gate, w_bf16)


def _order_key(x):
    b = lax.bitcast_convert_type(x, jnp.int32)
    return b ^ ((b >> 31) & 0x7FFFFFFF)


def _key_value(k):
    return lax.bitcast_convert_type(k ^ ((k >> 31) & 0x7FFFFFFF), F32)


def _precedes(ky, ry, kx, rx):
    return (ky + jnp.where(ry < rx, 1, 0)) > kx


def _merge_exchange(n):
    pairs = []
    t = max(1, (n - 1).bit_length())
    p = 1 << (t - 1)
    while p > 0:
        q, r, d = 1 << (t - 1), 0, p
        while d > 0:
            pairs.extend((i, i + d) for i in range(n - d) if (i & p) == r)
            d, q, r = q - p, q >> 1, p
        p >>= 1
    return pairs


def _top_rows(values, count, emit):
    n = len(values)
    sub = lax.broadcasted_iota(jnp.int32, (SUBLANES, TB), 0)
    keys = [_order_key(v) for v in values]
    rows = [sub + SUBLANES * i for i in range(n)]
    for i, j in _merge_exchange(n):
        swap = _precedes(keys[j], rows[j], keys[i], rows[i])
        keys[i], keys[j] = jnp.where(swap, keys[j], keys[i]), jnp.where(swap, keys[i], keys[j])
        rows[i], rows[j] = jnp.where(swap, rows[j], rows[i]), jnp.where(swap, rows[i], rows[j])
    lowest = jnp.full((SUBLANES, TB), jnp.iinfo(jnp.int32).min + 1, jnp.int32)
    for step in range(count):
        k = keys[0]
        for shift in (4, 2, 1):
            k = jnp.maximum(k, pltpu.roll(k, shift, 0))
        r = jnp.where(keys[0] == k, rows[0], jnp.iinfo(jnp.int32).max)
        for shift in (4, 2, 1):
            r = jnp.minimum(r, pltpu.roll(r, shift, 0))
        emit(step, _key_value(k[0:1, :]), r[0:1, :])
        depth = min(n - 1, count - 1 - step)
        column = rows[0] == r
        for d in range(depth):
            keys[d] = jnp.where(column, keys[d + 1], keys[d])
            rows[d] = jnp.where(column, rows[d + 1], rows[d])
        if depth == n - 1:
            keys[n - 1] = jnp.where(column, lowest, keys[n - 1])


def _router_kernel(*refs, n_mix, mix_fn):
    mix_refs = refs[:n_mix]
    (mod_ref, g_ref, wq_ref, keys_ref, x_ref, h_ref, a_ref, b_ref, gate_ref,
     q_s, v1_s, i1_s, v2_s, i2_s, sc_s, a_s, b_s) = refs[n_mix:]
    x = mix_fn(mix_refs, mod_ref[0])
    x_ref[...] = x
    h = _rms_mod(x, g_ref[...], mod_ref[0], 3, 4).astype(BF16)
    h_ref[...] = h
    q_s[...] = _dot(h, wq_ref[...])
    lax.fori_loop(0, PEER_HEADS, functools.partial(
        _route_head, keys_ref, a_ref, b_ref, gate_ref, q_s, v1_s, i1_s, v2_s, i2_s, sc_s, a_s, b_s), 0)


def _route_head(keys_ref, a_ref, b_ref, gate_ref, q_s, v1_s, i1_s, v2_s, i2_s, sc_s, a_s, b_s, hd, carry):
    for half, (v_s, i_s) in enumerate(((v1_s, i1_s), (v2_s, i2_s))):
        col = pl.multiple_of(hd * PEER_DK + half * (PEER_DK // 2), PEER_DK // 2)
        qh = q_s[:, pl.ds(col, PEER_DK // 2)].astype(BF16)
        st = _dot_nt(keys_ref[half, hd], qh)

        def emit(step, value, row, v_s=v_s, i_s=i_s):
            v_s[step:step + 1, :] = value
            i_s[step:step + 1, :] = row

        _top_rows([st[SUBLANES * i:SUBLANES * (i + 1), :] for i in range(PEER_NK // SUBLANES)], PEER_TOPK, emit)

    v1 = v1_s[...]
    v2 = v2_s[...]
    sub8 = lax.broadcasted_iota(jnp.int32, (8, TB), 0)
    pieces = [v1[0:1, :] + v2[0:8, :], v1[0:1, :] + v2[8:16, :]]
    for p in range(1, 8):
        pieces.append(jnp.where(sub8 < PEER_TOPK // (p + 1), v1[p:p + 1, :] + v2[0:8, :], -jnp.inf))
    pieces.append(v1[8:16, :] + v2[0:1, :])
    sub = lax.broadcasted_iota(jnp.int32, (PEER_TOPK, TB), 0)
    i1 = i1_s[...]
    i2 = i2_s[...]

    def emit2(step, value, pos):
        p = jnp.where(pos < 72, jnp.maximum((pos - 8) >> 3, 0), pos - 64)
        r = jnp.where(pos < 16, pos, jnp.where(pos < 72, pos & 7, 0))
        sc_s[step:step + 1, :] = value
        a_s[step:step + 1, :] = jnp.where(sub == p, i1, 0).sum(axis=0, keepdims=True)
        b_s[step:step + 1, :] = jnp.where(sub == r, i2, 0).sum(axis=0, keepdims=True)

    _top_rows(pieces, PEER_TOPK, emit2)
    sc = sc_s[...]
    e = jnp.exp(sc - sc[0:1, :])
    out_rows = pl.ds(pl.multiple_of(hd * PEER_TOPK, PEER_TOPK), PEER_TOPK)
    gate_ref[0, out_rows, :] = e / e.sum(axis=0, keepdims=True)
    a_ref[0, out_rows, :] = a_s[...].astype(F32)
    b_ref[0, out_rows, :] = b_s[...].astype(F32)
    return carry


def _router(mix, modt, g, wq_bf16, keys_bf16, layer):
    mix_fn, (mix_specs, mix_args) = mix
    kt = pl.BlockSpec((1, PEER_HEADS * PEER_TOPK, TB), lambda i: (i, 0, 0))
    kshape = jax.ShapeDtypeStruct((N_TILES, PEER_HEADS * PEER_TOPK, TB), F32)
    f32s = lambda: pltpu.VMEM((PEER_TOPK, TB), F32)
    i32s = lambda: pltpu.VMEM((PEER_TOPK, TB), jnp.int32)
    tile = pl.BlockSpec((TB, D_MODEL), lambda i: (i, 0))
    return pl.pallas_call(
        functools.partial(_router_kernel, n_mix=len(mix_args), mix_fn=mix_fn),
        grid=(N_TILES,),
        in_specs=list(mix_specs) + [
            pl.BlockSpec((1, 6, D_MODEL), lambda i: (i, 0, 0)),
            pl.BlockSpec((1, D_MODEL), lambda i: (0, 0)),
            pl.BlockSpec((None, D_MODEL, PEER_HEADS * PEER_DK), lambda i: (layer, 0, 0)),
            pl.BlockSpec((None, 2, PEER_HEADS, PEER_NK, PEER_DK // 2), lambda i: (layer, 0, 0, 0, 0)),
        ],
        out_specs=[tile, tile, kt, kt, kt],
        out_shape=[jax.ShapeDtypeStruct((N_TOK, D_MODEL), F32), jax.ShapeDtypeStruct((N_TOK, D_MODEL), BF16),
                   kshape, kshape, kshape],
        scratch_shapes=[pltpu.VMEM((TB, PEER_HEADS * PEER_DK), F32),
                        f32s(), i32s(), f32s(), i32s(), f32s(), i32s(), i32s()],
        compiler_params=_cparams("parallel"),
        name="peer_router",
    )(*mix_args, modt, g.reshape(1, D_MODEL), wq_bf16, keys_bf16)


def _expert_kernel(h_ref, x_ref, mod_ref, at_ref, bt_ref, gt_ref, u_ref, v_ref, fg_ref, o_ref, *rest, final):
    if final:
        y_ref, g_s, acc_s, a_s, b_s, w_s, rt_s = rest
    else:
        g_s, acc_s, a_s, b_s, w_s, rt_s = rest
    grp = pl.program_id(1)

    @pl.when(grp == 0)
    def _():
        acc_s[...] = jnp.zeros_like(acc_s)
        for part in range(TE // TB):
            a_s[part * TB:(part + 1) * TB, :] = at_ref[part].T
            b_s[part * TB:(part + 1) * TB, :] = bt_ref[part].T
            w_s[part * TB:(part + 1) * TB, :] = gt_ref[part].T
        sub = lax.broadcasted_iota(jnp.int32, (PEER_NK, PEER_NK), 0).astype(F32).astype(BF16)
        one = jnp.ones((PEER_NK, PEER_NK), BF16)
        zero = jnp.zeros((PEER_NK, PEER_NK), BF16)

        def gate_grid(t, slot):
            a_row = a_s[pl.ds(t, 1), :].astype(BF16)
            b_row = b_s[pl.ds(t, 1), :].astype(BF16)
            w_row = jnp.broadcast_to(w_s[pl.ds(t, 1), :].astype(BF16), (PEER_NK, PEER_NK))
            m_t = jnp.where(a_row == sub, one, zero)
            r_t = jnp.where(b_row == sub, w_row, zero)
            rt_s[slot] = r_t.T
            g = _dot(m_t, rt_s[slot]).astype(BF16).astype(F32)
            return lax.bitcast_convert_type(g, jnp.uint32)

        def body(t, carry):
            slot = t & (G_UNROLL - 1)
            word = (gate_grid(t, slot) >> 16) | (gate_grid(t + TE // 2, slot + G_UNROLL) & jnp.uint32(0xFFFF0000))
            g_s[pl.ds(t, PEER_NK, stride=G_PITCH), :] = word
            return carry

        lax.fori_loop(0, TE // 2, body, 0, unroll=G_UNROLL)

    h = h_ref[...]
    acc = acc_s[...]
    for sb in range(PEER_GROUP // EXPERT_SUB):
        e0 = sb * EXPERT_SUB
        hmat = _dot_nt(h, u_ref[e0:e0 + EXPERT_SUB, :])
        parts = []
        for j in range(EXPERT_SUB // PEER_NK):
            chunk = grp * CHUNKS_PER_GROUP + sb * (EXPERT_SUB // PEER_NK) + j
            word = g_s[pl.ds(pl.multiple_of(chunk * G_PITCH, 8), TE // 2), :]
            gates = jnp.concatenate(
                [lax.bitcast_convert_type(word << 16, F32),
                 lax.bitcast_convert_type(word & jnp.uint32(0xFFFF0000), F32)], axis=0)
            parts.append((_gelu_tanh(hmat[:, j * PEER_NK:(j + 1) * PEER_NK]) * gates).astype(BF16))
        acc = acc + _dot(jnp.concatenate(parts, axis=1), v_ref[e0:e0 + EXPERT_SUB, :])
    acc_s[...] = acc

    @pl.when(grp == PEER_NGROUPS - 1)
    def _():
        xo = x_ref[...] + mod_ref[0][5:6, :] * acc_s[...]
        if final:
            y = xo * lax.rsqrt(jnp.mean(xo * xo, axis=-1, keepdims=True) + EPS) * fg_ref[...]
            tile = pl.program_id(0)

            @pl.when(tile < N_PROMPT // TE)
            def _():
                o_ref[...] = y

            @pl.when(tile >= N_PROMPT // TE)
            def _():
                y_ref[...] = y
        else:
            o_ref[...] = xo


def _experts(h_bf16, x, modt_e, a_t, b_t, g_t, u_bf16, v_bf16, layer, final_g, final):
    tile = pl.BlockSpec((TE, D_MODEL), lambda i, g: (i, 0))
    kt = pl.BlockSpec((TE // TB, PEER_HEADS * PEER_TOPK, TB), lambda i, g: (i, 0, 0))
    wblk = pl.BlockSpec((None, PEER_GROUP, D_MODEL), lambda i, g: (layer, g, 0))
    xs = jax.ShapeDtypeStruct((N_TOK, D_MODEL), F32)
    n_prompt = N_PROMPT // TE
    sel = lambda: pltpu.VMEM((TE, PEER_HEADS * PEER_TOPK), F32)
    return pl.pallas_call(
        functools.partial(_expert_kernel, final=final),
        grid=(N_TOK // TE, PEER_NGROUPS),
        in_specs=[tile, tile, pl.BlockSpec((1, 6, D_MODEL), lambda i, g: (i, 0, 0)), kt, kt, kt, wblk, wblk,
                  pl.BlockSpec((1, D_MODEL), lambda i, g: (0, 0))],
        out_specs=[pl.BlockSpec((TE, D_MODEL), lambda i, g: (jnp.minimum(i, n_prompt - 1), 0)),
                   pl.BlockSpec((TE, D_MODEL), lambda i, g: (jnp.maximum(i - n_prompt, 0), 0))] if final else tile,
        out_shape=[jax.ShapeDtypeStruct((N_PROMPT, D_MODEL), F32),
                   jax.ShapeDtypeStruct((N_SAMPLE, D_MODEL), F32)] if final else xs,
        scratch_shapes=[pltpu.VMEM((PEER_NK * G_PITCH, PEER_NK), jnp.uint32), pltpu.VMEM((TE, D_MODEL), F32),
                        sel(), sel(), sel(), pltpu.VMEM((2 * G_UNROLL, PEER_NK, PEER_NK), BF16)],
        compiler_params=_cparams("arbitrary" if final else "parallel", "arbitrary"),
        name="peer_experts_final" if final else "peer_experts",
    )(h_bf16, x, modt_e, a_t, b_t, g_t, u_bf16, v_bf16, final_g.reshape(1, D_MODEL))


def _peer(mix, modt, modt_e, norm_g, peer_bf16, layer, final_g, final):
    w_q, keys, u, v = peer_bf16
    x, h, a_t, b_t, g_t = _router(mix, modt, norm_g, w_q, keys, layer)
    return _experts(h, x, modt_e, a_t, b_t, g_t, u, v, layer, final_g, final)


def kernel(x_prompt, x_sample, cache_k, cache_v, state_hgrn_fwd, state_hgrn_bwd, state_lru_fwd, state_lru_bwd, c, c_ctx, norm1_g, norm2_g, w_mod, b_mod, w_in_even, w_out_even, hgrn_lb_fwd, hgrn_lb_bwd, hgrn_gnorm_g, attn_sink, w_in_odd, w_out_odd, conv_w, conv_b, rg_w_a, rg_b_a, rg_w_x, rg_b_x, rg_lambda, peer_w_q, peer_keys, peer_u, peer_v, final_g):
    x = (x_prompt.reshape(N_PROMPT, D_MODEL), x_sample.reshape(N_SAMPLE, D_MODEL))
    cond = jnp.concatenate([c_ctx[None, :], c, jnp.zeros((COND_PAD - N_COND, D_MODEL), F32)], axis=0)
    mod = _modulation(cond, w_mod, b_mod)
    tile_row = np.asarray([0] * PROMPT_TILES + [1 + i // TILES_PER_SAMPLE for i in range(N_TILES - PROMPT_TILES)])
    modt = [mod[l][tile_row].reshape(N_TILES, 6, D_MODEL) for l in range(DEPTH)]
    modt_e = [m[::TE // TB] for m in modt]

    widths = (HG_W, HG_W, HG_W, HG_W, HG_W, ATT_W, KV_W, KV_W)
    zq, zf, zb, zi, zg, aq, ak, av = _in_proj(x, modt[0], norm1_g[0], w_in_even[0].astype(BF16), widths, "even_in")
    zero_state = jnp.zeros((1, HG_HEADS, HG_DV, HG_DK), F32)
    s0f = jnp.concatenate([zero_state, jnp.swapaxes(state_hgrn_fwd[:, 0], -1, -2)], axis=0)
    s0b = jnp.concatenate([zero_state, jnp.swapaxes(state_hgrn_bwd[:, 0], -1, -2)], axis=0)
    o_f, o_b, sf_t, sb_t = _hgrn_scan(zq, zf, zb, zi, hgrn_lb_fwd, hgrn_lb_bwd, s0f, s0b)
    oatt = _attention(aq, ak, av, cache_k[:, 0].reshape(DEC_BATCH, PAST_LEN, KV_W),
                      cache_v[:, 0].reshape(DEC_BATCH, PAST_LEN, KV_W), attn_sink[0])
    even_mix = (_even_mix, _even_mix_operands(x, o_f, o_b, zg, oatt, hgrn_gnorm_g[0], w_out_even[0].astype(BF16)))
    peer_bf16 = (peer_w_q.astype(BF16), peer_keys.astype(BF16), peer_u.astype(BF16), peer_v.astype(BF16))
    x = _peer(even_mix, modt[0], modt_e[0], norm2_g[0], peer_bf16, 0, final_g, False)

    gate, xc = _odd_in(x, modt[1], norm1_g[1], w_in_odd[0].astype(BF16), conv_w[0], conv_b[0])
    zero_h = jnp.zeros((1, 1, D_RNN), F32)
    h0f = jnp.concatenate([zero_h, state_lru_fwd[:, 0][:, None, :]], axis=0)
    h0b = jnp.concatenate([zero_h, state_lru_bwd[:, 0][:, None, :]], axis=0)
    hf, hb = _lru_scan(xc, rg_w_a[0].astype(BF16), rg_b_a[0], rg_w_x[0].astype(BF16), rg_b_x[0],
                       rg_lambda[0], h0f, h0b)
    odd_mix = (_odd_mix, _odd_mix_operands(x, hf, hb, gate, w_out_odd[0].astype(BF16)))
    y_p, y_s = _peer(odd_mix, modt[1], modt_e[1], norm2_g[1], peer_bf16, 1, final_g, True)

    y_prompt = y_p.reshape(BATCH, SEQ, D_MODEL)
    y_sample = y_s.reshape(DEC_BATCH, DEC_SEQ, D_MODEL)
    new_k = ak[:N_PROMPT].reshape(BATCH, 1, SEQ, KV_HEADS, HEAD_DIM)
    new_v = av[:N_PROMPT].reshape(BATCH, 1, SEQ, KV_HEADS, HEAD_DIM)
    new_hf = jnp.swapaxes(sf_t[:BATCH], -1, -2)[:, None]
    new_hb = jnp.swapaxes(sb_t[:BATCH], -1, -2)[:, None]
    new_lf = hf[:N_PROMPT].reshape(BATCH, SEQ, D_RNN)[:, -1][:, None, :]
    new_lb = hb[:N_PROMPT].reshape(BATCH, SEQ, D_RNN)[:, 0][:, None, :]
    return (y_prompt, y_sample, new_k, new_v, new_hf, new_hb, new_lf, new_lb)
```

```python
import functools

import numpy as np
import jax
import jax.numpy as jnp
from jax import lax
from jax.experimental import pallas as pl
from jax.experimental.pallas import tpu as pltpu

F32 = jnp.float32
BF16 = jnp.bfloat16

D_MODEL = 1024
BATCH = 16
SEQ = 256
DEPTH = 2
DEC_BATCH = 8
DEC_SEQ = 2048
PAST_LEN = 512
GRID_W = 64
EPS = 1e-6
HG_DK = 128
HG_DV = 128
HG_HEADS = 4
HG_CHUNK = 64
HG_BLOCK = 256
HG_W = HG_HEADS * HG_DK
HEAD_DIM = 64
N_HEADS = 8
KV_HEADS = 2
Q_PER_KV = 4
WINDOW = 128
ATT_BLOCK = 128
ROPE_BASE = 10000.0
ATT_W = N_HEADS * HEAD_DIM
KV_W = KV_HEADS * HEAD_DIM
D_RNN = 1024
RG_BLOCKS = 4
RG_BW = 256
RG_C = 8.0
PEER_HEADS = 8
PEER_NK = 128
PEER_DK = 256
PEER_TOPK = 16

N_PROMPT = BATCH * SEQ
N_SAMPLE = DEC_BATCH * DEC_SEQ
N_TOK = N_PROMPT + N_SAMPLE
TB = 256
N_TILES = N_TOK // TB
PROMPT_TILES = N_PROMPT // TB
TILES_PER_SAMPLE = DEC_SEQ // TB
N_COND = 1 + DEC_BATCH
COND_PAD = 16

TE = 512
PEER_GROUP = 2048
PEER_NGROUPS = PEER_NK * PEER_NK // PEER_GROUP
CHUNKS_PER_GROUP = PEER_GROUP // PEER_NK
EXPERT_SUB = 1024
G_PITCH = TE // 2 + 8
G_UNROLL = 16

LRU_CHUNK = 256
SUBLANES = 8

VMEM_LIMIT = 58 * 1024 * 1024


def _cparams(*sem):
    return pltpu.CompilerParams(dimension_semantics=sem, vmem_limit_bytes=VMEM_LIMIT)


def _dot(a, b):
    return jnp.dot(a, b, preferred_element_type=F32)


def _dot_nt(a, b):
    return lax.dot_general(a, b, (((1,), (1,)), ((), ())), preferred_element_type=F32)


def _dot_tn(a, b):
    return lax.dot_general(a, b, (((0,), (0,)), ((), ())), preferred_element_type=F32)


def _sigmoid(x):
    return 1.0 / (1.0 + jnp.exp(-x))


def _sigmoid_tanh(x):
    return 0.5 + 0.5 * jnp.tanh(0.5 * x)


def _silu(x):
    return x * _sigmoid(x)


def _gelu_tanh(x):
    c = np.float32(np.sqrt(2.0 / np.pi))
    return x * (0.5 * (1.0 + jnp.tanh(c * (x + 0.044715 * (x * x * x)))))


def _rms_mod(x, g, mod, sh, sc):
    y = x * lax.rsqrt(jnp.mean(x * x, axis=-1, keepdims=True) + EPS) * g
    return y * (1.0 + mod[sc:sc + 1, :]) + mod[sh:sh + 1, :]


def _mod_kernel(c_ref, w_ref, b_ref, o_ref):
    c = c_ref[...]
    s = _silu(c).astype(BF16)
    o_ref[0] = _dot(s, w_ref[0].astype(BF16)) + b_ref[0]


def _modulation(cond, w_mod, b_mod):
    nb = 6 * D_MODEL // 1024
    return pl.pallas_call(
        _mod_kernel,
        grid=(DEPTH, nb),
        in_specs=[
            pl.BlockSpec((COND_PAD, D_MODEL), lambda l, n: (0, 0)),
            pl.BlockSpec((1, D_MODEL, 1024), lambda l, n: (l, 0, n)),
            pl.BlockSpec((1, 1, 1024), lambda l, n: (l, 0, n)),
        ],
        out_specs=pl.BlockSpec((1, COND_PAD, 1024), lambda l, n: (l, 0, n)),
        out_shape=jax.ShapeDtypeStruct((DEPTH, COND_PAD, 6 * D_MODEL), F32),
        compiler_params=_cparams("parallel", "parallel"),
        name="modulation",
    )(cond, w_mod, b_mod.reshape(DEPTH, 1, 6 * D_MODEL))


def _token_specs(arrays, width):
    if not isinstance(arrays, tuple):
        return [pl.BlockSpec((TB, width), lambda i: (i, 0))]
    return [pl.BlockSpec((TB, width), lambda i: (jnp.minimum(i, PROMPT_TILES - 1), 0)),
            pl.BlockSpec((TB, width), lambda i: (jnp.maximum(i - PROMPT_TILES, 0), 0))]


def _token_tile(refs):
    if len(refs) == 1:
        return refs[0][...]
    return jnp.where(pl.program_id(0) < PROMPT_TILES, refs[0][...], refs[1][...])


def _in_proj_kernel(*refs, widths, n_x):
    x_refs, (mod_ref, g_ref, w_ref), o_refs = refs[:n_x], refs[n_x:n_x + 3], refs[n_x + 3:]
    h = _rms_mod(_token_tile(x_refs), g_ref[...], mod_ref[0], 0, 1)
    z = _dot(h.astype(BF16), w_ref[...])
    off = 0
    for o_ref, wd in zip(o_refs, widths):
        o_ref[...] = z[:, off:off + wd]
        off += wd


def _in_proj(x, modt, g, w_bf16, widths, name):
    n_out = sum(widths)
    xs = x if isinstance(x, tuple) else (x,)
    return pl.pallas_call(
        functools.partial(_in_proj_kernel, widths=widths, n_x=len(xs)),
        grid=(N_TILES,),
        in_specs=_token_specs(x, D_MODEL) + [
            pl.BlockSpec((1, 6, D_MODEL), lambda i: (i, 0, 0)),
            pl.BlockSpec((1, D_MODEL), lambda i: (0, 0)),
            pl.BlockSpec((D_MODEL, n_out), lambda i: (0, 0)),
        ],
        out_specs=[pl.BlockSpec((TB, wd), lambda i: (i, 0)) for wd in widths],
        out_shape=[jax.ShapeDtypeStruct((N_TOK, wd), F32) for wd in widths],
        compiler_params=_cparams("parallel"),
        name=name,
    )(*xs, modt, g.reshape(1, D_MODEL), w_bf16)


def _hgrn_direction(zq, zgate, zi, lb, states, fwd):
    n = HG_CHUNK
    row = lax.broadcasted_iota(jnp.int32, (n, n), 0)
    col = lax.broadcasted_iota(jnp.int32, (n, n), 1)
    keep = (col <= row) if fwd else (col >= row)
    tri = jnp.where(keep, 1.0, 0.0).astype(BF16)
    q = zq * _sigmoid_tanh(zq)
    sg = _sigmoid_tanh(zgate)
    f = lb + (1.0 - lb) * sg
    g = jnp.log(f)
    k = (1.0 - lb) * (1.0 - sg)
    g_hi = g.astype(BF16)
    g_lo = (g - g_hi.astype(F32)).astype(BF16)
    b = _dot(tri, g_hi) + _dot(tri, g_lo)
    mid = n // 2 - 1 if fwd else n // 2
    last = n - 1 if fwd else 0
    bm = b[mid:mid + 1, :]
    bl = b[last:last + 1, :]
    qi32 = q * jnp.exp(b - bm)
    ki32 = k * jnp.exp(bm - b)
    qi = qi32.astype(BF16)
    ki = ki32.astype(BF16)
    qs = (qi32 * jnp.exp(bm)).astype(BF16)
    ks = (ki32 * jnp.exp(bl - bm)).astype(BF16)
    vb = zi.astype(BF16)
    decay = jnp.exp(bl)
    outs = []
    new_states = []
    for h in range(HG_HEADS):
        sl = slice(h * HG_DK, (h + 1) * HG_DK)
        a = jnp.where(keep, _dot_nt(qi[:, sl], ki[:, sl]), 0.0)
        st = states[h]
        outs.append(_dot(a.astype(BF16), vb[:, sl]) + _dot_nt(qs[:, sl], st.astype(BF16)))
        new_states.append(decay[:, sl] * st + _dot_tn(vb[:, sl], ks[:, sl]))
    return jnp.concatenate(outs, axis=1), new_states


def _lower_bound(p):
    e = jnp.exp(p - p.max(axis=0, keepdims=True))
    return e[0:1, :] / e.sum(axis=0, keepdims=True)


def _hgrn_block(zq_ref, zgate_ref, zi_ref, lb, st_ref, o_ref, fwd):
    states = [st_ref[h] for h in range(HG_HEADS)]
    n_sub = HG_BLOCK // HG_CHUNK
    for c in (range(n_sub) if fwd else reversed(range(n_sub))):
        rows = slice(c * HG_CHUNK, (c + 1) * HG_CHUNK)
        o_ref[rows, :], states = _hgrn_direction(zq_ref[rows, :], zgate_ref[rows, :], zi_ref[rows, :], lb, states, fwd)
    for h in range(HG_HEADS):
        st_ref[h] = states[h]


def _hgrn_kernel(tbl_ref, zqf_ref, zff_ref, zif_ref, zqb_ref, zbb_ref, zib_ref, lbf_ref, lbb_ref,
                 s0f_ref, s0b_ref, of_ref, ob_ref, sf_ref, sb_ref, stf, stb):
    s = pl.program_id(0)

    @pl.when(tbl_ref[4, s] == 1)
    def _():
        stf[...] = s0f_ref[0]
        stb[...] = s0b_ref[0]

    _hgrn_block(zqf_ref, zff_ref, zif_ref, _lower_bound(lbf_ref[...]), stf, of_ref, True)
    _hgrn_block(zqb_ref, zbb_ref, zib_ref, _lower_bound(lbb_ref[...]), stb, ob_ref, False)

    @pl.when(tbl_ref[5, s] == 1)
    def _():
        sf_ref[0] = stf[...]
        sb_ref[0] = stb[...]


def _scan_table(chunk):
    rows = []
    blk = 0
    for (nseq, length, has_init) in ((BATCH, SEQ, False), (DEC_BATCH, DEC_SEQ, True)):
        nc = length // chunk
        for b in range(nseq):
            sid = b if not has_init else BATCH + b
            for i in range(nc):
                rows.append((blk + i, blk + nc - 1 - i, (1 + b) if has_init else 0, sid,
                             int(i == 0), int(i == nc - 1)))
            blk += nc
    return np.asarray(rows, np.int32).T.copy()


def _hgrn_scan(zq, zf, zb, zi, lbf, lbb, s0f_t, s0b_t):
    tbl = _scan_table(HG_BLOCK)
    n_steps = tbl.shape[1]
    n_seq = BATCH + DEC_BATCH
    blk = lambda r: pl.BlockSpec((HG_BLOCK, HG_W), lambda s, t: (t[r, s], 0))
    vec = pl.BlockSpec(lbf.shape, lambda s, t: (0, 0))
    st_in = pl.BlockSpec((1, HG_HEADS, HG_DV, HG_DK), lambda s, t: (t[2, s], 0, 0, 0))
    st_out = pl.BlockSpec((1, HG_HEADS, HG_DV, HG_DK), lambda s, t: (t[3, s], 0, 0, 0))
    grid_spec = pltpu.PrefetchScalarGridSpec(
        num_scalar_prefetch=1,
        grid=(n_steps,),
        in_specs=[blk(0), blk(0), blk(0), blk(1), blk(1), blk(1), vec, vec, st_in, st_in],
        out_specs=[blk(0), blk(1), st_out, st_out],
        scratch_shapes=[pltpu.VMEM((HG_HEADS, HG_DV, HG_DK), F32),
                        pltpu.VMEM((HG_HEADS, HG_DV, HG_DK), F32)],
    )
    st_shape = jax.ShapeDtypeStruct((n_seq, HG_HEADS, HG_DV, HG_DK), F32)
    return pl.pallas_call(
        _hgrn_kernel,
        grid_spec=grid_spec,
        out_shape=[jax.ShapeDtypeStruct((N_TOK, HG_W), F32), jax.ShapeDtypeStruct((N_TOK, HG_W), F32),
                   st_shape, st_shape],
        compiler_params=_cparams("arbitrary"),
        name="hgrn_scan",
    )(jnp.asarray(tbl), zq, zf, zi, zq, zb, zi, lbf, lbb, s0f_t, s0b_t)


def _rope(x, cos, sgn_sin):
    w = x.shape[1]
    lane = lax.broadcasted_iota(jnp.int32, x.shape, 1)
    partner = jnp.where((lane & 31) < 16, pltpu.roll(x, w - 16, 1), pltpu.roll(x, 16, 1))
    return x * cos + partner * sgn_sin


def _attend(q, key_parts, sink_ref, o_ref):
    lane = lax.broadcasted_iota(jnp.int32, (ATT_BLOCK, 128), 1)
    lo = lane < HEAD_DIM
    for kv in range(KV_HEADS):
        own = lo if kv == 0 else ~lo
        qs = []
        for gq in range(Q_PER_KV):
            hd = kv * Q_PER_KV + gq
            pair = q[:, (hd // 2) * 128:(hd // 2 + 1) * 128]
            if hd % 2 != kv:
                pair = pltpu.roll(pair, HEAD_DIM, 1)
            qs.append(jnp.where(own, pair, 0.0))
        qst = jnp.concatenate(qs, axis=0).astype(BF16)
        logits = []
        vals = []
        for (k, v, mask) in key_parts:
            own_rows = lax.broadcasted_iota(jnp.int32, v.shape, 1) < HEAD_DIM
            own_rows = own_rows if kv == 0 else ~own_rows
            vals.append(jnp.where(own_rows, v, 1.0).astype(BF16))
            s = _dot_nt(qst, k.astype(BF16))
            if mask is not None:
                s = jnp.where(mask, s, -1e30)
            logits.append(s)
        tiles = [s[:, t * 128:(t + 1) * 128] for s in logits for t in range(s.shape[1] // 128)]
        m = functools.reduce(jnp.maximum, tiles).max(axis=-1, keepdims=True)
        sink_col = jnp.concatenate(
            [jnp.full((ATT_BLOCK, 1), sink_ref[kv * Q_PER_KV + gq], F32) for gq in range(Q_PER_KV)], axis=0)
        m = jnp.maximum(m, sink_col)
        acc = jnp.zeros((Q_PER_KV * ATT_BLOCK, 128), F32)
        for s, v in zip(logits, vals):
            acc = acc + _dot(jnp.exp(s - m).astype(BF16), v)
        den = pltpu.roll(acc, HEAD_DIM, 1) + jnp.exp(sink_col - m)
        out = acc / den
        for pr in range(Q_PER_KV // 2):
            r0 = out[(2 * pr) * ATT_BLOCK:(2 * pr + 1) * ATT_BLOCK]
            r1 = out[(2 * pr + 1) * ATT_BLOCK:(2 * pr + 2) * ATT_BLOCK]
            if kv == 0:
                r1 = pltpu.roll(r1, HEAD_DIM, 1)
            else:
                r0 = pltpu.roll(r0, HEAD_DIM, 1)
            c0 = (kv * Q_PER_KV // 2 + pr) * 128
            o_ref[:, c0:c0 + 128] = jnp.where(lo, r0, r1)


def _attn_prompt_kernel(sink_ref, q_ref, k_ref, v_ref, o_ref):
    q = q_ref[...] * np.float32(HEAD_DIM ** -0.5)
    _attend(q, [(k_ref[...], v_ref[...], None)], sink_ref, o_ref)


def _attn_latent_kernel(sink_ref, q_ref, kp_ref, kc_ref, kn_ref, vp_ref, vc_ref, vn_ref,
                        ck_ref, cv_ref, cos_ref, sin_ref, o_ref):
    j = pl.program_id(1)
    nq = pl.num_programs(1)
    qstart = pl.multiple_of((j + 1) * ATT_BLOCK, ATT_BLOCK)
    kstart = pl.multiple_of(j * ATT_BLOCK, ATT_BLOCK)
    cq = cos_ref[pl.ds(qstart, ATT_BLOCK), :]
    sq = sin_ref[pl.ds(qstart, ATT_BLOCK), :]
    q = _rope(q_ref[...], jnp.concatenate([cq] * 4, axis=1), jnp.concatenate([sq] * 4, axis=1))
    q = q * np.float32(HEAD_DIM ** -0.5)
    k_loc = jnp.concatenate([kp_ref[...], kc_ref[...], kn_ref[...]], axis=0)
    v_loc = jnp.concatenate([vp_ref[...], vc_ref[...], vn_ref[...]], axis=0)
    k_loc = _rope(k_loc, cos_ref[pl.ds(kstart, 3 * ATT_BLOCK), :], sin_ref[pl.ds(kstart, 3 * ATT_BLOCK), :])
    r = lax.broadcasted_iota(jnp.int32, (Q_PER_KV * ATT_BLOCK, 3 * ATT_BLOCK), 0) & (ATT_BLOCK - 1)
    c = lax.broadcasted_iota(jnp.int32, (Q_PER_KV * ATT_BLOCK, 3 * ATT_BLOCK), 1)
    rel = c - ATT_BLOCK - r
    c_lo = jnp.where(j > 0, 0, ATT_BLOCK)
    c_hi = jnp.where(j < nq - 1, 3 * ATT_BLOCK, 2 * ATT_BLOCK)
    ok = (jnp.abs(rel) <= WINDOW) & (c >= c_lo) & (c < c_hi)
    _attend(q, [(ck_ref[0], cv_ref[0], None), (k_loc, v_loc, ok)], sink_ref, o_ref)


def _rope_tables():
    pos = np.arange(-ATT_BLOCK, DEC_SEQ + ATT_BLOCK)
    nf = HEAD_DIM // 4
    inv = (1.0 / (ROPE_BASE ** (np.arange(nf, dtype=np.float32) / nf))).astype(np.float32)
    rows = (pos // GRID_W).astype(np.float32)
    cols = (pos % GRID_W).astype(np.float32)
    ar = rows[:, None] * inv[None, :]
    ac = cols[:, None] * inv[None, :]
    cos64 = np.concatenate([np.cos(ar), np.cos(ar), np.cos(ac), np.cos(ac)], axis=1)
    sin64 = np.concatenate([-np.sin(ar), np.sin(ar), -np.sin(ac), np.sin(ac)], axis=1)
    cos = np.concatenate([cos64, cos64], axis=1).astype(np.float32)
    sin = np.concatenate([sin64, sin64], axis=1).astype(np.float32)
    return jnp.asarray(cos), jnp.asarray(sin)


def _attention(aq, ak, av, ck, cv, sink):
    smem = pl.BlockSpec(memory_space=pltpu.SMEM)
    nqp = SEQ // ATT_BLOCK
    o_prompt = pl.pallas_call(
        _attn_prompt_kernel,
        grid=(BATCH, nqp),
        in_specs=[
            smem,
            pl.BlockSpec((ATT_BLOCK, ATT_W), lambda b, j: (b * nqp + j, 0)),
            pl.BlockSpec((SEQ, KV_W), lambda b, j: (b, 0)),
            pl.BlockSpec((SEQ, KV_W), lambda b, j: (b, 0)),
        ],
        out_specs=pl.BlockSpec((ATT_BLOCK, ATT_W), lambda b, j: (b * nqp + j, 0)),
        out_shape=jax.ShapeDtypeStruct((N_PROMPT, ATT_W), F32),
        compiler_params=_cparams("parallel", "parallel"),
        name="attn_prompt",
    )(sink, aq, ak, av)

    nq = DEC_SEQ // ATT_BLOCK
    base = N_PROMPT // ATT_BLOCK
    cos, sin = _rope_tables()
    kv_blk = lambda d: pl.BlockSpec(
        (ATT_BLOCK, KV_W), lambda b, j: (base + b * nq + jnp.clip(j + d, 0, nq - 1), 0))
    tab = pl.BlockSpec((DEC_SEQ + 2 * ATT_BLOCK, 128), lambda b, j: (0, 0))
    o_latent = pl.pallas_call(
        _attn_latent_kernel,
        grid=(DEC_BATCH, nq),
        in_specs=[
            smem,
            pl.BlockSpec((ATT_BLOCK, ATT_W), lambda b, j: (base + b * nq + j, 0)),
            kv_blk(-1), kv_blk(0), kv_blk(1), kv_blk(-1), kv_blk(0), kv_blk(1),
            pl.BlockSpec((1, PAST_LEN, KV_W), lambda b, j: (b, 0, 0)),
            pl.BlockSpec((1, PAST_LEN, KV_W), lambda b, j: (b, 0, 0)),
            tab, tab,
        ],
        out_specs=pl.BlockSpec((ATT_BLOCK, ATT_W), lambda b, j: (b * nq + j, 0)),
        out_shape=jax.ShapeDtypeStruct((N_SAMPLE, ATT_W), F32),
        compiler_params=_cparams("parallel", "parallel"),
        name="attn_latent",
    )(sink, aq, ak, ak, ak, av, av, av, ck, cv, cos, sin)
    return o_prompt, o_latent


def _even_mix(refs, mod):
    xp_ref, xs_ref, of_ref, ob_ref, zg_ref, oattp_ref, oatts_ref, gn_ref, w_ref = refs
    x = _token_tile((xp_ref, xs_ref))
    oatt = _token_tile((oattp_ref, oatts_ref))
    o = of_ref[...] + ob_ref[...]
    parts = []
    for h in range(HG_HEADS):
        oh = o[:, h * HG_DV:(h + 1) * HG_DV]
        parts.append(oh * lax.rsqrt(jnp.mean(oh * oh, axis=-1, keepdims=True) + EPS))
    o_hg = jnp.concatenate(parts, axis=1) * gn_ref[...] * _silu(zg_ref[...])
    y = _dot(o_hg.astype(BF16), w_ref[0:HG_W, :]) + _dot(oatt.astype(BF16), w_ref[HG_W:, :])
    return x + mod[2:3, :] * y


def _even_mix_operands(x_pair, o_f, o_b, zg, oatt_pair, gn_g, w_bf16):
    tile = lambda w: pl.BlockSpec((TB, w), lambda i: (i, 0))
    specs = (_token_specs(x_pair, D_MODEL) + [tile(HG_W), tile(HG_W), tile(HG_W)] + _token_specs(oatt_pair, ATT_W)
             + [pl.BlockSpec((1, HG_W), lambda i: (0, 0)), pl.BlockSpec((HG_W + ATT_W, D_MODEL), lambda i: (0, 0))])
    return specs, (*x_pair, o_f, o_b, zg, *oatt_pair, gn_g.reshape(1, HG_W), w_bf16)


def _odd_in_kernel(x_ref, xp_ref, xn_ref, mod_ref, g_ref, w_ref, cw_ref, cb_ref, gate_ref, xc_ref):
    i = pl.program_id(0)
    in_sample = i >= PROMPT_TILES
    pos = (i - PROMPT_TILES) % TILES_PER_SAMPLE
    has_prev = in_sample & (pos != 0)
    has_next = in_sample & (pos != TILES_PER_SAMPLE - 1)
    mod = mod_ref[0]
    g = g_ref[...]
    z = _dot(_rms_mod(x_ref[...], g, mod, 0, 1).astype(BF16), w_ref[...])
    gate_ref[...] = z[:, :D_RNN]
    w_rec = w_ref[:, D_RNN:]
    prev = _dot(_rms_mod(xp_ref[...], g, mod, 0, 1).astype(BF16), w_rec)
    nxt = _dot(_rms_mod(xn_ref[...], g, mod, 0, 1).astype(BF16), w_rec)
    prev = jnp.where(has_prev, prev, 0.0)
    nxt = jnp.where(has_next, nxt, 0.0)
    ext = jnp.concatenate([prev, z[:, D_RNN:], nxt], axis=0)
    n = TB + 16
    w = cw_ref[...]
    acc = ext[8:8 + TB] * w[2:3, :] + cb_ref[...]
    acc = acc + pltpu.roll(ext, 2, 0)[8:8 + TB] * w[0:1, :]
    acc = acc + pltpu.roll(ext, 1, 0)[8:8 + TB] * w[1:2, :]
    acc = acc + pltpu.roll(ext, n - 1, 0)[8:8 + TB] * w[3:4, :]
    xc_ref[...] = acc


def _odd_in(x, modt, g, w_bf16, cw, cb):
    r8 = TB // 8
    last8 = N_TOK // 8 - 1
    tile = pl.BlockSpec((TB, D_MODEL), lambda i: (i, 0))
    return pl.pallas_call(
        _odd_in_kernel,
        grid=(N_TILES,),
        in_specs=[
            tile,
            pl.BlockSpec((8, D_MODEL), lambda i: (jnp.maximum(i * r8 - 1, 0), 0)),
            pl.BlockSpec((8, D_MODEL), lambda i: (jnp.minimum((i + 1) * r8, last8), 0)),
            pl.BlockSpec((1, 6, D_MODEL), lambda i: (i, 0, 0)),
            pl.BlockSpec((1, D_MODEL), lambda i: (0, 0)),
            pl.BlockSpec((D_MODEL, 2 * D_RNN), lambda i: (0, 0)),
            pl.BlockSpec((4, D_RNN), lambda i: (0, 0)),
            pl.BlockSpec((1, D_RNN), lambda i: (0, 0)),
        ],
        out_specs=[tile, tile],
        out_shape=[jax.ShapeDtypeStruct((N_TOK, D_RNN), F32), jax.ShapeDtypeStruct((N_TOK, D_RNN), F32)],
        compiler_params=_cparams("parallel"),
        name="odd_in",
    )(x, x, x, modt, g.reshape(1, D_MODEL), w_bf16, cw, cb.reshape(1, D_RNN))


def _log1p(y):
    w = 1.0 + y
    return jnp.where(w == 1.0, y, jnp.log(w) * (y / (w - 1.0)))


def _expm1_given(x, e):
    return jnp.where(e == 1.0, x, (e - 1.0) * (x / jnp.log(e)))


def _lru_direction(xc, wa_ref, ba, wx_ref, bx, lam, h_ref, o_ref, fwd):
    n = LRU_CHUNK
    xb = xc.astype(BF16)
    ra, ri = [], []
    for blk in range(RG_BLOCKS):
        sl = slice(blk * RG_BW, (blk + 1) * RG_BW)
        ra.append(_dot(xb[:, sl], wa_ref[blk]))
        ri.append(_dot(xb[:, sl], wx_ref[blk]))
    r = _sigmoid_tanh(jnp.concatenate(ra, axis=1) + ba)
    gi = _sigmoid_tanh(jnp.concatenate(ri, axis=1) + bx)
    neg = -lam
    softplus = jnp.maximum(neg, 0.0) + _log1p(jnp.exp(-jnp.abs(neg)))
    log_a = (-RG_C * softplus) * r
    a = jnp.exp(log_a)
    u = jnp.sqrt(-_expm1_given(2.0 * log_a, a * a)) * (gi * xc)
    sub = lax.broadcasted_iota(jnp.int32, (SUBLANES, D_RNN), 0)
    carry = h_ref[...]
    groups = range(n // SUBLANES)
    for g in (groups if fwd else reversed(groups)):
        rows = slice(g * SUBLANES, (g + 1) * SUBLANES)
        ag, ug = a[rows, :], u[rows, :]
        for d in (1, 2, 4):
            live = (sub >= d) if fwd else (sub < SUBLANES - d)
            shift = d if fwd else SUBLANES - d
            a_s = jnp.where(live, pltpu.roll(ag, shift, 0), 1.0)
            u_s = jnp.where(live, pltpu.roll(ug, shift, 0), 0.0)
            ug = ag * u_s + ug
            ag = ag * a_s
        h = ug + ag * carry
        o_ref[rows, :] = h
        carry = h[SUBLANES - 1:SUBLANES, :] if fwd else h[0:1, :]
    h_ref[...] = carry


def _lru_kernel(tbl_ref, xf_ref, xb_ref, wa_ref, ba_ref, wx_ref, bx_ref, lam_ref, h0f_ref, h0b_ref,
                hf_ref, hb_ref, hf_s, hb_s):
    s = pl.program_id(0)

    @pl.when(tbl_ref[4, s] == 1)
    def _():
        hf_s[...] = h0f_ref[0]
        hb_s[...] = h0b_ref[0]

    _lru_direction(xf_ref[...], wa_ref.at[0], ba_ref[0:1, :], wx_ref.at[0], bx_ref[0:1, :],
                   lam_ref[0:1, :], hf_s, hf_ref, True)
    _lru_direction(xb_ref[...], wa_ref.at[1], ba_ref[1:2, :], wx_ref.at[1], bx_ref[1:2, :],
                   lam_ref[1:2, :], hb_s, hb_ref, False)


def _lru_scan(xc, w_a_bf16, b_a, w_x_bf16, b_x, lam, h0f_all, h0b_all):
    tbl = _scan_table(LRU_CHUNK)
    n_steps = tbl.shape[1]
    blk = lambda r: pl.BlockSpec((LRU_CHUNK, D_RNN), lambda s, t: (t[r, s], 0))
    wspec = pl.BlockSpec((2, RG_BLOCKS, RG_BW, RG_BW), lambda s, t: (0, 0, 0, 0))
    vec2 = pl.BlockSpec((2, D_RNN), lambda s, t: (0, 0))
    h0 = pl.BlockSpec((1, 1, D_RNN), lambda s, t: (t[2, s], 0, 0))
    grid_spec = pltpu.PrefetchScalarGridSpec(
        num_scalar_prefetch=1,
        grid=(n_steps,),
        in_specs=[blk(0), blk(1), wspec, vec2, wspec, vec2, vec2, h0, h0],
        out_specs=[blk(0), blk(1)],
        scratch_shapes=[pltpu.VMEM((1, D_RNN), F32), pltpu.VMEM((1, D_RNN), F32)],
    )
    return pl.pallas_call(
        _lru_kernel,
        grid_spec=grid_spec,
        out_shape=[jax.ShapeDtypeStruct((N_TOK, D_RNN), F32), jax.ShapeDtypeStruct((N_TOK, D_RNN), F32)],
        compiler_params=_cparams("arbitrary"),
        name="lru_scan",
    )(jnp.asarray(tbl), xc, xc, w_a_bf16, b_a, w_x_bf16, b_x, lam, h0f_all, h0b_all)


def _odd_mix(refs, mod):
    x_ref, hf_ref, hb_ref, gate_ref, w_ref = refs
    y = (hf_ref[...] + hb_ref[...]) * _gelu_tanh(gate_ref[...])
    return x_ref[...] + mod[2:3, :] * _dot(y.astype(BF16), w_ref[...])


def _odd_mix_operands(x, hf, hb, gate, w_bf16):
    tile = pl.BlockSpec((TB, D_MODEL), lambda i: (i, 0))
    return [tile, tile, tile, tile, pl.BlockSpec((D_RNN, D_MODEL), lambda i: (0, 0))], (x, hf, hb, gate, w_bf16)


def _order_key(x):
    b = lax.bitcast_convert_type(x, jnp.int32)
    return b ^ ((b >> 31) & 0x7FFFFFFF)


def _key_value(k):
    return lax.bitcast_convert_type(k ^ ((k >> 31) & 0x7FFFFFFF), F32)


def _precedes(ky, ry, kx, rx):
    return (ky + jnp.where(ry < rx, 1, 0)) > kx


def _merge_exchange(n):
    pairs = []
    t = max(1, (n - 1).bit_length())
    p = 1 << (t - 1)
    while p > 0:
        q, r, d = 1 << (t - 1), 0, p
        while d > 0:
            pairs.extend((i, i + d) for i in range(n - d) if (i & p) == r)
            d, q, r = q - p, q >> 1, p
        p >>= 1
    return pairs


def _top_rows(values, count, emit):
    n = len(values)
    sub = lax.broadcasted_iota(jnp.int32, (SUBLANES, TB), 0)
    keys = [_order_key(v) for v in values]
    rows = [sub + SUBLANES * i for i in range(n)]
    for i, j in _merge_exchange(n):
        swap = _precedes(keys[j], rows[j], keys[i], rows[i])
        keys[i], keys[j] = jnp.where(swap, keys[j], keys[i]), jnp.where(swap, keys[i], keys[j])
        rows[i], rows[j] = jnp.where(swap, rows[j], rows[i]), jnp.where(swap, rows[i], rows[j])
    lowest = jnp.full((SUBLANES, TB), jnp.iinfo(jnp.int32).min + 1, jnp.int32)
    for step in range(count):
        k = keys[0]
        for shift in (4, 2, 1):
            k = jnp.maximum(k, pltpu.roll(k, shift, 0))
        r = jnp.where(keys[0] == k, rows[0], jnp.iinfo(jnp.int32).max)
        for shift in (4, 2, 1):
            r = jnp.minimum(r, pltpu.roll(r, shift, 0))
        emit(step, _key_value(k[0:1, :]), r[0:1, :])
        depth = min(n - 1, count - 1 - step)
        column = rows[0] == r
        for d in range(depth):
            keys[d] = jnp.where(column, keys[d + 1], keys[d])
            rows[d] = jnp.where(column, rows[d + 1], rows[d])
        if depth == n - 1:
            keys[n - 1] = jnp.where(column, lowest, keys[n - 1])


def _router_kernel(*refs, n_mix, mix_fn):
    mix_refs = refs[:n_mix]
    (mod_ref, g_ref, wq_ref, keys_ref, x_ref, h_ref, a_ref, b_ref, gate_ref,
     q_s, v1_s, i1_s, v2_s, i2_s, sc_s, a_s, b_s) = refs[n_mix:]
    x = mix_fn(mix_refs, mod_ref[0])
    x_ref[...] = x
    h = _rms_mod(x, g_ref[...], mod_ref[0], 3, 4).astype(BF16)
    h_ref[...] = h
    q_s[...] = _dot(h, wq_ref[...])
    lax.fori_loop(0, PEER_HEADS, functools.partial(
        _route_head, keys_ref, a_ref, b_ref, gate_ref, q_s, v1_s, i1_s, v2_s, i2_s, sc_s, a_s, b_s), 0)


def _route_head(keys_ref, a_ref, b_ref, gate_ref, q_s, v1_s, i1_s, v2_s, i2_s, sc_s, a_s, b_s, hd, carry):
    for half, (v_s, i_s) in enumerate(((v1_s, i1_s), (v2_s, i2_s))):
        col = pl.multiple_of(hd * PEER_DK + half * (PEER_DK // 2), PEER_DK // 2)
        qh = q_s[:, pl.ds(col, PEER_DK // 2)].astype(BF16)
        st = _dot_nt(keys_ref[half, hd], qh)

        def emit(step, value, row, v_s=v_s, i_s=i_s):
            v_s[step:step + 1, :] = value
            i_s[step:step + 1, :] = row

        _top_rows([st[SUBLANES * i:SUBLANES * (i + 1), :] for i in range(PEER_NK // SUBLANES)], PEER_TOPK, emit)

    v1 = v1_s[...]
    v2 = v2_s[...]
    sub8 = lax.broadcasted_iota(jnp.int32, (8, TB), 0)
    pieces = [v1[0:1, :] + v2[0:8, :], v1[0:1, :] + v2[8:16, :]]
    for p in range(1, 8):
        pieces.append(jnp.where(sub8 < PEER_TOPK // (p + 1), v1[p:p + 1, :] + v2[0:8, :], -jnp.inf))
    pieces.append(v1[8:16, :] + v2[0:1, :])
    sub = lax.broadcasted_iota(jnp.int32, (PEER_TOPK, TB), 0)
    i1 = i1_s[...]
    i2 = i2_s[...]

    def emit2(step, value, pos):
        p = jnp.where(pos < 72, jnp.maximum((pos - 8) >> 3, 0), pos - 64)
        r = jnp.where(pos < 16, pos, jnp.where(pos < 72, pos & 7, 0))
        sc_s[step:step + 1, :] = value
        a_s[step:step + 1, :] = jnp.where(sub == p, i1, 0).sum(axis=0, keepdims=True)
        b_s[step:step + 1, :] = jnp.where(sub == r, i2, 0).sum(axis=0, keepdims=True)

    _top_rows(pieces, PEER_TOPK, emit2)
    sc = sc_s[...]
    e = jnp.exp(sc - sc[0:1, :])
    out_rows = pl.ds(pl.multiple_of(hd * PEER_TOPK, PEER_TOPK), PEER_TOPK)
    gate_ref[0, out_rows, :] = e / e.sum(axis=0, keepdims=True)
    a_ref[0, out_rows, :] = a_s[...].astype(F32)
    b_ref[0, out_rows, :] = b_s[...].astype(F32)
    return carry


def _router(mix, modt, g, wq_bf16, keys_bf16, layer):
    mix_fn, (mix_specs, mix_args) = mix
    kt = pl.BlockSpec((1, PEER_HEADS * PEER_TOPK, TB), lambda i: (i, 0, 0))
    kshape = jax.ShapeDtypeStruct((N_TILES, PEER_HEADS * PEER_TOPK, TB), F32)
    f32s = lambda: pltpu.VMEM((PEER_TOPK, TB), F32)
    i32s = lambda: pltpu.VMEM((PEER_TOPK, TB), jnp.int32)
    tile = pl.BlockSpec((TB, D_MODEL), lambda i: (i, 0))
    return pl.pallas_call(
        functools.partial(_router_kernel, n_mix=len(mix_args), mix_fn=mix_fn),
        grid=(N_TILES,),
        in_specs=list(mix_specs) + [
            pl.BlockSpec((1, 6, D_MODEL), lambda i: (i, 0, 0)),
            pl.BlockSpec((1, D_MODEL), lambda i: (0, 0)),
            pl.BlockSpec((None, D_MODEL, PEER_HEADS * PEER_DK), lambda i: (layer, 0, 0)),
            pl.BlockSpec((None, 2, PEER_HEADS, PEER_NK, PEER_DK // 2), lambda i: (layer, 0, 0, 0, 0)),
        ],
        out_specs=[tile, tile, kt, kt, kt],
        out_shape=[jax.ShapeDtypeStruct((N_TOK, D_MODEL), F32), jax.ShapeDtypeStruct((N_TOK, D_MODEL), BF16),
                   kshape, kshape, kshape],
        scratch_shapes=[pltpu.VMEM((TB, PEER_HEADS * PEER_DK), F32),
                        f32s(), i32s(), f32s(), i32s(), f32s(), i32s(), i32s()],
        compiler_params=_cparams("parallel"),
        name="peer_router",
    )(*mix_args, modt, g.reshape(1, D_MODEL), wq_bf16, keys_bf16)


def _expert_kernel(h_ref, x_ref, mod_ref, at_ref, bt_ref, gt_ref, u_ref, v_ref, fg_ref, o_ref, *rest, final):
    if final:
        y_ref, g_s, acc_s, a_s, b_s, w_s, rt_s = rest
    else:
        g_s, acc_s, a_s, b_s, w_s, rt_s = rest
    grp = pl.program_id(1)

    @pl.when(grp == 0)
    def _():
        acc_s[...] = jnp.zeros_like(acc_s)
        for part in range(TE // TB):
            a_s[part * TB:(part + 1) * TB, :] = at_ref[part].T
            b_s[part * TB:(part + 1) * TB, :] = bt_ref[part].T
            w_s[part * TB:(part + 1) * TB, :] = gt_ref[part].T
        sub = lax.broadcasted_iota(jnp.int32, (PEER_NK, PEER_NK), 0).astype(F32).astype(BF16)
        one = jnp.ones((PEER_NK, PEER_NK), BF16)
        zero = jnp.zeros((PEER_NK, PEER_NK), BF16)

        def gate_grid(t, slot):
            a_row = a_s[pl.ds(t, 1), :].astype(BF16)
            b_row = b_s[pl.ds(t, 1), :].astype(BF16)
            w_row = jnp.broadcast_to(w_s[pl.ds(t, 1), :].astype(BF16), (PEER_NK, PEER_NK))
            m_t = jnp.where(a_row == sub, one, zero)
            r_t = jnp.where(b_row == sub, w_row, zero)
            rt_s[slot] = r_t.T
            g = _dot(m_t, rt_s[slot]).astype(BF16).astype(F32)
            return lax.bitcast_convert_type(g, jnp.uint32)

        def body(t, carry):
            slot = t & (G_UNROLL - 1)
            word = (gate_grid(t, slot) >> 16) | (gate_grid(t + TE // 2, slot + G_UNROLL) & jnp.uint32(0xFFFF0000))
            g_s[pl.ds(t, PEER_NK, stride=G_PITCH), :] = word
            return carry

        lax.fori_loop(0, TE // 2, body, 0, unroll=G_UNROLL)

    h = h_ref[...]
    acc = acc_s[...]
    for sb in range(PEER_GROUP // EXPERT_SUB):
        e0 = sb * EXPERT_SUB
        hmat = _dot_nt(h, u_ref[e0:e0 + EXPERT_SUB, :])
        parts = []
        for j in range(EXPERT_SUB // PEER_NK):
            chunk = grp * CHUNKS_PER_GROUP + sb * (EXPERT_SUB // PEER_NK) + j
            word = g_s[pl.ds(pl.multiple_of(chunk * G_PITCH, 8), TE // 2), :]
            gates = jnp.concatenate(
                [lax.bitcast_convert_type(word << 16, F32),
                 lax.bitcast_convert_type(word & jnp.uint32(0xFFFF0000), F32)], axis=0)
            parts.append((_gelu_tanh(hmat[:, j * PEER_NK:(j + 1) * PEER_NK]) * gates).astype(BF16))
        acc = acc + _dot(jnp.concatenate(parts, axis=1), v_ref[e0:e0 + EXPERT_SUB, :])
    acc_s[...] = acc

    @pl.when(grp == PEER_NGROUPS - 1)
    def _():
        xo = x_ref[...] + mod_ref[0][5:6, :] * acc_s[...]
        if final:
            y = xo * lax.rsqrt(jnp.mean(xo * xo, axis=-1, keepdims=True) + EPS) * fg_ref[...]
            tile = pl.program_id(0)

            @pl.when(tile < N_PROMPT // TE)
            def _():
                o_ref[...] = y

            @pl.when(tile >= N_PROMPT // TE)
            def _():
                y_ref[...] = y
        else:
            o_ref[...] = xo


def _experts(h_bf16, x, modt_e, a_t, b_t, g_t, u_bf16, v_bf16, layer, final_g, final):
    tile = pl.BlockSpec((TE, D_MODEL), lambda i, g: (i, 0))
    kt = pl.BlockSpec((TE // TB, PEER_HEADS * PEER_TOPK, TB), lambda i, g: (i, 0, 0))
    wblk = pl.BlockSpec((None, PEER_GROUP, D_MODEL), lambda i, g: (layer, g, 0))
    xs = jax.ShapeDtypeStruct((N_TOK, D_MODEL), F32)
    n_prompt = N_PROMPT // TE
    sel = lambda: pltpu.VMEM((TE, PEER_HEADS * PEER_TOPK), F32)
    return pl.pallas_call(
        functools.partial(_expert_kernel, final=final),
        grid=(N_TOK // TE, PEER_NGROUPS),
        in_specs=[tile, tile, pl.BlockSpec((1, 6, D_MODEL), lambda i, g: (i, 0, 0)), kt, kt, kt, wblk, wblk,
                  pl.BlockSpec((1, D_MODEL), lambda i, g: (0, 0))],
        out_specs=[pl.BlockSpec((TE, D_MODEL), lambda i, g: (jnp.minimum(i, n_prompt - 1), 0)),
                   pl.BlockSpec((TE, D_MODEL), lambda i, g: (jnp.maximum(i - n_prompt, 0), 0))] if final else tile,
        out_shape=[jax.ShapeDtypeStruct((N_PROMPT, D_MODEL), F32),
                   jax.ShapeDtypeStruct((N_SAMPLE, D_MODEL), F32)] if final else xs,
        scratch_shapes=[pltpu.VMEM((PEER_NK * G_PITCH, PEER_NK), jnp.uint32), pltpu.VMEM((TE, D_MODEL), F32),
                        sel(), sel(), sel(), pltpu.VMEM((2 * G_UNROLL, PEER_NK, PEER_NK), BF16)],
        compiler_params=_cparams("arbitrary" if final else "parallel", "arbitrary"),
        name="peer_experts_final" if final else "peer_experts",
    )(h_bf16, x, modt_e, a_t, b_t, g_t, u_bf16, v_bf16, final_g.reshape(1, D_MODEL))


def _peer(mix, modt, modt_e, norm_g, peer_bf16, layer, final_g, final):
    w_q, keys, u, v = peer_bf16
    x, h, a_t, b_t, g_t = _router(mix, modt, norm_g, w_q, keys, layer)
    return _experts(h, x, modt_e, a_t, b_t, g_t, u, v, layer, final_g, final)


def kernel(x_prompt, x_sample, cache_k, cache_v, state_hgrn_fwd, state_hgrn_bwd, state_lru_fwd, state_lru_bwd, c, c_ctx, norm1_g, norm2_g, w_mod, b_mod, w_in_even, w_out_even, hgrn_lb_fwd, hgrn_lb_bwd, hgrn_gnorm_g, attn_sink, w_in_odd, w_out_odd, conv_w, conv_b, rg_w_a, rg_b_a, rg_w_x, rg_b_x, rg_lambda, peer_w_q, peer_keys, peer_u, peer_v, final_g):
    x = (x_prompt.reshape(N_PROMPT, D_MODEL), x_sample.reshape(N_SAMPLE, D_MODEL))
    cond = jnp.concatenate([c_ctx[None, :], c, jnp.zeros((COND_PAD - N_COND, D_MODEL), F32)], axis=0)
    mod = _modulation(cond, w_mod, b_mod)
    tile_row = np.asarray([0] * PROMPT_TILES + [1 + i // TILES_PER_SAMPLE for i in range(N_TILES - PROMPT_TILES)])
    modt = [mod[l][tile_row].reshape(N_TILES, 6, D_MODEL) for l in range(DEPTH)]
    modt_e = [m[::TE // TB] for m in modt]

    widths = (HG_W, HG_W, HG_W, HG_W, HG_W, ATT_W, KV_W, KV_W)
    zq, zf, zb, zi, zg, aq, ak, av = _in_proj(x, modt[0], norm1_g[0], w_in_even[0].astype(BF16), widths, "even_in")
    zero_state = jnp.zeros((1, HG_HEADS, HG_DV, HG_DK), F32)
    s0f = jnp.concatenate([zero_state, jnp.swapaxes(state_hgrn_fwd[:, 0], -1, -2)], axis=0)
    s0b = jnp.concatenate([zero_state, jnp.swapaxes(state_hgrn_bwd[:, 0], -1, -2)], axis=0)
    o_f, o_b, sf_t, sb_t = _hgrn_scan(zq, zf, zb, zi, hgrn_lb_fwd, hgrn_lb_bwd, s0f, s0b)
    oatt = _attention(aq, ak, av, cache_k[:, 0].reshape(DEC_BATCH, PAST_LEN, KV_W),
                      cache_v[:, 0].reshape(DEC_BATCH, PAST_LEN, KV_W), attn_sink[0])
    even_mix = (_even_mix, _even_mix_operands(x, o_f, o_b, zg, oatt, hgrn_gnorm_g[0], w_out_even[0].astype(BF16)))
    peer_bf16 = (peer_w_q.astype(BF16), peer_keys.astype(BF16), peer_u.astype(BF16), peer_v.astype(BF16))
    x = _peer(even_mix, modt[0], modt_e[0], norm2_g[0], peer_bf16, 0, final_g, False)

    gate, xc = _odd_in(x, modt[1], norm1_g[1], w_in_odd[0].astype(BF16), conv_w[0], conv_b[0])
    zero_h = jnp.zeros((1, 1, D_RNN), F32)
    h0f = jnp.concatenate([zero_h, state_lru_fwd[:, 0][:, None, :]], axis=0)
    h0b = jnp.concatenate([zero_h, state_lru_bwd[:, 0][:, None, :]], axis=0)
    hf, hb = _lru_scan(xc, rg_w_a[0].astype(BF16), rg_b_a[0], rg_w_x[0].astype(BF16), rg_b_x[0],
                       rg_lambda[0], h0f, h0b)
    odd_mix = (_odd_mix, _odd_mix_operands(x, hf, hb, gate, w_out_odd[0].astype(BF16)))
    y_p, y_s = _peer(odd_mix, modt[1], modt_e[1], norm2_g[1], peer_bf16, 1, final_g, True)

    y_prompt = y_p.reshape(BATCH, SEQ, D_MODEL)
    y_sample = y_s.reshape(DEC_BATCH, DEC_SEQ, D_MODEL)
    new_k = ak[:N_PROMPT].reshape(BATCH, 1, SEQ, KV_HEADS, HEAD_DIM)
    new_v = av[:N_PROMPT].reshape(BATCH, 1, SEQ, KV_HEADS, HEAD_DIM)
    new_hf = jnp.swapaxes(sf_t[:BATCH], -1, -2)[:, None]
    new_hb = jnp.swapaxes(sb_t[:BATCH], -1, -2)[:, None]
    new_lf = hf[:N_PROMPT].reshape(BATCH, SEQ, D_RNN)[:, -1][:, None, :]
    new_lb = hb[:N_PROMPT].reshape(BATCH, SEQ, D_RNN)[:, 0][:, None, :]
    return (y_prompt, y_sample, new_k, new_v, new_hf, new_hb, new_lf, new_lb)
```

```python
import functools

import numpy as np
import jax
import jax.numpy as jnp
from jax import lax
from jax.experimental import pallas as pl
from jax.experimental.pallas import tpu as pltpu

F32 = jnp.float32
BF16 = jnp.bfloat16

D_MODEL = 1024
BATCH = 16
SEQ = 256
DEPTH = 2
DEC_BATCH = 8
DEC_SEQ = 2048
PAST_LEN = 512
GRID_W = 64
EPS = 1e-6
HG_DK = 128
HG_DV = 128
HG_HEADS = 4
HG_CHUNK = 64
HG_BLOCK = 256
HG_W = HG_HEADS * HG_DK
HEAD_DIM = 64
N_HEADS = 8
KV_HEADS = 2
Q_PER_KV = 4
WINDOW = 128
ATT_BLOCK = 128
ROPE_BASE = 10000.0
ATT_W = N_HEADS * HEAD_DIM
KV_W = KV_HEADS * HEAD_DIM
D_RNN = 1024
RG_BLOCKS = 4
RG_BW = 256
RG_C = 8.0
PEER_HEADS = 8
PEER_NK = 128
PEER_DK = 256
PEER_TOPK = 16

N_PROMPT = BATCH * SEQ
N_SAMPLE = DEC_BATCH * DEC_SEQ
N_TOK = N_PROMPT + N_SAMPLE
TB = 256
N_TILES = N_TOK // TB
PROMPT_TILES = N_PROMPT // TB
TILES_PER_SAMPLE = DEC_SEQ // TB
N_COND = 1 + DEC_BATCH
COND_PAD = 16

TE = 512
PEER_GROUP = 2048
PEER_NGROUPS = PEER_NK * PEER_NK // PEER_GROUP
CHUNKS_PER_GROUP = PEER_GROUP // PEER_NK
EXPERT_SUB = 1024
G_PITCH = TE // 2 + 8
G_UNROLL = 16

LRU_CHUNK = 256
CONV_SLAB = 512
SUBLANES = 8

VMEM_LIMIT = 58 * 1024 * 1024


def _cparams(*sem):
    return pltpu.CompilerParams(dimension_semantics=sem, vmem_limit_bytes=VMEM_LIMIT)


def _dot(a, b):
    return jnp.dot(a, b, preferred_element_type=F32)


def _dot_nt(a, b):
    return lax.dot_general(a, b, (((1,), (1,)), ((), ())), preferred_element_type=F32)


def _dot_tn(a, b):
    return lax.dot_general(a, b, (((0,), (0,)), ((), ())), preferred_element_type=F32)


def _sigmoid(x):
    return 1.0 / (1.0 + jnp.exp(-x))


def _sigmoid_tanh(x):
    return 0.5 + 0.5 * jnp.tanh(0.5 * x)


def _silu(x):
    return x * _sigmoid(x)


def _gelu_tanh(x):
    c = np.float32(np.sqrt(2.0 / np.pi))
    return x * (0.5 * (1.0 + jnp.tanh(c * (x + 0.044715 * (x * x * x)))))


def _rms_mod(x, g, mod, sh, sc):
    y = x * lax.rsqrt(jnp.mean(x * x, axis=-1, keepdims=True) + EPS) * g
    return y * (1.0 + mod[sc:sc + 1, :]) + mod[sh:sh + 1, :]


def _mod_kernel(c_ref, w_ref, b_ref, o_ref):
    c = c_ref[...]
    s = _silu(c).astype(BF16)
    o_ref[0] = _dot(s, w_ref[0].astype(BF16)) + b_ref[0]


def _modulation(cond, w_mod, b_mod):
    nb = 6 * D_MODEL // 1024
    return pl.pallas_call(
        _mod_kernel,
        grid=(DEPTH, nb),
        in_specs=[
            pl.BlockSpec((COND_PAD, D_MODEL), lambda l, n: (0, 0)),
            pl.BlockSpec((1, D_MODEL, 1024), lambda l, n: (l, 0, n)),
            pl.BlockSpec((1, 1, 1024), lambda l, n: (l, 0, n)),
        ],
        out_specs=pl.BlockSpec((1, COND_PAD, 1024), lambda l, n: (l, 0, n)),
        out_shape=jax.ShapeDtypeStruct((DEPTH, COND_PAD, 6 * D_MODEL), F32),
        compiler_params=_cparams("parallel", "parallel"),
        name="modulation",
    )(cond, w_mod, b_mod.reshape(DEPTH, 1, 6 * D_MODEL))


def _token_specs(arrays, width):
    if not isinstance(arrays, tuple):
        return [pl.BlockSpec((TB, width), lambda i: (i, 0))]
    return [pl.BlockSpec((TB, width), lambda i: (jnp.minimum(i, PROMPT_TILES - 1), 0)),
            pl.BlockSpec((TB, width), lambda i: (jnp.maximum(i - PROMPT_TILES, 0), 0))]


def _token_tile(refs):
    if len(refs) == 1:
        return refs[0][...]
    return jnp.where(pl.program_id(0) < PROMPT_TILES, refs[0][...], refs[1][...])


def _in_proj_kernel(*refs, widths, n_x):
    x_refs, (mod_ref, g_ref, w_ref), o_refs = refs[:n_x], refs[n_x:n_x + 3], refs[n_x + 3:]
    h = _rms_mod(_token_tile(x_refs), g_ref[...], mod_ref[0], 0, 1)
    z = _dot(h.astype(BF16), w_ref[...])
    off = 0
    for o_ref, wd in zip(o_refs, widths):
        o_ref[...] = z[:, off:off + wd]
        off += wd


def _in_proj(x, modt, g, w_bf16, widths, name):
    n_out = sum(widths)
    xs = x if isinstance(x, tuple) else (x,)
    return pl.pallas_call(
        functools.partial(_in_proj_kernel, widths=widths, n_x=len(xs)),
        grid=(N_TILES,),
        in_specs=_token_specs(x, D_MODEL) + [
            pl.BlockSpec((1, 6, D_MODEL), lambda i: (i, 0, 0)),
            pl.BlockSpec((1, D_MODEL), lambda i: (0, 0)),
            pl.BlockSpec((D_MODEL, n_out), lambda i: (0, 0)),
        ],
        out_specs=[pl.BlockSpec((TB, wd), lambda i: (i, 0)) for wd in widths],
        out_shape=[jax.ShapeDtypeStruct((N_TOK, wd), F32) for wd in widths],
        compiler_params=_cparams("parallel"),
        name=name,
    )(*xs, modt, g.reshape(1, D_MODEL), w_bf16)


def _hgrn_chunk_terms(zq, zgate, zi, lb, fwd):
    n = HG_CHUNK
    row = lax.broadcasted_iota(jnp.int32, (n, n), 0)
    col = lax.broadcasted_iota(jnp.int32, (n, n), 1)
    keep = (col <= row) if fwd else (col >= row)
    tri = jnp.where(keep, 1.0, 0.0).astype(BF16)
    q = zq * _sigmoid_tanh(zq)
    sg = _sigmoid_tanh(zgate)
    f = lb + (1.0 - lb) * sg
    g = jnp.log(f)
    k = (1.0 - lb) * (1.0 - sg)
    g_hi = g.astype(BF16)
    g_lo = (g - g_hi.astype(F32)).astype(BF16)
    b = _dot(tri, g_hi) + _dot(tri, g_lo)
    yield
    mid = n // 2 - 1 if fwd else n // 2
    last = n - 1 if fwd else 0
    bm = b[mid:mid + 1, :]
    bl = b[last:last + 1, :]
    qi32 = q * jnp.exp(b - bm)
    ki32 = k * jnp.exp(bm - b)
    qi = qi32.astype(BF16)
    ki = ki32.astype(BF16)
    qs = (qi32 * jnp.exp(bm)).astype(BF16)
    ks = (ki32 * jnp.exp(bl - bm)).astype(BF16)
    vb = zi.astype(BF16)
    decay = jnp.exp(bl)
    yield
    heads = [slice(h * HG_DK, (h + 1) * HG_DK) for h in range(HG_HEADS)]
    a = [jnp.where(keep, _dot_nt(qi[:, sl], ki[:, sl]), 0.0).astype(BF16) for sl in heads]
    kv = [_dot_tn(vb[:, sl], ks[:, sl]) for sl in heads]
    yield
    intra = [_dot(a[h], vb[:, sl]) for h, sl in enumerate(heads)]
    yield intra, qs, kv, decay


def _lower_bound(p):
    e = jnp.exp(p - p.max(axis=0, keepdims=True))
    return e[0:1, :] / e.sum(axis=0, keepdims=True)


def _hgrn_chunk_order(fwd):
    n_sub = HG_BLOCK // HG_CHUNK
    return [slice(c * HG_CHUNK, (c + 1) * HG_CHUNK) for c in (range(n_sub) if fwd else reversed(range(n_sub)))]


def _hgrn_block_terms(zq_ref, zgate_ref, zi_ref, lb, fwd):
    return [_hgrn_chunk_terms(zq_ref[rows, :], zgate_ref[rows, :], zi_ref[rows, :], lb, fwd)
            for rows in _hgrn_chunk_order(fwd)]


def _hgrn_block_chain(terms, st_ref, o_ref, fwd):
    states = [st_ref[h] for h in range(HG_HEADS)]
    for rows, (intra, qs, kv, decay) in zip(_hgrn_chunk_order(fwd), terms):
        outs = []
        for h in range(HG_HEADS):
            sl = slice(h * HG_DK, (h + 1) * HG_DK)
            outs.append(intra[h] + _dot_nt(qs[:, sl], states[h].astype(BF16)))
            states[h] = decay[:, sl] * states[h] + kv[h]
        o_ref[rows, :] = jnp.concatenate(outs, axis=1)
    for h in range(HG_HEADS):
        st_ref[h] = states[h]


def _hgrn_kernel(tbl_ref, zqf_ref, zff_ref, zif_ref, zqb_ref, zbb_ref, zib_ref, lbf_ref, lbb_ref,
                 s0f_ref, s0b_ref, of_ref, ob_ref, sf_ref, sb_ref, stf, stb):
    s = pl.program_id(0)

    @pl.when(tbl_ref[4, s] == 1)
    def _():
        stf[...] = s0f_ref[0]
        stb[...] = s0b_ref[0]

    gens_f = _hgrn_block_terms(zqf_ref, zff_ref, zif_ref, _lower_bound(lbf_ref[...]), True)
    gens_b = _hgrn_block_terms(zqb_ref, zbb_ref, zib_ref, _lower_bound(lbb_ref[...]), False)
    for _ in range(3):
        for gen in gens_f + gens_b:
            next(gen)
    _hgrn_block_chain([next(gen) for gen in gens_f], stf, of_ref, True)
    _hgrn_block_chain([next(gen) for gen in gens_b], stb, ob_ref, False)

    @pl.when(tbl_ref[5, s] == 1)
    def _():
        sf_ref[0] = stf[...]
        sb_ref[0] = stb[...]


def _scan_table(chunk):
    rows = []
    blk = 0
    for (nseq, length, has_init) in ((BATCH, SEQ, False), (DEC_BATCH, DEC_SEQ, True)):
        nc = length // chunk
        for b in range(nseq):
            sid = b if not has_init else BATCH + b
            for i in range(nc):
                rows.append((blk + i, blk + nc - 1 - i, (1 + b) if has_init else 0, sid,
                             int(i == 0), int(i == nc - 1)))
            blk += nc
    return np.asarray(rows, np.int32).T.copy()


def _hgrn_scan(zq, zf, zb, zi, lbf, lbb, s0f_t, s0b_t):
    tbl = _scan_table(HG_BLOCK)
    n_steps = tbl.shape[1]
    n_seq = BATCH + DEC_BATCH
    blk = lambda r: pl.BlockSpec((HG_BLOCK, HG_W), lambda s, t: (t[r, s], 0))
    vec = pl.BlockSpec(lbf.shape, lambda s, t: (0, 0))
    st_in = pl.BlockSpec((1, HG_HEADS, HG_DV, HG_DK), lambda s, t: (t[2, s], 0, 0, 0))
    st_out = pl.BlockSpec((1, HG_HEADS, HG_DV, HG_DK), lambda s, t: (t[3, s], 0, 0, 0))
    grid_spec = pltpu.PrefetchScalarGridSpec(
        num_scalar_prefetch=1,
        grid=(n_steps,),
        in_specs=[blk(0), blk(0), blk(0), blk(1), blk(1), blk(1), vec, vec, st_in, st_in],
        out_specs=[blk(0), blk(1), st_out, st_out],
        scratch_shapes=[pltpu.VMEM((HG_HEADS, HG_DV, HG_DK), F32),
                        pltpu.VMEM((HG_HEADS, HG_DV, HG_DK), F32)],
    )
    st_shape = jax.ShapeDtypeStruct((n_seq, HG_HEADS, HG_DV, HG_DK), F32)
    return pl.pallas_call(
        _hgrn_kernel,
        grid_spec=grid_spec,
        out_shape=[jax.ShapeDtypeStruct((N_TOK, HG_W), F32), jax.ShapeDtypeStruct((N_TOK, HG_W), F32),
                   st_shape, st_shape],
        compiler_params=_cparams("arbitrary"),
        name="hgrn_scan",
    )(jnp.asarray(tbl), zq, zf, zi, zq, zb, zi, lbf, lbb, s0f_t, s0b_t)


def _rope(x, cos, sgn_sin):
    w = x.shape[1]
    lane = lax.broadcasted_iota(jnp.int32, x.shape, 1)
    partner = jnp.where((lane & 31) < 16, pltpu.roll(x, w - 16, 1), pltpu.roll(x, 16, 1))
    return x * cos + partner * sgn_sin


def _attend(q, key_parts, sink_ref, o_ref):
    stages = [_attend_kv(q, key_parts, sink_ref, o_ref, kv) for kv in range(KV_HEADS)]
    for _ in range(4):
        for stage in stages:
            next(stage)


def _attend_kv(q, key_parts, sink_ref, o_ref, kv):
    lane = lax.broadcasted_iota(jnp.int32, (ATT_BLOCK, 128), 1)
    lo = lane < HEAD_DIM
    own = lo if kv == 0 else ~lo
    qs = []
    for gq in range(Q_PER_KV):
        hd = kv * Q_PER_KV + gq
        pair = q[:, (hd // 2) * 128:(hd // 2 + 1) * 128]
        if hd % 2 != kv:
            pair = pltpu.roll(pair, HEAD_DIM, 1)
        qs.append(jnp.where(own, pair, 0.0))
    qst = jnp.concatenate(qs, axis=0).astype(BF16)
    logits = []
    vals = []
    for (k, v, mask) in key_parts:
        own_rows = lax.broadcasted_iota(jnp.int32, v.shape, 1) < HEAD_DIM
        own_rows = own_rows if kv == 0 else ~own_rows
        vals.append(jnp.where(own_rows, v, 1.0).astype(BF16))
        s = _dot_nt(qst, k.astype(BF16))
        if mask is not None:
            s = jnp.where(mask, s, -1e30)
        logits.append(s)
    yield
    tiles = [s[:, t * 128:(t + 1) * 128] for s in logits for t in range(s.shape[1] // 128)]
    m = functools.reduce(jnp.maximum, tiles).max(axis=-1, keepdims=True)
    sink_col = jnp.concatenate(
        [jnp.full((ATT_BLOCK, 1), sink_ref[kv * Q_PER_KV + gq], F32) for gq in range(Q_PER_KV)], axis=0)
    m = jnp.maximum(m, sink_col)
    yield
    acc = jnp.zeros((Q_PER_KV * ATT_BLOCK, 128), F32)
    for s, v in zip(logits, vals):
        acc = acc + _dot(jnp.exp(s - m).astype(BF16), v)
    yield
    den = pltpu.roll(acc, HEAD_DIM, 1) + jnp.exp(sink_col - m)
    out = acc / den
    for pr in range(Q_PER_KV // 2):
        r0 = out[(2 * pr) * ATT_BLOCK:(2 * pr + 1) * ATT_BLOCK]
        r1 = out[(2 * pr + 1) * ATT_BLOCK:(2 * pr + 2) * ATT_BLOCK]
        if kv == 0:
            r1 = pltpu.roll(r1, HEAD_DIM, 1)
        else:
            r0 = pltpu.roll(r0, HEAD_DIM, 1)
        c0 = (kv * Q_PER_KV // 2 + pr) * 128
        o_ref[:, c0:c0 + 128] = jnp.where(lo, r0, r1)
    yield


def _attn_prompt_kernel(sink_ref, q_ref, k_ref, v_ref, o_ref):
    q = q_ref[...] * np.float32(HEAD_DIM ** -0.5)
    _attend(q, [(k_ref[...], v_ref[...], None)], sink_ref, o_ref)


def _attn_latent_kernel(sink_ref, q_ref, kp_ref, kc_ref, kn_ref, vp_ref, vc_ref, vn_ref,
                        ck_ref, cv_ref, cos_ref, sin_ref, o_ref):
    j = pl.program_id(1)
    nq = pl.num_programs(1)
    qstart = pl.multiple_of((j + 1) * ATT_BLOCK, ATT_BLOCK)
    kstart = pl.multiple_of(j * ATT_BLOCK, ATT_BLOCK)
    cq = cos_ref[pl.ds(qstart, ATT_BLOCK), :]
    sq = sin_ref[pl.ds(qstart, ATT_BLOCK), :]
    q = _rope(q_ref[...], jnp.concatenate([cq] * 4, axis=1), jnp.concatenate([sq] * 4, axis=1))
    q = q * np.float32(HEAD_DIM ** -0.5)
    k_loc = jnp.concatenate([kp_ref[...], kc_ref[...], kn_ref[...]], axis=0)
    v_loc = jnp.concatenate([vp_ref[...], vc_ref[...], vn_ref[...]], axis=0)
    k_loc = _rope(k_loc, cos_ref[pl.ds(kstart, 3 * ATT_BLOCK), :], sin_ref[pl.ds(kstart, 3 * ATT_BLOCK), :])
    r = lax.broadcasted_iota(jnp.int32, (Q_PER_KV * ATT_BLOCK, 3 * ATT_BLOCK), 0) & (ATT_BLOCK - 1)
    c = lax.broadcasted_iota(jnp.int32, (Q_PER_KV * ATT_BLOCK, 3 * ATT_BLOCK), 1)
    rel = c - ATT_BLOCK - r
    c_lo = jnp.where(j > 0, 0, ATT_BLOCK)
    c_hi = jnp.where(j < nq - 1, 3 * ATT_BLOCK, 2 * ATT_BLOCK)
    ok = (jnp.abs(rel) <= WINDOW) & (c >= c_lo) & (c < c_hi)
    _attend(q, [(ck_ref[0], cv_ref[0], None), (k_loc, v_loc, ok)], sink_ref, o_ref)


def _rope_tables():
    pos = np.arange(-ATT_BLOCK, DEC_SEQ + ATT_BLOCK)
    nf = HEAD_DIM // 4
    inv = (1.0 / (ROPE_BASE ** (np.arange(nf, dtype=np.float32) / nf))).astype(np.float32)
    rows = (pos // GRID_W).astype(np.float32)
    cols = (pos % GRID_W).astype(np.float32)
    ar = rows[:, None] * inv[None, :]
    ac = cols[:, None] * inv[None, :]
    cos64 = np.concatenate([np.cos(ar), np.cos(ar), np.cos(ac), np.cos(ac)], axis=1)
    sin64 = np.concatenate([-np.sin(ar), np.sin(ar), -np.sin(ac), np.sin(ac)], axis=1)
    cos = np.concatenate([cos64, cos64], axis=1).astype(np.float32)
    sin = np.concatenate([sin64, sin64], axis=1).astype(np.float32)
    return jnp.asarray(cos), jnp.asarray(sin)


def _attention(aq, ak, av, ck, cv, sink):
    smem = pl.BlockSpec(memory_space=pltpu.SMEM)
    nqp = SEQ // ATT_BLOCK
    o_prompt = pl.pallas_call(
        _attn_prompt_kernel,
        grid=(BATCH, nqp),
        in_specs=[
            smem,
            pl.BlockSpec((ATT_BLOCK, ATT_W), lambda b, j: (b * nqp + j, 0)),
            pl.BlockSpec((SEQ, KV_W), lambda b, j: (b, 0)),
            pl.BlockSpec((SEQ, KV_W), lambda b, j: (b, 0)),
        ],
        out_specs=pl.BlockSpec((ATT_BLOCK, ATT_W), lambda b, j: (b * nqp + j, 0)),
        out_shape=jax.ShapeDtypeStruct((N_PROMPT, ATT_W), F32),
        compiler_params=_cparams("parallel", "parallel"),
        name="attn_prompt",
    )(sink, aq, ak, av)

    nq = DEC_SEQ // ATT_BLOCK
    base = N_PROMPT // ATT_BLOCK
    cos, sin = _rope_tables()
    kv_blk = lambda d: pl.BlockSpec(
        (ATT_BLOCK, KV_W), lambda b, j: (base + b * nq + jnp.clip(j + d, 0, nq - 1), 0))
    tab = pl.BlockSpec((DEC_SEQ + 2 * ATT_BLOCK, 128), lambda b, j: (0, 0))
    o_latent = pl.pallas_call(
        _attn_latent_kernel,
        grid=(DEC_BATCH, nq),
        in_specs=[
            smem,
            pl.BlockSpec((ATT_BLOCK, ATT_W), lambda b, j: (base + b * nq + j, 0)),
            kv_blk(-1), kv_blk(0), kv_blk(1), kv_blk(-1), kv_blk(0), kv_blk(1),
            pl.BlockSpec((1, PAST_LEN, KV_W), lambda b, j: (b, 0, 0)),
            pl.BlockSpec((1, PAST_LEN, KV_W), lambda b, j: (b, 0, 0)),
            tab, tab,
        ],
        out_specs=pl.BlockSpec((ATT_BLOCK, ATT_W), lambda b, j: (b * nq + j, 0)),
        out_shape=jax.ShapeDtypeStruct((N_SAMPLE, ATT_W), F32),
        compiler_params=_cparams("parallel", "parallel"),
        name="attn_latent",
    )(sink, aq, ak, ak, ak, av, av, av, ck, cv, cos, sin)
    return o_prompt, o_latent


def _even_mix(refs, mod):
    xp_ref, xs_ref, of_ref, ob_ref, zg_ref, oattp_ref, oatts_ref, gn_ref, w_ref = refs
    x = _token_tile((xp_ref, xs_ref))
    oatt = _token_tile((oattp_ref, oatts_ref))
    o = of_ref[...] + ob_ref[...]
    parts = []
    for h in range(HG_HEADS):
        oh = o[:, h * HG_DV:(h + 1) * HG_DV]
        parts.append(oh * lax.rsqrt(jnp.mean(oh * oh, axis=-1, keepdims=True) + EPS))
    o_hg = jnp.concatenate(parts, axis=1) * gn_ref[...] * _silu(zg_ref[...])
    y = _dot(o_hg.astype(BF16), w_ref[0:HG_W, :]) + _dot(oatt.astype(BF16), w_ref[HG_W:, :])
    return x + mod[2:3, :] * y


def _even_mix_operands(x_pair, o_f, o_b, zg, oatt_pair, gn_g, w_bf16):
    tile = lambda w: pl.BlockSpec((TB, w), lambda i: (i, 0))
    specs = (_token_specs(x_pair, D_MODEL) + [tile(HG_W), tile(HG_W), tile(HG_W)] + _token_specs(oatt_pair, ATT_W)
             + [pl.BlockSpec((1, HG_W), lambda i: (0, 0)), pl.BlockSpec((HG_W + ATT_W, D_MODEL), lambda i: (0, 0))])
    return specs, (*x_pair, o_f, o_b, zg, *oatt_pair, gn_g.reshape(1, HG_W), w_bf16)


def _odd_in_kernel(x_ref, xp_ref, xn_ref, mod_ref, g_ref, w_ref, cw_ref, cb_ref, gate_ref, xc_ref):
    i = pl.program_id(0)
    in_sample = i >= PROMPT_TILES
    pos = (i - PROMPT_TILES) % TILES_PER_SAMPLE
    has_prev = in_sample & (pos != 0)
    has_next = in_sample & (pos != TILES_PER_SAMPLE - 1)
    mod = mod_ref[0]
    g = g_ref[...]
    h = jnp.concatenate([_rms_mod(r[...], g, mod, 0, 1) for r in (xp_ref, x_ref, xn_ref)], axis=0).astype(BF16)
    n = TB + 16
    row = lax.broadcasted_iota(jnp.int32, (n, CONV_SLAB), 0)
    live = (row >= jnp.where(has_prev, 0, 8)) & (row < jnp.where(has_next, n, 8 + TB))
    gate_ref[...] = _dot(h[8:8 + TB], w_ref[:, :D_RNN])
    for c0 in range(D_RNN, 2 * D_RNN, CONV_SLAB):
        ext = jnp.where(live, _dot(h, w_ref[:, c0:c0 + CONV_SLAB]), 0.0)
        cols = slice(c0 - D_RNN, c0 - D_RNN + CONV_SLAB)
        w = cw_ref[:, cols]
        acc = ext[8:8 + TB] * w[2:3, :] + cb_ref[:, cols]
        acc = acc + pltpu.roll(ext, 2, 0)[8:8 + TB] * w[0:1, :]
        acc = acc + pltpu.roll(ext, 1, 0)[8:8 + TB] * w[1:2, :]
        acc = acc + pltpu.roll(ext, n - 1, 0)[8:8 + TB] * w[3:4, :]
        xc_ref[:, cols] = acc


def _odd_in(x, modt, g, w_bf16, cw, cb):
    r8 = TB // 8
    last8 = N_TOK // 8 - 1
    tile = pl.BlockSpec((TB, D_MODEL), lambda i: (i, 0))
    return pl.pallas_call(
        _odd_in_kernel,
        grid=(N_TILES,),
        in_specs=[
            tile,
            pl.BlockSpec((8, D_MODEL), lambda i: (jnp.maximum(i * r8 - 1, 0), 0)),
            pl.BlockSpec((8, D_MODEL), lambda i: (jnp.minimum((i + 1) * r8, last8), 0)),
            pl.BlockSpec((1, 6, D_MODEL), lambda i: (i, 0, 0)),
            pl.BlockSpec((1, D_MODEL), lambda i: (0, 0)),
            pl.BlockSpec((D_MODEL, 2 * D_RNN), lambda i: (0, 0)),
            pl.BlockSpec((4, D_RNN), lambda i: (0, 0)),
            pl.BlockSpec((1, D_RNN), lambda i: (0, 0)),
        ],
        out_specs=[tile, tile],
        out_shape=[jax.ShapeDtypeStruct((N_TOK, D_RNN), F32), jax.ShapeDtypeStruct((N_TOK, D_RNN), F32)],
        compiler_params=_cparams("parallel"),
        name="odd_in",
    )(x, x, x, modt, g.reshape(1, D_MODEL), w_bf16, cw, cb.reshape(1, D_RNN))


def _log1p(y):
    w = 1.0 + y
    return jnp.where(w == 1.0, y, jnp.log(w) * (y / (w - 1.0)))


def _expm1_given(x, e):
    return jnp.where(e == 1.0, x, (e - 1.0) * (x / jnp.log(e)))


def _lru_direction(xc, wa_ref, ba, wx_ref, bx, lam, h_ref, o_ref, fwd):
    n = LRU_CHUNK
    xb = xc.astype(BF16)
    ra, ri = [], []
    for blk in range(RG_BLOCKS):
        sl = slice(blk * RG_BW, (blk + 1) * RG_BW)
        ra.append(_dot(xb[:, sl], wa_ref[blk]))
        ri.append(_dot(xb[:, sl], wx_ref[blk]))
    r = _sigmoid_tanh(jnp.concatenate(ra, axis=1) + ba)
    gi = _sigmoid_tanh(jnp.concatenate(ri, axis=1) + bx)
    neg = -lam
    softplus = jnp.maximum(neg, 0.0) + _log1p(jnp.exp(-jnp.abs(neg)))
    log_a = (-RG_C * softplus) * r
    a = jnp.exp(log_a)
    u = jnp.sqrt(-_expm1_given(2.0 * log_a, a * a)) * (gi * xc)
    sub = lax.broadcasted_iota(jnp.int32, (SUBLANES, D_RNN), 0)
    carry = h_ref[...]
    groups = range(n // SUBLANES)
    for g in (groups if fwd else reversed(groups)):
        rows = slice(g * SUBLANES, (g + 1) * SUBLANES)
        ag, ug = a[rows, :], u[rows, :]
        for d in (1, 2, 4):
            live = (sub >= d) if fwd else (sub < SUBLANES - d)
            shift = d if fwd else SUBLANES - d
            a_s = jnp.where(live, pltpu.roll(ag, shift, 0), 1.0)
            u_s = jnp.where(live, pltpu.roll(ug, shift, 0), 0.0)
            ug = ag * u_s + ug
            ag = ag * a_s
        h = ug + ag * carry
        o_ref[rows, :] = h
        carry = h[SUBLANES - 1:SUBLANES, :] if fwd else h[0:1, :]
    h_ref[...] = carry


def _lru_kernel(tbl_ref, xf_ref, xb_ref, wa_ref, ba_ref, wx_ref, bx_ref, lam_ref, h0f_ref, h0b_ref,
                hf_ref, hb_ref, hf_s, hb_s):
    s = pl.program_id(0)

    @pl.when(tbl_ref[4, s] == 1)
    def _():
        hf_s[...] = h0f_ref[0]
        hb_s[...] = h0b_ref[0]

    _lru_direction(xf_ref[...], wa_ref.at[0], ba_ref[0:1, :], wx_ref.at[0], bx_ref[0:1, :],
                   lam_ref[0:1, :], hf_s, hf_ref, True)
    _lru_direction(xb_ref[...], wa_ref.at[1], ba_ref[1:2, :], wx_ref.at[1], bx_ref[1:2, :],
                   lam_ref[1:2, :], hb_s, hb_ref, False)


def _lru_scan(xc, w_a_bf16, b_a, w_x_bf16, b_x, lam, h0f_all, h0b_all):
    tbl = _scan_table(LRU_CHUNK)
    n_steps = tbl.shape[1]
    blk = lambda r: pl.BlockSpec((LRU_CHUNK, D_RNN), lambda s, t: (t[r, s], 0))
    wspec = pl.BlockSpec((2, RG_BLOCKS, RG_BW, RG_BW), lambda s, t: (0, 0, 0, 0))
    vec2 = pl.BlockSpec((2, D_RNN), lambda s, t: (0, 0))
    h0 = pl.BlockSpec((1, 1, D_RNN), lambda s, t: (t[2, s], 0, 0))
    grid_spec = pltpu.PrefetchScalarGridSpec(
        num_scalar_prefetch=1,
        grid=(n_steps,),
        in_specs=[blk(0), blk(1), wspec, vec2, wspec, vec2, vec2, h0, h0],
        out_specs=[blk(0), blk(1)],
        scratch_shapes=[pltpu.VMEM((1, D_RNN), F32), pltpu.VMEM((1, D_RNN), F32)],
    )
    return pl.pallas_call(
        _lru_kernel,
        grid_spec=grid_spec,
        out_shape=[jax.ShapeDtypeStruct((N_TOK, D_RNN), F32), jax.ShapeDtypeStruct((N_TOK, D_RNN), F32)],
        compiler_params=_cparams("arbitrary"),
        name="lru_scan",
    )(jnp.asarray(tbl), xc, xc, w_a_bf16, b_a, w_x_bf16, b_x, lam, h0f_all, h0b_all)


def _odd_mix(refs, mod):
    x_ref, hf_ref, hb_ref, gate_ref, w_ref = refs
    y = (hf_ref[...] + hb_ref[...]) * _gelu_tanh(gate_ref[...])
    return x_ref[...] + mod[2:3, :] * _dot(y.astype(BF16), w_ref[...])


def _odd_mix_operands(x, hf, hb, gate, w_bf16):
    tile = pl.BlockSpec((TB, D_MODEL), lambda i: (i, 0))
    return [tile, tile, tile, tile, pl.BlockSpec((D_RNN, D_MODEL), lambda i: (0, 0))], (x, hf, hb, gate, w_bf16)


def _order_key(x):
    b = lax.bitcast_convert_type(x, jnp.int32)
    return b ^ ((b >> 31) & 0x7FFFFFFF)


def _key_value(k):
    return lax.bitcast_convert_type(k ^ ((k >> 31) & 0x7FFFFFFF), F32)


def _precedes(ky, ry, kx, rx):
    return (ky + jnp.where(ry < rx, 1, 0)) > kx


def _merge_exchange(n):
    pairs = []
    t = max(1, (n - 1).bit_length())
    p = 1 << (t - 1)
    while p > 0:
        q, r, d = 1 << (t - 1), 0, p
        while d > 0:
            pairs.extend((i, i + d) for i in range(n - d) if (i & p) == r)
            d, q, r = q - p, q >> 1, p
        p >>= 1
    return pairs


def _top_rows(values, count, emit):
    n = len(values)
    sub = lax.broadcasted_iota(jnp.int32, (SUBLANES, TB), 0)
    keys = [_order_key(v) for v in values]
    rows = [sub + SUBLANES * i for i in range(n)]
    for i, j in _merge_exchange(n):
        swap = _precedes(keys[j], rows[j], keys[i], rows[i])
        keys[i], keys[j] = jnp.where(swap, keys[j], keys[i]), jnp.where(swap, keys[i], keys[j])
        rows[i], rows[j] = jnp.where(swap, rows[j], rows[i]), jnp.where(swap, rows[i], rows[j])
    lowest = jnp.full((SUBLANES, TB), jnp.iinfo(jnp.int32).min + 1, jnp.int32)
    yield
    for step in range(count):
        k = keys[0]
        for shift in (4, 2, 1):
            k = jnp.maximum(k, pltpu.roll(k, shift, 0))
        r = jnp.where(keys[0] == k, rows[0], jnp.iinfo(jnp.int32).max)
        for shift in (4, 2, 1):
            r = jnp.minimum(r, pltpu.roll(r, shift, 0))
        emit(step, _key_value(k[0:1, :]), r[0:1, :])
        depth = min(n - 1, count - 1 - step)
        column = rows[0] == r
        for d in range(depth):
            keys[d] = jnp.where(column, keys[d + 1], keys[d])
            rows[d] = jnp.where(column, rows[d + 1], rows[d])
        if depth == n - 1:
            keys[n - 1] = jnp.where(column, lowest, keys[n - 1])
        yield


def _lockstep(generators):
    live = list(generators)
    while live:
        live = [g for g in live if next(g, StopIteration) is not StopIteration]


def _router_kernel(*refs, n_mix, mix_fn):
    mix_refs = refs[:n_mix]
    (mod_ref, g_ref, wq_ref, keys_ref, x_ref, h_ref, a_ref, b_ref, gate_ref,
     q_s, v1_s, i1_s, v2_s, i2_s, sc_s, a_s, b_s) = refs[n_mix:]
    x = mix_fn(mix_refs, mod_ref[0])
    x_ref[...] = x
    h = _rms_mod(x, g_ref[...], mod_ref[0], 3, 4).astype(BF16)
    h_ref[...] = h
    q_s[...] = _dot(h, wq_ref[...])
    lax.fori_loop(0, PEER_HEADS, functools.partial(
        _route_head, keys_ref, a_ref, b_ref, gate_ref, q_s, v1_s, i1_s, v2_s, i2_s, sc_s, a_s, b_s), 0)


def _route_head(keys_ref, a_ref, b_ref, gate_ref, q_s, v1_s, i1_s, v2_s, i2_s, sc_s, a_s, b_s, hd, carry):
    searches = []
    for half, (v_s, i_s) in enumerate(((v1_s, i1_s), (v2_s, i2_s))):
        col = pl.multiple_of(hd * PEER_DK + half * (PEER_DK // 2), PEER_DK // 2)
        qh = q_s[:, pl.ds(col, PEER_DK // 2)].astype(BF16)
        st = _dot_nt(keys_ref[half, hd], qh)

        def emit(step, value, row, v_s=v_s, i_s=i_s):
            v_s[step:step + 1, :] = value
            i_s[step:step + 1, :] = row

        blocks = [st[SUBLANES * i:SUBLANES * (i + 1), :] for i in range(PEER_NK // SUBLANES)]
        searches.append(_top_rows(blocks, PEER_TOPK, emit))
    _lockstep(searches)

    v1 = v1_s[...]
    v2 = v2_s[...]
    sub8 = lax.broadcasted_iota(jnp.int32, (8, TB), 0)
    pieces = [v1[0:1, :] + v2[0:8, :], v1[0:1, :] + v2[8:16, :]]
    for p in range(1, 8):
        pieces.append(jnp.where(sub8 < PEER_TOPK // (p + 1), v1[p:p + 1, :] + v2[0:8, :], -jnp.inf))
    pieces.append(v1[8:16, :] + v2[0:1, :])
    sub = lax.broadcasted_iota(jnp.int32, (PEER_TOPK, TB), 0)
    i1 = i1_s[...]
    i2 = i2_s[...]

    def emit2(step, value, pos):
        p = jnp.where(pos < 72, jnp.maximum((pos - 8) >> 3, 0), pos - 64)
        r = jnp.where(pos < 16, pos, jnp.where(pos < 72, pos & 7, 0))
        sc_s[step:step + 1, :] = value
        a_s[step:step + 1, :] = jnp.where(sub == p, i1, 0).sum(axis=0, keepdims=True)
        b_s[step:step + 1, :] = jnp.where(sub == r, i2, 0).sum(axis=0, keepdims=True)

    _lockstep([_top_rows(pieces, PEER_TOPK, emit2)])
    sc = sc_s[...]
    e = jnp.exp(sc - sc[0:1, :])
    out_rows = pl.ds(pl.multiple_of(hd * PEER_TOPK, PEER_TOPK), PEER_TOPK)
    gate_ref[0, out_rows, :] = e / e.sum(axis=0, keepdims=True)
    a_ref[0, out_rows, :] = a_s[...].astype(F32)
    b_ref[0, out_rows, :] = b_s[...].astype(F32)
    return carry


def _router(mix, modt, g, wq_bf16, keys_bf16, layer):
    mix_fn, (mix_specs, mix_args) = mix
    kt = pl.BlockSpec((1, PEER_HEADS * PEER_TOPK, TB), lambda i: (i, 0, 0))
    kshape = jax.ShapeDtypeStruct((N_TILES, PEER_HEADS * PEER_TOPK, TB), F32)
    f32s = lambda: pltpu.VMEM((PEER_TOPK, TB), F32)
    i32s = lambda: pltpu.VMEM((PEER_TOPK, TB), jnp.int32)
    tile = pl.BlockSpec((TB, D_MODEL), lambda i: (i, 0))
    return pl.pallas_call(
        functools.partial(_router_kernel, n_mix=len(mix_args), mix_fn=mix_fn),
        grid=(N_TILES,),
        in_specs=list(mix_specs) + [
            pl.BlockSpec((1, 6, D_MODEL), lambda i: (i, 0, 0)),
            pl.BlockSpec((1, D_MODEL), lambda i: (0, 0)),
            pl.BlockSpec((None, D_MODEL, PEER_HEADS * PEER_DK), lambda i: (layer, 0, 0)),
            pl.BlockSpec((None, 2, PEER_HEADS, PEER_NK, PEER_DK // 2), lambda i: (layer, 0, 0, 0, 0)),
        ],
        out_specs=[tile, tile, kt, kt, kt],
        out_shape=[jax.ShapeDtypeStruct((N_TOK, D_MODEL), F32), jax.ShapeDtypeStruct((N_TOK, D_MODEL), BF16),
                   kshape, kshape, kshape],
        scratch_shapes=[pltpu.VMEM((TB, PEER_HEADS * PEER_DK), F32),
                        f32s(), i32s(), f32s(), i32s(), f32s(), i32s(), i32s()],
        compiler_params=_cparams("parallel"),
        name="peer_router",
    )(*mix_args, modt, g.reshape(1, D_MODEL), wq_bf16, keys_bf16)


def _expert_kernel(h_ref, x_ref, mod_ref, at_ref, bt_ref, gt_ref, u_ref, v_ref, fg_ref, o_ref, *rest, final):
    if final:
        y_ref, g_s, acc_s, a_s, b_s, w_s, rt_s = rest
    else:
        g_s, acc_s, a_s, b_s, w_s, rt_s = rest
    grp = pl.program_id(1)

    @pl.when(grp == 0)
    def _():
        acc_s[...] = jnp.zeros_like(acc_s)
        for part in range(TE // TB):
            a_s[part * TB:(part + 1) * TB, :] = at_ref[part].T
            b_s[part * TB:(part + 1) * TB, :] = bt_ref[part].T
            w_s[part * TB:(part + 1) * TB, :] = gt_ref[part].T
        sub = lax.broadcasted_iota(jnp.int32, (PEER_NK, PEER_NK), 0).astype(F32).astype(BF16)
        one = jnp.ones((PEER_NK, PEER_NK), BF16)
        zero = jnp.zeros((PEER_NK, PEER_NK), BF16)

        def gate_grid(t, slot):
            a_row = a_s[pl.ds(t, 1), :].astype(BF16)
            b_row = b_s[pl.ds(t, 1), :].astype(BF16)
            w_row = jnp.broadcast_to(w_s[pl.ds(t, 1), :].astype(BF16), (PEER_NK, PEER_NK))
            m_t = jnp.where(a_row == sub, one, zero)
            r_t = jnp.where(b_row == sub, w_row, zero)
            rt_s[slot] = r_t.T
            g = _dot(m_t, rt_s[slot]).astype(BF16).astype(F32)
            return lax.bitcast_convert_type(g, jnp.uint32)

        def body(t, carry):
            slot = t & (G_UNROLL - 1)
            word = (gate_grid(t, slot) >> 16) | (gate_grid(t + TE // 2, slot + G_UNROLL) & jnp.uint32(0xFFFF0000))
            g_s[pl.ds(t, PEER_NK, stride=G_PITCH), :] = word
            return carry

        lax.fori_loop(0, TE // 2, body, 0, unroll=G_UNROLL)

    h = h_ref[...]
    acc = acc_s[...]
    for sb in range(PEER_GROUP // EXPERT_SUB):
        e0 = sb * EXPERT_SUB
        hmat = _dot_nt(h, u_ref[e0:e0 + EXPERT_SUB, :])
        parts = []
        for j in range(EXPERT_SUB // PEER_NK):
            chunk = grp * CHUNKS_PER_GROUP + sb * (EXPERT_SUB // PEER_NK) + j
            word = g_s[pl.ds(pl.multiple_of(chunk * G_PITCH, 8), TE // 2), :]
            gates = jnp.concatenate(
                [lax.bitcast_convert_type(word << 16, F32),
                 lax.bitcast_convert_type(word & jnp.uint32(0xFFFF0000), F32)], axis=0)
            parts.append((_gelu_tanh(hmat[:, j * PEER_NK:(j + 1) * PEER_NK]) * gates).astype(BF16))
        acc = acc + _dot(jnp.concatenate(parts, axis=1), v_ref[e0:e0 + EXPERT_SUB, :])
    acc_s[...] = acc

    @pl.when(grp == PEER_NGROUPS - 1)
    def _():
        xo = x_ref[...] + mod_ref[0][5:6, :] * acc_s[...]
        if final:
            y = xo * lax.rsqrt(jnp.mean(xo * xo, axis=-1, keepdims=True) + EPS) * fg_ref[...]
            tile = pl.program_id(0)

            @pl.when(tile < N_PROMPT // TE)
            def _():
                o_ref[...] = y

            @pl.when(tile >= N_PROMPT // TE)
            def _():
                y_ref[...] = y
        else:
            o_ref[...] = xo


def _experts(h_bf16, x, modt_e, a_t, b_t, g_t, u_bf16, v_bf16, layer, final_g, final):
    tile = pl.BlockSpec((TE, D_MODEL), lambda i, g: (i, 0))
    kt = pl.BlockSpec((TE // TB, PEER_HEADS * PEER_TOPK, TB), lambda i, g: (i, 0, 0))
    wblk = pl.BlockSpec((None, PEER_GROUP, D_MODEL), lambda i, g: (layer, g, 0))
    xs = jax.ShapeDtypeStruct((N_TOK, D_MODEL), F32)
    n_prompt = N_PROMPT // TE
    sel = lambda: pltpu.VMEM((TE, PEER_HEADS * PEER_TOPK), F32)
    return pl.pallas_call(
        functools.partial(_expert_kernel, final=final),
        grid=(N_TOK // TE, PEER_NGROUPS),
        in_specs=[tile, tile, pl.BlockSpec((1, 6, D_MODEL), lambda i, g: (i, 0, 0)), kt, kt, kt, wblk, wblk,
                  pl.BlockSpec((1, D_MODEL), lambda i, g: (0, 0))],
        out_specs=[pl.BlockSpec((TE, D_MODEL), lambda i, g: (jnp.minimum(i, n_prompt - 1), 0)),
                   pl.BlockSpec((TE, D_MODEL), lambda i, g: (jnp.maximum(i - n_prompt, 0), 0))] if final else tile,
        out_shape=[jax.ShapeDtypeStruct((N_PROMPT, D_MODEL), F32),
                   jax.ShapeDtypeStruct((N_SAMPLE, D_MODEL), F32)] if final else xs,
        scratch_shapes=[pltpu.VMEM((PEER_NK * G_PITCH, PEER_NK), jnp.uint32), pltpu.VMEM((TE, D_MODEL), F32),
                        sel(), sel(), sel(), pltpu.VMEM((2 * G_UNROLL, PEER_NK, PEER_NK), BF16)],
        compiler_params=_cparams("arbitrary" if final else "parallel", "arbitrary"),
        name="peer_experts_final" if final else "peer_experts",
    )(h_bf16, x, modt_e, a_t, b_t, g_t, u_bf16, v_bf16, final_g.reshape(1, D_MODEL))


def _peer(mix, modt, modt_e, norm_g, peer_bf16, layer, final_g, final):
    w_q, keys, u, v = peer_bf16
    x, h, a_t, b_t, g_t = _router(mix, modt, norm_g, w_q, keys, layer)
    return _experts(h, x, modt_e, a_t, b_t, g_t, u, v, layer, final_g, final)


def kernel(x_prompt, x_sample, cache_k, cache_v, state_hgrn_fwd, state_hgrn_bwd, state_lru_fwd, state_lru_bwd, c, c_ctx, norm1_g, norm2_g, w_mod, b_mod, w_in_even, w_out_even, hgrn_lb_fwd, hgrn_lb_bwd, hgrn_gnorm_g, attn_sink, w_in_odd, w_out_odd, conv_w, conv_b, rg_w_a, rg_b_a, rg_w_x, rg_b_x, rg_lambda, peer_w_q, peer_keys, peer_u, peer_v, final_g):
    x = (x_prompt.reshape(N_PROMPT, D_MODEL), x_sample.reshape(N_SAMPLE, D_MODEL))
    cond = jnp.concatenate([c_ctx[None, :], c, jnp.zeros((COND_PAD - N_COND, D_MODEL), F32)], axis=0)
    mod = _modulation(cond, w_mod, b_mod)
    tile_row = np.asarray([0] * PROMPT_TILES + [1 + i // TILES_PER_SAMPLE for i in range(N_TILES - PROMPT_TILES)])
    modt = [mod[l][tile_row].reshape(N_TILES, 6, D_MODEL) for l in range(DEPTH)]
    modt_e = [m[::TE // TB] for m in modt]

    widths = (HG_W, HG_W, HG_W, HG_W, HG_W, ATT_W, KV_W, KV_W)
    zq, zf, zb, zi, zg, aq, ak, av = _in_proj(x, modt[0], norm1_g[0], w_in_even[0].astype(BF16), widths, "even_in")
    zero_state = jnp.zeros((1, HG_HEADS, HG_DV, HG_DK), F32)
    s0f = jnp.concatenate([zero_state, jnp.swapaxes(state_hgrn_fwd[:, 0], -1, -2)], axis=0)
    s0b = jnp.concatenate([zero_state, jnp.swapaxes(state_hgrn_bwd[:, 0], -1, -2)], axis=0)
    o_f, o_b, sf_t, sb_t = _hgrn_scan(zq, zf, zb, zi, hgrn_lb_fwd, hgrn_lb_bwd, s0f, s0b)
    oatt = _attention(aq, ak, av, cache_k[:, 0].reshape(DEC_BATCH, PAST_LEN, KV_W),
                      cache_v[:, 0].reshape(DEC_BATCH, PAST_LEN, KV_W), attn_sink[0])
    even_mix = (_even_mix, _even_mix_operands(x, o_f, o_b, zg, oatt, hgrn_gnorm_g[0], w_out_even[0].astype(BF16)))
    peer_bf16 = (peer_w_q.astype(BF16), peer_keys.astype(BF16), peer_u.astype(BF16), peer_v.astype(BF16))
    x = _peer(even_mix, modt[0], modt_e[0], norm2_g[0], peer_bf16, 0, final_g, False)

    gate, xc = _odd_in(x, modt[1], norm1_g[1], w_in_odd[0].astype(BF16), conv_w[0], conv_b[0])
    zero_h = jnp.zeros((1, 1, D_RNN), F32)
    h0f = jnp.concatenate([zero_h, state_lru_fwd[:, 0][:, None, :]], axis=0)
    h0b = jnp.concatenate([zero_h, state_lru_bwd[:, 0][:, None, :]], axis=0)
    hf, hb = _lru_scan(xc, rg_w_a[0].astype(BF16), rg_b_a[0], rg_w_x[0].astype(BF16), rg_b_x[0],
                       rg_lambda[0], h0f, h0b)
    odd_mix = (_odd_mix, _odd_mix_operands(x, hf, hb, gate, w_out_odd[0].astype(BF16)))
    y_p, y_s = _peer(odd_mix, modt[1], modt_e[1], norm2_g[1], peer_bf16, 1, final_g, True)

    y_prompt = y_p.reshape(BATCH, SEQ, D_MODEL)
    y_sample = y_s.reshape(DEC_BATCH, DEC_SEQ, D_MODEL)
    new_k = ak[:N_PROMPT].reshape(BATCH, 1, SEQ, KV_HEADS, HEAD_DIM)
    new_v = av[:N_PROMPT].reshape(BATCH, 1, SEQ, KV_HEADS, HEAD_DIM)
    new_hf = jnp.swapaxes(sf_t[:BATCH], -1, -2)[:, None]
    new_hb = jnp.swapaxes(sb_t[:BATCH], -1, -2)[:, None]
    new_lf = hf[:N_PROMPT].reshape(BATCH, SEQ, D_RNN)[:, -1][:, None, :]
    new_lb = hb[:N_PROMPT].reshape(BATCH, SEQ, D_RNN)[:, 0][:, None, :]
    return (y_prompt, y_sample, new_k, new_v, new_hf, new_hb, new_lf, new_lb)
```

```python
import functools

import numpy as np
import jax
import jax.numpy as jnp
from jax import lax
from jax.experimental import pallas as pl
from jax.experimental.pallas import tpu as pltpu

F32 = jnp.float32
BF16 = jnp.bfloat16

D_MODEL = 1024
BATCH = 16
SEQ = 256
DEPTH = 2
DEC_BATCH = 8
DEC_SEQ = 2048
PAST_LEN = 512
GRID_W = 64
EPS = 1e-6
HG_DK = 128
HG_DV = 128
HG_HEADS = 4
HG_CHUNK = 64
HG_BLOCK = 256
HG_W = HG_HEADS * HG_DK
HEAD_DIM = 64
N_HEADS = 8
KV_HEADS = 2
Q_PER_KV = 4
WINDOW = 128
ATT_BLOCK = 128
ROPE_BASE = 10000.0
ATT_W = N_HEADS * HEAD_DIM
KV_W = KV_HEADS * HEAD_DIM
D_RNN = 1024
RG_BLOCKS = 4
RG_BW = 256
RG_C = 8.0
PEER_HEADS = 8
PEER_NK = 128
PEER_DK = 256
PEER_TOPK = 16
HEADS_PER_TRIP = 4

N_PROMPT = BATCH * SEQ
N_SAMPLE = DEC_BATCH * DEC_SEQ
N_TOK = N_PROMPT + N_SAMPLE
TB = 256
N_TILES = N_TOK // TB
PROMPT_TILES = N_PROMPT // TB
TILES_PER_SAMPLE = DEC_SEQ // TB
N_COND = 1 + DEC_BATCH
COND_PAD = 16

TE = 512
PEER_GROUP = 2048
PEER_NGROUPS = PEER_NK * PEER_NK // PEER_GROUP
CHUNKS_PER_GROUP = PEER_GROUP // PEER_NK
EXPERT_SUB = 1024
G_PITCH = TE // 2 + 8
G_UNROLL = 16

LRU_CHUNK = 256
CONV_SLAB = 512
SUBLANES = 8

VMEM_LIMIT = 58 * 1024 * 1024


def _cparams(*sem):
    return pltpu.CompilerParams(dimension_semantics=sem, vmem_limit_bytes=VMEM_LIMIT)


def _dot(a, b):
    return jnp.dot(a, b, preferred_element_type=F32)


def _dot_nt(a, b):
    return lax.dot_general(a, b, (((1,), (1,)), ((), ())), preferred_element_type=F32)


def _dot_tn(a, b):
    return lax.dot_general(a, b, (((0,), (0,)), ((), ())), preferred_element_type=F32)


def _sigmoid(x):
    return 1.0 / (1.0 + jnp.exp(-x))


def _sigmoid_tanh(x):
    return 0.5 + 0.5 * jnp.tanh(0.5 * x)


def _silu(x):
    return x * _sigmoid(x)


def _gelu_tanh(x):
    c = np.float32(np.sqrt(2.0 / np.pi))
    return x * (0.5 * (1.0 + jnp.tanh(c * (x + 0.044715 * (x * x * x)))))


def _rms_mod(x, g, mod, sh, sc):
    y = x * lax.rsqrt(jnp.mean(x * x, axis=-1, keepdims=True) + EPS) * g
    return y * (1.0 + mod[sc:sc + 1, :]) + mod[sh:sh + 1, :]


def _mod_kernel(c_ref, w_ref, b_ref, o_ref):
    c = c_ref[...]
    s = _silu(c).astype(BF16)
    o_ref[0] = _dot(s, w_ref[0].astype(BF16)) + b_ref[0]


def _modulation(cond, w_mod, b_mod):
    nb = 6 * D_MODEL // 1024
    return pl.pallas_call(
        _mod_kernel,
        grid=(DEPTH, nb),
        in_specs=[
            pl.BlockSpec((COND_PAD, D_MODEL), lambda l, n: (0, 0)),
            pl.BlockSpec((1, D_MODEL, 1024), lambda l, n: (l, 0, n)),
            pl.BlockSpec((1, 1, 1024), lambda l, n: (l, 0, n)),
        ],
        out_specs=pl.BlockSpec((1, COND_PAD, 1024), lambda l, n: (l, 0, n)),
        out_shape=jax.ShapeDtypeStruct((DEPTH, COND_PAD, 6 * D_MODEL), F32),
        compiler_params=_cparams("parallel", "parallel"),
        name="modulation",
    )(cond, w_mod, b_mod.reshape(DEPTH, 1, 6 * D_MODEL))


def _token_specs(arrays, width):
    if not isinstance(arrays, tuple):
        return [pl.BlockSpec((TB, width), lambda i: (i, 0))]
    return [pl.BlockSpec((TB, width), lambda i: (jnp.minimum(i, PROMPT_TILES - 1), 0)),
            pl.BlockSpec((TB, width), lambda i: (jnp.maximum(i - PROMPT_TILES, 0), 0))]


def _token_tile(refs):
    if len(refs) == 1:
        return refs[0][...]
    return jnp.where(pl.program_id(0) < PROMPT_TILES, refs[0][...], refs[1][...])


def _in_proj_kernel(*refs, widths, n_x):
    x_refs, (mod_ref, g_ref, w_ref), o_refs = refs[:n_x], refs[n_x:n_x + 3], refs[n_x + 3:]
    h = _rms_mod(_token_tile(x_refs), g_ref[...], mod_ref[0], 0, 1)
    z = _dot(h.astype(BF16), w_ref[...])
    off = 0
    for o_ref, wd in zip(o_refs, widths):
        o_ref[...] = z[:, off:off + wd]
        off += wd


def _in_proj(x, modt, g, w_bf16, widths, name):
    n_out = sum(widths)
    xs = x if isinstance(x, tuple) else (x,)
    return pl.pallas_call(
        functools.partial(_in_proj_kernel, widths=widths, n_x=len(xs)),
        grid=(N_TILES,),
        in_specs=_token_specs(x, D_MODEL) + [
            pl.BlockSpec((1, 6, D_MODEL), lambda i: (i, 0, 0)),
            pl.BlockSpec((1, D_MODEL), lambda i: (0, 0)),
            pl.BlockSpec((D_MODEL, n_out), lambda i: (0, 0)),
        ],
        out_specs=[pl.BlockSpec((TB, wd), lambda i: (i, 0)) for wd in widths],
        out_shape=[jax.ShapeDtypeStruct((N_TOK, wd), F32) for wd in widths],
        compiler_params=_cparams("parallel"),
        name=name,
    )(*xs, modt, g.reshape(1, D_MODEL), w_bf16)


def _hgrn_chunk_terms(zq, zgate, zi, lb, fwd):
    n = HG_CHUNK
    row = lax.broadcasted_iota(jnp.int32, (n, n), 0)
    col = lax.broadcasted_iota(jnp.int32, (n, n), 1)
    keep = (col <= row) if fwd else (col >= row)
    tri = jnp.where(keep, 1.0, 0.0).astype(BF16)
    q = zq * _sigmoid_tanh(zq)
    sg = _sigmoid_tanh(zgate)
    f = lb + (1.0 - lb) * sg
    g = jnp.log(f)
    k = (1.0 - lb) * (1.0 - sg)
    g_hi = g.astype(BF16)
    g_lo = (g - g_hi.astype(F32)).astype(BF16)
    b = _dot(tri, g_hi) + _dot(tri, g_lo)
    yield
    mid = n // 2 - 1 if fwd else n // 2
    last = n - 1 if fwd else 0
    bm = b[mid:mid + 1, :]
    bl = b[last:last + 1, :]
    qi32 = q * jnp.exp(b - bm)
    ki32 = k * jnp.exp(bm - b)
    qi = qi32.astype(BF16)
    ki = ki32.astype(BF16)
    qs = (qi32 * jnp.exp(bm)).astype(BF16)
    ks = (ki32 * jnp.exp(bl - bm)).astype(BF16)
    vb = zi.astype(BF16)
    decay = jnp.exp(bl)
    yield
    heads = [slice(h * HG_DK, (h + 1) * HG_DK) for h in range(HG_HEADS)]
    a = [jnp.where(keep, _dot_nt(qi[:, sl], ki[:, sl]), 0.0).astype(BF16) for sl in heads]
    kv = [_dot_tn(vb[:, sl], ks[:, sl]) for sl in heads]
    yield
    intra = [_dot(a[h], vb[:, sl]) for h, sl in enumerate(heads)]
    yield intra, qs, kv, decay


def _lower_bound(p):
    e = jnp.exp(p - p.max(axis=0, keepdims=True))
    return e[0:1, :] / e.sum(axis=0, keepdims=True)


def _hgrn_chunk_order(fwd):
    n_sub = HG_BLOCK // HG_CHUNK
    return [slice(c * HG_CHUNK, (c + 1) * HG_CHUNK) for c in (range(n_sub) if fwd else reversed(range(n_sub)))]


def _hgrn_block_terms(zq_ref, zgate_ref, zi_ref, lb, fwd):
    return [_hgrn_chunk_terms(zq_ref[rows, :], zgate_ref[rows, :], zi_ref[rows, :], lb, fwd)
            for rows in _hgrn_chunk_order(fwd)]


def _hgrn_block_chain(terms, st_ref, o_ref, fwd):
    states = [st_ref[h] for h in range(HG_HEADS)]
    for rows, (intra, qs, kv, decay) in zip(_hgrn_chunk_order(fwd), terms):
        outs = []
        for h in range(HG_HEADS):
            sl = slice(h * HG_DK, (h + 1) * HG_DK)
            outs.append(intra[h] + _dot_nt(qs[:, sl], states[h].astype(BF16)))
            states[h] = decay[:, sl] * states[h] + kv[h]
        o_ref[rows, :] = jnp.concatenate(outs, axis=1)
    for h in range(HG_HEADS):
        st_ref[h] = states[h]


def _hgrn_kernel(tbl_ref, zqf_ref, zff_ref, zif_ref, zqb_ref, zbb_ref, zib_ref, lbf_ref, lbb_ref,
                 s0f_ref, s0b_ref, of_ref, ob_ref, sf_ref, sb_ref, stf, stb):
    s = pl.program_id(0)

    @pl.when(tbl_ref[4, s] == 1)
    def _():
        stf[...] = s0f_ref[0]
        stb[...] = s0b_ref[0]

    gens_f = _hgrn_block_terms(zqf_ref, zff_ref, zif_ref, _lower_bound(lbf_ref[...]), True)
    gens_b = _hgrn_block_terms(zqb_ref, zbb_ref, zib_ref, _lower_bound(lbb_ref[...]), False)
    for _ in range(3):
        for gen in gens_f + gens_b:
            next(gen)
    _hgrn_block_chain([next(gen) for gen in gens_f], stf, of_ref, True)
    _hgrn_block_chain([next(gen) for gen in gens_b], stb, ob_ref, False)

    @pl.when(tbl_ref[5, s] == 1)
    def _():
        sf_ref[0] = stf[...]
        sb_ref[0] = stb[...]


def _scan_table(chunk):
    rows = []
    blk = 0
    for (nseq, length, has_init) in ((BATCH, SEQ, False), (DEC_BATCH, DEC_SEQ, True)):
        nc = length // chunk
        for b in range(nseq):
            sid = b if not has_init else BATCH + b
            for i in range(nc):
                rows.append((blk + i, blk + nc - 1 - i, (1 + b) if has_init else 0, sid,
                             int(i == 0), int(i == nc - 1)))
            blk += nc
    return np.asarray(rows, np.int32).T.copy()


def _hgrn_scan(zq, zf, zb, zi, lbf, lbb, s0f_t, s0b_t):
    tbl = _scan_table(HG_BLOCK)
    n_steps = tbl.shape[1]
    n_seq = BATCH + DEC_BATCH
    blk = lambda r: pl.BlockSpec((HG_BLOCK, HG_W), lambda s, t: (t[r, s], 0))
    vec = pl.BlockSpec(lbf.shape, lambda s, t: (0, 0))
    st_in = pl.BlockSpec((1, HG_HEADS, HG_DV, HG_DK), lambda s, t: (t[2, s], 0, 0, 0))
    st_out = pl.BlockSpec((1, HG_HEADS, HG_DV, HG_DK), lambda s, t: (t[3, s], 0, 0, 0))
    grid_spec = pltpu.PrefetchScalarGridSpec(
        num_scalar_prefetch=1,
        grid=(n_steps,),
        in_specs=[blk(0), blk(0), blk(0), blk(1), blk(1), blk(1), vec, vec, st_in, st_in],
        out_specs=[blk(0), blk(1), st_out, st_out],
        scratch_shapes=[pltpu.VMEM((HG_HEADS, HG_DV, HG_DK), F32),
                        pltpu.VMEM((HG_HEADS, HG_DV, HG_DK), F32)],
    )
    st_shape = jax.ShapeDtypeStruct((n_seq, HG_HEADS, HG_DV, HG_DK), F32)
    return pl.pallas_call(
        _hgrn_kernel,
        grid_spec=grid_spec,
        out_shape=[jax.ShapeDtypeStruct((N_TOK, HG_W), F32), jax.ShapeDtypeStruct((N_TOK, HG_W), F32),
                   st_shape, st_shape],
        compiler_params=_cparams("arbitrary"),
        name="hgrn_scan",
    )(jnp.asarray(tbl), zq, zf, zi, zq, zb, zi, lbf, lbb, s0f_t, s0b_t)


def _rope(x, cos, sgn_sin):
    w = x.shape[1]
    lane = lax.broadcasted_iota(jnp.int32, x.shape, 1)
    partner = jnp.where((lane & 31) < 16, pltpu.roll(x, w - 16, 1), pltpu.roll(x, 16, 1))
    return x * cos + partner * sgn_sin


def _attend(q, key_parts, sink_ref, o_ref):
    stages = [_attend_kv(q, key_parts, sink_ref, o_ref, kv) for kv in range(KV_HEADS)]
    for _ in range(4):
        for stage in stages:
            next(stage)


def _attend_kv(q, key_parts, sink_ref, o_ref, kv):
    lane = lax.broadcasted_iota(jnp.int32, (ATT_BLOCK, 128), 1)
    lo = lane < HEAD_DIM
    own = lo if kv == 0 else ~lo
    qs = []
    for gq in range(Q_PER_KV):
        hd = kv * Q_PER_KV + gq
        pair = q[:, (hd // 2) * 128:(hd // 2 + 1) * 128]
        if hd % 2 != kv:
            pair = pltpu.roll(pair, HEAD_DIM, 1)
        qs.append(jnp.where(own, pair, 0.0))
    qst = jnp.concatenate(qs, axis=0).astype(BF16)
    logits = []
    vals = []
    for (k, v, mask) in key_parts:
        own_rows = lax.broadcasted_iota(jnp.int32, v.shape, 1) < HEAD_DIM
        own_rows = own_rows if kv == 0 else ~own_rows
        vals.append(jnp.where(own_rows, v, 1.0).astype(BF16))
        s = _dot_nt(qst, k.astype(BF16))
        if mask is not None:
            s = jnp.where(mask, s, -1e30)
        logits.append(s)
    yield
    tiles = [s[:, t * 128:(t + 1) * 128] for s in logits for t in range(s.shape[1] // 128)]
    m = functools.reduce(jnp.maximum, tiles).max(axis=-1, keepdims=True)
    sink_col = jnp.concatenate(
        [jnp.full((ATT_BLOCK, 1), sink_ref[kv * Q_PER_KV + gq], F32) for gq in range(Q_PER_KV)], axis=0)
    m = jnp.maximum(m, sink_col)
    yield
    acc = jnp.zeros((Q_PER_KV * ATT_BLOCK, 128), F32)
    for s, v in zip(logits, vals):
        acc = acc + _dot(jnp.exp(s - m).astype(BF16), v)
    yield
    den = pltpu.roll(acc, HEAD_DIM, 1) + jnp.exp(sink_col - m)
    out = acc / den
    for pr in range(Q_PER_KV // 2):
        r0 = out[(2 * pr) * ATT_BLOCK:(2 * pr + 1) * ATT_BLOCK]
        r1 = out[(2 * pr + 1) * ATT_BLOCK:(2 * pr + 2) * ATT_BLOCK]
        if kv == 0:
            r1 = pltpu.roll(r1, HEAD_DIM, 1)
        else:
            r0 = pltpu.roll(r0, HEAD_DIM, 1)
        c0 = (kv * Q_PER_KV // 2 + pr) * 128
        o_ref[:, c0:c0 + 128] = jnp.where(lo, r0, r1)
    yield


def _attn_prompt_kernel(sink_ref, q_ref, k_ref, v_ref, o_ref):
    q = q_ref[...] * np.float32(HEAD_DIM ** -0.5)
    _attend(q, [(k_ref[...], v_ref[...], None)], sink_ref, o_ref)


def _attn_latent_kernel(sink_ref, q_ref, kp_ref, kc_ref, kn_ref, vp_ref, vc_ref, vn_ref,
                        ck_ref, cv_ref, cos_ref, sin_ref, o_ref):
    j = pl.program_id(1)
    nq = pl.num_programs(1)
    qstart = pl.multiple_of((j + 1) * ATT_BLOCK, ATT_BLOCK)
    kstart = pl.multiple_of(j * ATT_BLOCK, ATT_BLOCK)
    cq = cos_ref[pl.ds(qstart, ATT_BLOCK), :]
    sq = sin_ref[pl.ds(qstart, ATT_BLOCK), :]
    q = _rope(q_ref[...], jnp.concatenate([cq] * 4, axis=1), jnp.concatenate([sq] * 4, axis=1))
    q = q * np.float32(HEAD_DIM ** -0.5)
    k_loc = jnp.concatenate([kp_ref[...], kc_ref[...], kn_ref[...]], axis=0)
    v_loc = jnp.concatenate([vp_ref[...], vc_ref[...], vn_ref[...]], axis=0)
    k_loc = _rope(k_loc, cos_ref[pl.ds(kstart, 3 * ATT_BLOCK), :], sin_ref[pl.ds(kstart, 3 * ATT_BLOCK), :])
    r = lax.broadcasted_iota(jnp.int32, (Q_PER_KV * ATT_BLOCK, 3 * ATT_BLOCK), 0) & (ATT_BLOCK - 1)
    c = lax.broadcasted_iota(jnp.int32, (Q_PER_KV * ATT_BLOCK, 3 * ATT_BLOCK), 1)
    rel = c - ATT_BLOCK - r
    c_lo = jnp.where(j > 0, 0, ATT_BLOCK)
    c_hi = jnp.where(j < nq - 1, 3 * ATT_BLOCK, 2 * ATT_BLOCK)
    ok = (jnp.abs(rel) <= WINDOW) & (c >= c_lo) & (c < c_hi)
    _attend(q, [(ck_ref[0], cv_ref[0], None), (k_loc, v_loc, ok)], sink_ref, o_ref)


def _rope_tables():
    pos = np.arange(-ATT_BLOCK, DEC_SEQ + ATT_BLOCK)
    nf = HEAD_DIM // 4
    inv = (1.0 / (ROPE_BASE ** (np.arange(nf, dtype=np.float32) / nf))).astype(np.float32)
    rows = (pos // GRID_W).astype(np.float32)
    cols = (pos % GRID_W).astype(np.float32)
    ar = rows[:, None] * inv[None, :]
    ac = cols[:, None] * inv[None, :]
    cos64 = np.concatenate([np.cos(ar), np.cos(ar), np.cos(ac), np.cos(ac)], axis=1)
    sin64 = np.concatenate([-np.sin(ar), np.sin(ar), -np.sin(ac), np.sin(ac)], axis=1)
    cos = np.concatenate([cos64, cos64], axis=1).astype(np.float32)
    sin = np.concatenate([sin64, sin64], axis=1).astype(np.float32)
    return jnp.asarray(cos), jnp.asarray(sin)


def _attention(aq, ak, av, ck, cv, sink):
    smem = pl.BlockSpec(memory_space=pltpu.SMEM)
    nqp = SEQ // ATT_BLOCK
    o_prompt = pl.pallas_call(
        _attn_prompt_kernel,
        grid=(BATCH, nqp),
        in_specs=[
            smem,
            pl.BlockSpec((ATT_BLOCK, ATT_W), lambda b, j: (b * nqp + j, 0)),
            pl.BlockSpec((SEQ, KV_W), lambda b, j: (b, 0)),
            pl.BlockSpec((SEQ, KV_W), lambda b, j: (b, 0)),
        ],
        out_specs=pl.BlockSpec((ATT_BLOCK, ATT_W), lambda b, j: (b * nqp + j, 0)),
        out_shape=jax.ShapeDtypeStruct((N_PROMPT, ATT_W), F32),
        compiler_params=_cparams("parallel", "parallel"),
        name="attn_prompt",
    )(sink, aq, ak, av)

    nq = DEC_SEQ // ATT_BLOCK
    base = N_PROMPT // ATT_BLOCK
    cos, sin = _rope_tables()
    kv_blk = lambda d: pl.BlockSpec(
        (ATT_BLOCK, KV_W), lambda b, j: (base + b * nq + jnp.clip(j + d, 0, nq - 1), 0))
    tab = pl.BlockSpec((DEC_SEQ + 2 * ATT_BLOCK, 128), lambda b, j: (0, 0))
    o_latent = pl.pallas_call(
        _attn_latent_kernel,
        grid=(DEC_BATCH, nq),
        in_specs=[
            smem,
            pl.BlockSpec((ATT_BLOCK, ATT_W), lambda b, j: (base + b * nq + j, 0)),
            kv_blk(-1), kv_blk(0), kv_blk(1), kv_blk(-1), kv_blk(0), kv_blk(1),
            pl.BlockSpec((1, PAST_LEN, KV_W), lambda b, j: (b, 0, 0)),
            pl.BlockSpec((1, PAST_LEN, KV_W), lambda b, j: (b, 0, 0)),
            tab, tab,
        ],
        out_specs=pl.BlockSpec((ATT_BLOCK, ATT_W), lambda b, j: (b * nq + j, 0)),
        out_shape=jax.ShapeDtypeStruct((N_SAMPLE, ATT_W), F32),
        compiler_params=_cparams("parallel", "parallel"),
        name="attn_latent",
    )(sink, aq, ak, ak, ak, av, av, av, ck, cv, cos, sin)
    return o_prompt, o_latent


def _even_mix(refs, mod):
    xp_ref, xs_ref, of_ref, ob_ref, zg_ref, oattp_ref, oatts_ref, gn_ref, w_ref = refs
    x = _token_tile((xp_ref, xs_ref))
    oatt = _token_tile((oattp_ref, oatts_ref))
    o = of_ref[...] + ob_ref[...]
    parts = []
    for h in range(HG_HEADS):
        oh = o[:, h * HG_DV:(h + 1) * HG_DV]
        parts.append(oh * lax.rsqrt(jnp.mean(oh * oh, axis=-1, keepdims=True) + EPS))
    o_hg = jnp.concatenate(parts, axis=1) * gn_ref[...] * _silu(zg_ref[...])
    y = _dot(o_hg.astype(BF16), w_ref[0:HG_W, :]) + _dot(oatt.astype(BF16), w_ref[HG_W:, :])
    return x + mod[2:3, :] * y


def _even_mix_operands(x_pair, o_f, o_b, zg, oatt_pair, gn_g, w_bf16):
    tile = lambda w: pl.BlockSpec((TB, w), lambda i: (i, 0))
    specs = (_token_specs(x_pair, D_MODEL) + [tile(HG_W), tile(HG_W), tile(HG_W)] + _token_specs(oatt_pair, ATT_W)
             + [pl.BlockSpec((1, HG_W), lambda i: (0, 0)), pl.BlockSpec((HG_W + ATT_W, D_MODEL), lambda i: (0, 0))])
    return specs, (*x_pair, o_f, o_b, zg, *oatt_pair, gn_g.reshape(1, HG_W), w_bf16)


def _odd_in_kernel(x_ref, xp_ref, xn_ref, mod_ref, g_ref, w_ref, cw_ref, cb_ref, gate_ref, xc_ref):
    i = pl.program_id(0)
    in_sample = i >= PROMPT_TILES
    pos = (i - PROMPT_TILES) % TILES_PER_SAMPLE
    has_prev = in_sample & (pos != 0)
    has_next = in_sample & (pos != TILES_PER_SAMPLE - 1)
    mod = mod_ref[0]
    g = g_ref[...]
    h = jnp.concatenate([_rms_mod(r[...], g, mod, 0, 1) for r in (xp_ref, x_ref, xn_ref)], axis=0).astype(BF16)
    n = TB + 16
    row = lax.broadcasted_iota(jnp.int32, (n, CONV_SLAB), 0)
    live = (row >= jnp.where(has_prev, 0, 8)) & (row < jnp.where(has_next, n, 8 + TB))
    gate_ref[...] = _dot(h[8:8 + TB], w_ref[:, :D_RNN])
    for c0 in range(D_RNN, 2 * D_RNN, CONV_SLAB):
        ext = jnp.where(live, _dot(h, w_ref[:, c0:c0 + CONV_SLAB]), 0.0)
        cols = slice(c0 - D_RNN, c0 - D_RNN + CONV_SLAB)
        w = cw_ref[:, cols]
        acc = ext[8:8 + TB] * w[2:3, :] + cb_ref[:, cols]
        acc = acc + pltpu.roll(ext, 2, 0)[8:8 + TB] * w[0:1, :]
        acc = acc + pltpu.roll(ext, 1, 0)[8:8 + TB] * w[1:2, :]
        acc = acc + pltpu.roll(ext, n - 1, 0)[8:8 + TB] * w[3:4, :]
        xc_ref[:, cols] = acc


def _odd_in(x, modt, g, w_bf16, cw, cb):
    r8 = TB // 8
    last8 = N_TOK // 8 - 1
    tile = pl.BlockSpec((TB, D_MODEL), lambda i: (i, 0))
    return pl.pallas_call(
        _odd_in_kernel,
        grid=(N_TILES,),
        in_specs=[
            tile,
            pl.BlockSpec((8, D_MODEL), lambda i: (jnp.maximum(i * r8 - 1, 0), 0)),
            pl.BlockSpec((8, D_MODEL), lambda i: (jnp.minimum((i + 1) * r8, last8), 0)),
            pl.BlockSpec((1, 6, D_MODEL), lambda i: (i, 0, 0)),
            pl.BlockSpec((1, D_MODEL), lambda i: (0, 0)),
            pl.BlockSpec((D_MODEL, 2 * D_RNN), lambda i: (0, 0)),
            pl.BlockSpec((4, D_RNN), lambda i: (0, 0)),
            pl.BlockSpec((1, D_RNN), lambda i: (0, 0)),
        ],
        out_specs=[tile, tile],
        out_shape=[jax.ShapeDtypeStruct((N_TOK, D_RNN), F32), jax.ShapeDtypeStruct((N_TOK, D_RNN), F32)],
        compiler_params=_cparams("parallel"),
        name="odd_in",
    )(x, x, x, modt, g.reshape(1, D_MODEL), w_bf16, cw, cb.reshape(1, D_RNN))


def _log1p(y):
    w = 1.0 + y
    return jnp.where(w == 1.0, y, jnp.log(w) * (y / (w - 1.0)))


def _expm1_given(x, e):
    return jnp.where(e == 1.0, x, (e - 1.0) * (x / jnp.log(e)))


def _lru_direction(xc, wa_ref, ba, wx_ref, bx, lam, h_ref, o_ref, fwd):
    n = LRU_CHUNK
    xb = xc.astype(BF16)
    ra, ri = [], []
    for blk in range(RG_BLOCKS):
        sl = slice(blk * RG_BW, (blk + 1) * RG_BW)
        ra.append(_dot(xb[:, sl], wa_ref[blk]))
        ri.append(_dot(xb[:, sl], wx_ref[blk]))
    r = _sigmoid_tanh(jnp.concatenate(ra, axis=1) + ba)
    gi = _sigmoid_tanh(jnp.concatenate(ri, axis=1) + bx)
    neg = -lam
    softplus = jnp.maximum(neg, 0.0) + _log1p(jnp.exp(-jnp.abs(neg)))
    log_a = (-RG_C * softplus) * r
    a = jnp.exp(log_a)
    u = jnp.sqrt(-_expm1_given(2.0 * log_a, a * a)) * (gi * xc)
    sub = lax.broadcasted_iota(jnp.int32, (SUBLANES, D_RNN), 0)
    carry = h_ref[...]
    groups = range(n // SUBLANES)
    for g in (groups if fwd else reversed(groups)):
        rows = slice(g * SUBLANES, (g + 1) * SUBLANES)
        ag, ug = a[rows, :], u[rows, :]
        for d in (1, 2, 4):
            live = (sub >= d) if fwd else (sub < SUBLANES - d)
            shift = d if fwd else SUBLANES - d
            a_s = jnp.where(live, pltpu.roll(ag, shift, 0), 1.0)
            u_s = jnp.where(live, pltpu.roll(ug, shift, 0), 0.0)
            ug = ag * u_s + ug
            ag = ag * a_s
        h = ug + ag * carry
        o_ref[rows, :] = h
        carry = h[SUBLANES - 1:SUBLANES, :] if fwd else h[0:1, :]
    h_ref[...] = carry


def _lru_kernel(tbl_ref, xf_ref, xb_ref, wa_ref, ba_ref, wx_ref, bx_ref, lam_ref, h0f_ref, h0b_ref,
                hf_ref, hb_ref, hf_s, hb_s):
    s = pl.program_id(0)

    @pl.when(tbl_ref[4, s] == 1)
    def _():
        hf_s[...] = h0f_ref[0]
        hb_s[...] = h0b_ref[0]

    _lru_direction(xf_ref[...], wa_ref.at[0], ba_ref[0:1, :], wx_ref.at[0], bx_ref[0:1, :],
                   lam_ref[0:1, :], hf_s, hf_ref, True)
    _lru_direction(xb_ref[...], wa_ref.at[1], ba_ref[1:2, :], wx_ref.at[1], bx_ref[1:2, :],
                   lam_ref[1:2, :], hb_s, hb_ref, False)


def _lru_scan(xc, w_a_bf16, b_a, w_x_bf16, b_x, lam, h0f_all, h0b_all):
    tbl = _scan_table(LRU_CHUNK)
    n_steps = tbl.shape[1]
    blk = lambda r: pl.BlockSpec((LRU_CHUNK, D_RNN), lambda s, t: (t[r, s], 0))
    wspec = pl.BlockSpec((2, RG_BLOCKS, RG_BW, RG_BW), lambda s, t: (0, 0, 0, 0))
    vec2 = pl.BlockSpec((2, D_RNN), lambda s, t: (0, 0))
    h0 = pl.BlockSpec((1, 1, D_RNN), lambda s, t: (t[2, s], 0, 0))
    grid_spec = pltpu.PrefetchScalarGridSpec(
        num_scalar_prefetch=1,
        grid=(n_steps,),
        in_specs=[blk(0), blk(1), wspec, vec2, wspec, vec2, vec2, h0, h0],
        out_specs=[blk(0), blk(1)],
        scratch_shapes=[pltpu.VMEM((1, D_RNN), F32), pltpu.VMEM((1, D_RNN), F32)],
    )
    return pl.pallas_call(
        _lru_kernel,
        grid_spec=grid_spec,
        out_shape=[jax.ShapeDtypeStruct((N_TOK, D_RNN), F32), jax.ShapeDtypeStruct((N_TOK, D_RNN), F32)],
        compiler_params=_cparams("arbitrary"),
        name="lru_scan",
    )(jnp.asarray(tbl), xc, xc, w_a_bf16, b_a, w_x_bf16, b_x, lam, h0f_all, h0b_all)


def _odd_mix(refs, mod):
    x_ref, hf_ref, hb_ref, gate_ref, w_ref = refs
    y = (hf_ref[...] + hb_ref[...]) * _gelu_tanh(gate_ref[...])
    return x_ref[...] + mod[2:3, :] * _dot(y.astype(BF16), w_ref[...])


def _odd_mix_operands(x, hf, hb, gate, w_bf16):
    tile = pl.BlockSpec((TB, D_MODEL), lambda i: (i, 0))
    return [tile, tile, tile, tile, pl.BlockSpec((D_RNN, D_MODEL), lambda i: (0, 0))], (x, hf, hb, gate, w_bf16)


def _order_key(x):
    b = lax.bitcast_convert_type(x, jnp.int32)
    return b ^ ((b >> 31) & 0x7FFFFFFF)


def _key_value(k):
    return lax.bitcast_convert_type(k ^ ((k >> 31) & 0x7FFFFFFF), F32)


def _precedes(ky, ry, kx, rx):
    return (ky + jnp.where(ry < rx, 1, 0)) > kx


def _merge_exchange(n):
    pairs = []
    t = max(1, (n - 1).bit_length())
    p = 1 << (t - 1)
    while p > 0:
        q, r, d = 1 << (t - 1), 0, p
        while d > 0:
            pairs.extend((i, i + d) for i in range(n - d) if (i & p) == r)
            d, q, r = q - p, q >> 1, p
        p >>= 1
    return pairs


def _top_rows(values, count, emit):
    n = len(values)
    sub = lax.broadcasted_iota(jnp.int32, (SUBLANES, TB), 0)
    keys = [_order_key(v) for v in values]
    rows = [sub + SUBLANES * i for i in range(n)]
    for i, j in _merge_exchange(n):
        swap = _precedes(keys[j], rows[j], keys[i], rows[i])
        keys[i], keys[j] = jnp.where(swap, keys[j], keys[i]), jnp.where(swap, keys[i], keys[j])
        rows[i], rows[j] = jnp.where(swap, rows[j], rows[i]), jnp.where(swap, rows[i], rows[j])
    lowest = jnp.full((SUBLANES, TB), jnp.iinfo(jnp.int32).min + 1, jnp.int32)
    yield
    for step in range(count):
        k = keys[0]
        for shift in (4, 2, 1):
            k = jnp.maximum(k, pltpu.roll(k, shift, 0))
        r = jnp.where(keys[0] == k, rows[0], jnp.iinfo(jnp.int32).max)
        for shift in (4, 2, 1):
            r = jnp.minimum(r, pltpu.roll(r, shift, 0))
        emit(step, _key_value(k[0:1, :]), r[0:1, :])
        depth = min(n - 1, count - 1 - step)
        column = rows[0] == r
        for d in range(depth):
            keys[d] = jnp.where(column, keys[d + 1], keys[d])
            rows[d] = jnp.where(column, rows[d + 1], rows[d])
        if depth == n - 1:
            keys[n - 1] = jnp.where(column, lowest, keys[n - 1])
        yield


def _interleave(generators):
    live = list(generators)
    while live:
        live = [g for g in live if next(g, StopIteration) is not StopIteration]
        yield


def _lockstep(generators):
    for _ in _interleave(generators):
        pass


def _router_kernel(*refs, n_mix, mix_fn):
    mix_refs = refs[:n_mix]
    mod_ref, g_ref, wq_ref, keys_ref, x_ref, h_ref, a_ref, b_ref, gate_ref, q_s = refs[n_mix:n_mix + 10]
    scratch = refs[n_mix + 10:]
    x = mix_fn(mix_refs, mod_ref[0])
    x_ref[...] = x
    h = _rms_mod(x, g_ref[...], mod_ref[0], 3, 4).astype(BF16)
    h_ref[...] = h
    q_s[...] = _dot(h, wq_ref[...])
    def route_heads(trip, carry):
        _lockstep([_route_head(keys_ref, a_ref, b_ref, gate_ref, q_s, [s.at[j] for s in scratch],
                               trip * HEADS_PER_TRIP + j) for j in range(HEADS_PER_TRIP)])
        return carry

    lax.fori_loop(0, PEER_HEADS // HEADS_PER_TRIP, route_heads, 0)


def _route_head(keys_ref, a_ref, b_ref, gate_ref, q_s, scratch, hd):
    v1_s, i1_s, v2_s, i2_s, sc_s, a_s, b_s = scratch
    searches = []
    for half, (v_s, i_s) in enumerate(((v1_s, i1_s), (v2_s, i2_s))):
        col = pl.multiple_of(hd * PEER_DK + half * (PEER_DK // 2), PEER_DK // 2)
        qh = q_s[:, pl.ds(col, PEER_DK // 2)].astype(BF16)
        st = _dot_nt(keys_ref[half, hd], qh)

        def emit(step, value, row, v_s=v_s, i_s=i_s):
            v_s[step:step + 1, :] = value
            i_s[step:step + 1, :] = row

        blocks = [st[SUBLANES * i:SUBLANES * (i + 1), :] for i in range(PEER_NK // SUBLANES)]
        searches.append(_top_rows(blocks, PEER_TOPK, emit))
    yield from _interleave(searches)

    v1 = v1_s[...]
    v2 = v2_s[...]
    sub8 = lax.broadcasted_iota(jnp.int32, (8, TB), 0)
    pieces = [v1[0:1, :] + v2[0:8, :], v1[0:1, :] + v2[8:16, :]]
    for p in range(1, 8):
        pieces.append(jnp.where(sub8 < PEER_TOPK // (p + 1), v1[p:p + 1, :] + v2[0:8, :], -jnp.inf))
    pieces.append(v1[8:16, :] + v2[0:1, :])
    sub = lax.broadcasted_iota(jnp.int32, (PEER_TOPK, TB), 0)
    i1 = i1_s[...]
    i2 = i2_s[...]

    def emit2(step, value, pos):
        p = jnp.where(pos < 72, jnp.maximum((pos - 8) >> 3, 0), pos - 64)
        r = jnp.where(pos < 16, pos, jnp.where(pos < 72, pos & 7, 0))
        sc_s[step:step + 1, :] = value
        a_s[step:step + 1, :] = jnp.where(sub == p, i1, 0).sum(axis=0, keepdims=True)
        b_s[step:step + 1, :] = jnp.where(sub == r, i2, 0).sum(axis=0, keepdims=True)

    yield from _top_rows(pieces, PEER_TOPK, emit2)
    sc = sc_s[...]
    e = jnp.exp(sc - sc[0:1, :])
    out_rows = pl.ds(pl.multiple_of(hd * PEER_TOPK, PEER_TOPK), PEER_TOPK)
    gate_ref[0, out_rows, :] = e / e.sum(axis=0, keepdims=True)
    a_ref[0, out_rows, :] = a_s[...].astype(F32)
    b_ref[0, out_rows, :] = b_s[...].astype(F32)


def _router(mix, modt, g, wq_bf16, keys_bf16, layer):
    mix_fn, (mix_specs, mix_args) = mix
    kt = pl.BlockSpec((1, PEER_HEADS * PEER_TOPK, TB), lambda i: (i, 0, 0))
    kshape = jax.ShapeDtypeStruct((N_TILES, PEER_HEADS * PEER_TOPK, TB), F32)
    f32s = lambda: pltpu.VMEM((HEADS_PER_TRIP, PEER_TOPK, TB), F32)
    i32s = lambda: pltpu.VMEM((HEADS_PER_TRIP, PEER_TOPK, TB), jnp.int32)
    tile = pl.BlockSpec((TB, D_MODEL), lambda i: (i, 0))
    return pl.pallas_call(
        functools.partial(_router_kernel, n_mix=len(mix_args), mix_fn=mix_fn),
        grid=(N_TILES,),
        in_specs=list(mix_specs) + [
            pl.BlockSpec((1, 6, D_MODEL), lambda i: (i, 0, 0)),
            pl.BlockSpec((1, D_MODEL), lambda i: (0, 0)),
            pl.BlockSpec((None, D_MODEL, PEER_HEADS * PEER_DK), lambda i: (layer, 0, 0)),
            pl.BlockSpec((None, 2, PEER_HEADS, PEER_NK, PEER_DK // 2), lambda i: (layer, 0, 0, 0, 0)),
        ],
        out_specs=[tile, tile, kt, kt, kt],
        out_shape=[jax.ShapeDtypeStruct((N_TOK, D_MODEL), F32), jax.ShapeDtypeStruct((N_TOK, D_MODEL), BF16),
                   kshape, kshape, kshape],
        scratch_shapes=[pltpu.VMEM((TB, PEER_HEADS * PEER_DK), F32),
                        f32s(), i32s(), f32s(), i32s(), f32s(), i32s(), i32s()],
        compiler_params=_cparams("parallel"),
        name="peer_router",
    )(*mix_args, modt, g.reshape(1, D_MODEL), wq_bf16, keys_bf16)


def _expert_kernel(h_ref, x_ref, mod_ref, at_ref, bt_ref, gt_ref, u_ref, v_ref, fg_ref, o_ref, *rest, final):
    if final:
        y_ref, g_s, acc_s, a_s, b_s, w_s, rt_s = rest
    else:
        g_s, acc_s, a_s, b_s, w_s, rt_s = rest
    grp = pl.program_id(1)

    @pl.when(grp == 0)
    def _():
        acc_s[...] = jnp.zeros_like(acc_s)
        for part in range(TE // TB):
            a_s[part * TB:(part + 1) * TB, :] = at_ref[part].T
            b_s[part * TB:(part + 1) * TB, :] = bt_ref[part].T
            w_s[part * TB:(part + 1) * TB, :] = gt_ref[part].T
        sub = lax.broadcasted_iota(jnp.int32, (PEER_NK, PEER_NK), 0).astype(F32).astype(BF16)
        one = jnp.ones((PEER_NK, PEER_NK), BF16)
        zero = jnp.zeros((PEER_NK, PEER_NK), BF16)

        def gate_grid(t, slot):
            a_row = a_s[pl.ds(t, 1), :].astype(BF16)
            b_row = b_s[pl.ds(t, 1), :].astype(BF16)
            w_row = jnp.broadcast_to(w_s[pl.ds(t, 1), :].astype(BF16), (PEER_NK, PEER_NK))
            m_t = jnp.where(a_row == sub, one, zero)
            r_t = jnp.where(b_row == sub, w_row, zero)
            yield
            rt_s[slot] = r_t.T
            yield
            g = _dot(m_t, rt_s[slot]).astype(BF16).astype(F32)
            yield lax.bitcast_convert_type(g, jnp.uint32)

        def body(trip, carry):
            t0 = trip * G_UNROLL
            grids = [gate_grid(t0 + j + half * (TE // 2), j + half * G_UNROLL)
                     for j in range(G_UNROLL) for half in range(2)]
            for _ in range(2):
                for grid in grids:
                    next(grid)
            bits = [next(grid) for grid in grids]
            for j in range(G_UNROLL):
                word = (bits[2 * j] >> 16) | (bits[2 * j + 1] & jnp.uint32(0xFFFF0000))
                g_s[pl.ds(t0 + j, PEER_NK, stride=G_PITCH), :] = word
            return carry

        lax.fori_loop(0, TE // 2 // G_UNROLL, body, 0)

    h = h_ref[...]
    acc = acc_s[...]
    for sb in range(PEER_GROUP // EXPERT_SUB):
        e0 = sb * EXPERT_SUB
        hmat = _dot_nt(h, u_ref[e0:e0 + EXPERT_SUB, :])
        parts = []
        for j in range(EXPERT_SUB // PEER_NK):
            chunk = grp * CHUNKS_PER_GROUP + sb * (EXPERT_SUB // PEER_NK) + j
            word = g_s[pl.ds(pl.multiple_of(chunk * G_PITCH, 8), TE // 2), :]
            gates = jnp.concatenate(
                [lax.bitcast_convert_type(word << 16, F32),
                 lax.bitcast_convert_type(word & jnp.uint32(0xFFFF0000), F32)], axis=0)
            parts.append((_gelu_tanh(hmat[:, j * PEER_NK:(j + 1) * PEER_NK]) * gates).astype(BF16))
        acc = acc + _dot(jnp.concatenate(parts, axis=1), v_ref[e0:e0 + EXPERT_SUB, :])
    acc_s[...] = acc

    @pl.when(grp == PEER_NGROUPS - 1)
    def _():
        xo = x_ref[...] + mod_ref[0][5:6, :] * acc_s[...]
        if final:
            y = xo * lax.rsqrt(jnp.mean(xo * xo, axis=-1, keepdims=True) + EPS) * fg_ref[...]
            tile = pl.program_id(0)

            @pl.when(tile < N_PROMPT // TE)
            def _():
                o_ref[...] = y

            @pl.when(tile >= N_PROMPT // TE)
            def _():
                y_ref[...] = y
        else:
            o_ref[...] = xo


def _experts(h_bf16, x, modt_e, a_t, b_t, g_t, u_bf16, v_bf16, layer, final_g, final):
    tile = pl.BlockSpec((TE, D_MODEL), lambda i, g: (i, 0))
    kt = pl.BlockSpec((TE // TB, PEER_HEADS * PEER_TOPK, TB), lambda i, g: (i, 0, 0))
    wblk = pl.BlockSpec((None, PEER_GROUP, D_MODEL), lambda i, g: (layer, g, 0))
    xs = jax.ShapeDtypeStruct((N_TOK, D_MODEL), F32)
    n_prompt = N_PROMPT // TE
    sel = lambda: pltpu.VMEM((TE, PEER_HEADS * PEER_TOPK), F32)
    return pl.pallas_call(
        functools.partial(_expert_kernel, final=final),
        grid=(N_TOK // TE, PEER_NGROUPS),
        in_specs=[tile, tile, pl.BlockSpec((1, 6, D_MODEL), lambda i, g: (i, 0, 0)), kt, kt, kt, wblk, wblk,
                  pl.BlockSpec((1, D_MODEL), lambda i, g: (0, 0))],
        out_specs=[pl.BlockSpec((TE, D_MODEL), lambda i, g: (jnp.minimum(i, n_prompt - 1), 0)),
                   pl.BlockSpec((TE, D_MODEL), lambda i, g: (jnp.maximum(i - n_prompt, 0), 0))] if final else tile,
        out_shape=[jax.ShapeDtypeStruct((N_PROMPT, D_MODEL), F32),
                   jax.ShapeDtypeStruct((N_SAMPLE, D_MODEL), F32)] if final else xs,
        scratch_shapes=[pltpu.VMEM((PEER_NK * G_PITCH, PEER_NK), jnp.uint32), pltpu.VMEM((TE, D_MODEL), F32),
                        sel(), sel(), sel(), pltpu.VMEM((2 * G_UNROLL, PEER_NK, PEER_NK), BF16)],
        compiler_params=_cparams("arbitrary" if final else "parallel", "arbitrary"),
        name="peer_experts_final" if final else "peer_experts",
    )(h_bf16, x, modt_e, a_t, b_t, g_t, u_bf16, v_bf16, final_g.reshape(1, D_MODEL))


def _peer(mix, modt, modt_e, norm_g, peer_bf16, layer, final_g, final):
    w_q, keys, u, v = peer_bf16
    x, h, a_t, b_t, g_t = _router(mix, modt, norm_g, w_q, keys, layer)
    return _experts(h, x, modt_e, a_t, b_t, g_t, u, v, layer, final_g, final)


def kernel(x_prompt, x_sample, cache_k, cache_v, state_hgrn_fwd, state_hgrn_bwd, state_lru_fwd, state_lru_bwd, c, c_ctx, norm1_g, norm2_g, w_mod, b_mod, w_in_even, w_out_even, hgrn_lb_fwd, hgrn_lb_bwd, hgrn_gnorm_g, attn_sink, w_in_odd, w_out_odd, conv_w, conv_b, rg_w_a, rg_b_a, rg_w_x, rg_b_x, rg_lambda, peer_w_q, peer_keys, peer_u, peer_v, final_g):
    x = (x_prompt.reshape(N_PROMPT, D_MODEL), x_sample.reshape(N_SAMPLE, D_MODEL))
    cond = jnp.concatenate([c_ctx[None, :], c, jnp.zeros((COND_PAD - N_COND, D_MODEL), F32)], axis=0)
    mod = _modulation(cond, w_mod, b_mod)
    tile_row = np.asarray([0] * PROMPT_TILES + [1 + i // TILES_PER_SAMPLE for i in range(N_TILES - PROMPT_TILES)])
    modt = [mod[l][tile_row].reshape(N_TILES, 6, D_MODEL) for l in range(DEPTH)]
    modt_e = [m[::TE // TB] for m in modt]

    widths = (HG_W, HG_W, HG_W, HG_W, HG_W, ATT_W, KV_W, KV_W)
    zq, zf, zb, zi, zg, aq, ak, av = _in_proj(x, modt[0], norm1_g[0], w_in_even[0].astype(BF16), widths, "even_in")
    zero_state = jnp.zeros((1, HG_HEADS, HG_DV, HG_DK), F32)
    s0f = jnp.concatenate([zero_state, jnp.swapaxes(state_hgrn_fwd[:, 0], -1, -2)], axis=0)
    s0b = jnp.concatenate([zero_state, jnp.swapaxes(state_hgrn_bwd[:, 0], -1, -2)], axis=0)
    o_f, o_b, sf_t, sb_t = _hgrn_scan(zq, zf, zb, zi, hgrn_lb_fwd, hgrn_lb_bwd, s0f, s0b)
    oatt = _attention(aq, ak, av, cache_k[:, 0].reshape(DEC_BATCH, PAST_LEN, KV_W),
                      cache_v[:, 0].reshape(DEC_BATCH, PAST_LEN, KV_W), attn_sink[0])
    even_mix = (_even_mix, _even_mix_operands(x, o_f, o_b, zg, oatt, hgrn_gnorm_g[0], w_out_even[0].astype(BF16)))
    peer_bf16 = (peer_w_q.astype(BF16), peer_keys.astype(BF16), peer_u.astype(BF16), peer_v.astype(BF16))
    x = _peer(even_mix, modt[0], modt_e[0], norm2_g[0], peer_bf16, 0, final_g, False)

    gate, xc = _odd_in(x, modt[1], norm1_g[1], w_in_odd[0].astype(BF16), conv_w[0], conv_b[0])
    zero_h = jnp.zeros((1, 1, D_RNN), F32)
    h0f = jnp.concatenate([zero_h, state_lru_fwd[:, 0][:, None, :]], axis=0)
    h0b = jnp.concatenate([zero_h, state_lru_bwd[:, 0][:, None, :]], axis=0)
    hf, hb = _lru_scan(xc, rg_w_a[0].astype(BF16), rg_b_a[0], rg_w_x[0].astype(BF16), rg_b_x[0],
                       rg_lambda[0], h0f, h0b)
    odd_mix = (_odd_mix, _odd_mix_operands(x, hf, hb, gate, w_out_odd[0].astype(BF16)))
    y_p, y_s = _peer(odd_mix, modt[1], modt_e[1], norm2_g[1], peer_bf16, 1, final_g, True)

    y_prompt = y_p.reshape(BATCH, SEQ, D_MODEL)
    y_sample = y_s.reshape(DEC_BATCH, DEC_SEQ, D_MODEL)
    new_k = ak[:N_PROMPT].reshape(BATCH, 1, SEQ, KV_HEADS, HEAD_DIM)
    new_v = av[:N_PROMPT].reshape(BATCH, 1, SEQ, KV_HEADS, HEAD_DIM)
    new_hf = jnp.swapaxes(sf_t[:BATCH], -1, -2)[:, None]
    new_hb = jnp.swapaxes(sb_t[:BATCH], -1, -2)[:, None]
    new_lf = hf[:N_PROMPT].reshape(BATCH, SEQ, D_RNN)[:, -1][:, None, :]
    new_lb = hb[:N_PROMPT].reshape(BATCH, SEQ, D_RNN)[:, 0][:, None, :]
    return (y_prompt, y_sample, new_k, new_v, new_hf, new_hb, new_lf, new_lb)
```

```python
import functools

import numpy as np
import jax
import jax.numpy as jnp
from jax import lax
from jax.experimental import pallas as pl
from jax.experimental.pallas import tpu as pltpu

F32 = jnp.float32
BF16 = jnp.bfloat16

D_MODEL = 1024
BATCH = 16
SEQ = 256
DEPTH = 2
DEC_BATCH = 8
DEC_SEQ = 2048
PAST_LEN = 512
GRID_W = 64
EPS = 1e-6
HG_DK = 128
HG_DV = 128
HG_HEADS = 4
HG_CHUNK = 64
HG_BLOCK = 256
HG_W = HG_HEADS * HG_DK
HEAD_DIM = 64
N_HEADS = 8
KV_HEADS = 2
Q_PER_KV = 4
WINDOW = 128
ATT_BLOCK = 128
ROPE_BASE = 10000.0
ATT_W = N_HEADS * HEAD_DIM
KV_W = KV_HEADS * HEAD_DIM
D_RNN = 1024
RG_BLOCKS = 4
RG_BW = 256
RG_C = 8.0
PEER_HEADS = 8
PEER_NK = 128
PEER_DK = 256
PEER_TOPK = 16
HEADS_PER_TRIP = 4

N_PROMPT = BATCH * SEQ
N_SAMPLE = DEC_BATCH * DEC_SEQ
N_TOK = N_PROMPT + N_SAMPLE
TB = 256
N_TILES = N_TOK // TB
PROMPT_TILES = N_PROMPT // TB
TILES_PER_SAMPLE = DEC_SEQ // TB
N_COND = 1 + DEC_BATCH
COND_PAD = 16

TE = 512
PEER_GROUP = 2048
PEER_NGROUPS = PEER_NK * PEER_NK // PEER_GROUP
CHUNKS_PER_GROUP = PEER_GROUP // PEER_NK
EXPERT_SUB = 1024
G_PITCH = TE // 2 + 8
G_UNROLL = 16

LRU_CHUNK = 256
CONV_SLAB = 512
SUBLANES = 8

VMEM_LIMIT = 58 * 1024 * 1024


def _cparams(*sem):
    return pltpu.CompilerParams(dimension_semantics=sem, vmem_limit_bytes=VMEM_LIMIT)


def _dot(a, b):
    return jnp.dot(a, b, preferred_element_type=F32)


def _dot_nt(a, b):
    return lax.dot_general(a, b, (((1,), (1,)), ((), ())), preferred_element_type=F32)


def _dot_tn(a, b):
    return lax.dot_general(a, b, (((0,), (0,)), ((), ())), preferred_element_type=F32)


def _sigmoid(x):
    return 1.0 / (1.0 + jnp.exp(-x))


def _sigmoid_tanh(x):
    return 0.5 + 0.5 * jnp.tanh(0.5 * x)


def _silu(x):
    return x * _sigmoid(x)


def _gelu_tanh(x):
    c = np.float32(np.sqrt(2.0 / np.pi))
    return x * (0.5 * (1.0 + jnp.tanh(c * (x + 0.044715 * (x * x * x)))))


def _rms_mod(x, g, mod, sh, sc):
    y = x * lax.rsqrt(jnp.mean(x * x, axis=-1, keepdims=True) + EPS) * g
    return y * (1.0 + mod[sc:sc + 1, :]) + mod[sh:sh + 1, :]


def _mod_kernel(c_ref, w_ref, b_ref, o_ref):
    c = c_ref[...]
    s = _silu(c).astype(BF16)
    o_ref[0] = _dot(s, w_ref[0].astype(BF16)) + b_ref[0]


def _modulation(cond, w_mod, b_mod):
    nb = 6 * D_MODEL // 1024
    return pl.pallas_call(
        _mod_kernel,
        grid=(DEPTH, nb),
        in_specs=[
            pl.BlockSpec((COND_PAD, D_MODEL), lambda l, n: (0, 0)),
            pl.BlockSpec((1, D_MODEL, 1024), lambda l, n: (l, 0, n)),
            pl.BlockSpec((1, 1, 1024), lambda l, n: (l, 0, n)),
        ],
        out_specs=pl.BlockSpec((1, COND_PAD, 1024), lambda l, n: (l, 0, n)),
        out_shape=jax.ShapeDtypeStruct((DEPTH, COND_PAD, 6 * D_MODEL), F32),
        compiler_params=_cparams("parallel", "parallel"),
        name="modulation",
    )(cond, w_mod, b_mod.reshape(DEPTH, 1, 6 * D_MODEL))


def _token_specs(arrays, width):
    if not isinstance(arrays, tuple):
        return [pl.BlockSpec((TB, width), lambda i: (i, 0))]
    return [pl.BlockSpec((TB, width), lambda i: (jnp.minimum(i, PROMPT_TILES - 1), 0)),
            pl.BlockSpec((TB, width), lambda i: (jnp.maximum(i - PROMPT_TILES, 0), 0))]


def _token_tile(refs):
    if len(refs) == 1:
        return refs[0][...]
    return jnp.where(pl.program_id(0) < PROMPT_TILES, refs[0][...], refs[1][...])


def _in_proj_kernel(*refs, widths, n_x):
    x_refs, (mod_ref, g_ref, w_ref), o_refs = refs[:n_x], refs[n_x:n_x + 3], refs[n_x + 3:]
    h = _rms_mod(_token_tile(x_refs), g_ref[...], mod_ref[0], 0, 1)
    z = _dot(h.astype(BF16), w_ref[...])
    off = 0
    for o_ref, wd in zip(o_refs, widths):
        o_ref[...] = z[:, off:off + wd]
        off += wd


def _in_proj(x, modt, g, w_bf16, widths, name):
    n_out = sum(widths)
    xs = x if isinstance(x, tuple) else (x,)
    return pl.pallas_call(
        functools.partial(_in_proj_kernel, widths=widths, n_x=len(xs)),
        grid=(N_TILES,),
        in_specs=_token_specs(x, D_MODEL) + [
            pl.BlockSpec((1, 6, D_MODEL), lambda i: (i, 0, 0)),
            pl.BlockSpec((1, D_MODEL), lambda i: (0, 0)),
            pl.BlockSpec((D_MODEL, n_out), lambda i: (0, 0)),
        ],
        out_specs=[pl.BlockSpec((TB, wd), lambda i: (i, 0)) for wd in widths],
        out_shape=[jax.ShapeDtypeStruct((N_TOK, wd), F32) for wd in widths],
        compiler_params=_cparams("parallel"),
        name=name,
    )(*xs, modt, g.reshape(1, D_MODEL), w_bf16)


def _hgrn_chunk_terms(zq, zgate, zi, lb, fwd):
    n = HG_CHUNK
    row = lax.broadcasted_iota(jnp.int32, (n, n), 0)
    col = lax.broadcasted_iota(jnp.int32, (n, n), 1)
    keep = (col <= row) if fwd else (col >= row)
    tri = jnp.where(keep, 1.0, 0.0).astype(BF16)
    q = zq * _sigmoid_tanh(zq)
    sg = _sigmoid_tanh(zgate)
    f = lb + (1.0 - lb) * sg
    g = jnp.log(f)
    k = (1.0 - lb) * (1.0 - sg)
    g_hi = g.astype(BF16)
    g_lo = (g - g_hi.astype(F32)).astype(BF16)
    b = _dot(tri, g_hi) + _dot(tri, g_lo)
    yield
    mid = n // 2 - 1 if fwd else n // 2
    last = n - 1 if fwd else 0
    bm = b[mid:mid + 1, :]
    bl = b[last:last + 1, :]
    qi32 = q * jnp.exp(b - bm)
    ki32 = k * jnp.exp(bm - b)
    qi = qi32.astype(BF16)
    ki = ki32.astype(BF16)
    qs = (qi32 * jnp.exp(bm)).astype(BF16)
    ks = (ki32 * jnp.exp(bl - bm)).astype(BF16)
    vb = zi.astype(BF16)
    decay = jnp.exp(bl)
    yield
    heads = [slice(h * HG_DK, (h + 1) * HG_DK) for h in range(HG_HEADS)]
    a = [jnp.where(keep, _dot_nt(qi[:, sl], ki[:, sl]), 0.0).astype(BF16) for sl in heads]
    kv = [_dot_tn(vb[:, sl], ks[:, sl]) for sl in heads]
    yield
    intra = [_dot(a[h], vb[:, sl]) for h, sl in enumerate(heads)]
    yield intra, qs, kv, decay


def _lower_bound(p):
    e = jnp.exp(p - p.max(axis=0, keepdims=True))
    return e[0:1, :] / e.sum(axis=0, keepdims=True)


def _hgrn_chunk_order(fwd):
    n_sub = HG_BLOCK // HG_CHUNK
    return [slice(c * HG_CHUNK, (c + 1) * HG_CHUNK) for c in (range(n_sub) if fwd else reversed(range(n_sub)))]


def _hgrn_block_terms(zq_ref, zgate_ref, zi_ref, lb, fwd):
    return [_hgrn_chunk_terms(zq_ref[rows, :], zgate_ref[rows, :], zi_ref[rows, :], lb, fwd)
            for rows in _hgrn_chunk_order(fwd)]


def _hgrn_block_chain(terms, st_ref, o_ref, fwd):
    states = [st_ref[h] for h in range(HG_HEADS)]
    for rows, (intra, qs, kv, decay) in zip(_hgrn_chunk_order(fwd), terms):
        outs = []
        for h in range(HG_HEADS):
            sl = slice(h * HG_DK, (h + 1) * HG_DK)
            outs.append(intra[h] + _dot_nt(qs[:, sl], states[h].astype(BF16)))
            states[h] = decay[:, sl] * states[h] + kv[h]
        o_ref[rows, :] = jnp.concatenate(outs, axis=1)
    for h in range(HG_HEADS):
        st_ref[h] = states[h]


def _hgrn_kernel(tbl_ref, zqf_ref, zff_ref, zif_ref, zqb_ref, zbb_ref, zib_ref, lbf_ref, lbb_ref,
                 s0f_ref, s0b_ref, of_ref, ob_ref, sf_ref, sb_ref, stf, stb):
    s = pl.program_id(0)

    @pl.when(tbl_ref[4, s] == 1)
    def _():
        stf[...] = s0f_ref[0]
        stb[...] = s0b_ref[0]

    gens_f = _hgrn_block_terms(zqf_ref, zff_ref, zif_ref, _lower_bound(lbf_ref[...]), True)
    gens_b = _hgrn_block_terms(zqb_ref, zbb_ref, zib_ref, _lower_bound(lbb_ref[...]), False)
    for _ in range(3):
        for gen in gens_f + gens_b:
            next(gen)
    _hgrn_block_chain([next(gen) for gen in gens_f], stf, of_ref, True)
    _hgrn_block_chain([next(gen) for gen in gens_b], stb, ob_ref, False)

    @pl.when(tbl_ref[5, s] == 1)
    def _():
        sf_ref[0] = stf[...]
        sb_ref[0] = stb[...]


def _scan_table(chunk):
    rows = []
    blk = 0
    for (nseq, length, has_init) in ((BATCH, SEQ, False), (DEC_BATCH, DEC_SEQ, True)):
        nc = length // chunk
        for b in range(nseq):
            sid = b if not has_init else BATCH + b
            for i in range(nc):
                rows.append((blk + i, blk + nc - 1 - i, (1 + b) if has_init else 0, sid,
                             int(i == 0), int(i == nc - 1)))
            blk += nc
    return np.asarray(rows, np.int32).T.copy()


def _hgrn_scan(zq, zf, zb, zi, lbf, lbb, s0f_t, s0b_t):
    tbl = _scan_table(HG_BLOCK)
    n_steps = tbl.shape[1]
    n_seq = BATCH + DEC_BATCH
    blk = lambda r: pl.BlockSpec((HG_BLOCK, HG_W), lambda s, t: (t[r, s], 0))
    vec = pl.BlockSpec(lbf.shape, lambda s, t: (0, 0))
    st_in = pl.BlockSpec((1, HG_HEADS, HG_DV, HG_DK), lambda s, t: (t[2, s], 0, 0, 0))
    st_out = pl.BlockSpec((1, HG_HEADS, HG_DV, HG_DK), lambda s, t: (t[3, s], 0, 0, 0))
    grid_spec = pltpu.PrefetchScalarGridSpec(
        num_scalar_prefetch=1,
        grid=(n_steps,),
        in_specs=[blk(0), blk(0), blk(0), blk(1), blk(1), blk(1), vec, vec, st_in, st_in],
        out_specs=[blk(0), blk(1), st_out, st_out],
        scratch_shapes=[pltpu.VMEM((HG_HEADS, HG_DV, HG_DK), F32),
                        pltpu.VMEM((HG_HEADS, HG_DV, HG_DK), F32)],
    )
    st_shape = jax.ShapeDtypeStruct((n_seq, HG_HEADS, HG_DV, HG_DK), F32)
    return pl.pallas_call(
        _hgrn_kernel,
        grid_spec=grid_spec,
        out_shape=[jax.ShapeDtypeStruct((N_TOK, HG_W), F32), jax.ShapeDtypeStruct((N_TOK, HG_W), F32),
                   st_shape, st_shape],
        compiler_params=_cparams("arbitrary"),
        name="hgrn_scan",
    )(jnp.asarray(tbl), zq, zf, zi, zq, zb, zi, lbf, lbb, s0f_t, s0b_t)


def _rope(x, cos, sgn_sin):
    w = x.shape[1]
    lane = lax.broadcasted_iota(jnp.int32, x.shape, 1)
    partner = jnp.where((lane & 31) < 16, pltpu.roll(x, w - 16, 1), pltpu.roll(x, 16, 1))
    return x * cos + partner * sgn_sin


def _attend(q, key_parts, sink_ref, o_ref):
    stages = [_attend_kv(q, key_parts, sink_ref, o_ref, kv) for kv in range(KV_HEADS)]
    for _ in range(4):
        for stage in stages:
            next(stage)


def _attend_kv(q, key_parts, sink_ref, o_ref, kv):
    lane = lax.broadcasted_iota(jnp.int32, (ATT_BLOCK, 128), 1)
    lo = lane < HEAD_DIM
    own = lo if kv == 0 else ~lo
    qs = []
    for gq in range(Q_PER_KV):
        hd = kv * Q_PER_KV + gq
        pair = q[:, (hd // 2) * 128:(hd // 2 + 1) * 128]
        if hd % 2 != kv:
            pair = pltpu.roll(pair, HEAD_DIM, 1)
        qs.append(jnp.where(own, pair, 0.0))
    qst = jnp.concatenate(qs, axis=0).astype(BF16)
    logits = []
    vals = []
    for (k, v, mask) in key_parts:
        own_rows = lax.broadcasted_iota(jnp.int32, v.shape, 1) < HEAD_DIM
        own_rows = own_rows if kv == 0 else ~own_rows
        vals.append(jnp.where(own_rows, v, 1.0).astype(BF16))
        s = _dot_nt(qst, k.astype(BF16))
        if mask is not None:
            s = jnp.where(mask, s, -1e30)
        logits.append(s)
    yield
    tiles = [s[:, t * 128:(t + 1) * 128] for s in logits for t in range(s.shape[1] // 128)]
    m = functools.reduce(jnp.maximum, tiles).max(axis=-1, keepdims=True)
    sink_col = jnp.concatenate(
        [jnp.full((ATT_BLOCK, 1), sink_ref[kv * Q_PER_KV + gq], F32) for gq in range(Q_PER_KV)], axis=0)
    m = jnp.maximum(m, sink_col)
    yield
    acc = jnp.zeros((Q_PER_KV * ATT_BLOCK, 128), F32)
    for s, v in zip(logits, vals):
        acc = acc + _dot(jnp.exp(s - m).astype(BF16), v)
    yield
    den = pltpu.roll(acc, HEAD_DIM, 1) + jnp.exp(sink_col - m)
    out = acc / den
    for pr in range(Q_PER_KV // 2):
        r0 = out[(2 * pr) * ATT_BLOCK:(2 * pr + 1) * ATT_BLOCK]
        r1 = out[(2 * pr + 1) * ATT_BLOCK:(2 * pr + 2) * ATT_BLOCK]
        if kv == 0:
            r1 = pltpu.roll(r1, HEAD_DIM, 1)
        else:
            r0 = pltpu.roll(r0, HEAD_DIM, 1)
        c0 = (kv * Q_PER_KV // 2 + pr) * 128
        o_ref[:, c0:c0 + 128] = jnp.where(lo, r0, r1)
    yield


def _attn_prompt_kernel(sink_ref, q_ref, k_ref, v_ref, o_ref):
    q = q_ref[...] * np.float32(HEAD_DIM ** -0.5)
    _attend(q, [(k_ref[...], v_ref[...], None)], sink_ref, o_ref)


def _attn_latent_kernel(sink_ref, q_ref, kp_ref, kc_ref, kn_ref, vp_ref, vc_ref, vn_ref,
                        ck_ref, cv_ref, cos_ref, sin_ref, o_ref):
    j = pl.program_id(1)
    nq = pl.num_programs(1)
    qstart = pl.multiple_of((j + 1) * ATT_BLOCK, ATT_BLOCK)
    kstart = pl.multiple_of(j * ATT_BLOCK, ATT_BLOCK)
    cq = cos_ref[pl.ds(qstart, ATT_BLOCK), :]
    sq = sin_ref[pl.ds(qstart, ATT_BLOCK), :]
    q = _rope(q_ref[...], jnp.concatenate([cq] * 4, axis=1), jnp.concatenate([sq] * 4, axis=1))
    q = q * np.float32(HEAD_DIM ** -0.5)
    k_loc = jnp.concatenate([kp_ref[...], kc_ref[...], kn_ref[...]], axis=0)
    v_loc = jnp.concatenate([vp_ref[...], vc_ref[...], vn_ref[...]], axis=0)
    k_loc = _rope(k_loc, cos_ref[pl.ds(kstart, 3 * ATT_BLOCK), :], sin_ref[pl.ds(kstart, 3 * ATT_BLOCK), :])
    r = lax.broadcasted_iota(jnp.int32, (Q_PER_KV * ATT_BLOCK, 3 * ATT_BLOCK), 0) & (ATT_BLOCK - 1)
    c = lax.broadcasted_iota(jnp.int32, (Q_PER_KV * ATT_BLOCK, 3 * ATT_BLOCK), 1)
    rel = c - ATT_BLOCK - r
    c_lo = jnp.where(j > 0, 0, ATT_BLOCK)
    c_hi = jnp.where(j < nq - 1, 3 * ATT_BLOCK, 2 * ATT_BLOCK)
    ok = (jnp.abs(rel) <= WINDOW) & (c >= c_lo) & (c < c_hi)
    _attend(q, [(ck_ref[0], cv_ref[0], None), (k_loc, v_loc, ok)], sink_ref, o_ref)


def _rope_tables():
    pos = np.arange(-ATT_BLOCK, DEC_SEQ + ATT_BLOCK)
    nf = HEAD_DIM // 4
    inv = (1.0 / (ROPE_BASE ** (np.arange(nf, dtype=np.float32) / nf))).astype(np.float32)
    rows = (pos // GRID_W).astype(np.float32)
    cols = (pos % GRID_W).astype(np.float32)
    ar = rows[:, None] * inv[None, :]
    ac = cols[:, None] * inv[None, :]
    cos64 = np.concatenate([np.cos(ar), np.cos(ar), np.cos(ac), np.cos(ac)], axis=1)
    sin64 = np.concatenate([-np.sin(ar), np.sin(ar), -np.sin(ac), np.sin(ac)], axis=1)
    cos = np.concatenate([cos64, cos64], axis=1).astype(np.float32)
    sin = np.concatenate([sin64, sin64], axis=1).astype(np.float32)
    return jnp.asarray(cos), jnp.asarray(sin)


def _attention(aq, ak, av, ck, cv, sink):
    smem = pl.BlockSpec(memory_space=pltpu.SMEM)
    nqp = SEQ // ATT_BLOCK
    o_prompt = pl.pallas_call(
        _attn_prompt_kernel,
        grid=(BATCH, nqp),
        in_specs=[
            smem,
            pl.BlockSpec((ATT_BLOCK, ATT_W), lambda b, j: (b * nqp + j, 0)),
            pl.BlockSpec((SEQ, KV_W), lambda b, j: (b, 0)),
            pl.BlockSpec((SEQ, KV_W), lambda b, j: (b, 0)),
        ],
        out_specs=pl.BlockSpec((ATT_BLOCK, ATT_W), lambda b, j: (b * nqp + j, 0)),
        out_shape=jax.ShapeDtypeStruct((N_PROMPT, ATT_W), F32),
        compiler_params=_cparams("parallel", "parallel"),
        name="attn_prompt",
    )(sink, aq, ak, av)

    nq = DEC_SEQ // ATT_BLOCK
    base = N_PROMPT // ATT_BLOCK
    cos, sin = _rope_tables()
    kv_blk = lambda d: pl.BlockSpec(
        (ATT_BLOCK, KV_W), lambda b, j: (base + b * nq + jnp.clip(j + d, 0, nq - 1), 0))
    tab = pl.BlockSpec((DEC_SEQ + 2 * ATT_BLOCK, 128), lambda b, j: (0, 0))
    o_latent = pl.pallas_call(
        _attn_latent_kernel,
        grid=(DEC_BATCH, nq),
        in_specs=[
            smem,
            pl.BlockSpec((ATT_BLOCK, ATT_W), lambda b, j: (base + b * nq + j, 0)),
            kv_blk(-1), kv_blk(0), kv_blk(1), kv_blk(-1), kv_blk(0), kv_blk(1),
            pl.BlockSpec((1, PAST_LEN, KV_W), lambda b, j: (b, 0, 0)),
            pl.BlockSpec((1, PAST_LEN, KV_W), lambda b, j: (b, 0, 0)),
            tab, tab,
        ],
        out_specs=pl.BlockSpec((ATT_BLOCK, ATT_W), lambda b, j: (b * nq + j, 0)),
        out_shape=jax.ShapeDtypeStruct((N_SAMPLE, ATT_W), F32),
        compiler_params=_cparams("parallel", "parallel"),
        name="attn_latent",
    )(sink, aq, ak, ak, ak, av, av, av, ck, cv, cos, sin)
    return o_prompt, o_latent


def _even_mix(refs, mod):
    xp_ref, xs_ref, of_ref, ob_ref, zg_ref, oattp_ref, oatts_ref, gn_ref, w_ref = refs
    x = _token_tile((xp_ref, xs_ref))
    oatt = _token_tile((oattp_ref, oatts_ref))
    o = of_ref[...] + ob_ref[...]
    parts = []
    for h in range(HG_HEADS):
        oh = o[:, h * HG_DV:(h + 1) * HG_DV]
        parts.append(oh * lax.rsqrt(jnp.mean(oh * oh, axis=-1, keepdims=True) + EPS))
    o_hg = jnp.concatenate(parts, axis=1) * gn_ref[...] * _silu(zg_ref[...])
    y = _dot(o_hg.astype(BF16), w_ref[0:HG_W, :]) + _dot(oatt.astype(BF16), w_ref[HG_W:, :])
    return x + mod[2:3, :] * y


def _even_mix_operands(x_pair, o_f, o_b, zg, oatt_pair, gn_g, w_bf16):
    tile = lambda w: pl.BlockSpec((TB, w), lambda i: (i, 0))
    specs = (_token_specs(x_pair, D_MODEL) + [tile(HG_W), tile(HG_W), tile(HG_W)] + _token_specs(oatt_pair, ATT_W)
             + [pl.BlockSpec((1, HG_W), lambda i: (0, 0)), pl.BlockSpec((HG_W + ATT_W, D_MODEL), lambda i: (0, 0))])
    return specs, (*x_pair, o_f, o_b, zg, *oatt_pair, gn_g.reshape(1, HG_W), w_bf16)


def _odd_in_kernel(x_ref, xp_ref, xn_ref, mod_ref, g_ref, w_ref, cw_ref, cb_ref, gate_ref, xc_ref):
    i = pl.program_id(0)
    in_sample = i >= PROMPT_TILES
    pos = (i - PROMPT_TILES) % TILES_PER_SAMPLE
    has_prev = in_sample & (pos != 0)
    has_next = in_sample & (pos != TILES_PER_SAMPLE - 1)
    mod = mod_ref[0]
    g = g_ref[...]
    h = jnp.concatenate([_rms_mod(r[...], g, mod, 0, 1) for r in (xp_ref, x_ref, xn_ref)], axis=0).astype(BF16)
    n = TB + 16
    row = lax.broadcasted_iota(jnp.int32, (n, CONV_SLAB), 0)
    live = (row >= jnp.where(has_prev, 0, 8)) & (row < jnp.where(has_next, n, 8 + TB))
    gate_ref[...] = _dot(h[8:8 + TB], w_ref[:, :D_RNN])
    for c0 in range(D_RNN, 2 * D_RNN, CONV_SLAB):
        ext = jnp.where(live, _dot(h, w_ref[:, c0:c0 + CONV_SLAB]), 0.0)
        cols = slice(c0 - D_RNN, c0 - D_RNN + CONV_SLAB)
        w = cw_ref[:, cols]
        acc = ext[8:8 + TB] * w[2:3, :] + cb_ref[:, cols]
        acc = acc + pltpu.roll(ext, 2, 0)[8:8 + TB] * w[0:1, :]
        acc = acc + pltpu.roll(ext, 1, 0)[8:8 + TB] * w[1:2, :]
        acc = acc + pltpu.roll(ext, n - 1, 0)[8:8 + TB] * w[3:4, :]
        xc_ref[:, cols] = acc


def _odd_in(x, modt, g, w_bf16, cw, cb):
    r8 = TB // 8
    last8 = N_TOK // 8 - 1
    tile = pl.BlockSpec((TB, D_MODEL), lambda i: (i, 0))
    return pl.pallas_call(
        _odd_in_kernel,
        grid=(N_TILES,),
        in_specs=[
            tile,
            pl.BlockSpec((8, D_MODEL), lambda i: (jnp.maximum(i * r8 - 1, 0), 0)),
            pl.BlockSpec((8, D_MODEL), lambda i: (jnp.minimum((i + 1) * r8, last8), 0)),
            pl.BlockSpec((1, 6, D_MODEL), lambda i: (i, 0, 0)),
            pl.BlockSpec((1, D_MODEL), lambda i: (0, 0)),
            pl.BlockSpec((D_MODEL, 2 * D_RNN), lambda i: (0, 0)),
            pl.BlockSpec((4, D_RNN), lambda i: (0, 0)),
            pl.BlockSpec((1, D_RNN), lambda i: (0, 0)),
        ],
        out_specs=[tile, tile],
        out_shape=[jax.ShapeDtypeStruct((N_TOK, D_RNN), F32), jax.ShapeDtypeStruct((N_TOK, D_RNN), F32)],
        compiler_params=_cparams("parallel"),
        name="odd_in",
    )(x, x, x, modt, g.reshape(1, D_MODEL), w_bf16, cw, cb.reshape(1, D_RNN))


def _log1p(y):
    w = 1.0 + y
    return jnp.where(w == 1.0, y, jnp.log(w) * (y / (w - 1.0)))


def _expm1_given(x, e):
    return jnp.where(e == 1.0, x, (e - 1.0) * (x / jnp.log(e)))


def _lru_direction(xc, wa_ref, ba, wx_ref, bx, lam, h_ref, o_ref, tm_refs, fwd):
    n = LRU_CHUNK
    xb = xc.astype(BF16)
    ra, ri = [], []
    for blk in range(RG_BLOCKS):
        sl = slice(blk * RG_BW, (blk + 1) * RG_BW)
        ra.append(_dot(xb[:, sl], wa_ref[blk]))
        ri.append(_dot(xb[:, sl], wx_ref[blk]))
    yield
    r = _sigmoid_tanh(jnp.concatenate(ra, axis=1) + ba)
    gi = _sigmoid_tanh(jnp.concatenate(ri, axis=1) + bx)
    neg = -lam
    softplus = jnp.maximum(neg, 0.0) + _log1p(jnp.exp(-jnp.abs(neg)))
    log_a = (-RG_C * softplus) * r
    a = jnp.exp(log_a)
    u = jnp.sqrt(-_expm1_given(2.0 * log_a, a * a)) * (gi * xc)
    a_t, u_t, h_t = tm_refs
    n_tiles = D_RNN // 128
    for g in range(n // SUBLANES):
        for j in range(n_tiles):
            rows = pl.ds(g * SUBLANES * n_tiles + j, SUBLANES, stride=n_tiles)
            a_t[rows, :] = a[g * SUBLANES:(g + 1) * SUBLANES, j * 128:(j + 1) * 128]
            u_t[rows, :] = u[g * SUBLANES:(g + 1) * SUBLANES, j * 128:(j + 1) * 128]
    yield
    h = h_ref[...]
    for t in (range(n) if fwd else reversed(range(n))):
        step = slice(t * n_tiles, (t + 1) * n_tiles)
        h = a_t[step, :] * h + u_t[step, :]
        h_t[step, :] = h
        if t % SUBLANES == 0:
            yield
    h_ref[...] = h
    for g in range(n // SUBLANES):
        for j in range(n_tiles):
            rows = pl.ds(g * SUBLANES * n_tiles + j, SUBLANES, stride=n_tiles)
            o_ref[g * SUBLANES:(g + 1) * SUBLANES, j * 128:(j + 1) * 128] = h_t[rows, :]


def _lru_kernel(tbl_ref, xf_ref, xb_ref, wa_ref, ba_ref, wx_ref, bx_ref, lam_ref, h0f_ref, h0b_ref,
                hf_ref, hb_ref, hf_s, hb_s, *tm):
    s = pl.program_id(0)

    @pl.when(tbl_ref[4, s] == 1)
    def _():
        hf_s[...] = h0f_ref[0]
        hb_s[...] = h0b_ref[0]

    _lockstep([
        _lru_direction(xf_ref[...], wa_ref.at[0], ba_ref[0:1, :], wx_ref.at[0], bx_ref[0:1, :],
                       lam_ref[0:1, :], hf_s, hf_ref, tm[0:3], True),
        _lru_direction(xb_ref[...], wa_ref.at[1], ba_ref[1:2, :], wx_ref.at[1], bx_ref[1:2, :],
                       lam_ref[1:2, :], hb_s, hb_ref, tm[3:6], False)])


def _lru_scan(xc, w_a_bf16, b_a, w_x_bf16, b_x, lam, h0f_all, h0b_all):
    tbl = _scan_table(LRU_CHUNK)
    n_steps = tbl.shape[1]
    blk = lambda r: pl.BlockSpec((LRU_CHUNK, D_RNN), lambda s, t: (t[r, s], 0))
    wspec = pl.BlockSpec((2, RG_BLOCKS, RG_BW, RG_BW), lambda s, t: (0, 0, 0, 0))
    vec2 = pl.BlockSpec((2, D_RNN), lambda s, t: (0, 0))
    h0 = pl.BlockSpec((1, SUBLANES, D_RNN // SUBLANES), lambda s, t: (t[2, s], 0, 0))
    state = lambda: pltpu.VMEM((SUBLANES, D_RNN // SUBLANES), F32)
    time_major = lambda: pltpu.VMEM((LRU_CHUNK * D_RNN // 128, 128), F32)
    grid_spec = pltpu.PrefetchScalarGridSpec(
        num_scalar_prefetch=1,
        grid=(n_steps,),
        in_specs=[blk(0), blk(1), wspec, vec2, wspec, vec2, vec2, h0, h0],
        out_specs=[blk(0), blk(1)],
        scratch_shapes=[state(), state()] + [time_major() for _ in range(6)],
    )
    return pl.pallas_call(
        _lru_kernel,
        grid_spec=grid_spec,
        out_shape=[jax.ShapeDtypeStruct((N_TOK, D_RNN), F32), jax.ShapeDtypeStruct((N_TOK, D_RNN), F32)],
        compiler_params=_cparams("arbitrary"),
        name="lru_scan",
    )(jnp.asarray(tbl), xc, xc, w_a_bf16, b_a, w_x_bf16, b_x, lam,
      h0f_all.reshape(-1, SUBLANES, D_RNN // SUBLANES), h0b_all.reshape(-1, SUBLANES, D_RNN // SUBLANES))


def _odd_mix(refs, mod):
    x_ref, hf_ref, hb_ref, gate_ref, w_ref = refs
    y = (hf_ref[...] + hb_ref[...]) * _gelu_tanh(gate_ref[...])
    return x_ref[...] + mod[2:3, :] * _dot(y.astype(BF16), w_ref[...])


def _odd_mix_operands(x, hf, hb, gate, w_bf16):
    tile = pl.BlockSpec((TB, D_MODEL), lambda i: (i, 0))
    return [tile, tile, tile, tile, pl.BlockSpec((D_RNN, D_MODEL), lambda i: (0, 0))], (x, hf, hb, gate, w_bf16)


def _order_key(x):
    b = lax.bitcast_convert_type(x, jnp.int32)
    return b ^ ((b >> 31) & 0x7FFFFFFF)


def _key_value(k):
    return lax.bitcast_convert_type(k ^ ((k >> 31) & 0x7FFFFFFF), F32)


def _precedes(ky, ry, kx, rx):
    return (ky + jnp.where(ry < rx, 1, 0)) > kx


def _merge_exchange(n):
    pairs = []
    t = max(1, (n - 1).bit_length())
    p = 1 << (t - 1)
    while p > 0:
        q, r, d = 1 << (t - 1), 0, p
        while d > 0:
            pairs.extend((i, i + d) for i in range(n - d) if (i & p) == r)
            d, q, r = q - p, q >> 1, p
        p >>= 1
    return pairs


def _top_rows(values, count, emit):
    n = len(values)
    sub = lax.broadcasted_iota(jnp.int32, (SUBLANES, TB), 0)
    keys = [_order_key(v) for v in values]
    rows = [sub + SUBLANES * i for i in range(n)]
    for i, j in _merge_exchange(n):
        swap = _precedes(keys[j], rows[j], keys[i], rows[i])
        keys[i], keys[j] = jnp.where(swap, keys[j], keys[i]), jnp.where(swap, keys[i], keys[j])
        rows[i], rows[j] = jnp.where(swap, rows[j], rows[i]), jnp.where(swap, rows[i], rows[j])
    lowest = jnp.full((SUBLANES, TB), jnp.iinfo(jnp.int32).min + 1, jnp.int32)
    yield
    for step in range(count):
        k = keys[0]
        for shift in (4, 2, 1):
            k = jnp.maximum(k, pltpu.roll(k, shift, 0))
        r = jnp.where(keys[0] == k, rows[0], jnp.iinfo(jnp.int32).max)
        for shift in (4, 2, 1):
            r = jnp.minimum(r, pltpu.roll(r, shift, 0))
        emit(step, _key_value(k[0:1, :]), r[0:1, :])
        depth = min(n - 1, count - 1 - step)
        column = rows[0] == r
        for d in range(depth):
            keys[d] = jnp.where(column, keys[d + 1], keys[d])
            rows[d] = jnp.where(column, rows[d + 1], rows[d])
        if depth == n - 1:
            keys[n - 1] = jnp.where(column, lowest, keys[n - 1])
        yield


def _interleave(generators):
    live = list(generators)
    while live:
        live = [g for g in live if next(g, StopIteration) is not StopIteration]
        yield


def _lockstep(generators):
    for _ in _interleave(generators):
        pass


def _router_kernel(*refs, n_mix, mix_fn):
    mix_refs = refs[:n_mix]
    mod_ref, g_ref, wq_ref, keys_ref, x_ref, h_ref, a_ref, b_ref, gate_ref, q_s = refs[n_mix:n_mix + 10]
    scratch = refs[n_mix + 10:]
    x = mix_fn(mix_refs, mod_ref[0])
    x_ref[...] = x
    h = _rms_mod(x, g_ref[...], mod_ref[0], 3, 4).astype(BF16)
    h_ref[...] = h
    q_s[...] = _dot(h, wq_ref[...])
    def route_heads(trip, carry):
        _lockstep([_route_head(keys_ref, a_ref, b_ref, gate_ref, q_s, [s.at[j] for s in scratch],
                               trip * HEADS_PER_TRIP + j) for j in range(HEADS_PER_TRIP)])
        return carry

    lax.fori_loop(0, PEER_HEADS // HEADS_PER_TRIP, route_heads, 0)


def _route_head(keys_ref, a_ref, b_ref, gate_ref, q_s, scratch, hd):
    v1_s, i1_s, v2_s, i2_s, sc_s, a_s, b_s = scratch
    searches = []
    for half, (v_s, i_s) in enumerate(((v1_s, i1_s), (v2_s, i2_s))):
        col = pl.multiple_of(hd * PEER_DK + half * (PEER_DK // 2), PEER_DK // 2)
        qh = q_s[:, pl.ds(col, PEER_DK // 2)].astype(BF16)
        st = _dot_nt(keys_ref[half, hd], qh)

        def emit(step, value, row, v_s=v_s, i_s=i_s):
            v_s[step:step + 1, :] = value
            i_s[step:step + 1, :] = row

        blocks = [st[SUBLANES * i:SUBLANES * (i + 1), :] for i in range(PEER_NK // SUBLANES)]
        searches.append(_top_rows(blocks, PEER_TOPK, emit))
    yield from _interleave(searches)

    v1 = v1_s[...]
    v2 = v2_s[...]
    sub8 = lax.broadcasted_iota(jnp.int32, (8, TB), 0)
    pieces = [v1[0:1, :] + v2[0:8, :], v1[0:1, :] + v2[8:16, :]]
    for p in range(1, 8):
        pieces.append(jnp.where(sub8 < PEER_TOPK // (p + 1), v1[p:p + 1, :] + v2[0:8, :], -jnp.inf))
    pieces.append(v1[8:16, :] + v2[0:1, :])
    sub = lax.broadcasted_iota(jnp.int32, (PEER_TOPK, TB), 0)
    i1 = i1_s[...]
    i2 = i2_s[...]

    def emit2(step, value, pos):
        p = jnp.where(pos < 72, jnp.maximum((pos - 8) >> 3, 0), pos - 64)
        r = jnp.where(pos < 16, pos, jnp.where(pos < 72, pos & 7, 0))
        sc_s[step:step + 1, :] = value
        a_s[step:step + 1, :] = jnp.where(sub == p, i1, 0).sum(axis=0, keepdims=True)
        b_s[step:step + 1, :] = jnp.where(sub == r, i2, 0).sum(axis=0, keepdims=True)

    yield from _top_rows(pieces, PEER_TOPK, emit2)
    sc = sc_s[...]
    e = jnp.exp(sc - sc[0:1, :])
    out_rows = pl.ds(pl.multiple_of(hd * PEER_TOPK, PEER_TOPK), PEER_TOPK)
    gate_ref[0, out_rows, :] = e / e.sum(axis=0, keepdims=True)
    a_ref[0, out_rows, :] = a_s[...].astype(F32)
    b_ref[0, out_rows, :] = b_s[...].astype(F32)


def _router(mix, modt, g, wq_bf16, keys_bf16, layer):
    mix_fn, (mix_specs, mix_args) = mix
    kt = pl.BlockSpec((1, PEER_HEADS * PEER_TOPK, TB), lambda i: (i, 0, 0))
    kshape = jax.ShapeDtypeStruct((N_TILES, PEER_HEADS * PEER_TOPK, TB), F32)
    f32s = lambda: pltpu.VMEM((HEADS_PER_TRIP, PEER_TOPK, TB), F32)
    i32s = lambda: pltpu.VMEM((HEADS_PER_TRIP, PEER_TOPK, TB), jnp.int32)
    tile = pl.BlockSpec((TB, D_MODEL), lambda i: (i, 0))
    return pl.pallas_call(
        functools.partial(_router_kernel, n_mix=len(mix_args), mix_fn=mix_fn),
        grid=(N_TILES,),
        in_specs=list(mix_specs) + [
            pl.BlockSpec((1, 6, D_MODEL), lambda i: (i, 0, 0)),
            pl.BlockSpec((1, D_MODEL), lambda i: (0, 0)),
            pl.BlockSpec((None, D_MODEL, PEER_HEADS * PEER_DK), lambda i: (layer, 0, 0)),
            pl.BlockSpec((None, 2, PEER_HEADS, PEER_NK, PEER_DK // 2), lambda i: (layer, 0, 0, 0, 0)),
        ],
        out_specs=[tile, tile, kt, kt, kt],
        out_shape=[jax.ShapeDtypeStruct((N_TOK, D_MODEL), F32), jax.ShapeDtypeStruct((N_TOK, D_MODEL), BF16),
                   kshape, kshape, kshape],
        scratch_shapes=[pltpu.VMEM((TB, PEER_HEADS * PEER_DK), F32),
                        f32s(), i32s(), f32s(), i32s(), f32s(), i32s(), i32s()],
        compiler_params=_cparams("parallel"),
        name="peer_router",
    )(*mix_args, modt, g.reshape(1, D_MODEL), wq_bf16, keys_bf16)


def _expert_kernel(h_ref, x_ref, mod_ref, at_ref, bt_ref, gt_ref, u_ref, v_ref, fg_ref, o_ref, *rest, final):
    if final:
        y_ref, g_s, acc_s, a_s, b_s, w_s, rt_s = rest
    else:
        g_s, acc_s, a_s, b_s, w_s, rt_s = rest
    grp = pl.program_id(1)

    @pl.when(grp == 0)
    def _():
        acc_s[...] = jnp.zeros_like(acc_s)
        for part in range(TE // TB):
            a_s[part * TB:(part + 1) * TB, :] = at_ref[part].T
            b_s[part * TB:(part + 1) * TB, :] = bt_ref[part].T
            w_s[part * TB:(part + 1) * TB, :] = gt_ref[part].T
        sub = lax.broadcasted_iota(jnp.int32, (PEER_NK, PEER_NK), 0).astype(F32).astype(BF16)
        one = jnp.ones((PEER_NK, PEER_NK), BF16)
        zero = jnp.zeros((PEER_NK, PEER_NK), BF16)

        def gate_grid(t, slot):
            a_row = a_s[pl.ds(t, 1), :].astype(BF16)
            b_row = b_s[pl.ds(t, 1), :].astype(BF16)
            w_row = jnp.broadcast_to(w_s[pl.ds(t, 1), :].astype(BF16), (PEER_NK, PEER_NK))
            m_t = jnp.where(a_row == sub, one, zero)
            r_t = jnp.where(b_row == sub, w_row, zero)
            yield
            rt_s[slot] = r_t.T
            yield
            g = _dot(m_t, rt_s[slot]).astype(BF16).astype(F32)
            yield lax.bitcast_convert_type(g, jnp.uint32)

        def body(trip, carry):
            t0 = trip * G_UNROLL
            grids = [gate_grid(t0 + j + half * (TE // 2), j + half * G_UNROLL)
                     for j in range(G_UNROLL) for half in range(2)]
            for _ in range(2):
                for grid in grids:
                    next(grid)
            bits = [next(grid) for grid in grids]
            for j in range(G_UNROLL):
                word = (bits[2 * j] >> 16) | (bits[2 * j + 1] & jnp.uint32(0xFFFF0000))
                g_s[pl.ds(t0 + j, PEER_NK, stride=G_PITCH), :] = word
            return carry

        lax.fori_loop(0, TE // 2 // G_UNROLL, body, 0)

    h = h_ref[...]
    acc = acc_s[...]
    for sb in range(PEER_GROUP // EXPERT_SUB):
        e0 = sb * EXPERT_SUB
        hmat = _dot_nt(h, u_ref[e0:e0 + EXPERT_SUB, :])
        parts = []
        for j in range(EXPERT_SUB // PEER_NK):
            chunk = grp * CHUNKS_PER_GROUP + sb * (EXPERT_SUB // PEER_NK) + j
            word = g_s[pl.ds(pl.multiple_of(chunk * G_PITCH, 8), TE // 2), :]
            gates = jnp.concatenate(
                [lax.bitcast_convert_type(word << 16, F32),
                 lax.bitcast_convert_type(word & jnp.uint32(0xFFFF0000), F32)], axis=0)
            parts.append((_gelu_tanh(hmat[:, j * PEER_NK:(j + 1) * PEER_NK]) * gates).astype(BF16))
        acc = acc + _dot(jnp.concatenate(parts, axis=1), v_ref[e0:e0 + EXPERT_SUB, :])
    acc_s[...] = acc

    @pl.when(grp == PEER_NGROUPS - 1)
    def _():
        xo = x_ref[...] + mod_ref[0][5:6, :] * acc_s[...]
        if final:
            y = xo * lax.rsqrt(jnp.mean(xo * xo, axis=-1, keepdims=True) + EPS) * fg_ref[...]
            tile = pl.program_id(0)

            @pl.when(tile < N_PROMPT // TE)
            def _():
                o_ref[...] = y

            @pl.when(tile >= N_PROMPT // TE)
            def _():
                y_ref[...] = y
        else:
            o_ref[...] = xo


def _experts(h_bf16, x, modt_e, a_t, b_t, g_t, u_bf16, v_bf16, layer, final_g, final):
    tile = pl.BlockSpec((TE, D_MODEL), lambda i, g: (i, 0))
    kt = pl.BlockSpec((TE // TB, PEER_HEADS * PEER_TOPK, TB), lambda i, g: (i, 0, 0))
    wblk = pl.BlockSpec((None, PEER_GROUP, D_MODEL), lambda i, g: (layer, g, 0))
    xs = jax.ShapeDtypeStruct((N_TOK, D_MODEL), F32)
    n_prompt = N_PROMPT // TE
    sel = lambda: pltpu.VMEM((TE, PEER_HEADS * PEER_TOPK), F32)
    return pl.pallas_call(
        functools.partial(_expert_kernel, final=final),
        grid=(N_TOK // TE, PEER_NGROUPS),
        in_specs=[tile, tile, pl.BlockSpec((1, 6, D_MODEL), lambda i, g: (i, 0, 0)), kt, kt, kt, wblk, wblk,
                  pl.BlockSpec((1, D_MODEL), lambda i, g: (0, 0))],
        out_specs=[pl.BlockSpec((TE, D_MODEL), lambda i, g: (jnp.minimum(i, n_prompt - 1), 0)),
                   pl.BlockSpec((TE, D_MODEL), lambda i, g: (jnp.maximum(i - n_prompt, 0), 0))] if final else tile,
        out_shape=[jax.ShapeDtypeStruct((N_PROMPT, D_MODEL), F32),
                   jax.ShapeDtypeStruct((N_SAMPLE, D_MODEL), F32)] if final else xs,
        scratch_shapes=[pltpu.VMEM((PEER_NK * G_PITCH, PEER_NK), jnp.uint32), pltpu.VMEM((TE, D_MODEL), F32),
                        sel(), sel(), sel(), pltpu.VMEM((2 * G_UNROLL, PEER_NK, PEER_NK), BF16)],
        compiler_params=_cparams("arbitrary" if final else "parallel", "arbitrary"),
        name="peer_experts_final" if final else "peer_experts",
    )(h_bf16, x, modt_e, a_t, b_t, g_t, u_bf16, v_bf16, final_g.reshape(1, D_MODEL))


def _peer(mix, modt, modt_e, norm_g, peer_bf16, layer, final_g, final):
    w_q, keys, u, v = peer_bf16
    x, h, a_t, b_t, g_t = _router(mix, modt, norm_g, w_q, keys, layer)
    return _experts(h, x, modt_e, a_t, b_t, g_t, u, v, layer, final_g, final)


def kernel(x_prompt, x_sample, cache_k, cache_v, state_hgrn_fwd, state_hgrn_bwd, state_lru_fwd, state_lru_bwd, c, c_ctx, norm1_g, norm2_g, w_mod, b_mod, w_in_even, w_out_even, hgrn_lb_fwd, hgrn_lb_bwd, hgrn_gnorm_g, attn_sink, w_in_odd, w_out_odd, conv_w, conv_b, rg_w_a, rg_b_a, rg_w_x, rg_b_x, rg_lambda, peer_w_q, peer_keys, peer_u, peer_v, final_g):
    x = (x_prompt.reshape(N_PROMPT, D_MODEL), x_sample.reshape(N_SAMPLE, D_MODEL))
    cond = jnp.concatenate([c_ctx[None, :], c, jnp.zeros((COND_PAD - N_COND, D_MODEL), F32)], axis=0)
    mod = _modulation(cond, w_mod, b_mod)
    tile_row = np.asarray([0] * PROMPT_TILES + [1 + i // TILES_PER_SAMPLE for i in range(N_TILES - PROMPT_TILES)])
    modt = [mod[l][tile_row].reshape(N_TILES, 6, D_MODEL) for l in range(DEPTH)]
    modt_e = [m[::TE // TB] for m in modt]

    widths = (HG_W, HG_W, HG_W, HG_W, HG_W, ATT_W, KV_W, KV_W)
    zq, zf, zb, zi, zg, aq, ak, av = _in_proj(x, modt[0], norm1_g[0], w_in_even[0].astype(BF16), widths, "even_in")
    zero_state = jnp.zeros((1, HG_HEADS, HG_DV, HG_DK), F32)
    s0f = jnp.concatenate([zero_state, jnp.swapaxes(state_hgrn_fwd[:, 0], -1, -2)], axis=0)
    s0b = jnp.concatenate([zero_state, jnp.swapaxes(state_hgrn_bwd[:, 0], -1, -2)], axis=0)
    o_f, o_b, sf_t, sb_t = _hgrn_scan(zq, zf, zb, zi, hgrn_lb_fwd, hgrn_lb_bwd, s0f, s0b)
    oatt = _attention(aq, ak, av, cache_k[:, 0].reshape(DEC_BATCH, PAST_LEN, KV_W),
                      cache_v[:, 0].reshape(DEC_BATCH, PAST_LEN, KV_W), attn_sink[0])
    even_mix = (_even_mix, _even_mix_operands(x, o_f, o_b, zg, oatt, hgrn_gnorm_g[0], w_out_even[0].astype(BF16)))
    peer_bf16 = (peer_w_q.astype(BF16), peer_keys.astype(BF16), peer_u.astype(BF16), peer_v.astype(BF16))
    x = _peer(even_mix, modt[0], modt_e[0], norm2_g[0], peer_bf16, 0, final_g, False)

    gate, xc = _odd_in(x, modt[1], norm1_g[1], w_in_odd[0].astype(BF16), conv_w[0], conv_b[0])
    zero_h = jnp.zeros((1, 1, D_RNN), F32)
    h0f = jnp.concatenate([zero_h, state_lru_fwd[:, 0][:, None, :]], axis=0)
    h0b = jnp.concatenate([zero_h, state_lru_bwd[:, 0][:, None, :]], axis=0)
    hf, hb = _lru_scan(xc, rg_w_a[0].astype(BF16), rg_b_a[0], rg_w_x[0].astype(BF16), rg_b_x[0],
                       rg_lambda[0], h0f, h0b)
    odd_mix = (_odd_mix, _odd_mix_operands(x, hf, hb, gate, w_out_odd[0].astype(BF16)))
    y_p, y_s = _peer(odd_mix, modt[1], modt_e[1], norm2_g[1], peer_bf16, 1, final_g, True)

    y_prompt = y_p.reshape(BATCH, SEQ, D_MODEL)
    y_sample = y_s.reshape(DEC_BATCH, DEC_SEQ, D_MODEL)
    new_k = ak[:N_PROMPT].reshape(BATCH, 1, SEQ, KV_HEADS, HEAD_DIM)
    new_v = av[:N_PROMPT].reshape(BATCH, 1, SEQ, KV_HEADS, HEAD_DIM)
    new_hf = jnp.swapaxes(sf_t[:BATCH], -1, -2)[:, None]
    new_hb = jnp.swapaxes(sb_t[:BATCH], -1, -2)[:, None]
    new_lf = hf[:N_PROMPT].reshape(BATCH, SEQ, D_RNN)[:, -1][:, None, :]
    new_lb = hb[:N_PROMPT].reshape(BATCH, SEQ, D_RNN)[:, 0][:, None, :]
    return (y_prompt, y_sample, new_k, new_v, new_hf, new_hb, new_lf, new_lb)
```

```python
import functools

import numpy as np
import jax
import jax.numpy as jnp
from jax import lax
from jax.experimental import pallas as pl
from jax.experimental.pallas import tpu as pltpu

F32 = jnp.float32
BF16 = jnp.bfloat16

D_MODEL = 1024
BATCH = 16
SEQ = 256
DEPTH = 2
DEC_BATCH = 8
DEC_SEQ = 2048
PAST_LEN = 512
GRID_W = 64
EPS = 1e-6
HG_DK = 128
HG_DV = 128
HG_HEADS = 4
HG_CHUNK = 64
HG_BLOCK = 256
HG_W = HG_HEADS * HG_DK
HEAD_DIM = 64
N_HEADS = 8
KV_HEADS = 2
Q_PER_KV = 4
WINDOW = 128
ATT_BLOCK = 128
ROPE_BASE = 10000.0
ATT_W = N_HEADS * HEAD_DIM
KV_W = KV_HEADS * HEAD_DIM
D_RNN = 1024
RG_BLOCKS = 4
RG_BW = 256
RG_C = 8.0
PEER_HEADS = 8
PEER_NK = 128
PEER_DK = 256
PEER_TOPK = 16
HEADS_PER_TRIP = 4

N_PROMPT = BATCH * SEQ
N_SAMPLE = DEC_BATCH * DEC_SEQ
N_TOK = N_PROMPT + N_SAMPLE
TB = 256
N_TILES = N_TOK // TB
PROMPT_TILES = N_PROMPT // TB
TILES_PER_SAMPLE = DEC_SEQ // TB
N_COND = 1 + DEC_BATCH
COND_PAD = 16

TE = 512
PEER_GROUP = 2048
PEER_NGROUPS = PEER_NK * PEER_NK // PEER_GROUP
CHUNKS_PER_GROUP = PEER_GROUP // PEER_NK
EXPERT_SUB = 1024
G_PITCH = TE // 2 + 8
G_UNROLL = 16

LRU_CHUNK = 256
CONV_SLAB = 512
SUBLANES = 8

VMEM_LIMIT = 58 * 1024 * 1024


def _cparams(*sem):
    return pltpu.CompilerParams(dimension_semantics=sem, vmem_limit_bytes=VMEM_LIMIT)


def _dot(a, b):
    return jnp.dot(a, b, preferred_element_type=F32)


def _dot_nt(a, b):
    return lax.dot_general(a, b, (((1,), (1,)), ((), ())), preferred_element_type=F32)


def _dot_tn(a, b):
    return lax.dot_general(a, b, (((0,), (0,)), ((), ())), preferred_element_type=F32)


def _sigmoid(x):
    return 1.0 / (1.0 + jnp.exp(-x))


def _sigmoid_tanh(x):
    return 0.5 + 0.5 * jnp.tanh(0.5 * x)


def _silu(x):
    return x * _sigmoid(x)


def _gelu_tanh(x):
    c = np.float32(np.sqrt(2.0 / np.pi))
    return x * (0.5 * (1.0 + jnp.tanh(c * (x + 0.044715 * (x * x * x)))))


def _rms_mod(x, g, mod, sh, sc):
    y = x * lax.rsqrt(jnp.mean(x * x, axis=-1, keepdims=True) + EPS) * g
    return y * (1.0 + mod[sc:sc + 1, :]) + mod[sh:sh + 1, :]


def _mod_kernel(c_ref, w_ref, b_ref, o_ref):
    c = c_ref[...]
    s = _silu(c).astype(BF16)
    o_ref[0] = _dot(s, w_ref[0].astype(BF16)) + b_ref[0]


def _modulation(cond, w_mod, b_mod):
    nb = 6 * D_MODEL // 1024
    return pl.pallas_call(
        _mod_kernel,
        grid=(DEPTH, nb),
        in_specs=[
            pl.BlockSpec((COND_PAD, D_MODEL), lambda l, n: (0, 0)),
            pl.BlockSpec((1, D_MODEL, 1024), lambda l, n: (l, 0, n)),
            pl.BlockSpec((1, 1, 1024), lambda l, n: (l, 0, n)),
        ],
        out_specs=pl.BlockSpec((1, COND_PAD, 1024), lambda l, n: (l, 0, n)),
        out_shape=jax.ShapeDtypeStruct((DEPTH, COND_PAD, 6 * D_MODEL), F32),
        compiler_params=_cparams("parallel", "parallel"),
        name="modulation",
    )(cond, w_mod, b_mod.reshape(DEPTH, 1, 6 * D_MODEL))


def _token_specs(arrays, width):
    if not isinstance(arrays, tuple):
        return [pl.BlockSpec((TB, width), lambda i: (i, 0))]
    return [pl.BlockSpec((TB, width), lambda i: (jnp.minimum(i, PROMPT_TILES - 1), 0)),
            pl.BlockSpec((TB, width), lambda i: (jnp.maximum(i - PROMPT_TILES, 0), 0))]


def _token_tile(refs):
    if len(refs) == 1:
        return refs[0][...]
    return jnp.where(pl.program_id(0) < PROMPT_TILES, refs[0][...], refs[1][...])


def _in_proj_kernel(*refs, widths, n_x):
    x_refs, (mod_ref, g_ref, w_ref), o_refs = refs[:n_x], refs[n_x:n_x + 3], refs[n_x + 3:]
    h = _rms_mod(_token_tile(x_refs), g_ref[...], mod_ref[0], 0, 1)
    z = _dot(h.astype(BF16), w_ref[...])
    off = 0
    for o_ref, wd in zip(o_refs, widths):
        o_ref[...] = z[:, off:off + wd]
        off += wd


def _in_proj(x, modt, g, w_bf16, widths, name):
    n_out = sum(widths)
    xs = x if isinstance(x, tuple) else (x,)
    return pl.pallas_call(
        functools.partial(_in_proj_kernel, widths=widths, n_x=len(xs)),
        grid=(N_TILES,),
        in_specs=_token_specs(x, D_MODEL) + [
            pl.BlockSpec((1, 6, D_MODEL), lambda i: (i, 0, 0)),
            pl.BlockSpec((1, D_MODEL), lambda i: (0, 0)),
            pl.BlockSpec((D_MODEL, n_out), lambda i: (0, 0)),
        ],
        out_specs=[pl.BlockSpec((TB, wd), lambda i: (i, 0)) for wd in widths],
        out_shape=[jax.ShapeDtypeStruct((N_TOK, wd), F32) for wd in widths],
        compiler_params=_cparams("parallel"),
        name=name,
    )(*xs, modt, g.reshape(1, D_MODEL), w_bf16)


def _hgrn_chunk_terms(zq, zgate, zi, lb, fwd):
    n = HG_CHUNK
    row = lax.broadcasted_iota(jnp.int32, (n, n), 0)
    col = lax.broadcasted_iota(jnp.int32, (n, n), 1)
    keep = (col <= row) if fwd else (col >= row)
    tri = jnp.where(keep, 1.0, 0.0).astype(BF16)
    q = zq * _sigmoid_tanh(zq)
    sg = _sigmoid_tanh(zgate)
    f = lb + (1.0 - lb) * sg
    g = jnp.log(f)
    k = (1.0 - lb) * (1.0 - sg)
    g_hi = g.astype(BF16)
    g_lo = (g - g_hi.astype(F32)).astype(BF16)
    b = _dot(tri, g_hi) + _dot(tri, g_lo)
    yield
    mid = n // 2 - 1 if fwd else n // 2
    last = n - 1 if fwd else 0
    bm = b[mid:mid + 1, :]
    bl = b[last:last + 1, :]
    qi32 = q * jnp.exp(b - bm)
    ki32 = k * jnp.exp(bm - b)
    qi = qi32.astype(BF16)
    ki = ki32.astype(BF16)
    qs = (qi32 * jnp.exp(bm)).astype(BF16)
    ks = (ki32 * jnp.exp(bl - bm)).astype(BF16)
    vb = zi.astype(BF16)
    decay = jnp.exp(bl)
    yield
    heads = [slice(h * HG_DK, (h + 1) * HG_DK) for h in range(HG_HEADS)]
    a = [jnp.where(keep, _dot_nt(qi[:, sl], ki[:, sl]), 0.0).astype(BF16) for sl in heads]
    kv = [_dot_tn(vb[:, sl], ks[:, sl]) for sl in heads]
    yield
    intra = [_dot(a[h], vb[:, sl]) for h, sl in enumerate(heads)]
    yield intra, qs, kv, decay


def _lower_bound(p):
    e = jnp.exp(p - p.max(axis=0, keepdims=True))
    return e[0:1, :] / e.sum(axis=0, keepdims=True)


def _hgrn_chunk_order(fwd):
    n_sub = HG_BLOCK // HG_CHUNK
    return [slice(c * HG_CHUNK, (c + 1) * HG_CHUNK) for c in (range(n_sub) if fwd else reversed(range(n_sub)))]


def _hgrn_block_terms(zq_ref, zgate_ref, zi_ref, lb, fwd):
    return [_hgrn_chunk_terms(zq_ref[rows, :], zgate_ref[rows, :], zi_ref[rows, :], lb, fwd)
            for rows in _hgrn_chunk_order(fwd)]


def _hgrn_block_chain(terms, st_ref, o_ref, fwd):
    states = [st_ref[h] for h in range(HG_HEADS)]
    for rows, (intra, qs, kv, decay) in zip(_hgrn_chunk_order(fwd), terms):
        outs = []
        for h in range(HG_HEADS):
            sl = slice(h * HG_DK, (h + 1) * HG_DK)
            outs.append(intra[h] + _dot_nt(qs[:, sl], states[h].astype(BF16)))
            states[h] = decay[:, sl] * states[h] + kv[h]
        o_ref[rows, :] = jnp.concatenate(outs, axis=1)
    for h in range(HG_HEADS):
        st_ref[h] = states[h]


def _hgrn_kernel(tbl_ref, zqf_ref, zff_ref, zif_ref, zqb_ref, zbb_ref, zib_ref, lbf_ref, lbb_ref,
                 s0f_ref, s0b_ref, of_ref, ob_ref, sf_ref, sb_ref, stf, stb):
    s = pl.program_id(0)

    @pl.when(tbl_ref[4, s] == 1)
    def _():
        stf[...] = s0f_ref[0]
        stb[...] = s0b_ref[0]

    gens_f = _hgrn_block_terms(zqf_ref, zff_ref, zif_ref, _lower_bound(lbf_ref[...]), True)
    gens_b = _hgrn_block_terms(zqb_ref, zbb_ref, zib_ref, _lower_bound(lbb_ref[...]), False)
    for _ in range(3):
        for gen in gens_f + gens_b:
            next(gen)
    _hgrn_block_chain([next(gen) for gen in gens_f], stf, of_ref, True)
    _hgrn_block_chain([next(gen) for gen in gens_b], stb, ob_ref, False)

    @pl.when(tbl_ref[5, s] == 1)
    def _():
        sf_ref[0] = stf[...]
        sb_ref[0] = stb[...]


def _scan_table(chunk):
    rows = []
    blk = 0
    for (nseq, length, has_init) in ((BATCH, SEQ, False), (DEC_BATCH, DEC_SEQ, True)):
        nc = length // chunk
        for b in range(nseq):
            sid = b if not has_init else BATCH + b
            for i in range(nc):
                rows.append((blk + i, blk + nc - 1 - i, (1 + b) if has_init else 0, sid,
                             int(i == 0), int(i == nc - 1)))
            blk += nc
    return np.asarray(rows, np.int32).T.copy()


def _hgrn_scan(zq, zf, zb, zi, lbf, lbb, s0f_t, s0b_t):
    tbl = _scan_table(HG_BLOCK)
    n_steps = tbl.shape[1]
    n_seq = BATCH + DEC_BATCH
    blk = lambda r: pl.BlockSpec((HG_BLOCK, HG_W), lambda s, t: (t[r, s], 0))
    vec = pl.BlockSpec(lbf.shape, lambda s, t: (0, 0))
    st_in = pl.BlockSpec((1, HG_HEADS, HG_DV, HG_DK), lambda s, t: (t[2, s], 0, 0, 0))
    st_out = pl.BlockSpec((1, HG_HEADS, HG_DV, HG_DK), lambda s, t: (t[3, s], 0, 0, 0))
    grid_spec = pltpu.PrefetchScalarGridSpec(
        num_scalar_prefetch=1,
        grid=(n_steps,),
        in_specs=[blk(0), blk(0), blk(0), blk(1), blk(1), blk(1), vec, vec, st_in, st_in],
        out_specs=[blk(0), blk(1), st_out, st_out],
        scratch_shapes=[pltpu.VMEM((HG_HEADS, HG_DV, HG_DK), F32),
                        pltpu.VMEM((HG_HEADS, HG_DV, HG_DK), F32)],
    )
    st_shape = jax.ShapeDtypeStruct((n_seq, HG_HEADS, HG_DV, HG_DK), F32)
    return pl.pallas_call(
        _hgrn_kernel,
        grid_spec=grid_spec,
        out_shape=[jax.ShapeDtypeStruct((N_TOK, HG_W), F32), jax.ShapeDtypeStruct((N_TOK, HG_W), F32),
                   st_shape, st_shape],
        compiler_params=_cparams("arbitrary"),
        name="hgrn_scan",
    )(jnp.asarray(tbl), zq, zf, zi, zq, zb, zi, lbf, lbb, s0f_t, s0b_t)


def _rope(x, cos, sgn_sin):
    w = x.shape[1]
    lane = lax.broadcasted_iota(jnp.int32, x.shape, 1)
    partner = jnp.where((lane & 31) < 16, pltpu.roll(x, w - 16, 1), pltpu.roll(x, 16, 1))
    return x * cos + partner * sgn_sin


def _attend(q, key_parts, sink_ref, o_ref):
    stages = [_attend_kv(q, key_parts, sink_ref, o_ref, kv) for kv in range(KV_HEADS)]
    for _ in range(4):
        for stage in stages:
            next(stage)


def _attend_kv(q, key_parts, sink_ref, o_ref, kv):
    lane = lax.broadcasted_iota(jnp.int32, (ATT_BLOCK, 128), 1)
    lo = lane < HEAD_DIM
    own = lo if kv == 0 else ~lo
    qs = []
    for gq in range(Q_PER_KV):
        hd = kv * Q_PER_KV + gq
        pair = q[:, (hd // 2) * 128:(hd // 2 + 1) * 128]
        if hd % 2 != kv:
            pair = pltpu.roll(pair, HEAD_DIM, 1)
        qs.append(jnp.where(own, pair, 0.0))
    qst = jnp.concatenate(qs, axis=0).astype(BF16)
    logits = []
    vals = []
    for (k, v, mask) in key_parts:
        own_rows = lax.broadcasted_iota(jnp.int32, v.shape, 1) < HEAD_DIM
        own_rows = own_rows if kv == 0 else ~own_rows
        vals.append(jnp.where(own_rows, v, 1.0).astype(BF16))
        s = _dot_nt(qst, k.astype(BF16))
        if mask is not None:
            s = jnp.where(mask, s, -1e30)
        logits.append(s)
    yield
    tiles = [s[:, t * 128:(t + 1) * 128] for s in logits for t in range(s.shape[1] // 128)]
    m = functools.reduce(jnp.maximum, tiles).max(axis=-1, keepdims=True)
    sink_col = jnp.concatenate(
        [jnp.full((ATT_BLOCK, 1), sink_ref[kv * Q_PER_KV + gq], F32) for gq in range(Q_PER_KV)], axis=0)
    m = jnp.maximum(m, sink_col)
    yield
    acc = jnp.zeros((Q_PER_KV * ATT_BLOCK, 128), F32)
    for s, v in zip(logits, vals):
        acc = acc + _dot(jnp.exp(s - m).astype(BF16), v)
    yield
    den = pltpu.roll(acc, HEAD_DIM, 1) + jnp.exp(sink_col - m)
    out = acc / den
    for pr in range(Q_PER_KV // 2):
        r0 = out[(2 * pr) * ATT_BLOCK:(2 * pr + 1) * ATT_BLOCK]
        r1 = out[(2 * pr + 1) * ATT_BLOCK:(2 * pr + 2) * ATT_BLOCK]
        if kv == 0:
            r1 = pltpu.roll(r1, HEAD_DIM, 1)
        else:
            r0 = pltpu.roll(r0, HEAD_DIM, 1)
        c0 = (kv * Q_PER_KV // 2 + pr) * 128
        o_ref[:, c0:c0 + 128] = jnp.where(lo, r0, r1)
    yield


def _attn_prompt_kernel(sink_ref, q_ref, k_ref, v_ref, o_ref):
    q = q_ref[...] * np.float32(HEAD_DIM ** -0.5)
    _attend(q, [(k_ref[...], v_ref[...], None)], sink_ref, o_ref)


def _attn_latent_kernel(sink_ref, q_ref, kp_ref, kc_ref, kn_ref, vp_ref, vc_ref, vn_ref,
                        ck_ref, cv_ref, cos_ref, sin_ref, o_ref):
    j = pl.program_id(1)
    nq = pl.num_programs(1)
    qstart = pl.multiple_of((j + 1) * ATT_BLOCK, ATT_BLOCK)
    kstart = pl.multiple_of(j * ATT_BLOCK, ATT_BLOCK)
    cq = cos_ref[pl.ds(qstart, ATT_BLOCK), :]
    sq = sin_ref[pl.ds(qstart, ATT_BLOCK), :]
    q = _rope(q_ref[...], jnp.concatenate([cq] * 4, axis=1), jnp.concatenate([sq] * 4, axis=1))
    q = q * np.float32(HEAD_DIM ** -0.5)
    k_loc = jnp.concatenate([kp_ref[...], kc_ref[...], kn_ref[...]], axis=0)
    v_loc = jnp.concatenate([vp_ref[...], vc_ref[...], vn_ref[...]], axis=0)
    k_loc = _rope(k_loc, cos_ref[pl.ds(kstart, 3 * ATT_BLOCK), :], sin_ref[pl.ds(kstart, 3 * ATT_BLOCK), :])
    r = lax.broadcasted_iota(jnp.int32, (Q_PER_KV * ATT_BLOCK, 3 * ATT_BLOCK), 0) & (ATT_BLOCK - 1)
    c = lax.broadcasted_iota(jnp.int32, (Q_PER_KV * ATT_BLOCK, 3 * ATT_BLOCK), 1)
    rel = c - ATT_BLOCK - r
    c_lo = jnp.where(j > 0, 0, ATT_BLOCK)
    c_hi = jnp.where(j < nq - 1, 3 * ATT_BLOCK, 2 * ATT_BLOCK)
    ok = (jnp.abs(rel) <= WINDOW) & (c >= c_lo) & (c < c_hi)
    _attend(q, [(ck_ref[0], cv_ref[0], None), (k_loc, v_loc, ok)], sink_ref, o_ref)


def _rope_tables():
    pos = np.arange(-ATT_BLOCK, DEC_SEQ + ATT_BLOCK)
    nf = HEAD_DIM // 4
    inv = (1.0 / (ROPE_BASE ** (np.arange(nf, dtype=np.float32) / nf))).astype(np.float32)
    rows = (pos // GRID_W).astype(np.float32)
    cols = (pos % GRID_W).astype(np.float32)
    ar = rows[:, None] * inv[None, :]
    ac = cols[:, None] * inv[None, :]
    cos64 = np.concatenate([np.cos(ar), np.cos(ar), np.cos(ac), np.cos(ac)], axis=1)
    sin64 = np.concatenate([-np.sin(ar), np.sin(ar), -np.sin(ac), np.sin(ac)], axis=1)
    cos = np.concatenate([cos64, cos64], axis=1).astype(np.float32)
    sin = np.concatenate([sin64, sin64], axis=1).astype(np.float32)
    return jnp.asarray(cos), jnp.asarray(sin)


def _attention(aq, ak, av, ck, cv, sink):
    smem = pl.BlockSpec(memory_space=pltpu.SMEM)
    nqp = SEQ // ATT_BLOCK
    o_prompt = pl.pallas_call(
        _attn_prompt_kernel,
        grid=(BATCH, nqp),
        in_specs=[
            smem,
            pl.BlockSpec((ATT_BLOCK, ATT_W), lambda b, j: (b * nqp + j, 0)),
            pl.BlockSpec((SEQ, KV_W), lambda b, j: (b, 0)),
            pl.BlockSpec((SEQ, KV_W), lambda b, j: (b, 0)),
        ],
        out_specs=pl.BlockSpec((ATT_BLOCK, ATT_W), lambda b, j: (b * nqp + j, 0)),
        out_shape=jax.ShapeDtypeStruct((N_PROMPT, ATT_W), F32),
        compiler_params=_cparams("parallel", "parallel"),
        name="attn_prompt",
    )(sink, aq, ak, av)

    nq = DEC_SEQ // ATT_BLOCK
    base = N_PROMPT // ATT_BLOCK
    cos, sin = _rope_tables()
    kv_blk = lambda d: pl.BlockSpec(
        (ATT_BLOCK, KV_W), lambda b, j: (base + b * nq + jnp.clip(j + d, 0, nq - 1), 0))
    tab = pl.BlockSpec((DEC_SEQ + 2 * ATT_BLOCK, 128), lambda b, j: (0, 0))
    o_latent = pl.pallas_call(
        _attn_latent_kernel,
        grid=(DEC_BATCH, nq),
        in_specs=[
            smem,
            pl.BlockSpec((ATT_BLOCK, ATT_W), lambda b, j: (base + b * nq + j, 0)),
            kv_blk(-1), kv_blk(0), kv_blk(1), kv_blk(-1), kv_blk(0), kv_blk(1),
            pl.BlockSpec((1, PAST_LEN, KV_W), lambda b, j: (b, 0, 0)),
            pl.BlockSpec((1, PAST_LEN, KV_W), lambda b, j: (b, 0, 0)),
            tab, tab,
        ],
        out_specs=pl.BlockSpec((ATT_BLOCK, ATT_W), lambda b, j: (b * nq + j, 0)),
        out_shape=jax.ShapeDtypeStruct((N_SAMPLE, ATT_W), F32),
        compiler_params=_cparams("parallel", "parallel"),
        name="attn_latent",
    )(sink, aq, ak, ak, ak, av, av, av, ck, cv, cos, sin)
    return o_prompt, o_latent


def _even_mix(refs, mod):
    xp_ref, xs_ref, of_ref, ob_ref, zg_ref, oattp_ref, oatts_ref, gn_ref, w_ref = refs
    x = _token_tile((xp_ref, xs_ref))
    oatt = _token_tile((oattp_ref, oatts_ref))
    o = of_ref[...] + ob_ref[...]
    parts = []
    for h in range(HG_HEADS):
        oh = o[:, h * HG_DV:(h + 1) * HG_DV]
        parts.append(oh * lax.rsqrt(jnp.mean(oh * oh, axis=-1, keepdims=True) + EPS))
    o_hg = jnp.concatenate(parts, axis=1) * gn_ref[...] * _silu(zg_ref[...])
    y = _dot(o_hg.astype(BF16), w_ref[0:HG_W, :]) + _dot(oatt.astype(BF16), w_ref[HG_W:, :])
    return x + mod[2:3, :] * y


def _even_mix_operands(x_pair, o_f, o_b, zg, oatt_pair, gn_g, w_bf16):
    tile = lambda w: pl.BlockSpec((TB, w), lambda i: (i, 0))
    specs = (_token_specs(x_pair, D_MODEL) + [tile(HG_W), tile(HG_W), tile(HG_W)] + _token_specs(oatt_pair, ATT_W)
             + [pl.BlockSpec((1, HG_W), lambda i: (0, 0)), pl.BlockSpec((HG_W + ATT_W, D_MODEL), lambda i: (0, 0))])
    return specs, (*x_pair, o_f, o_b, zg, *oatt_pair, gn_g.reshape(1, HG_W), w_bf16)


def _odd_in_kernel(x_ref, xp_ref, xn_ref, mod_ref, g_ref, w_ref, cw_ref, cb_ref, gate_ref, xc_ref):
    i = pl.program_id(0)
    in_sample = i >= PROMPT_TILES
    pos = (i - PROMPT_TILES) % TILES_PER_SAMPLE
    has_prev = in_sample & (pos != 0)
    has_next = in_sample & (pos != TILES_PER_SAMPLE - 1)
    mod = mod_ref[0]
    g = g_ref[...]
    h = jnp.concatenate([_rms_mod(r[...], g, mod, 0, 1) for r in (xp_ref, x_ref, xn_ref)], axis=0).astype(BF16)
    n = TB + 16
    row = lax.broadcasted_iota(jnp.int32, (n, CONV_SLAB), 0)
    live = (row >= jnp.where(has_prev, 0, 8)) & (row < jnp.where(has_next, n, 8 + TB))
    gate_ref[...] = _dot(h[8:8 + TB], w_ref[:, :D_RNN])
    for c0 in range(D_RNN, 2 * D_RNN, CONV_SLAB):
        ext = jnp.where(live, _dot(h, w_ref[:, c0:c0 + CONV_SLAB]), 0.0)
        cols = slice(c0 - D_RNN, c0 - D_RNN + CONV_SLAB)
        w = cw_ref[:, cols]
        acc = ext[8:8 + TB] * w[2:3, :] + cb_ref[:, cols]
        acc = acc + pltpu.roll(ext, 2, 0)[8:8 + TB] * w[0:1, :]
        acc = acc + pltpu.roll(ext, 1, 0)[8:8 + TB] * w[1:2, :]
        acc = acc + pltpu.roll(ext, n - 1, 0)[8:8 + TB] * w[3:4, :]
        xc_ref[:, cols] = acc


def _odd_in(x, modt, g, w_bf16, cw, cb):
    r8 = TB // 8
    last8 = N_TOK // 8 - 1
    tile = pl.BlockSpec((TB, D_MODEL), lambda i: (i, 0))
    return pl.pallas_call(
        _odd_in_kernel,
        grid=(N_TILES,),
        in_specs=[
            tile,
            pl.BlockSpec((8, D_MODEL), lambda i: (jnp.maximum(i * r8 - 1, 0), 0)),
            pl.BlockSpec((8, D_MODEL), lambda i: (jnp.minimum((i + 1) * r8, last8), 0)),
            pl.BlockSpec((1, 6, D_MODEL), lambda i: (i, 0, 0)),
            pl.BlockSpec((1, D_MODEL), lambda i: (0, 0)),
            pl.BlockSpec((D_MODEL, 2 * D_RNN), lambda i: (0, 0)),
            pl.BlockSpec((4, D_RNN), lambda i: (0, 0)),
            pl.BlockSpec((1, D_RNN), lambda i: (0, 0)),
        ],
        out_specs=[tile, tile],
        out_shape=[jax.ShapeDtypeStruct((N_TOK, D_RNN), F32), jax.ShapeDtypeStruct((N_TOK, D_RNN), F32)],
        compiler_params=_cparams("parallel"),
        name="odd_in",
    )(x, x, x, modt, g.reshape(1, D_MODEL), w_bf16, cw, cb.reshape(1, D_RNN))


def _log1p(y):
    w = 1.0 + y
    return jnp.where(w == 1.0, y, jnp.log(w) * (y / (w - 1.0)))


def _expm1_given(x, e):
    return jnp.where(e == 1.0, x, (e - 1.0) * (x / jnp.log(e)))


def _lru_direction(xc, wa_ref, ba, wx_ref, bx, lam, h_ref, o_ref, tm_refs, fwd):
    n = LRU_CHUNK
    xb = xc.astype(BF16)
    ra, ri = [], []
    for blk in range(RG_BLOCKS):
        sl = slice(blk * RG_BW, (blk + 1) * RG_BW)
        ra.append(_dot(xb[:, sl], wa_ref[blk]))
        ri.append(_dot(xb[:, sl], wx_ref[blk]))
    yield
    r = _sigmoid_tanh(jnp.concatenate(ra, axis=1) + ba)
    gi = _sigmoid_tanh(jnp.concatenate(ri, axis=1) + bx)
    neg = -lam
    softplus = jnp.maximum(neg, 0.0) + _log1p(jnp.exp(-jnp.abs(neg)))
    log_a = (-RG_C * softplus) * r
    a = jnp.exp(log_a)
    u = jnp.sqrt(-_expm1_given(2.0 * log_a, a * a)) * (gi * xc)
    a_t, u_t, h_t = tm_refs
    n_tiles = D_RNN // 128
    for g in range(n // SUBLANES):
        for j in range(n_tiles):
            rows = pl.ds(g * SUBLANES * n_tiles + j, SUBLANES, stride=n_tiles)
            a_t[rows, :] = a[g * SUBLANES:(g + 1) * SUBLANES, j * 128:(j + 1) * 128]
            u_t[rows, :] = u[g * SUBLANES:(g + 1) * SUBLANES, j * 128:(j + 1) * 128]
    yield
    h = h_ref[...]
    for t in (range(n) if fwd else reversed(range(n))):
        step = slice(t * n_tiles, (t + 1) * n_tiles)
        h = a_t[step, :] * h + u_t[step, :]
        h_t[step, :] = h
        if t % SUBLANES == 0:
            yield
    h_ref[...] = h
    for g in range(n // SUBLANES):
        for j in range(n_tiles):
            rows = pl.ds(g * SUBLANES * n_tiles + j, SUBLANES, stride=n_tiles)
            o_ref[g * SUBLANES:(g + 1) * SUBLANES, j * 128:(j + 1) * 128] = h_t[rows, :]


def _lru_kernel(tbl_ref, xf_ref, xb_ref, wa_ref, ba_ref, wx_ref, bx_ref, lam_ref, h0f_ref, h0b_ref,
                hf_ref, hb_ref, hf_s, hb_s, *tm):
    s = pl.program_id(0)

    @pl.when(tbl_ref[4, s] == 1)
    def _():
        hf_s[...] = h0f_ref[0]
        hb_s[...] = h0b_ref[0]

    _lockstep([
        _lru_direction(xf_ref[...], wa_ref.at[0], ba_ref[0:1, :], wx_ref.at[0], bx_ref[0:1, :],
                       lam_ref[0:1, :], hf_s, hf_ref, tm[0:3], True),
        _lru_direction(xb_ref[...], wa_ref.at[1], ba_ref[1:2, :], wx_ref.at[1], bx_ref[1:2, :],
                       lam_ref[1:2, :], hb_s, hb_ref, tm[3:6], False)])


def _lru_scan(xc, w_a_bf16, b_a, w_x_bf16, b_x, lam, h0f_all, h0b_all):
    tbl = _scan_table(LRU_CHUNK)
    n_steps = tbl.shape[1]
    blk = lambda r: pl.BlockSpec((LRU_CHUNK, D_RNN), lambda s, t: (t[r, s], 0))
    wspec = pl.BlockSpec((2, RG_BLOCKS, RG_BW, RG_BW), lambda s, t: (0, 0, 0, 0))
    vec2 = pl.BlockSpec((2, D_RNN), lambda s, t: (0, 0))
    h0 = pl.BlockSpec((1, SUBLANES, D_RNN // SUBLANES), lambda s, t: (t[2, s], 0, 0))
    state = lambda: pltpu.VMEM((SUBLANES, D_RNN // SUBLANES), F32)
    time_major = lambda: pltpu.VMEM((LRU_CHUNK * D_RNN // 128, 128), F32)
    grid_spec = pltpu.PrefetchScalarGridSpec(
        num_scalar_prefetch=1,
        grid=(n_steps,),
        in_specs=[blk(0), blk(1), wspec, vec2, wspec, vec2, vec2, h0, h0],
        out_specs=[blk(0), blk(1)],
        scratch_shapes=[state(), state()] + [time_major() for _ in range(6)],
    )
    return pl.pallas_call(
        _lru_kernel,
        grid_spec=grid_spec,
        out_shape=[jax.ShapeDtypeStruct((N_TOK, D_RNN), F32), jax.ShapeDtypeStruct((N_TOK, D_RNN), F32)],
        compiler_params=_cparams("arbitrary"),
        name="lru_scan",
    )(jnp.asarray(tbl), xc, xc, w_a_bf16, b_a, w_x_bf16, b_x, lam,
      h0f_all.reshape(-1, SUBLANES, D_RNN // SUBLANES), h0b_all.reshape(-1, SUBLANES, D_RNN // SUBLANES))


def _odd_mix(refs, mod):
    x_ref, hf_ref, hb_ref, gate_ref, w_ref = refs
    y = (hf_ref[...] + hb_ref[...]) * _gelu_tanh(gate_ref[...])
    return x_ref[...] + mod[2:3, :] * _dot(y.astype(BF16), w_ref[...])


def _odd_mix_operands(x, hf, hb, gate, w_bf16):
    tile = pl.BlockSpec((TB, D_MODEL), lambda i: (i, 0))
    return [tile, tile, tile, tile, pl.BlockSpec((D_RNN, D_MODEL), lambda i: (0, 0))], (x, hf, hb, gate, w_bf16)


def _order_key(x):
    b = lax.bitcast_convert_type(x, jnp.int32)
    return b ^ ((b >> 31) & 0x7FFFFFFF)


def _key_value(k):
    return lax.bitcast_convert_type(k ^ ((k >> 31) & 0x7FFFFFFF), F32)


def _emitted_values(bits_as_f32):
    return _key_value(lax.bitcast_convert_type(bits_as_f32, jnp.int32))


def _precedes(ky, ry, kx, rx):
    return (ky + jnp.where(ry < rx, 1, 0)) > kx


def _merge_exchange(n):
    pairs = []
    t = max(1, (n - 1).bit_length())
    p = 1 << (t - 1)
    while p > 0:
        q, r, d = 1 << (t - 1), 0, p
        while d > 0:
            pairs.extend((i, i + d) for i in range(n - d) if (i & p) == r)
            d, q, r = q - p, q >> 1, p
        p >>= 1
    return pairs


def _top_rows(values, count, emit):
    n = len(values)
    sub = lax.broadcasted_iota(jnp.int32, (SUBLANES, TB), 0)
    keys = [_order_key(v) for v in values]
    rows = [sub + SUBLANES * i for i in range(n)]
    for i, j in _merge_exchange(n):
        swap = _precedes(keys[j], rows[j], keys[i], rows[i])
        keys[i], keys[j] = jnp.where(swap, keys[j], keys[i]), jnp.where(swap, keys[i], keys[j])
        rows[i], rows[j] = jnp.where(swap, rows[j], rows[i]), jnp.where(swap, rows[i], rows[j])
    lowest = jnp.full((SUBLANES, TB), jnp.iinfo(jnp.int32).min + 1, jnp.int32)
    yield
    for step in range(count):
        k = keys[0]
        for shift in (4, 2, 1):
            k = jnp.maximum(k, pltpu.roll(k, shift, 0))
        r = jnp.where(keys[0] == k, rows[0], jnp.iinfo(jnp.int32).max)
        for shift in (4, 2, 1):
            r = jnp.minimum(r, pltpu.roll(r, shift, 0))
        emit(step, lax.bitcast_convert_type(k[0:1, :], F32), r[0:1, :])
        depth = min(n - 1, count - 1 - step)
        column = rows[0] == r
        for d in range(depth):
            keys[d] = jnp.where(column, keys[d + 1], keys[d])
            rows[d] = jnp.where(column, rows[d + 1], rows[d])
        if depth == n - 1:
            keys[n - 1] = jnp.where(column, lowest, keys[n - 1])
        yield


def _interleave(generators):
    live = list(generators)
    while live:
        live = [g for g in live if next(g, StopIteration) is not StopIteration]
        yield


def _lockstep(generators):
    for _ in _interleave(generators):
        pass


def _router_kernel(*refs, n_mix, mix_fn):
    mix_refs = refs[:n_mix]
    mod_ref, g_ref, wq_ref, keys_ref, x_ref, h_ref, a_ref, b_ref, gate_ref, q_s = refs[n_mix:n_mix + 10]
    scratch = refs[n_mix + 10:]
    x = mix_fn(mix_refs, mod_ref[0])
    x_ref[...] = x
    h = _rms_mod(x, g_ref[...], mod_ref[0], 3, 4).astype(BF16)
    h_ref[...] = h
    q_s[...] = _dot(h, wq_ref[...])
    def route_heads(trip, carry):
        _lockstep([_route_head(keys_ref, a_ref, b_ref, gate_ref, q_s, [s.at[j] for s in scratch],
                               trip * HEADS_PER_TRIP + j) for j in range(HEADS_PER_TRIP)])
        return carry

    lax.fori_loop(0, PEER_HEADS // HEADS_PER_TRIP, route_heads, 0)


def _route_head(keys_ref, a_ref, b_ref, gate_ref, q_s, scratch, hd):
    v1_s, i1_s, v2_s, i2_s, sc_s, a_s, b_s = scratch
    searches = []
    for half, (v_s, i_s) in enumerate(((v1_s, i1_s), (v2_s, i2_s))):
        col = pl.multiple_of(hd * PEER_DK + half * (PEER_DK // 2), PEER_DK // 2)
        qh = q_s[:, pl.ds(col, PEER_DK // 2)].astype(BF16)
        st = _dot_nt(keys_ref[half, hd], qh)

        def emit(step, value, row, v_s=v_s, i_s=i_s):
            v_s[step:step + 1, :] = value
            i_s[step:step + 1, :] = row

        blocks = [st[SUBLANES * i:SUBLANES * (i + 1), :] for i in range(PEER_NK // SUBLANES)]
        searches.append(_top_rows(blocks, PEER_TOPK, emit))
    yield from _interleave(searches)

    v1 = _emitted_values(v1_s[...])
    v2 = _emitted_values(v2_s[...])
    sub8 = lax.broadcasted_iota(jnp.int32, (8, TB), 0)
    pieces = [v1[0:1, :] + v2[0:8, :], v1[0:1, :] + v2[8:16, :]]
    for p in range(1, 8):
        pieces.append(jnp.where(sub8 < PEER_TOPK // (p + 1), v1[p:p + 1, :] + v2[0:8, :], -jnp.inf))
    pieces.append(v1[8:16, :] + v2[0:1, :])
    i1 = i1_s[...]
    i2 = i2_s[...]

    def emit2(step, value, pos):
        sc_s[step:step + 1, :] = value
        a_s[step:step + 1, :] = pos

    yield from _top_rows(pieces, PEER_TOPK, emit2)
    pos = a_s[...]
    p = jnp.where(pos < 72, jnp.maximum((pos - 8) >> 3, 0), pos - 64)
    r = jnp.where(pos < 16, pos, jnp.where(pos < 72, pos & 7, 0))
    a_idx = jnp.zeros((PEER_TOPK, TB), jnp.int32)
    b_idx = jnp.zeros((PEER_TOPK, TB), jnp.int32)
    for c in range(PEER_TOPK):
        a_idx = jnp.where(p == c, i1[c:c + 1, :], a_idx)
        b_idx = jnp.where(r == c, i2[c:c + 1, :], b_idx)
    sc = _emitted_values(sc_s[...])
    e = jnp.exp(sc - sc[0:1, :])
    out_rows = pl.ds(pl.multiple_of(hd * PEER_TOPK, PEER_TOPK), PEER_TOPK)
    gate_ref[0, out_rows, :] = e / e.sum(axis=0, keepdims=True)
    a_ref[0, out_rows, :] = a_idx.astype(F32)
    b_ref[0, out_rows, :] = b_idx.astype(F32)


def _router(mix, modt, g, wq_bf16, keys_bf16, layer):
    mix_fn, (mix_specs, mix_args) = mix
    kt = pl.BlockSpec((1, PEER_HEADS * PEER_TOPK, TB), lambda i: (i, 0, 0))
    kshape = jax.ShapeDtypeStruct((N_TILES, PEER_HEADS * PEER_TOPK, TB), F32)
    f32s = lambda: pltpu.VMEM((HEADS_PER_TRIP, PEER_TOPK, TB), F32)
    i32s = lambda: pltpu.VMEM((HEADS_PER_TRIP, PEER_TOPK, TB), jnp.int32)
    tile = pl.BlockSpec((TB, D_MODEL), lambda i: (i, 0))
    return pl.pallas_call(
        functools.partial(_router_kernel, n_mix=len(mix_args), mix_fn=mix_fn),
        grid=(N_TILES,),
        in_specs=list(mix_specs) + [
            pl.BlockSpec((1, 6, D_MODEL), lambda i: (i, 0, 0)),
            pl.BlockSpec((1, D_MODEL), lambda i: (0, 0)),
            pl.BlockSpec((None, D_MODEL, PEER_HEADS * PEER_DK), lambda i: (layer, 0, 0)),
            pl.BlockSpec((None, 2, PEER_HEADS, PEER_NK, PEER_DK // 2), lambda i: (layer, 0, 0, 0, 0)),
        ],
        out_specs=[tile, tile, kt, kt, kt],
        out_shape=[jax.ShapeDtypeStruct((N_TOK, D_MODEL), F32), jax.ShapeDtypeStruct((N_TOK, D_MODEL), BF16),
                   kshape, kshape, kshape],
        scratch_shapes=[pltpu.VMEM((TB, PEER_HEADS * PEER_DK), F32),
                        f32s(), i32s(), f32s(), i32s(), f32s(), i32s(), i32s()],
        compiler_params=_cparams("parallel"),
        name="peer_router",
    )(*mix_args, modt, g.reshape(1, D_MODEL), wq_bf16, keys_bf16)


def _expert_kernel(h_ref, x_ref, mod_ref, at_ref, bt_ref, gt_ref, u_ref, v_ref, fg_ref, o_ref, *rest, final):
    if final:
        y_ref, g_s, acc_s, a_s, b_s, w_s, rt_s = rest
    else:
        g_s, acc_s, a_s, b_s, w_s, rt_s = rest
    grp = pl.program_id(1)

    @pl.when(grp == 0)
    def _():
        acc_s[...] = jnp.zeros_like(acc_s)
        for part in range(TE // TB):
            a_s[part * TB:(part + 1) * TB, :] = at_ref[part].T
            b_s[part * TB:(part + 1) * TB, :] = bt_ref[part].T
            w_s[part * TB:(part + 1) * TB, :] = gt_ref[part].T
        sub = lax.broadcasted_iota(jnp.int32, (PEER_NK, PEER_NK), 0).astype(F32).astype(BF16)
        one = jnp.ones((PEER_NK, PEER_NK), BF16)
        zero = jnp.zeros((PEER_NK, PEER_NK), BF16)

        def gate_grid(t, slot):
            a_row = a_s[pl.ds(t, 1), :].astype(BF16)
            b_row = b_s[pl.ds(t, 1), :].astype(BF16)
            w_row = jnp.broadcast_to(w_s[pl.ds(t, 1), :].astype(BF16), (PEER_NK, PEER_NK))
            m_t = jnp.where(a_row == sub, one, zero)
            r_t = jnp.where(b_row == sub, w_row, zero)
            yield
            rt_s[slot] = r_t.T
            yield
            g = _dot(m_t, rt_s[slot]).astype(BF16).astype(F32)
            yield lax.bitcast_convert_type(g, jnp.uint32)

        def body(trip, carry):
            t0 = trip * G_UNROLL
            grids = [gate_grid(t0 + j + half * (TE // 2), j + half * G_UNROLL)
                     for j in range(G_UNROLL) for half in range(2)]
            for _ in range(2):
                for grid in grids:
                    next(grid)
            bits = [next(grid) for grid in grids]
            for j in range(G_UNROLL):
                word = (bits[2 * j] >> 16) | (bits[2 * j + 1] & jnp.uint32(0xFFFF0000))
                g_s[pl.ds(t0 + j, PEER_NK, stride=G_PITCH), :] = word
            return carry

        lax.fori_loop(0, TE // 2 // G_UNROLL, body, 0)

    h = h_ref[...]
    acc = acc_s[...]
    for sb in range(PEER_GROUP // EXPERT_SUB):
        e0 = sb * EXPERT_SUB
        hmat = _dot_nt(h, u_ref[e0:e0 + EXPERT_SUB, :])
        parts = []
        for j in range(EXPERT_SUB // PEER_NK):
            chunk = grp * CHUNKS_PER_GROUP + sb * (EXPERT_SUB // PEER_NK) + j
            word = g_s[pl.ds(pl.multiple_of(chunk * G_PITCH, 8), TE // 2), :]
            gates = jnp.concatenate(
                [lax.bitcast_convert_type(word << 16, F32),
                 lax.bitcast_convert_type(word & jnp.uint32(0xFFFF0000), F32)], axis=0)
            parts.append((_gelu_tanh(hmat[:, j * PEER_NK:(j + 1) * PEER_NK]) * gates).astype(BF16))
        acc = acc + _dot(jnp.concatenate(parts, axis=1), v_ref[e0:e0 + EXPERT_SUB, :])
    acc_s[...] = acc

    @pl.when(grp == PEER_NGROUPS - 1)
    def _():
        xo = x_ref[...] + mod_ref[0][5:6, :] * acc_s[...]
        if final:
            y = xo * lax.rsqrt(jnp.mean(xo * xo, axis=-1, keepdims=True) + EPS) * fg_ref[...]
            tile = pl.program_id(0)

            @pl.when(tile < N_PROMPT // TE)
            def _():
                o_ref[...] = y

            @pl.when(tile >= N_PROMPT // TE)
            def _():
                y_ref[...] = y
        else:
            o_ref[...] = xo


def _experts(h_bf16, x, modt_e, a_t, b_t, g_t, u_bf16, v_bf16, layer, final_g, final):
    tile = pl.BlockSpec((TE, D_MODEL), lambda i, g: (i, 0))
    kt = pl.BlockSpec((TE // TB, PEER_HEADS * PEER_TOPK, TB), lambda i, g: (i, 0, 0))
    wblk = pl.BlockSpec((None, PEER_GROUP, D_MODEL), lambda i, g: (layer, g, 0))
    xs = jax.ShapeDtypeStruct((N_TOK, D_MODEL), F32)
    n_prompt = N_PROMPT // TE
    sel = lambda: pltpu.VMEM((TE, PEER_HEADS * PEER_TOPK), F32)
    return pl.pallas_call(
        functools.partial(_expert_kernel, final=final),
        grid=(N_TOK // TE, PEER_NGROUPS),
        in_specs=[tile, tile, pl.BlockSpec((1, 6, D_MODEL), lambda i, g: (i, 0, 0)), kt, kt, kt, wblk, wblk,
                  pl.BlockSpec((1, D_MODEL), lambda i, g: (0, 0))],
        out_specs=[pl.BlockSpec((TE, D_MODEL), lambda i, g: (jnp.minimum(i, n_prompt - 1), 0)),
                   pl.BlockSpec((TE, D_MODEL), lambda i, g: (jnp.maximum(i - n_prompt, 0), 0))] if final else tile,
        out_shape=[jax.ShapeDtypeStruct((N_PROMPT, D_MODEL), F32),
                   jax.ShapeDtypeStruct((N_SAMPLE, D_MODEL), F32)] if final else xs,
        scratch_shapes=[pltpu.VMEM((PEER_NK * G_PITCH, PEER_NK), jnp.uint32), pltpu.VMEM((TE, D_MODEL), F32),
                        sel(), sel(), sel(), pltpu.VMEM((2 * G_UNROLL, PEER_NK, PEER_NK), BF16)],
        compiler_params=_cparams("arbitrary" if final else "parallel", "arbitrary"),
        name="peer_experts_final" if final else "peer_experts",
    )(h_bf16, x, modt_e, a_t, b_t, g_t, u_bf16, v_bf16, final_g.reshape(1, D_MODEL))


def _peer(mix, modt, modt_e, norm_g, peer_bf16, layer, final_g, final):
    w_q, keys, u, v = peer_bf16
    x, h, a_t, b_t, g_t = _router(mix, modt, norm_g, w_q, keys, layer)
    return _experts(h, x, modt_e, a_t, b_t, g_t, u, v, layer, final_g, final)


def kernel(x_prompt, x_sample, cache_k, cache_v, state_hgrn_fwd, state_hgrn_bwd, state_lru_fwd, state_lru_bwd, c, c_ctx, norm1_g, norm2_g, w_mod, b_mod, w_in_even, w_out_even, hgrn_lb_fwd, hgrn_lb_bwd, hgrn_gnorm_g, attn_sink, w_in_odd, w_out_odd, conv_w, conv_b, rg_w_a, rg_b_a, rg_w_x, rg_b_x, rg_lambda, peer_w_q, peer_keys, peer_u, peer_v, final_g):
    x = (x_prompt.reshape(N_PROMPT, D_MODEL), x_sample.reshape(N_SAMPLE, D_MODEL))
    cond = jnp.concatenate([c_ctx[None, :], c, jnp.zeros((COND_PAD - N_COND, D_MODEL), F32)], axis=0)
    mod = _modulation(cond, w_mod, b_mod)
    tile_row = np.asarray([0] * PROMPT_TILES + [1 + i // TILES_PER_SAMPLE for i in range(N_TILES - PROMPT_TILES)])
    modt = [mod[l][tile_row].reshape(N_TILES, 6, D_MODEL) for l in range(DEPTH)]
    modt_e = [m[::TE // TB] for m in modt]

    widths = (HG_W, HG_W, HG_W, HG_W, HG_W, ATT_W, KV_W, KV_W)
    zq, zf, zb, zi, zg, aq, ak, av = _in_proj(x, modt[0], norm1_g[0], w_in_even[0].astype(BF16), widths, "even_in")
    zero_state = jnp.zeros((1, HG_HEADS, HG_DV, HG_DK), F32)
    s0f = jnp.concatenate([zero_state, jnp.swapaxes(state_hgrn_fwd[:, 0], -1, -2)], axis=0)
    s0b = jnp.concatenate([zero_state, jnp.swapaxes(state_hgrn_bwd[:, 0], -1, -2)], axis=0)
    o_f, o_b, sf_t, sb_t = _hgrn_scan(zq, zf, zb, zi, hgrn_lb_fwd, hgrn_lb_bwd, s0f, s0b)
    oatt = _attention(aq, ak, av, cache_k[:, 0].reshape(DEC_BATCH, PAST_LEN, KV_W),
                      cache_v[:, 0].reshape(DEC_BATCH, PAST_LEN, KV_W), attn_sink[0])
    even_mix = (_even_mix, _even_mix_operands(x, o_f, o_b, zg, oatt, hgrn_gnorm_g[0], w_out_even[0].astype(BF16)))
    peer_bf16 = (peer_w_q.astype(BF16), peer_keys.astype(BF16), peer_u.astype(BF16), peer_v.astype(BF16))
    x = _peer(even_mix, modt[0], modt_e[0], norm2_g[0], peer_bf16, 0, final_g, False)

    gate, xc = _odd_in(x, modt[1], norm1_g[1], w_in_odd[0].astype(BF16), conv_w[0], conv_b[0])
    zero_h = jnp.zeros((1, 1, D_RNN), F32)
    h0f = jnp.concatenate([zero_h, state_lru_fwd[:, 0][:, None, :]], axis=0)
    h0b = jnp.concatenate([zero_h, state_lru_bwd[:, 0][:, None, :]], axis=0)
    hf, hb = _lru_scan(xc, rg_w_a[0].astype(BF16), rg_b_a[0], rg_w_x[0].astype(BF16), rg_b_x[0],
                       rg_lambda[0], h0f, h0b)
    odd_mix = (_odd_mix, _odd_mix_operands(x, hf, hb, gate, w_out_odd[0].astype(BF16)))
    y_p, y_s = _peer(odd_mix, modt[1], modt_e[1], norm2_g[1], peer_bf16, 1, final_g, True)

    y_prompt = y_p.reshape(BATCH, SEQ, D_MODEL)
    y_sample = y_s.reshape(DEC_BATCH, DEC_SEQ, D_MODEL)
    new_k = ak[:N_PROMPT].reshape(BATCH, 1, SEQ, KV_HEADS, HEAD_DIM)
    new_v = av[:N_PROMPT].reshape(BATCH, 1, SEQ, KV_HEADS, HEAD_DIM)
    new_hf = jnp.swapaxes(sf_t[:BATCH], -1, -2)[:, None]
    new_hb = jnp.swapaxes(sb_t[:BATCH], -1, -2)[:, None]
    new_lf = hf[:N_PROMPT].reshape(BATCH, SEQ, D_RNN)[:, -1][:, None, :]
    new_lb = hb[:N_PROMPT].reshape(BATCH, SEQ, D_RNN)[:, 0][:, None, :]
    return (y_prompt, y_sample, new_k, new_v, new_hf, new_hb, new_lf, new_lb)
```

```python
import functools

import numpy as np
import jax
import jax.numpy as jnp
from jax import lax
from jax.experimental import pallas as pl
from jax.experimental.pallas import tpu as pltpu

F32 = jnp.float32
BF16 = jnp.bfloat16

D_MODEL = 1024
BATCH = 16
SEQ = 256
DEPTH = 2
DEC_BATCH = 8
DEC_SEQ = 2048
PAST_LEN = 512
GRID_W = 64
EPS = 1e-6
HG_DK = 128
HG_DV = 128
HG_HEADS = 4
HG_CHUNK = 64
HG_BLOCK = 256
HG_W = HG_HEADS * HG_DK
HEAD_DIM = 64
N_HEADS = 8
KV_HEADS = 2
Q_PER_KV = 4
WINDOW = 128
ATT_BLOCK = 128
ROPE_BASE = 10000.0
ATT_W = N_HEADS * HEAD_DIM
KV_W = KV_HEADS * HEAD_DIM
D_RNN = 1024
RG_BLOCKS = 4
RG_BW = 256
RG_C = 8.0
PEER_HEADS = 8
PEER_NK = 128
PEER_DK = 256
PEER_TOPK = 16
HEADS_PER_TRIP = 8

N_PROMPT = BATCH * SEQ
N_SAMPLE = DEC_BATCH * DEC_SEQ
N_TOK = N_PROMPT + N_SAMPLE
TB = 256
N_TILES = N_TOK // TB
PROMPT_TILES = N_PROMPT // TB
TILES_PER_SAMPLE = DEC_SEQ // TB
N_COND = 1 + DEC_BATCH
COND_PAD = 16

TE = 512
PEER_GROUP = 2048
PEER_NGROUPS = PEER_NK * PEER_NK // PEER_GROUP
CHUNKS_PER_GROUP = PEER_GROUP // PEER_NK
EXPERT_SUB = 1024
G_PITCH = TE // 2 + 8
G_UNROLL = 16

LRU_CHUNK = 256
CONV_SLAB = 512
SUBLANES = 8

VMEM_LIMIT = 58 * 1024 * 1024


def _cparams(*sem):
    return pltpu.CompilerParams(dimension_semantics=sem, vmem_limit_bytes=VMEM_LIMIT)


def _dot(a, b):
    return jnp.dot(a, b, preferred_element_type=F32)


def _dot_nt(a, b):
    return lax.dot_general(a, b, (((1,), (1,)), ((), ())), preferred_element_type=F32)


def _dot_tn(a, b):
    return lax.dot_general(a, b, (((0,), (0,)), ((), ())), preferred_element_type=F32)


def _sigmoid(x):
    return 1.0 / (1.0 + jnp.exp(-x))


def _sigmoid_tanh(x):
    return 0.5 + 0.5 * jnp.tanh(0.5 * x)


def _silu(x):
    return x * _sigmoid(x)


def _gelu_tanh(x):
    c = np.float32(np.sqrt(2.0 / np.pi))
    return x * (0.5 * (1.0 + jnp.tanh(c * (x + 0.044715 * (x * x * x)))))


def _rms_mod(x, g, mod, sh, sc):
    y = x * lax.rsqrt(jnp.mean(x * x, axis=-1, keepdims=True) + EPS) * g
    return y * (1.0 + mod[sc:sc + 1, :]) + mod[sh:sh + 1, :]


def _mod_kernel(c_ref, w_ref, b_ref, o_ref):
    c = c_ref[...]
    s = _silu(c).astype(BF16)
    o_ref[0] = _dot(s, w_ref[0].astype(BF16)) + b_ref[0]


def _modulation(cond, w_mod, b_mod):
    nb = 6 * D_MODEL // 1024
    return pl.pallas_call(
        _mod_kernel,
        grid=(DEPTH, nb),
        in_specs=[
            pl.BlockSpec((COND_PAD, D_MODEL), lambda l, n: (0, 0)),
            pl.BlockSpec((1, D_MODEL, 1024), lambda l, n: (l, 0, n)),
            pl.BlockSpec((1, 1, 1024), lambda l, n: (l, 0, n)),
        ],
        out_specs=pl.BlockSpec((1, COND_PAD, 1024), lambda l, n: (l, 0, n)),
        out_shape=jax.ShapeDtypeStruct((DEPTH, COND_PAD, 6 * D_MODEL), F32),
        compiler_params=_cparams("parallel", "parallel"),
        name="modulation",
    )(cond, w_mod, b_mod.reshape(DEPTH, 1, 6 * D_MODEL))


def _token_specs(arrays, width):
    if not isinstance(arrays, tuple):
        return [pl.BlockSpec((TB, width), lambda i: (i, 0))]
    return [pl.BlockSpec((TB, width), lambda i: (jnp.minimum(i, PROMPT_TILES - 1), 0)),
            pl.BlockSpec((TB, width), lambda i: (jnp.maximum(i - PROMPT_TILES, 0), 0))]


def _token_tile(refs):
    if len(refs) == 1:
        return refs[0][...]
    return jnp.where(pl.program_id(0) < PROMPT_TILES, refs[0][...], refs[1][...])


def _in_proj_kernel(*refs, widths, n_x):
    x_refs, (mod_ref, g_ref, w_ref), o_refs = refs[:n_x], refs[n_x:n_x + 3], refs[n_x + 3:]
    h = _rms_mod(_token_tile(x_refs), g_ref[...], mod_ref[0], 0, 1)
    z = _dot(h.astype(BF16), w_ref[...])
    off = 0
    for o_ref, wd in zip(o_refs, widths):
        o_ref[...] = z[:, off:off + wd]
        off += wd


def _in_proj(x, modt, g, w_bf16, widths, name):
    n_out = sum(widths)
    xs = x if isinstance(x, tuple) else (x,)
    return pl.pallas_call(
        functools.partial(_in_proj_kernel, widths=widths, n_x=len(xs)),
        grid=(N_TILES,),
        in_specs=_token_specs(x, D_MODEL) + [
            pl.BlockSpec((1, 6, D_MODEL), lambda i: (i, 0, 0)),
            pl.BlockSpec((1, D_MODEL), lambda i: (0, 0)),
            pl.BlockSpec((D_MODEL, n_out), lambda i: (0, 0)),
        ],
        out_specs=[pl.BlockSpec((TB, wd), lambda i: (i, 0)) for wd in widths],
        out_shape=[jax.ShapeDtypeStruct((N_TOK, wd), F32) for wd in widths],
        compiler_params=_cparams("parallel"),
        name=name,
    )(*xs, modt, g.reshape(1, D_MODEL), w_bf16)


def _hgrn_chunk_terms(zq, zgate, zi, lb, fwd):
    n = HG_CHUNK
    row = lax.broadcasted_iota(jnp.int32, (n, n), 0)
    col = lax.broadcasted_iota(jnp.int32, (n, n), 1)
    keep = (col <= row) if fwd else (col >= row)
    tri = jnp.where(keep, 1.0, 0.0).astype(BF16)
    q = zq * _sigmoid_tanh(zq)
    sg = _sigmoid_tanh(zgate)
    f = lb + (1.0 - lb) * sg
    g = jnp.log(f)
    k = (1.0 - lb) * (1.0 - sg)
    g_hi = g.astype(BF16)
    g_lo = (g - g_hi.astype(F32)).astype(BF16)
    b = _dot(tri, g_hi) + _dot(tri, g_lo)
    yield
    mid = n // 2 - 1 if fwd else n // 2
    last = n - 1 if fwd else 0
    bm = b[mid:mid + 1, :]
    bl = b[last:last + 1, :]
    qi32 = q * jnp.exp(b - bm)
    ki32 = k * jnp.exp(bm - b)
    qi = qi32.astype(BF16)
    ki = ki32.astype(BF16)
    qs = (qi32 * jnp.exp(bm)).astype(BF16)
    ks = (ki32 * jnp.exp(bl - bm)).astype(BF16)
    vb = zi.astype(BF16)
    decay = jnp.exp(bl)
    yield
    heads = [slice(h * HG_DK, (h + 1) * HG_DK) for h in range(HG_HEADS)]
    a = [jnp.where(keep, _dot_nt(qi[:, sl], ki[:, sl]), 0.0).astype(BF16) for sl in heads]
    kv = [_dot_tn(vb[:, sl], ks[:, sl]) for sl in heads]
    yield
    intra = [_dot(a[h], vb[:, sl]) for h, sl in enumerate(heads)]
    yield intra, qs, kv, decay


def _lower_bound(p):
    e = jnp.exp(p - p.max(axis=0, keepdims=True))
    return e[0:1, :] / e.sum(axis=0, keepdims=True)


def _hgrn_chunk_order(fwd):
    n_sub = HG_BLOCK // HG_CHUNK
    return [slice(c * HG_CHUNK, (c + 1) * HG_CHUNK) for c in (range(n_sub) if fwd else reversed(range(n_sub)))]


def _hgrn_block_terms(zq_ref, zgate_ref, zi_ref, lb, fwd):
    return [_hgrn_chunk_terms(zq_ref[rows, :], zgate_ref[rows, :], zi_ref[rows, :], lb, fwd)
            for rows in _hgrn_chunk_order(fwd)]


def _hgrn_block_chain(terms, st_ref, o_ref, fwd):
    states = [st_ref[h] for h in range(HG_HEADS)]
    for rows, (intra, qs, kv, decay) in zip(_hgrn_chunk_order(fwd), terms):
        outs = []
        for h in range(HG_HEADS):
            sl = slice(h * HG_DK, (h + 1) * HG_DK)
            outs.append(intra[h] + _dot_nt(qs[:, sl], states[h].astype(BF16)))
            states[h] = decay[:, sl] * states[h] + kv[h]
        o_ref[rows, :] = jnp.concatenate(outs, axis=1)
    for h in range(HG_HEADS):
        st_ref[h] = states[h]


def _hgrn_kernel(tbl_ref, zqf_ref, zff_ref, zif_ref, zqb_ref, zbb_ref, zib_ref, lbf_ref, lbb_ref,
                 s0f_ref, s0b_ref, of_ref, ob_ref, sf_ref, sb_ref, stf, stb):
    s = pl.program_id(0)

    @pl.when(tbl_ref[4, s] == 1)
    def _():
        stf[...] = s0f_ref[0]
        stb[...] = s0b_ref[0]

    gens_f = _hgrn_block_terms(zqf_ref, zff_ref, zif_ref, _lower_bound(lbf_ref[...]), True)
    gens_b = _hgrn_block_terms(zqb_ref, zbb_ref, zib_ref, _lower_bound(lbb_ref[...]), False)
    for _ in range(3):
        for gen in gens_f + gens_b:
            next(gen)
    _hgrn_block_chain([next(gen) for gen in gens_f], stf, of_ref, True)
    _hgrn_block_chain([next(gen) for gen in gens_b], stb, ob_ref, False)

    @pl.when(tbl_ref[5, s] == 1)
    def _():
        sf_ref[0] = stf[...]
        sb_ref[0] = stb[...]


def _scan_table(chunk):
    rows = []
    blk = 0
    for (nseq, length, has_init) in ((BATCH, SEQ, False), (DEC_BATCH, DEC_SEQ, True)):
        nc = length // chunk
        for b in range(nseq):
            sid = b if not has_init else BATCH + b
            for i in range(nc):
                rows.append((blk + i, blk + nc - 1 - i, (1 + b) if has_init else 0, sid,
                             int(i == 0), int(i == nc - 1)))
            blk += nc
    return np.asarray(rows, np.int32).T.copy()


def _hgrn_scan(zq, zf, zb, zi, lbf, lbb, s0f_t, s0b_t):
    tbl = _scan_table(HG_BLOCK)
    n_steps = tbl.shape[1]
    n_seq = BATCH + DEC_BATCH
    blk = lambda r: pl.BlockSpec((HG_BLOCK, HG_W), lambda s, t: (t[r, s], 0))
    vec = pl.BlockSpec(lbf.shape, lambda s, t: (0, 0))
    st_in = pl.BlockSpec((1, HG_HEADS, HG_DV, HG_DK), lambda s, t: (t[2, s], 0, 0, 0))
    st_out = pl.BlockSpec((1, HG_HEADS, HG_DV, HG_DK), lambda s, t: (t[3, s], 0, 0, 0))
    grid_spec = pltpu.PrefetchScalarGridSpec(
        num_scalar_prefetch=1,
        grid=(n_steps,),
        in_specs=[blk(0), blk(0), blk(0), blk(1), blk(1), blk(1), vec, vec, st_in, st_in],
        out_specs=[blk(0), blk(1), st_out, st_out],
        scratch_shapes=[pltpu.VMEM((HG_HEADS, HG_DV, HG_DK), F32),
                        pltpu.VMEM((HG_HEADS, HG_DV, HG_DK), F32)],
    )
    st_shape = jax.ShapeDtypeStruct((n_seq, HG_HEADS, HG_DV, HG_DK), F32)
    return pl.pallas_call(
        _hgrn_kernel,
        grid_spec=grid_spec,
        out_shape=[jax.ShapeDtypeStruct((N_TOK, HG_W), F32), jax.ShapeDtypeStruct((N_TOK, HG_W), F32),
                   st_shape, st_shape],
        compiler_params=_cparams("arbitrary"),
        name="hgrn_scan",
    )(jnp.asarray(tbl), zq, zf, zi, zq, zb, zi, lbf, lbb, s0f_t, s0b_t)


def _rope(x, cos, sgn_sin):
    w = x.shape[1]
    lane = lax.broadcasted_iota(jnp.int32, x.shape, 1)
    partner = jnp.where((lane & 31) < 16, pltpu.roll(x, w - 16, 1), pltpu.roll(x, 16, 1))
    return x * cos + partner * sgn_sin


def _attend(q, key_parts, sink_ref, o_ref):
    stages = [_attend_kv(q, key_parts, sink_ref, o_ref, kv) for kv in range(KV_HEADS)]
    for _ in range(4):
        for stage in stages:
            next(stage)


def _attend_kv(q, key_parts, sink_ref, o_ref, kv):
    lane = lax.broadcasted_iota(jnp.int32, (ATT_BLOCK, 128), 1)
    lo = lane < HEAD_DIM
    own = lo if kv == 0 else ~lo
    qs = []
    for gq in range(Q_PER_KV):
        hd = kv * Q_PER_KV + gq
        pair = q[:, (hd // 2) * 128:(hd // 2 + 1) * 128]
        if hd % 2 != kv:
            pair = pltpu.roll(pair, HEAD_DIM, 1)
        qs.append(jnp.where(own, pair, 0.0))
    qst = jnp.concatenate(qs, axis=0).astype(BF16)
    logits = []
    vals = []
    for (k, v, mask) in key_parts:
        own_rows = lax.broadcasted_iota(jnp.int32, v.shape, 1) < HEAD_DIM
        own_rows = own_rows if kv == 0 else ~own_rows
        vals.append(jnp.where(own_rows, v, 1.0).astype(BF16))
        s = _dot_nt(qst, k.astype(BF16))
        if mask is not None:
            s = jnp.where(mask, s, -1e30)
        logits.append(s)
    yield
    tiles = [s[:, t * 128:(t + 1) * 128] for s in logits for t in range(s.shape[1] // 128)]
    m = functools.reduce(jnp.maximum, tiles).max(axis=-1, keepdims=True)
    sink_col = jnp.concatenate(
        [jnp.full((ATT_BLOCK, 1), sink_ref[kv * Q_PER_KV + gq], F32) for gq in range(Q_PER_KV)], axis=0)
    m = jnp.maximum(m, sink_col)
    yield
    acc = jnp.zeros((Q_PER_KV * ATT_BLOCK, 128), F32)
    for s, v in zip(logits, vals):
        acc = acc + _dot(jnp.exp(s - m).astype(BF16), v)
    yield
    den = pltpu.roll(acc, HEAD_DIM, 1) + jnp.exp(sink_col - m)
    out = acc / den
    for pr in range(Q_PER_KV // 2):
        r0 = out[(2 * pr) * ATT_BLOCK:(2 * pr + 1) * ATT_BLOCK]
        r1 = out[(2 * pr + 1) * ATT_BLOCK:(2 * pr + 2) * ATT_BLOCK]
        if kv == 0:
            r1 = pltpu.roll(r1, HEAD_DIM, 1)
        else:
            r0 = pltpu.roll(r0, HEAD_DIM, 1)
        c0 = (kv * Q_PER_KV // 2 + pr) * 128
        o_ref[:, c0:c0 + 128] = jnp.where(lo, r0, r1)
    yield


def _attn_prompt_kernel(sink_ref, q_ref, k_ref, v_ref, o_ref):
    q = q_ref[...] * np.float32(HEAD_DIM ** -0.5)
    _attend(q, [(k_ref[...], v_ref[...], None)], sink_ref, o_ref)


def _attn_latent_kernel(sink_ref, q_ref, kp_ref, kc_ref, kn_ref, vp_ref, vc_ref, vn_ref,
                        ck_ref, cv_ref, cos_ref, sin_ref, o_ref):
    j = pl.program_id(1)
    nq = pl.num_programs(1)
    qstart = pl.multiple_of((j + 1) * ATT_BLOCK, ATT_BLOCK)
    kstart = pl.multiple_of(j * ATT_BLOCK, ATT_BLOCK)
    cq = cos_ref[pl.ds(qstart, ATT_BLOCK), :]
    sq = sin_ref[pl.ds(qstart, ATT_BLOCK), :]
    q = _rope(q_ref[...], jnp.concatenate([cq] * 4, axis=1), jnp.concatenate([sq] * 4, axis=1))
    q = q * np.float32(HEAD_DIM ** -0.5)
    k_loc = jnp.concatenate([kp_ref[...], kc_ref[...], kn_ref[...]], axis=0)
    v_loc = jnp.concatenate([vp_ref[...], vc_ref[...], vn_ref[...]], axis=0)
    k_loc = _rope(k_loc, cos_ref[pl.ds(kstart, 3 * ATT_BLOCK), :], sin_ref[pl.ds(kstart, 3 * ATT_BLOCK), :])
    r = lax.broadcasted_iota(jnp.int32, (Q_PER_KV * ATT_BLOCK, 3 * ATT_BLOCK), 0) & (ATT_BLOCK - 1)
    c = lax.broadcasted_iota(jnp.int32, (Q_PER_KV * ATT_BLOCK, 3 * ATT_BLOCK), 1)
    rel = c - ATT_BLOCK - r
    c_lo = jnp.where(j > 0, 0, ATT_BLOCK)
    c_hi = jnp.where(j < nq - 1, 3 * ATT_BLOCK, 2 * ATT_BLOCK)
    ok = (jnp.abs(rel) <= WINDOW) & (c >= c_lo) & (c < c_hi)
    _attend(q, [(ck_ref[0], cv_ref[0], None), (k_loc, v_loc, ok)], sink_ref, o_ref)


def _rope_tables():
    pos = np.arange(-ATT_BLOCK, DEC_SEQ + ATT_BLOCK)
    nf = HEAD_DIM // 4
    inv = (1.0 / (ROPE_BASE ** (np.arange(nf, dtype=np.float32) / nf))).astype(np.float32)
    rows = (pos // GRID_W).astype(np.float32)
    cols = (pos % GRID_W).astype(np.float32)
    ar = rows[:, None] * inv[None, :]
    ac = cols[:, None] * inv[None, :]
    cos64 = np.concatenate([np.cos(ar), np.cos(ar), np.cos(ac), np.cos(ac)], axis=1)
    sin64 = np.concatenate([-np.sin(ar), np.sin(ar), -np.sin(ac), np.sin(ac)], axis=1)
    cos = np.concatenate([cos64, cos64], axis=1).astype(np.float32)
    sin = np.concatenate([sin64, sin64], axis=1).astype(np.float32)
    return jnp.asarray(cos), jnp.asarray(sin)


def _attention(aq, ak, av, ck, cv, sink):
    smem = pl.BlockSpec(memory_space=pltpu.SMEM)
    nqp = SEQ // ATT_BLOCK
    o_prompt = pl.pallas_call(
        _attn_prompt_kernel,
        grid=(BATCH, nqp),
        in_specs=[
            smem,
            pl.BlockSpec((ATT_BLOCK, ATT_W), lambda b, j: (b * nqp + j, 0)),
            pl.BlockSpec((SEQ, KV_W), lambda b, j: (b, 0)),
            pl.BlockSpec((SEQ, KV_W), lambda b, j: (b, 0)),
        ],
        out_specs=pl.BlockSpec((ATT_BLOCK, ATT_W), lambda b, j: (b * nqp + j, 0)),
        out_shape=jax.ShapeDtypeStruct((N_PROMPT, ATT_W), F32),
        compiler_params=_cparams("parallel", "parallel"),
        name="attn_prompt",
    )(sink, aq, ak, av)

    nq = DEC_SEQ // ATT_BLOCK
    base = N_PROMPT // ATT_BLOCK
    cos, sin = _rope_tables()
    kv_blk = lambda d: pl.BlockSpec(
        (ATT_BLOCK, KV_W), lambda b, j: (base + b * nq + jnp.clip(j + d, 0, nq - 1), 0))
    tab = pl.BlockSpec((DEC_SEQ + 2 * ATT_BLOCK, 128), lambda b, j: (0, 0))
    o_latent = pl.pallas_call(
        _attn_latent_kernel,
        grid=(DEC_BATCH, nq),
        in_specs=[
            smem,
            pl.BlockSpec((ATT_BLOCK, ATT_W), lambda b, j: (base + b * nq + j, 0)),
            kv_blk(-1), kv_blk(0), kv_blk(1), kv_blk(-1), kv_blk(0), kv_blk(1),
            pl.BlockSpec((1, PAST_LEN, KV_W), lambda b, j: (b, 0, 0)),
            pl.BlockSpec((1, PAST_LEN, KV_W), lambda b, j: (b, 0, 0)),
            tab, tab,
        ],
        out_specs=pl.BlockSpec((ATT_BLOCK, ATT_W), lambda b, j: (b * nq + j, 0)),
        out_shape=jax.ShapeDtypeStruct((N_SAMPLE, ATT_W), F32),
        compiler_params=_cparams("parallel", "parallel"),
        name="attn_latent",
    )(sink, aq, ak, ak, ak, av, av, av, ck, cv, cos, sin)
    return o_prompt, o_latent


def _even_mix(refs, mod):
    xp_ref, xs_ref, of_ref, ob_ref, zg_ref, oattp_ref, oatts_ref, gn_ref, w_ref = refs
    x = _token_tile((xp_ref, xs_ref))
    oatt = _token_tile((oattp_ref, oatts_ref))
    o = of_ref[...] + ob_ref[...]
    parts = []
    for h in range(HG_HEADS):
        oh = o[:, h * HG_DV:(h + 1) * HG_DV]
        parts.append(oh * lax.rsqrt(jnp.mean(oh * oh, axis=-1, keepdims=True) + EPS))
    o_hg = jnp.concatenate(parts, axis=1) * gn_ref[...] * _silu(zg_ref[...])
    y = _dot(o_hg.astype(BF16), w_ref[0:HG_W, :]) + _dot(oatt.astype(BF16), w_ref[HG_W:, :])
    return x + mod[2:3, :] * y


def _even_mix_operands(x_pair, o_f, o_b, zg, oatt_pair, gn_g, w_bf16):
    tile = lambda w: pl.BlockSpec((TB, w), lambda i: (i, 0))
    specs = (_token_specs(x_pair, D_MODEL) + [tile(HG_W), tile(HG_W), tile(HG_W)] + _token_specs(oatt_pair, ATT_W)
             + [pl.BlockSpec((1, HG_W), lambda i: (0, 0)), pl.BlockSpec((HG_W + ATT_W, D_MODEL), lambda i: (0, 0))])
    return specs, (*x_pair, o_f, o_b, zg, *oatt_pair, gn_g.reshape(1, HG_W), w_bf16)


def _odd_in_kernel(x_ref, xp_ref, xn_ref, mod_ref, g_ref, w_ref, cw_ref, cb_ref, gate_ref, xc_ref):
    i = pl.program_id(0)
    in_sample = i >= PROMPT_TILES
    pos = (i - PROMPT_TILES) % TILES_PER_SAMPLE
    has_prev = in_sample & (pos != 0)
    has_next = in_sample & (pos != TILES_PER_SAMPLE - 1)
    mod = mod_ref[0]
    g = g_ref[...]
    h = jnp.concatenate([_rms_mod(r[...], g, mod, 0, 1) for r in (xp_ref, x_ref, xn_ref)], axis=0).astype(BF16)
    n = TB + 16
    row = lax.broadcasted_iota(jnp.int32, (n, CONV_SLAB), 0)
    live = (row >= jnp.where(has_prev, 0, 8)) & (row < jnp.where(has_next, n, 8 + TB))
    gate_ref[...] = _dot(h[8:8 + TB], w_ref[:, :D_RNN])
    for c0 in range(D_RNN, 2 * D_RNN, CONV_SLAB):
        ext = jnp.where(live, _dot(h, w_ref[:, c0:c0 + CONV_SLAB]), 0.0)
        cols = slice(c0 - D_RNN, c0 - D_RNN + CONV_SLAB)
        w = cw_ref[:, cols]
        acc = ext[8:8 + TB] * w[2:3, :] + cb_ref[:, cols]
        acc = acc + pltpu.roll(ext, 2, 0)[8:8 + TB] * w[0:1, :]
        acc = acc + pltpu.roll(ext, 1, 0)[8:8 + TB] * w[1:2, :]
        acc = acc + pltpu.roll(ext, n - 1, 0)[8:8 + TB] * w[3:4, :]
        xc_ref[:, cols] = acc


def _odd_in(x, modt, g, w_bf16, cw, cb):
    r8 = TB // 8
    last8 = N_TOK // 8 - 1
    tile = pl.BlockSpec((TB, D_MODEL), lambda i: (i, 0))
    return pl.pallas_call(
        _odd_in_kernel,
        grid=(N_TILES,),
        in_specs=[
            tile,
            pl.BlockSpec((8, D_MODEL), lambda i: (jnp.maximum(i * r8 - 1, 0), 0)),
            pl.BlockSpec((8, D_MODEL), lambda i: (jnp.minimum((i + 1) * r8, last8), 0)),
            pl.BlockSpec((1, 6, D_MODEL), lambda i: (i, 0, 0)),
            pl.BlockSpec((1, D_MODEL), lambda i: (0, 0)),
            pl.BlockSpec((D_MODEL, 2 * D_RNN), lambda i: (0, 0)),
            pl.BlockSpec((4, D_RNN), lambda i: (0, 0)),
            pl.BlockSpec((1, D_RNN), lambda i: (0, 0)),
        ],
        out_specs=[tile, tile],
        out_shape=[jax.ShapeDtypeStruct((N_TOK, D_RNN), F32), jax.ShapeDtypeStruct((N_TOK, D_RNN), F32)],
        compiler_params=_cparams("parallel"),
        name="odd_in",
    )(x, x, x, modt, g.reshape(1, D_MODEL), w_bf16, cw, cb.reshape(1, D_RNN))


def _log1p(y):
    w = 1.0 + y
    return jnp.where(w == 1.0, y, jnp.log(w) * (y / (w - 1.0)))


def _expm1_given(x, e):
    return jnp.where(e == 1.0, x, (e - 1.0) * (x / jnp.log(e)))


def _lru_direction(xc, wa_ref, ba, wx_ref, bx, lam, h_ref, o_ref, tm_refs, fwd):
    n = LRU_CHUNK
    xb = xc.astype(BF16)
    ra, ri = [], []
    for blk in range(RG_BLOCKS):
        sl = slice(blk * RG_BW, (blk + 1) * RG_BW)
        ra.append(_dot(xb[:, sl], wa_ref[blk]))
        ri.append(_dot(xb[:, sl], wx_ref[blk]))
    yield
    r = _sigmoid_tanh(jnp.concatenate(ra, axis=1) + ba)
    gi = _sigmoid_tanh(jnp.concatenate(ri, axis=1) + bx)
    neg = -lam
    softplus = jnp.maximum(neg, 0.0) + _log1p(jnp.exp(-jnp.abs(neg)))
    log_a = (-RG_C * softplus) * r
    a = jnp.exp(log_a)
    u = jnp.sqrt(-_expm1_given(2.0 * log_a, a * a)) * (gi * xc)
    a_t, u_t, h_t = tm_refs
    n_tiles = D_RNN // 128
    for g in range(n // SUBLANES):
        for j in range(n_tiles):
            rows = pl.ds(g * SUBLANES * n_tiles + j, SUBLANES, stride=n_tiles)
            a_t[rows, :] = a[g * SUBLANES:(g + 1) * SUBLANES, j * 128:(j + 1) * 128]
            u_t[rows, :] = u[g * SUBLANES:(g + 1) * SUBLANES, j * 128:(j + 1) * 128]
    yield
    h = h_ref[...]
    for t in (range(n) if fwd else reversed(range(n))):
        step = slice(t * n_tiles, (t + 1) * n_tiles)
        h = a_t[step, :] * h + u_t[step, :]
        h_t[step, :] = h
        if t % SUBLANES == 0:
            yield
    h_ref[...] = h
    for g in range(n // SUBLANES):
        for j in range(n_tiles):
            rows = pl.ds(g * SUBLANES * n_tiles + j, SUBLANES, stride=n_tiles)
            o_ref[g * SUBLANES:(g + 1) * SUBLANES, j * 128:(j + 1) * 128] = h_t[rows, :]


def _lru_kernel(tbl_ref, xf_ref, xb_ref, wa_ref, ba_ref, wx_ref, bx_ref, lam_ref, h0f_ref, h0b_ref,
                hf_ref, hb_ref, hf_s, hb_s, *tm):
    s = pl.program_id(0)

    @pl.when(tbl_ref[4, s] == 1)
    def _():
        hf_s[...] = h0f_ref[0]
        hb_s[...] = h0b_ref[0]

    _lockstep([
        _lru_direction(xf_ref[...], wa_ref.at[0], ba_ref[0:1, :], wx_ref.at[0], bx_ref[0:1, :],
                       lam_ref[0:1, :], hf_s, hf_ref, tm[0:3], True),
        _lru_direction(xb_ref[...], wa_ref.at[1], ba_ref[1:2, :], wx_ref.at[1], bx_ref[1:2, :],
                       lam_ref[1:2, :], hb_s, hb_ref, tm[3:6], False)])


def _lru_scan(xc, w_a_bf16, b_a, w_x_bf16, b_x, lam, h0f_all, h0b_all):
    tbl = _scan_table(LRU_CHUNK)
    n_steps = tbl.shape[1]
    blk = lambda r: pl.BlockSpec((LRU_CHUNK, D_RNN), lambda s, t: (t[r, s], 0))
    wspec = pl.BlockSpec((2, RG_BLOCKS, RG_BW, RG_BW), lambda s, t: (0, 0, 0, 0))
    vec2 = pl.BlockSpec((2, D_RNN), lambda s, t: (0, 0))
    h0 = pl.BlockSpec((1, SUBLANES, D_RNN // SUBLANES), lambda s, t: (t[2, s], 0, 0))
    state = lambda: pltpu.VMEM((SUBLANES, D_RNN // SUBLANES), F32)
    time_major = lambda: pltpu.VMEM((LRU_CHUNK * D_RNN // 128, 128), F32)
    grid_spec = pltpu.PrefetchScalarGridSpec(
        num_scalar_prefetch=1,
        grid=(n_steps,),
        in_specs=[blk(0), blk(1), wspec, vec2, wspec, vec2, vec2, h0, h0],
        out_specs=[blk(0), blk(1)],
        scratch_shapes=[state(), state()] + [time_major() for _ in range(6)],
    )
    return pl.pallas_call(
        _lru_kernel,
        grid_spec=grid_spec,
        out_shape=[jax.ShapeDtypeStruct((N_TOK, D_RNN), F32), jax.ShapeDtypeStruct((N_TOK, D_RNN), F32)],
        compiler_params=_cparams("arbitrary"),
        name="lru_scan",
    )(jnp.asarray(tbl), xc, xc, w_a_bf16, b_a, w_x_bf16, b_x, lam,
      h0f_all.reshape(-1, SUBLANES, D_RNN // SUBLANES), h0b_all.reshape(-1, SUBLANES, D_RNN // SUBLANES))


def _odd_mix(refs, mod):
    x_ref, hf_ref, hb_ref, gate_ref, w_ref = refs
    y = (hf_ref[...] + hb_ref[...]) * _gelu_tanh(gate_ref[...])
    return x_ref[...] + mod[2:3, :] * _dot(y.astype(BF16), w_ref[...])


def _odd_mix_operands(x, hf, hb, gate, w_bf16):
    tile = pl.BlockSpec((TB, D_MODEL), lambda i: (i, 0))
    return [tile, tile, tile, tile, pl.BlockSpec((D_RNN, D_MODEL), lambda i: (0, 0))], (x, hf, hb, gate, w_bf16)


def _order_key(x):
    b = lax.bitcast_convert_type(x, jnp.int32)
    return b ^ ((b >> 31) & 0x7FFFFFFF)


def _key_value(k):
    return lax.bitcast_convert_type(k ^ ((k >> 31) & 0x7FFFFFFF), F32)


def _emitted_values(bits_as_f32):
    return _key_value(lax.bitcast_convert_type(bits_as_f32, jnp.int32))


def _precedes(ky, ry, kx, rx):
    return (ky + jnp.where(ry < rx, 1, 0)) > kx


def _merge_exchange(n):
    pairs = []
    t = max(1, (n - 1).bit_length())
    p = 1 << (t - 1)
    while p > 0:
        q, r, d = 1 << (t - 1), 0, p
        while d > 0:
            pairs.extend((i, i + d) for i in range(n - d) if (i & p) == r)
            d, q, r = q - p, q >> 1, p
        p >>= 1
    return pairs


def _top_rows(values, count, emit):
    n = len(values)
    sub = lax.broadcasted_iota(jnp.int32, (SUBLANES, TB), 0)
    keys = [_order_key(v) for v in values]
    rows = [sub + SUBLANES * i for i in range(n)]
    for i, j in _merge_exchange(n):
        swap = _precedes(keys[j], rows[j], keys[i], rows[i])
        keys[i], keys[j] = jnp.where(swap, keys[j], keys[i]), jnp.where(swap, keys[i], keys[j])
        rows[i], rows[j] = jnp.where(swap, rows[j], rows[i]), jnp.where(swap, rows[i], rows[j])
    lowest = jnp.full((SUBLANES, TB), jnp.iinfo(jnp.int32).min + 1, jnp.int32)
    yield
    for step in range(count):
        k = keys[0]
        for shift in (4, 2, 1):
            k = jnp.maximum(k, pltpu.roll(k, shift, 0))
        r = jnp.where(keys[0] == k, rows[0], jnp.iinfo(jnp.int32).max)
        for shift in (4, 2, 1):
            r = jnp.minimum(r, pltpu.roll(r, shift, 0))
        emit(step, lax.bitcast_convert_type(k[0:1, :], F32), r[0:1, :])
        depth = min(n - 1, count - 1 - step)
        column = rows[0] == r
        for d in range(depth):
            keys[d] = jnp.where(column, keys[d + 1], keys[d])
            rows[d] = jnp.where(column, rows[d + 1], rows[d])
        if depth == n - 1:
            keys[n - 1] = jnp.where(column, lowest, keys[n - 1])
        yield


def _interleave(generators):
    live = list(generators)
    while live:
        live = [g for g in live if next(g, StopIteration) is not StopIteration]
        yield


def _lockstep(generators):
    for _ in _interleave(generators):
        pass


def _router_kernel(*refs, n_mix, mix_fn):
    mix_refs = refs[:n_mix]
    mod_ref, g_ref, wq_ref, keys_ref, x_ref, h_ref, a_ref, b_ref, gate_ref, q_s = refs[n_mix:n_mix + 10]
    scratch = refs[n_mix + 10:]
    x = mix_fn(mix_refs, mod_ref[0])
    x_ref[...] = x
    h = _rms_mod(x, g_ref[...], mod_ref[0], 3, 4).astype(BF16)
    h_ref[...] = h
    q_s[...] = _dot(h, wq_ref[...])
    def route_heads(trip, carry):
        _lockstep([_route_head(keys_ref, a_ref, b_ref, gate_ref, q_s, [s.at[j] for s in scratch],
                               trip * HEADS_PER_TRIP + j) for j in range(HEADS_PER_TRIP)])
        return carry

    lax.fori_loop(0, PEER_HEADS // HEADS_PER_TRIP, route_heads, 0)


def _route_head(keys_ref, a_ref, b_ref, gate_ref, q_s, scratch, hd):
    v1_s, i1_s, v2_s, i2_s, sc_s, a_s, b_s = scratch
    searches = []
    for half, (v_s, i_s) in enumerate(((v1_s, i1_s), (v2_s, i2_s))):
        col = pl.multiple_of(hd * PEER_DK + half * (PEER_DK // 2), PEER_DK // 2)
        qh = q_s[:, pl.ds(col, PEER_DK // 2)].astype(BF16)
        st = _dot_nt(keys_ref[half, hd], qh)

        def emit(step, value, row, v_s=v_s, i_s=i_s):
            v_s[step:step + 1, :] = value
            i_s[step:step + 1, :] = row

        blocks = [st[SUBLANES * i:SUBLANES * (i + 1), :] for i in range(PEER_NK // SUBLANES)]
        searches.append(_top_rows(blocks, PEER_TOPK, emit))
    yield from _interleave(searches)

    v1 = _emitted_values(v1_s[...])
    v2 = _emitted_values(v2_s[...])
    sub8 = lax.broadcasted_iota(jnp.int32, (8, TB), 0)
    pieces = [v1[0:1, :] + v2[0:8, :], v1[0:1, :] + v2[8:16, :]]
    for p in range(1, 8):
        pieces.append(jnp.where(sub8 < PEER_TOPK // (p + 1), v1[p:p + 1, :] + v2[0:8, :], -jnp.inf))
    pieces.append(v1[8:16, :] + v2[0:1, :])
    i1 = i1_s[...]
    i2 = i2_s[...]

    def emit2(step, value, pos):
        sc_s[step:step + 1, :] = value
        a_s[step:step + 1, :] = pos

    yield from _top_rows(pieces, PEER_TOPK, emit2)
    pos = a_s[...]
    p = jnp.where(pos < 72, jnp.maximum((pos - 8) >> 3, 0), pos - 64)
    r = jnp.where(pos < 16, pos, jnp.where(pos < 72, pos & 7, 0))
    a_idx = jnp.zeros((PEER_TOPK, TB), jnp.int32)
    b_idx = jnp.zeros((PEER_TOPK, TB), jnp.int32)
    for c in range(PEER_TOPK):
        a_idx = jnp.where(p == c, i1[c:c + 1, :], a_idx)
        b_idx = jnp.where(r == c, i2[c:c + 1, :], b_idx)
    sc = _emitted_values(sc_s[...])
    e = jnp.exp(sc - sc[0:1, :])
    out_rows = pl.ds(pl.multiple_of(hd * PEER_TOPK, PEER_TOPK), PEER_TOPK)
    gate_ref[0, out_rows, :] = e / e.sum(axis=0, keepdims=True)
    a_ref[0, out_rows, :] = a_idx.astype(F32)
    b_ref[0, out_rows, :] = b_idx.astype(F32)


def _router(mix, modt, g, wq_bf16, keys_bf16, layer):
    mix_fn, (mix_specs, mix_args) = mix
    kt = pl.BlockSpec((1, PEER_HEADS * PEER_TOPK, TB), lambda i: (i, 0, 0))
    kshape = jax.ShapeDtypeStruct((N_TILES, PEER_HEADS * PEER_TOPK, TB), F32)
    f32s = lambda: pltpu.VMEM((HEADS_PER_TRIP, PEER_TOPK, TB), F32)
    i32s = lambda: pltpu.VMEM((HEADS_PER_TRIP, PEER_TOPK, TB), jnp.int32)
    tile = pl.BlockSpec((TB, D_MODEL), lambda i: (i, 0))
    return pl.pallas_call(
        functools.partial(_router_kernel, n_mix=len(mix_args), mix_fn=mix_fn),
        grid=(N_TILES,),
        in_specs=list(mix_specs) + [
            pl.BlockSpec((1, 6, D_MODEL), lambda i: (i, 0, 0)),
            pl.BlockSpec((1, D_MODEL), lambda i: (0, 0)),
            pl.BlockSpec((None, D_MODEL, PEER_HEADS * PEER_DK), lambda i: (layer, 0, 0)),
            pl.BlockSpec((None, 2, PEER_HEADS, PEER_NK, PEER_DK // 2), lambda i: (layer, 0, 0, 0, 0)),
        ],
        out_specs=[tile, tile, kt, kt, kt],
        out_shape=[jax.ShapeDtypeStruct((N_TOK, D_MODEL), F32), jax.ShapeDtypeStruct((N_TOK, D_MODEL), BF16),
                   kshape, kshape, kshape],
        scratch_shapes=[pltpu.VMEM((TB, PEER_HEADS * PEER_DK), F32),
                        f32s(), i32s(), f32s(), i32s(), f32s(), i32s(), i32s()],
        compiler_params=_cparams("parallel"),
        name="peer_router",
    )(*mix_args, modt, g.reshape(1, D_MODEL), wq_bf16, keys_bf16)


def _expert_kernel(h_ref, x_ref, mod_ref, at_ref, bt_ref, gt_ref, u_ref, v_ref, fg_ref, o_ref, *rest, final):
    if final:
        y_ref, g_s, acc_s, a_s, b_s, w_s, rt_s = rest
    else:
        g_s, acc_s, a_s, b_s, w_s, rt_s = rest
    grp = pl.program_id(1)

    @pl.when(grp == 0)
    def _():
        acc_s[...] = jnp.zeros_like(acc_s)
        for part in range(TE // TB):
            a_s[part * TB:(part + 1) * TB, :] = at_ref[part].T
            b_s[part * TB:(part + 1) * TB, :] = bt_ref[part].T
            w_s[part * TB:(part + 1) * TB, :] = gt_ref[part].T
        sub = lax.broadcasted_iota(jnp.int32, (PEER_NK, PEER_NK), 0).astype(F32).astype(BF16)
        one = jnp.ones((PEER_NK, PEER_NK), BF16)
        zero = jnp.zeros((PEER_NK, PEER_NK), BF16)

        def gate_grid(t, slot):
            a_row = a_s[pl.ds(t, 1), :].astype(BF16)
            b_row = b_s[pl.ds(t, 1), :].astype(BF16)
            w_row = jnp.broadcast_to(w_s[pl.ds(t, 1), :].astype(BF16), (PEER_NK, PEER_NK))
            m_t = jnp.where(a_row == sub, one, zero)
            r_t = jnp.where(b_row == sub, w_row, zero)
            yield
            rt_s[slot] = r_t.T
            yield
            g = _dot(m_t, rt_s[slot]).astype(BF16).astype(F32)
            yield lax.bitcast_convert_type(g, jnp.uint32)

        def body(trip, carry):
            t0 = trip * G_UNROLL
            grids = [gate_grid(t0 + j + half * (TE // 2), j + half * G_UNROLL)
                     for j in range(G_UNROLL) for half in range(2)]
            for _ in range(2):
                for grid in grids:
                    next(grid)
            bits = [next(grid) for grid in grids]
            for j in range(G_UNROLL):
                word = (bits[2 * j] >> 16) | (bits[2 * j + 1] & jnp.uint32(0xFFFF0000))
                g_s[pl.ds(t0 + j, PEER_NK, stride=G_PITCH), :] = word
            return carry

        lax.fori_loop(0, TE // 2 // G_UNROLL, body, 0)

    h = h_ref[...]
    acc = acc_s[...]
    for sb in range(PEER_GROUP // EXPERT_SUB):
        e0 = sb * EXPERT_SUB
        hmat = _dot_nt(h, u_ref[e0:e0 + EXPERT_SUB, :])
        parts = []
        for j in range(EXPERT_SUB // PEER_NK):
            chunk = grp * CHUNKS_PER_GROUP + sb * (EXPERT_SUB // PEER_NK) + j
            word = g_s[pl.ds(pl.multiple_of(chunk * G_PITCH, 8), TE // 2), :]
            gates = jnp.concatenate(
                [lax.bitcast_convert_type(word << 16, F32),
                 lax.bitcast_convert_type(word & jnp.uint32(0xFFFF0000), F32)], axis=0)
            parts.append((_gelu_tanh(hmat[:, j * PEER_NK:(j + 1) * PEER_NK]) * gates).astype(BF16))
        acc = acc + _dot(jnp.concatenate(parts, axis=1), v_ref[e0:e0 + EXPERT_SUB, :])
    acc_s[...] = acc

    @pl.when(grp == PEER_NGROUPS - 1)
    def _():
        xo = x_ref[...] + mod_ref[0][5:6, :] * acc_s[...]
        if final:
            y = xo * lax.rsqrt(jnp.mean(xo * xo, axis=-1, keepdims=True) + EPS) * fg_ref[...]
            tile = pl.program_id(0)

            @pl.when(tile < N_PROMPT // TE)
            def _():
                o_ref[...] = y

            @pl.when(tile >= N_PROMPT // TE)
            def _():
                y_ref[...] = y
        else:
            o_ref[...] = xo


def _experts(h_bf16, x, modt_e, a_t, b_t, g_t, u_bf16, v_bf16, layer, final_g, final):
    tile = pl.BlockSpec((TE, D_MODEL), lambda i, g: (i, 0))
    kt = pl.BlockSpec((TE // TB, PEER_HEADS * PEER_TOPK, TB), lambda i, g: (i, 0, 0))
    wblk = pl.BlockSpec((None, PEER_GROUP, D_MODEL), lambda i, g: (layer, g, 0))
    xs = jax.ShapeDtypeStruct((N_TOK, D_MODEL), F32)
    n_prompt = N_PROMPT // TE
    sel = lambda: pltpu.VMEM((TE, PEER_HEADS * PEER_TOPK), F32)
    return pl.pallas_call(
        functools.partial(_expert_kernel, final=final),
        grid=(N_TOK // TE, PEER_NGROUPS),
        in_specs=[tile, tile, pl.BlockSpec((1, 6, D_MODEL), lambda i, g: (i, 0, 0)), kt, kt, kt, wblk, wblk,
                  pl.BlockSpec((1, D_MODEL), lambda i, g: (0, 0))],
        out_specs=[pl.BlockSpec((TE, D_MODEL), lambda i, g: (jnp.minimum(i, n_prompt - 1), 0)),
                   pl.BlockSpec((TE, D_MODEL), lambda i, g: (jnp.maximum(i - n_prompt, 0), 0))] if final else tile,
        out_shape=[jax.ShapeDtypeStruct((N_PROMPT, D_MODEL), F32),
                   jax.ShapeDtypeStruct((N_SAMPLE, D_MODEL), F32)] if final else xs,
        scratch_shapes=[pltpu.VMEM((PEER_NK * G_PITCH, PEER_NK), jnp.uint32), pltpu.VMEM((TE, D_MODEL), F32),
                        sel(), sel(), sel(), pltpu.VMEM((2 * G_UNROLL, PEER_NK, PEER_NK), BF16)],
        compiler_params=_cparams("arbitrary" if final else "parallel", "arbitrary"),
        name="peer_experts_final" if final else "peer_experts",
    )(h_bf16, x, modt_e, a_t, b_t, g_t, u_bf16, v_bf16, final_g.reshape(1, D_MODEL))


def _peer(mix, modt, modt_e, norm_g, peer_bf16, layer, final_g, final):
    w_q, keys, u, v = peer_bf16
    x, h, a_t, b_t, g_t = _router(mix, modt, norm_g, w_q, keys, layer)
    return _experts(h, x, modt_e, a_t, b_t, g_t, u, v, layer, final_g, final)


def kernel(x_prompt, x_sample, cache_k, cache_v, state_hgrn_fwd, state_hgrn_bwd, state_lru_fwd, state_lru_bwd, c, c_ctx, norm1_g, norm2_g, w_mod, b_mod, w_in_even, w_out_even, hgrn_lb_fwd, hgrn_lb_bwd, hgrn_gnorm_g, attn_sink, w_in_odd, w_out_odd, conv_w, conv_b, rg_w_a, rg_b_a, rg_w_x, rg_b_x, rg_lambda, peer_w_q, peer_keys, peer_u, peer_v, final_g):
    x = (x_prompt.reshape(N_PROMPT, D_MODEL), x_sample.reshape(N_SAMPLE, D_MODEL))
    cond = jnp.concatenate([c_ctx[None, :], c, jnp.zeros((COND_PAD - N_COND, D_MODEL), F32)], axis=0)
    mod = _modulation(cond, w_mod, b_mod)
    tile_row = np.asarray([0] * PROMPT_TILES + [1 + i // TILES_PER_SAMPLE for i in range(N_TILES - PROMPT_TILES)])
    modt = [mod[l][tile_row].reshape(N_TILES, 6, D_MODEL) for l in range(DEPTH)]
    modt_e = [m[::TE // TB] for m in modt]

    widths = (HG_W, HG_W, HG_W, HG_W, HG_W, ATT_W, KV_W, KV_W)
    zq, zf, zb, zi, zg, aq, ak, av = _in_proj(x, modt[0], norm1_g[0], w_in_even[0].astype(BF16), widths, "even_in")
    zero_state = jnp.zeros((1, HG_HEADS, HG_DV, HG_DK), F32)
    s0f = jnp.concatenate([zero_state, jnp.swapaxes(state_hgrn_fwd[:, 0], -1, -2)], axis=0)
    s0b = jnp.concatenate([zero_state, jnp.swapaxes(state_hgrn_bwd[:, 0], -1, -2)], axis=0)
    o_f, o_b, sf_t, sb_t = _hgrn_scan(zq, zf, zb, zi, hgrn_lb_fwd, hgrn_lb_bwd, s0f, s0b)
    oatt = _attention(aq, ak, av, cache_k[:, 0].reshape(DEC_BATCH, PAST_LEN, KV_W),
                      cache_v[:, 0].reshape(DEC_BATCH, PAST_LEN, KV_W), attn_sink[0])
    even_mix = (_even_mix, _even_mix_operands(x, o_f, o_b, zg, oatt, hgrn_gnorm_g[0], w_out_even[0].astype(BF16)))
    peer_bf16 = (peer_w_q.astype(BF16), peer_keys.astype(BF16), peer_u.astype(BF16), peer_v.astype(BF16))
    x = _peer(even_mix, modt[0], modt_e[0], norm2_g[0], peer_bf16, 0, final_g, False)

    gate, xc = _odd_in(x, modt[1], norm1_g[1], w_in_odd[0].astype(BF16), conv_w[0], conv_b[0])
    zero_h = jnp.zeros((1, 1, D_RNN), F32)
    h0f = jnp.concatenate([zero_h, state_lru_fwd[:, 0][:, None, :]], axis=0)
    h0b = jnp.concatenate([zero_h, state_lru_bwd[:, 0][:, None, :]], axis=0)
    hf, hb = _lru_scan(xc, rg_w_a[0].astype(BF16), rg_b_a[0], rg_w_x[0].astype(BF16), rg_b_x[0],
                       rg_lambda[0], h0f, h0b)
    odd_mix = (_odd_mix, _odd_mix_operands(x, hf, hb, gate, w_out_odd[0].astype(BF16)))
    y_p, y_s = _peer(odd_mix, modt[1], modt_e[1], norm2_g[1], peer_bf16, 1, final_g, True)

    y_prompt = y_p.reshape(BATCH, SEQ, D_MODEL)
    y_sample = y_s.reshape(DEC_BATCH, DEC_SEQ, D_MODEL)
    new_k = ak[:N_PROMPT].reshape(BATCH, 1, SEQ, KV_HEADS, HEAD_DIM)
    new_v = av[:N_PROMPT].reshape(BATCH, 1, SEQ, KV_HEADS, HEAD_DIM)
    new_hf = jnp.swapaxes(sf_t[:BATCH], -1, -2)[:, None]
    new_hb = jnp.swapaxes(sb_t[:BATCH], -1, -2)[:, None]
    new_lf = hf[:N_PROMPT].reshape(BATCH, SEQ, D_RNN)[:, -1][:, None, :]
    new_lb = hb[:N_PROMPT].reshape(BATCH, SEQ, D_RNN)[:, 0][:, None, :]
    return (y_prompt, y_sample, new_k, new_v, new_hf, new_hb, new_lf, new_lb)
```
